```python
import jax
import jax.numpy as jnp
from jax import lax

D_MODEL = 1024
BATCH = 8
SEQ = 2048
DEPTH = 2
DEC_BATCH = 16
DEC_SEQ = 64
PAST_LEN = 4096

CHUNK = 64
N_A = DEPTH // 2
N_B = DEPTH - N_A
RW_HEAD = 64
RW_HEADS = D_MODEL // RW_HEAD
LORA_W = 64
LORA_A = 64
LORA_G = 160
N_MU = 6
GN_EPS = 64e-5
HEAD_DIM = 64
N_HEADS = D_MODEL // HEAD_DIM
N_KV = N_HEADS // 4
Q_PER_KV = N_HEADS // N_KV
WINDOW = 128
WIN_CHUNKS = WINDOW // CHUNK
ROPE_DIM = HEAD_DIM // 4
ROPE_THETA = 500000.0
ATT_SCALE = HEAD_DIM ** -0.5
N_GROUPS = 4
EXP_PER_GROUP = 8
N_EXPERTS = N_GROUPS * EXP_PER_GROUP
TOP_K = 2
D_EXPERT = D_MODEL // 4
RMS_EPS = 1e-6
NEG_INF = -1e30

kernel_name = 'yoco_rwkv7_swa_sink_hmoe_stream_step'


def rmsnorm(x, g):
    xf = x.astype(jnp.float32)
    y = xf * lax.rsqrt(jnp.mean(xf * xf, axis=-1, keepdims=True) + RMS_EPS)
    return (y * g.astype(jnp.float32)).astype(x.dtype)


def modulate(h, shift, scale):
    return h * (1 + scale[:, None, :]) + shift[:, None, :]


def rope_partial(x, pos):
    half = ROPE_DIM // 2
    inv = jnp.power(jnp.float32(ROPE_THETA), -jnp.arange(half, dtype=jnp.float32) * (2.0 / ROPE_DIM))
    ang = pos[:, None] * inv[None, :]
    cos = jnp.cos(ang)[None, :, None, :]
    sin = jnp.sin(ang)[None, :, None, :]
    xr = x[..., :ROPE_DIM].astype(jnp.float32)
    x1, x2 = xr[..., :half], xr[..., half:]
    rot = jnp.concatenate([x1 * cos - x2 * sin, x2 * cos + x1 * sin], axis=-1)
    return jnp.concatenate([rot.astype(x.dtype), x[..., ROPE_DIM:]], axis=-1)


def wkv_step(state, inp):
    r, w, k, v, a, b = inp
    sa = jnp.einsum('bhvk,bhk->bhv', state, a)
    state = state * w[:, :, None, :] + sa[..., None] * b[:, :, None, :] + v[..., None] * k[:, :, None, :]
    return state, jnp.einsum('bhvk,bhk->bhv', state, r)


def rwkv7_time_mix(h, prev_x, prev_wkv, p, l):
    B, S, D = h.shape
    shifted = jnp.concatenate([prev_x[:, None, :].astype(h.dtype), h[:, :-1]], axis=1)
    xm = h[:, :, None, :] + (shifted - h)[:, :, None, :] * p['rw_mu'][l].astype(h.dtype)
    rkv = jnp.einsum('bsnd,nde->bsne', xm[:, :, :3], p['rw_w_rkv'][l])
    r, k, v = rkv[:, :, 0], rkv[:, :, 1], rkv[:, :, 2]
    xw, xa, xg = xm[:, :, 3], xm[:, :, 4], xm[:, :, 5]
    w_log = -jax.nn.softplus(-(p['rw_w0'][l] + jnp.tanh(xw @ p['rw_w1'][l]) @ p['rw_w2'][l])) - 0.5
    a = jax.nn.sigmoid(p['rw_a0'][l] + (xa @ p['rw_a1'][l]) @ p['rw_a2'][l])
    g = jax.nn.sigmoid(xg @ p['rw_g1'][l]) @ p['rw_g2'][l]

    def heads(t):
        return t.reshape(B, S, RW_HEADS, RW_HEAD).astype(jnp.float32)

    kk = heads(k * p['rw_k_k'][l])
    kk = kk * lax.rsqrt(jnp.maximum(jnp.sum(kk * kk, axis=-1, keepdims=True), 1e-24))
    k = k * (1 + (a - 1) * p['rw_k_a'][l])
    decay = jnp.exp(-jnp.exp(heads(w_log)))
    rh, kh, vh, ah = heads(r), heads(k), heads(v), heads(a)
    xs = tuple(jnp.moveaxis(t, 1, 0) for t in (rh, decay, kh, vh, -kk, kk * ah))
    final, ys = lax.scan(wkv_step, prev_wkv.astype(jnp.float32), xs)
    y = jnp.moveaxis(ys, 0, 1)
    mu = jnp.mean(y, axis=-1, keepdims=True)
    var = jnp.mean(jnp.square(y - mu), axis=-1, keepdims=True)
    y = ((y - mu) * lax.rsqrt(var + GN_EPS)).reshape(B, S, D) * p['rw_lnx_w'][l].astype(jnp.float32) + p['rw_lnx_b'][l].astype(jnp.float32)
    bonus = jnp.sum(rh * kh * p['rw_r_k'][l].astype(jnp.float32), axis=-1, keepdims=True) * vh
    out = ((y + bonus.reshape(B, S, D)) * g.astype(jnp.float32)).astype(h.dtype) @ p['rw_w_o'][l]
    return out, h[:, -1], final.astype(prev_wkv.dtype)


def hier_moe(h, p, l):
    wg, we = p['moe_w_group'][l], p['moe_w_expert'][l]
    w_gu, w_down = p['moe_w_gu'][l], p['moe_w_down'][l]

    def per_sequence(t):
        S = t.shape[0]
        lg = (t @ wg).astype(jnp.float32)
        gi = jnp.argmax(lg, axis=-1)
        gp = jnp.take_along_axis(jax.nn.softmax(lg, axis=-1), gi[:, None], axis=1)[:, 0]
        le = (t @ we).astype(jnp.float32).reshape(S, N_GROUPS, EXP_PER_GROUP)
        le = jnp.take_along_axis(le, gi[:, None, None], axis=1)[:, 0]
        tv, ti = lax.top_k(jax.nn.softmax(le, axis=-1), TOP_K)
        tw = tv / jnp.sum(tv, axis=-1, keepdims=True) * gp[:, None]
        eidx = gi[:, None] * EXP_PER_GROUP + ti
        cw = jnp.sum(jax.nn.one_hot(eidx, N_EXPERTS, dtype=jnp.float32) * tw[..., None], axis=1)
        gu = jnp.einsum('sd,edf->sef', t, w_gu)
        hid = jax.nn.silu(gu[..., :D_EXPERT]) * gu[..., D_EXPERT:] * cw[..., None].astype(t.dtype)
        return jnp.einsum('sef,efd->sd', hid, w_down)

    return lax.map(per_sequence, h)


def shared_kv(x, c, pos, p):
    mod = jax.nn.silu(c) @ p['kv_ada_w'] + p['kv_ada_b']
    shift, scale = jnp.split(mod, 2, axis=-1)
    h = modulate(rmsnorm(x, p['kv_norm_g']), shift, scale)
    B, S, _ = x.shape
    kv = (h @ p['w_kv'] + p['b_kv']).reshape(B, S, 2, N_KV, HEAD_DIM)
    return rope_partial(kv[:, :, 0], pos), kv[:, :, 1]


def sink_probs(s, sink, mask):
    if mask is not None:
        s = jnp.where(mask, s, NEG_INF)
    m = jnp.maximum(jnp.max(s, axis=-1, keepdims=True), sink)
    e = jnp.exp(s - m)
    return e / (jnp.sum(e, axis=-1, keepdims=True) + jnp.exp(sink - m))


def banded_window_attn(q, k, v, sinks):
    B, S, H, hd = q.shape
    NC = S // CHUNK
    qc = q.reshape(B, NC, CHUNK, N_KV, Q_PER_KV, hd)

    def band(t):
        tc = t.reshape(B, NC, CHUNK, N_KV, hd)
        tp = jnp.pad(tc, ((0, 0), (WIN_CHUNKS, 0), (0, 0), (0, 0), (0, 0)))
        return jnp.concatenate([tp[:, i:i + NC] for i in range(WIN_CHUNKS + 1)], axis=2)

    kb, vb = band(k), band(v)
    s = jnp.einsum('bnqkgd,bnskd->bnkgqs', qc, kb).astype(jnp.float32) * ATT_SCALE
    slot_chunk = jnp.arange(NC)[:, None] - WIN_CHUNKS + jnp.arange((WIN_CHUNKS + 1) * CHUNK)[None, :] // CHUNK
    mask = (slot_chunk >= 0)[None, :, None, None, None, :]
    sink = sinks.astype(jnp.float32).reshape(N_KV, Q_PER_KV)[None, None, :, :, None, None]
    probs = sink_probs(s, sink, mask)
    o = jnp.einsum('bnkgqs,bnskd->bnqkgd', probs.astype(vb.dtype), vb)
    return o.reshape(B, S, H * hd)


def recent_window_attn(q, k_all, v_all, sinks):
    B, T, H, hd = q.shape
    qg = q.reshape(B, T, N_KV, Q_PER_KV, hd)
    s = jnp.einsum('btkgd,bskd->bkgts', qg, k_all).astype(jnp.float32) * ATT_SCALE
    sink = sinks.astype(jnp.float32).reshape(N_KV, Q_PER_KV)[None, :, :, None, None]
    probs = sink_probs(s, sink, None)
    o = jnp.einsum('bkgts,bskd->btkgd', probs.astype(v_all.dtype), v_all)
    return o.reshape(B, T, H * hd)


def window_attn(h, pos, k, v, past_k, past_v, p, j):
    B, S, _ = h.shape
    q = (h @ p['at_w_q'][j] + p['at_b_q'][j]).reshape(B, S, N_HEADS, HEAD_DIM)
    q = rope_partial(q, pos)
    if past_k is None:
        o = banded_window_attn(q, k, v, p['at_sinks'][j])
    else:
        k_all = jnp.concatenate([past_k.astype(k.dtype), k], axis=1)
        v_all = jnp.concatenate([past_v.astype(v.dtype), v], axis=1)
        o = recent_window_attn(q, k_all, v_all, p['at_sinks'][j])
    return o @ p['at_w_o'][j] + p['at_b_o'][j]


def run_trunk(x, c, pos, prev_x, prev_wkv, past_k, past_v, p):
    cs = jax.nn.silu(c)
    new_x, new_wkv = [], []
    k = v = None
    for l in range(DEPTH):
        mod = cs @ p['ada_w'][l] + p['ada_b'][l]
        sh1, sc1, g1, sh2, sc2, g2 = jnp.split(mod, 6, axis=-1)
        if l >= N_A and k is None:
            k, v = shared_kv(x, c, pos, p)
        h = modulate(rmsnorm(x, p['norm_g'][l, 0]), sh1, sc1)
        if l < N_A:
            y, last_x, st = rwkv7_time_mix(h, prev_x[l], prev_wkv[l], p, l)
            new_x.append(last_x)
            new_wkv.append(st)
        else:
            y = window_attn(h, pos, k, v, past_k, past_v, p, l - N_A)
        x = x + g1[:, None, :] * y
        h = modulate(rmsnorm(x, p['norm_g'][l, 1]), sh2, sc2)
        x = x + g2[:, None, :] * hier_moe(h, p, l)
    return rmsnorm(x, p['final_norm_g']), jnp.stack(new_x), jnp.stack(new_wkv), k, v


def setup_inputs(seed: int = 0) -> dict:
    key = jax.random.key(seed)
    ks = iter(jax.random.split(key, 64))
    f32 = jnp.float32

    def nrm(shape, scale):
        return jax.random.normal(next(ks), shape, f32) * scale

    D = D_MODEL
    cache_rows = min(WINDOW, PAST_LEN)
    HQ = N_HEADS * HEAD_DIM
    KVW = 2 * N_KV * HEAD_DIM
    return {
        'x_prompt': nrm((BATCH, SEQ, D), 1.0),
        'x_sample': nrm((DEC_BATCH, DEC_SEQ, D), 1.0),
        'state_shift': nrm((N_A, DEC_BATCH, D), 1.0),
        'state_wkv': nrm((N_A, DEC_BATCH, RW_HEADS, RW_HEAD, RW_HEAD), 0.05),
        'cache_k': nrm((DEC_BATCH, cache_rows, N_KV, HEAD_DIM), 1.0),
        'cache_v': nrm((DEC_BATCH, cache_rows, N_KV, HEAD_DIM), 1.0),
        'c_prompt': nrm((BATCH, D), 1.0),
        'c_sample': nrm((DEC_BATCH, D), 1.0),
        'ada_w': nrm((DEPTH, D, 6 * D), 0.5 * D ** -0.5),
        'ada_b': nrm((DEPTH, 6 * D), 0.02),
        'norm_g': 1.0 + nrm((DEPTH, 2, D), 0.05),
        'rw_mu': jax.random.uniform(next(ks), (N_A, N_MU, D), f32),
        'rw_w_rkv': nrm((N_A, 3, D, D), D ** -0.5),
        'rw_w0': nrm((N_A, D), 0.5) - 1.0,
        'rw_w1': nrm((N_A, D, LORA_W), D ** -0.5),
        'rw_w2': nrm((N_A, LORA_W, D), 0.1 * LORA_W ** -0.5),
        'rw_a0': nrm((N_A, D), 0.1),
        'rw_a1': nrm((N_A, D, LORA_A), D ** -0.5),
        'rw_a2': nrm((N_A, LORA_A, D), 0.1 * LORA_A ** -0.5),
        'rw_g1': nrm((N_A, D, LORA_G), D ** -0.5),
        'rw_g2': nrm((N_A, LORA_G, D), LORA_G ** -0.5),
        'rw_k_k': 0.85 + nrm((N_A, D), 0.05),
        'rw_k_a': 1.0 + nrm((N_A, D), 0.05),
        'rw_r_k': nrm((N_A, RW_HEADS, RW_HEAD), 0.1),
        'rw_lnx_w': 1.0 + nrm((N_A, D), 0.05),
        'rw_lnx_b': nrm((N_A, D), 0.02),
        'rw_w_o': nrm((N_A, D, D), D ** -0.5),
        'kv_ada_w': nrm((D, 2 * D), 0.5 * D ** -0.5),
        'kv_ada_b': nrm((2 * D,), 0.02),
        'kv_norm_g': 1.0 + nrm((D,), 0.05),
        'w_kv': nrm((D, KVW), D ** -0.5),
        'b_kv': nrm((KVW,), 0.02),
        'at_w_q': nrm((N_B, D, HQ), D ** -0.5),
        'at_b_q': nrm((N_B, HQ), 0.02),
        'at_sinks': nrm((N_B, N_HEADS), 0.5),
        'at_w_o': nrm((N_B, HQ, D), HQ ** -0.5),
        'at_b_o': nrm((N_B, D), 0.02),
        'moe_w_group': nrm((DEPTH, D, N_GROUPS), D ** -0.5),
        'moe_w_expert': nrm((DEPTH, D, N_EXPERTS), D ** -0.5),
        'moe_w_gu': nrm((DEPTH, N_EXPERTS, D, 2 * D_EXPERT), D ** -0.5),
        'moe_w_down': nrm((DEPTH, N_EXPERTS, D_EXPERT, D), D_EXPERT ** -0.5),
        'final_norm_g': 1.0 + nrm((D,), 0.05),
    }


def reference(x_prompt, x_sample, state_shift, state_wkv, cache_k, cache_v, c_prompt, c_sample,
              ada_w, ada_b, norm_g, rw_mu, rw_w_rkv, rw_w0, rw_w1, rw_w2, rw_a0, rw_a1, rw_a2,
              rw_g1, rw_g2, rw_k_k, rw_k_a, rw_r_k, rw_lnx_w, rw_lnx_b, rw_w_o,
              kv_ada_w, kv_ada_b, kv_norm_g, w_kv, b_kv,
              at_w_q, at_b_q, at_sinks, at_w_o, at_b_o,
              moe_w_group, moe_w_expert, moe_w_gu, moe_w_down, final_norm_g):
    p = dict(ada_w=ada_w, ada_b=ada_b, norm_g=norm_g, rw_mu=rw_mu, rw_w_rkv=rw_w_rkv,
             rw_w0=rw_w0, rw_w1=rw_w1, rw_w2=rw_w2, rw_a0=rw_a0, rw_a1=rw_a1, rw_a2=rw_a2,
             rw_g1=rw_g1, rw_g2=rw_g2, rw_k_k=rw_k_k, rw_k_a=rw_k_a, rw_r_k=rw_r_k,
             rw_lnx_w=rw_lnx_w, rw_lnx_b=rw_lnx_b, rw_w_o=rw_w_o,
             kv_ada_w=kv_ada_w, kv_ada_b=kv_ada_b, kv_norm_g=kv_norm_g, w_kv=w_kv, b_kv=b_kv,
             at_w_q=at_w_q, at_b_q=at_b_q, at_sinks=at_sinks, at_w_o=at_w_o, at_b_o=at_b_o,
             moe_w_group=moe_w_group, moe_w_expert=moe_w_expert, moe_w_gu=moe_w_gu,
             moe_w_down=moe_w_down, final_norm_g=final_norm_g)
    B, S, _ = x_prompt.shape
    T = x_sample.shape[1]
    pos_p = jnp.arange(S, dtype=jnp.float32)
    pos_s = PAST_LEN + jnp.arange(T, dtype=jnp.float32)
    zero_x = jnp.zeros((N_A, B, D_MODEL), x_prompt.dtype)
    zero_wkv = jnp.zeros((N_A, B, RW_HEADS, RW_HEAD, RW_HEAD), state_wkv.dtype)
    y_prompt, p_shift, p_wkv, p_k, p_v = run_trunk(x_prompt, c_prompt, pos_p, zero_x, zero_wkv, None, None, p)
    y_sample, s_shift, s_wkv, s_k, s_v = run_trunk(x_sample, c_sample, pos_s, state_shift, state_wkv, cache_k, cache_v, p)
    keep = min(WINDOW, S)
    return (y_prompt, y_sample, p_shift, p_wkv, p_k[:, S - keep:], p_v[:, S - keep:], s_shift, s_wkv, s_k, s_v)
```

```python
import functools

import jax
import jax.numpy as jnp
from jax import lax
from jax.experimental import pallas as pl
from jax.experimental.pallas import tpu as pltpu

F32 = jnp.float32
BF16 = jnp.bfloat16

D_MODEL = 1024
LANES = 128
HEAD = 64
PAIRS = D_MODEL // LANES
CHUNK = 64
PAST_LEN = 4096
N_KV = 4
KV_W = N_KV * HEAD
ROPE_DIM = 16
ROPE_THETA = 500000.0
ATT_SCALE = HEAD ** -0.5
N_GROUPS = 4
EXP_PER_GROUP = 8
N_EXPERTS = N_GROUPS * EXP_PER_GROUP
D_EXPERT = D_MODEL // 4
ROUTE_LANE0 = N_GROUPS
RMS_EPS = 1e-6
GN_EPS = 64e-5
NEG_INF = -1e30
VMEM_LIMIT = 56 * 1024 * 1024


def _params(sem):
    return pltpu.CompilerParams(dimension_semantics=sem, vmem_limit_bytes=VMEM_LIMIT)


def _dot(a, b):
    return jnp.dot(a, b, preferred_element_type=F32)


def _dot_nt(a, b):
    return lax.dot_general(a, b, (((1,), (1,)), ((), ())), preferred_element_type=F32)


def _dot_tn(a, b):
    return lax.dot_general(a, b, (((0,), (0,)), ((), ())), preferred_element_type=F32)


def _split2(x):
    hi = x.astype(BF16)
    lo = (x - hi.astype(F32)).astype(BF16)
    return hi, lo


def _dot_x3(a, b, dot=_dot):
    ah, al = _split2(a)
    bh, bl = _split2(b)
    return dot(ah, bh) + dot(ah, bl) + dot(al, bh)


def _dot_lhs_x2(a, m):
    ah, al = _split2(a)
    return _dot(ah, m) + _dot(al, m)


def _sigmoid(x):
    return 1.0 / (1.0 + jnp.exp(-x))


def _rms(x):
    return x * lax.rsqrt(jnp.mean(x * x, axis=-1, keepdims=True) + RMS_EPS)


def _modulate(n, gain, shift, scale, nb):
    rows = n.shape[0]
    h = (n * gain).reshape(nb, rows // nb, D_MODEL)
    return (h * (1.0 + scale) + shift).reshape(rows, D_MODEL)


def _seg_matrix(scale):
    r = lax.broadcasted_iota(jnp.int32, (LANES, LANES), 0)
    c = lax.broadcasted_iota(jnp.int32, (LANES, LANES), 1)
    return jnp.where((r < HEAD) == (c < HEAD), scale, 0.0).astype(BF16)


def _const_spec(shape):
    nd = len(shape)
    return pl.BlockSpec(shape, lambda *_: (0,) * nd)


def _cond_kernel(c_ref, w_ref, b_ref, o_ref):
    c = c_ref[...]
    cs = (c * _sigmoid(c)).astype(BF16)
    o_ref[...] = _dot(cs, w_ref[...].astype(BF16)) + b_ref[...]


def _cond_linear(c, w, b, tn=512):
    nl, _, n = w.shape
    m = c.shape[0]
    return pl.pallas_call(
        _cond_kernel,
        grid=(nl, n // tn),
        in_specs=[
            pl.BlockSpec((m, D_MODEL), lambda l, j: (0, 0)),
            pl.BlockSpec((None, D_MODEL, tn), lambda l, j: (l, 0, j)),
            pl.BlockSpec((None, 1, tn), lambda l, j: (l, 0, j)),
        ],
        out_specs=pl.BlockSpec((None, m, tn), lambda l, j: (l, 0, j)),
        out_shape=jax.ShapeDtypeStruct((nl, m, n), F32),
        compiler_params=_params(("parallel", "parallel")),
        name="cond_linear",
    )(c, w, b)


def _rwkv_proj_kernel(x_ref, mod_ref, prev_ref, ng_ref, mu_ref, wrkv_ref, w0_ref, w1_ref, w2_ref,
                      a0_ref, a1_ref, a2_ref, g1_ref, g2_ref, kk_ref, ka_ref,
                      r_ref, lw_ref, k_ref, v_ref, a_ref, b_ref, g_ref, last_ref, carry_ref):
    tm = x_ref.shape[0]

    @pl.when(pl.program_id(1) == 0)
    def _():
        carry_ref[...] = prev_ref[...]

    shift = mod_ref[:, 0:D_MODEL].reshape(1, 1, D_MODEL)
    scale = mod_ref[:, D_MODEL:2 * D_MODEL].reshape(1, 1, D_MODEL)
    h = _modulate(_rms(x_ref[...]), ng_ref[...], shift, scale, 1)
    row = lax.broadcasted_iota(jnp.int32, (tm, 1), 0)
    shifted = jnp.where(row == 0, carry_ref[...], pltpu.roll(h, 1, axis=0))
    carry_ref[...] = h[tm - 1:tm, :]
    last_ref[...] = h[tm - 1:tm, :]
    dx = shifted - h

    def mix(n):
        return (h + dx * mu_ref[n:n + 1, :]).astype(BF16)

    r = _dot(mix(0), wrkv_ref[0])
    k = _dot(mix(1), wrkv_ref[1])
    v = _dot(mix(2), wrkv_ref[2])
    ww = _dot(jnp.tanh(_dot(mix(3), w1_ref[...])).astype(BF16), w2_ref[...])
    z = -(w0_ref[...] + ww)
    softplus = jnp.maximum(z, 0.0) + jnp.log(1.0 + jnp.exp(-jnp.abs(z)))
    lw = -jnp.exp(-softplus - 0.5)
    asig = _sigmoid(a0_ref[...] + _dot(_dot(mix(4), a1_ref[...]).astype(BF16), a2_ref[...]))
    g = _dot(_sigmoid(_dot(mix(5), g1_ref[...])).astype(BF16), g2_ref[...])

    kk = k * kk_ref[...]
    kk2 = kk * kk
    seg = _seg_matrix(1.0)
    ss = jnp.concatenate(
        [_dot_lhs_x2(kk2[:, p * LANES:(p + 1) * LANES], seg) for p in range(PAIRS)], axis=1)
    kkn = kk * lax.rsqrt(jnp.maximum(ss, 1e-24))

    r_ref[...] = r
    lw_ref[...] = lw
    k_ref[...] = k * (1.0 + (asig - 1.0) * ka_ref[...])
    v_ref[...] = v
    a_ref[...] = -kkn
    b_ref[...] = kkn * asig
    g_ref[...] = g


def _rwkv_proj(x, mod, prev_x, w, nb, s, tm):
    n = nb * s
    nt = s // tm
    row_spec = pl.BlockSpec((tm, D_MODEL), lambda b, j: (b * nt + j, 0))
    vec = _const_spec((1, D_MODEL))
    lora_w, lora_g = w["w1"].shape[1], w["g1"].shape[1]
    in_specs = [
        row_spec,
        pl.BlockSpec((None, 1, 6 * D_MODEL), lambda b, j: (b, 0, 0)),
        pl.BlockSpec((None, 1, D_MODEL), lambda b, j: (b, 0, 0)),
        vec, _const_spec((6, D_MODEL)), _const_spec((3, D_MODEL, D_MODEL)),
        vec, _const_spec((D_MODEL, lora_w)), _const_spec((lora_w, D_MODEL)),
        vec, _const_spec((D_MODEL, lora_w)), _const_spec((lora_w, D_MODEL)),
        _const_spec((D_MODEL, lora_g)), _const_spec((lora_g, D_MODEL)),
        vec, vec,
    ]
    act = jax.ShapeDtypeStruct((n, D_MODEL), F32)
    outs = pl.pallas_call(
        _rwkv_proj_kernel,
        grid=(nb, nt),
        in_specs=in_specs,
        out_specs=[row_spec] * 7 + [pl.BlockSpec((None, 1, D_MODEL), lambda b, j: (b, 0, 0))],
        out_shape=[act] * 7 + [jax.ShapeDtypeStruct((nb, 1, D_MODEL), F32)],
        scratch_shapes=[pltpu.VMEM((1, D_MODEL), F32)],
        compiler_params=_params(("parallel", "arbitrary")),
        name="rwkv_proj",
    )(x, mod, prev_x, w["norm_g"], w["mu"], w["w_rkv"], w["w0"], w["w1"], w["w2"],
      w["a0"], w["a1"], w["a2"], w["g1"], w["g2"], w["k_k"], w["k_a"])
    return outs


def _wkv_kernel(r_ref, lw_ref, k_ref, v_ref, a_ref, b_ref, g_ref, s0_ref, rk_ref, lnw_ref, lnb_ref,
                z_ref, st_ref, st_scr):
    nc = pl.num_programs(2)
    n = pl.program_id(2)

    @pl.when(n == 0)
    def _():
        st_scr[...] = s0_ref[...]

    r, lw, k, v, a, b = (t[...] for t in (r_ref, lw_ref, k_ref, v_ref, a_ref, b_ref))
    lane = lax.broadcasted_iota(jnp.int32, (1, LANES), 1)
    m0 = (lane < HEAD).astype(F32)
    m1 = 1.0 - m0

    ti = lax.broadcasted_iota(jnp.int32, (CHUNK, CHUNK), 0)
    tj = lax.broadcasted_iota(jnp.int32, (CHUNK, CHUNK), 1)
    tri = (tj <= ti).astype(BF16)
    lw_hi = lw.astype(BF16)
    lw_r1 = lw - lw_hi.astype(F32)
    lw_mid = lw_r1.astype(BF16)
    lw_lo = (lw_r1 - lw_mid.astype(F32)).astype(BF16)
    cum = _dot(tri, lw_hi) + _dot(tri, lw_mid) + _dot(tri, lw_lo)
    cl = cum[CHUNK - 1:CHUNK, :]
    e_end = jnp.exp(cl - cum)
    e_neg = jnp.exp(-cum)
    at = a * jnp.exp(cum - lw)
    rt = r * jnp.exp(cum)
    bt = b * e_neg
    kt = k * e_neg
    bh = b * e_end
    kh = k * e_end

    g0 = _dot_x3(jnp.concatenate([at * m0, rt * m0], axis=0), jnp.concatenate([bt, kt], axis=0), _dot_nt)
    g1 = _dot_x3(jnp.concatenate([at * m1, rt * m1], axis=0), jnp.concatenate([kt, bt], axis=0), _dot_nt)
    rr = lax.broadcasted_iota(jnp.int32, (LANES, LANES), 0)
    cc = lax.broadcasted_iota(jnp.int32, (LANES, LANES), 1)
    t_idx = rr & (CHUNK - 1)
    j_idx = cc & (CHUNK - 1)
    keep = (j_idx < t_idx) | ((rr >= CHUNK) & (j_idx == t_idx))
    g0 = jnp.where(keep, g0, 0.0)
    g1 = jnp.where(keep, g1, 0.0)
    lo_cols = lax.broadcasted_iota(jnp.int32, (CHUNK, LANES), 1) < HEAD

    st = st_scr[...]
    v0, v1 = v * m0, v * m1
    ak = jnp.concatenate([jnp.where(lo_cols, 0.0, g0[0:CHUNK]), jnp.where(lo_cols, g1[0:CHUNK], 0.0)], axis=1)
    rhs = _dot_x3(at, st) + _dot_x3(ak, jnp.concatenate([v0, v0, v1, v1], axis=0))

    p = jnp.concatenate([jnp.where(lo_cols, g0[0:CHUNK], 0.0), jnp.where(lo_cols, 0.0, g1[0:CHUNK])], axis=0)
    tm = jnp.where(rr == cc, 1.0, 0.0) + p
    pk = p
    for _ in range(5):
        pkb = pk.astype(BF16)
        pk = _dot(pkb, pkb)
        tm = tm + _dot(tm.astype(BF16), pk.astype(BF16))
    tcat = tm[0:CHUNK] + tm[CHUNK:2 * CHUNK]
    u = _dot_x3(tcat, jnp.concatenate([rhs * m0, rhs * m1], axis=0))
    u0, u1 = u * m0, u * m1

    rbk = jnp.concatenate([g0[CHUNK:2 * CHUNK], g1[CHUNK:2 * CHUNK]], axis=1)
    y = _dot_x3(rt, st) + _dot_x3(rbk, jnp.concatenate([u0, v0, v1, u1], axis=0))

    decay = jnp.where(rr == cc, jnp.exp(cl), 0.0)
    st_new = _dot_x3(jnp.concatenate([bh, kh, decay], axis=0), jnp.concatenate([u, v, st], axis=0), _dot_tn)
    st_new = jnp.where((rr < HEAD) == (cc < HEAD), st_new, 0.0)
    st_scr[...] = st_new

    @pl.when(n == nc - 1)
    def _():
        st_ref[...] = st_new

    avg = _seg_matrix(1.0 / HEAD)
    yc = y - _dot_lhs_x2(y, avg)
    var = _dot_lhs_x2(yc * yc, avg)
    yn = yc * lax.rsqrt(var + GN_EPS) * lnw_ref[...] + lnb_ref[...]
    bonus = _dot_lhs_x2(r * k * rk_ref[...], _seg_matrix(1.0)) * v
    z_ref[...] = ((yn + bonus) * g_ref[...]).astype(z_ref.dtype)


def _wkv(r, lw, k, v, a, b, g, st0, w, nb, s):
    n = nb * s
    nc = s // CHUNK
    blk = pl.BlockSpec((CHUNK, LANES), lambda bi, p, c: (bi * nc + c, p))
    vec = pl.BlockSpec((1, LANES), lambda bi, p, c: (0, p))
    st_spec = pl.BlockSpec((None, None, LANES, LANES), lambda bi, p, c: (bi, p, 0, 0))
    return pl.pallas_call(
        _wkv_kernel,
        grid=(nb, PAIRS, nc),
        in_specs=[blk] * 7 + [st_spec, vec, vec, vec],
        out_specs=[blk, st_spec],
        out_shape=[jax.ShapeDtypeStruct((n, D_MODEL), BF16),
                   jax.ShapeDtypeStruct((nb, PAIRS, LANES, LANES), F32)],
        scratch_shapes=[pltpu.VMEM((LANES, LANES), F32)],
        compiler_params=_params(("parallel", "parallel", "arbitrary")),
        name="wkv",
    )(r, lw, k, v, a, b, g, st0, w["r_k"], w["lnx_w"], w["lnx_b"])


def _out_proj_kernel(x_ref, z_ref, gate_ref, w_ref, bias_ref, o_ref, *, nb):
    rows = x_ref.shape[0]
    y = _dot(z_ref[...], w_ref[...]) + bias_ref[...]
    y = (y.reshape(nb, rows // nb, D_MODEL) * gate_ref[...]).reshape(rows, D_MODEL)
    o_ref[...] = x_ref[...] + y


def _out_proj(x, z, gate, w, bias, nb, s, tm):
    n = nb * s
    tb = max(tm // s, 1)
    per_seq = max(s // tm, 1)
    row_spec = pl.BlockSpec((tm, D_MODEL), lambda i: (i, 0))
    return pl.pallas_call(
        functools.partial(_out_proj_kernel, nb=tb),
        grid=(n // tm,),
        in_specs=[row_spec, row_spec,
                  pl.BlockSpec((tb, 1, D_MODEL), lambda i: (i // per_seq, 0, 0)),
                  _const_spec((D_MODEL, D_MODEL)), _const_spec((1, D_MODEL))],
        out_specs=row_spec,
        out_shape=jax.ShapeDtypeStruct((n, D_MODEL), F32),
        compiler_params=_params(("parallel",)),
        name="out_proj",
    )(x, z, gate, w, bias)


def _route(logits):
    lane = lax.broadcasted_iota(jnp.int32, logits.shape, 1)
    big = jnp.int32(1 << 20)
    lg = jnp.where(lane < N_GROUPS, logits, -jnp.inf)
    gmax = jnp.max(lg, axis=-1, keepdims=True)
    gi = jnp.min(jnp.where(lg == gmax, lane, big), axis=-1, keepdims=True)
    gp = 1.0 / jnp.sum(jnp.exp(lg - gmax), axis=-1, keepdims=True)
    first = ROUTE_LANE0 + gi * EXP_PER_GROUP
    le = jnp.where((lane >= first) & (lane < first + EXP_PER_GROUP), logits, -jnp.inf)
    top1 = jnp.max(le, axis=-1, keepdims=True)
    i1 = jnp.min(jnp.where(le == top1, lane, big), axis=-1, keepdims=True)
    le2 = jnp.where(lane == i1, -jnp.inf, le)
    top2 = jnp.max(le2, axis=-1, keepdims=True)
    i2 = jnp.min(jnp.where(le2 == top2, lane, big), axis=-1, keepdims=True)
    e2 = jnp.exp(top2 - top1)
    w1 = gp / (1.0 + e2)
    return jnp.where(lane == i1, w1, 0.0) + jnp.where(lane == i2, w1 * e2, 0.0)


def _moe_kernel(x_ref, mod_ref, ng_ref, wr_ref, wgu_ref, wdn_ref, fg_ref, o_ref, h_scr, cw_scr, acc_scr,
                *, nb, final_norm):
    e = pl.program_id(1)
    rows = x_ref.shape[0]

    @pl.when(e == 0)
    def _():
        shift = mod_ref[:, :, 3 * D_MODEL:4 * D_MODEL]
        scale = mod_ref[:, :, 4 * D_MODEL:5 * D_MODEL]
        h = _modulate(_rms(x_ref[...]), ng_ref[...], shift, scale, nb)
        h_scr[...] = h.astype(BF16)
        cw_scr[...] = _route(_dot_x3(h, wr_ref[...]))
        acc_scr[...] = jnp.zeros_like(acc_scr)

    lane = lax.broadcasted_iota(jnp.int32, (rows, LANES), 1)
    cw = jnp.sum(jnp.where(lane == ROUTE_LANE0 + e, cw_scr[...], 0.0), axis=-1, keepdims=True)
    gu = _dot(h_scr[...], wgu_ref[...])
    gate = gu[:, :D_EXPERT]
    hid = gate * _sigmoid(gate) * gu[:, D_EXPERT:] * cw
    acc_scr[...] += _dot(hid.astype(BF16), wdn_ref[...])

    @pl.when(e == N_EXPERTS - 1)
    def _():
        g2 = mod_ref[:, :, 5 * D_MODEL:6 * D_MODEL]
        y = (acc_scr[...].reshape(nb, rows // nb, D_MODEL) * g2).reshape(rows, D_MODEL)
        out = x_ref[...] + y
        if final_norm:
            out = _rms(out) * fg_ref[...]
        o_ref[...] = out


def _moe(x, mod, norm_g, w_route, w_gu, w_down, final_g, nb, s, tm, final_norm):
    n = nb * s
    tb = max(tm // s, 1)
    per_seq = max(s // tm, 1)
    row_spec = pl.BlockSpec((tm, D_MODEL), lambda i, e: (i, 0))
    return pl.pallas_call(
        functools.partial(_moe_kernel, nb=tb, final_norm=final_norm),
        grid=(n // tm, N_EXPERTS),
        in_specs=[row_spec,
                  pl.BlockSpec((tb, 1, 6 * D_MODEL), lambda i, e: (i // per_seq, 0, 0)),
                  _const_spec((1, D_MODEL)), _const_spec((D_MODEL, LANES)),
                  pl.BlockSpec((None, D_MODEL, 2 * D_EXPERT), lambda i, e: (e, 0, 0)),
                  pl.BlockSpec((None, D_EXPERT, D_MODEL), lambda i, e: (e, 0, 0)),
                  _const_spec((1, D_MODEL))],
        out_specs=row_spec,
        out_shape=jax.ShapeDtypeStruct((n, D_MODEL), F32),
        scratch_shapes=[pltpu.VMEM((tm, D_MODEL), BF16), pltpu.VMEM((tm, LANES), F32),
                        pltpu.VMEM((tm, D_MODEL), F32)],
        compiler_params=_params(("parallel", "arbitrary")),
        name="moe",
    )(x, mod, norm_g, w_route, w_gu, w_down, final_g)


def _rope(x, cos, sin_lo, sin_hi):
    return x * cos + pltpu.roll(x, LANES - ROPE_DIM // 2, axis=1) * sin_lo + pltpu.roll(x, ROPE_DIM // 2, axis=1) * sin_hi


def _qkv_kernel(x_ref, mod_ref, kvmod_ref, ngq_ref, ngkv_ref, wq_ref, bq_ref, wkv_ref, bkv_ref,
                cos_ref, slo_ref, shi_ref, q_ref, k_ref, v_ref):
    n = _rms(x_ref[...])
    shift = mod_ref[:, 0:D_MODEL].reshape(1, 1, D_MODEL)
    scale = mod_ref[:, D_MODEL:2 * D_MODEL].reshape(1, 1, D_MODEL)
    hq = _modulate(n, ngq_ref[...], shift, scale, 1)
    kshift = kvmod_ref[:, 0:D_MODEL].reshape(1, 1, D_MODEL)
    kscale = kvmod_ref[:, D_MODEL:2 * D_MODEL].reshape(1, 1, D_MODEL)
    hkv = _modulate(n, ngkv_ref[...], kshift, kscale, 1)
    cos, slo, shi = cos_ref[...], slo_ref[...], shi_ref[...]
    q = _dot(hq.astype(BF16), wq_ref[...]) + bq_ref[...]
    for p in range(PAIRS):
        cols = slice(p * LANES, (p + 1) * LANES)
        q_ref[:, cols] = (_rope(q[:, cols], cos, slo, shi) * ATT_SCALE).astype(q_ref.dtype)
    kv = _dot(hkv.astype(BF16), wkv_ref[...]) + bkv_ref[...]
    for p in range(KV_W // LANES):
        cols = slice(p * LANES, (p + 1) * LANES)
        k_ref[:, cols] = _rope(kv[:, cols], cos, slo, shi)
    v_ref[...] = kv[:, KV_W:]


def _qkv_proj(x, mod, kvmod, w, tables, nb, s, tm):
    n = nb * s
    nt = s // tm
    row_spec = pl.BlockSpec((tm, D_MODEL), lambda b, j: (b * nt + j, 0))
    kv_spec = pl.BlockSpec((tm, KV_W), lambda b, j: (b * nt + j, 0))
    tab_spec = pl.BlockSpec((tm, LANES), lambda b, j: (j, 0))
    vec = _const_spec((1, D_MODEL))
    return pl.pallas_call(
        _qkv_kernel,
        grid=(nb, nt),
        in_specs=[row_spec,
                  pl.BlockSpec((None, 1, 6 * D_MODEL), lambda b, j: (b, 0, 0)),
                  pl.BlockSpec((None, 1, 2 * D_MODEL), lambda b, j: (b, 0, 0)),
                  vec, vec, _const_spec((D_MODEL, D_MODEL)), vec,
                  _const_spec((D_MODEL, 2 * KV_W)), _const_spec((1, 2 * KV_W)),
                  tab_spec, tab_spec, tab_spec],
        out_specs=[row_spec, kv_spec, kv_spec],
        out_shape=[jax.ShapeDtypeStruct((n, D_MODEL), BF16),
                   jax.ShapeDtypeStruct((n, KV_W), F32), jax.ShapeDtypeStruct((n, KV_W), F32)],
        compiler_params=_params(("parallel", "parallel")),
        name="qkv_proj",
    )(x, mod, kvmod, w["norm_gq"], w["norm_gkv"], w["w_q"], w["b_q"], w["w_kv"], w["b_kv"], *tables)


def _attn_kernel(sink_ref, q_ref, k0_ref, k1_ref, k2_ref, v0_ref, v1_ref, v2_ref, o_ref, *, banded):
    n = pl.program_id(1)
    k_all = jnp.concatenate([k0_ref[...], k1_ref[...], k2_ref[...]], axis=0)
    v_all = jnp.concatenate([v0_ref[...], v1_ref[...], v2_ref[...]], axis=0)
    lane = lax.broadcasted_iota(jnp.int32, (1, LANES), 1)
    lo = lane < HEAD
    col = lax.broadcasted_iota(jnp.int32, (1, 3 * CHUNK), 1)
    if banded:
        visible = (col >= 2 * CHUNK) | ((col >= CHUNK) & (n >= 1)) | (n >= 2)

    def halves(t, kv_head):
        blk = t[:, (kv_head // 2) * LANES:(kv_head // 2 + 1) * LANES]
        if kv_head % 2 == 0:
            t_lo = jnp.where(lo, blk, 0.0)
            t_hi = pltpu.roll(t_lo, HEAD, axis=1)
        else:
            t_hi = jnp.where(lo, 0.0, blk)
            t_lo = pltpu.roll(t_hi, HEAD, axis=1)
        return t_lo.astype(BF16), t_hi.astype(BF16)

    def probs(qp, kk, head):
        s = _dot_nt(qp, kk)
        if banded:
            s = jnp.where(visible, s, NEG_INF)
        sink = sink_ref[head]
        m = jnp.maximum(jnp.max(s, axis=-1, keepdims=True), sink)
        e = jnp.exp(s - m)
        return (e / (jnp.sum(e, axis=-1, keepdims=True) + jnp.exp(sink - m))).astype(BF16)

    for g in range(N_KV):
        k_lo, k_hi = halves(k_all, g)
        v_lo, v_hi = halves(v_all, g)
        for j in range(2):
            pair = 2 * g + j
            qp = q_ref[:, pair * LANES:(pair + 1) * LANES]
            o = _dot(probs(qp, k_lo, 2 * pair), v_lo) + _dot(probs(qp, k_hi, 2 * pair + 1), v_hi)
            o_ref[:, pair * LANES:(pair + 1) * LANES] = o.astype(o_ref.dtype)


def _attention(q, kv_arrays, kv_maps, sinks, nb, s, banded):
    n = nb * s
    nc = s // CHUNK
    q_spec = pl.BlockSpec((CHUNK, D_MODEL), lambda b, c: (b * nc + c, 0))
    kv_specs = [pl.BlockSpec((CHUNK, KV_W), m) for m in kv_maps]
    return pl.pallas_call(
        functools.partial(_attn_kernel, banded=banded),
        grid=(nb, nc),
        in_specs=[pl.BlockSpec(memory_space=pltpu.SMEM), q_spec] + kv_specs,
        out_specs=q_spec,
        out_shape=jax.ShapeDtypeStruct((n, D_MODEL), BF16),
        compiler_params=_params(("parallel", "parallel")),
        name="attention",
    )(sinks, q, *kv_arrays)


def _pad_cols(w, n):
    return jnp.pad(w, ((0, 0), (0, n - w.shape[1])))


def _pad_rows(w, n):
    return jnp.pad(w, ((0, n - w.shape[0]), (0, 0)))


def _rope_tables(pos):
    half = ROPE_DIM // 2
    inv = jnp.power(jnp.float32(ROPE_THETA), -jnp.arange(half, dtype=F32) * (2.0 / ROPE_DIM))
    ang = pos[:, None] * inv[None, :]
    cos, sin = jnp.cos(ang), jnp.sin(ang)
    rest = HEAD - ROPE_DIM
    ones = jnp.ones((pos.shape[0], rest), F32)
    zeros = jnp.zeros((pos.shape[0], rest), F32)
    z8 = jnp.zeros_like(sin)
    per_head = (jnp.concatenate([cos, cos, ones], axis=1),
                jnp.concatenate([-sin, z8, zeros], axis=1),
                jnp.concatenate([z8, sin, zeros], axis=1))
    return tuple(jnp.tile(t, (1, LANES // HEAD)) for t in per_head)


def _state_to_pairs(state):
    nb = state.shape[0]
    st = jnp.swapaxes(state.astype(F32), -1, -2).reshape(nb, PAIRS, 2, HEAD, HEAD)
    z = jnp.zeros_like(st[:, :, 0])
    top = jnp.concatenate([st[:, :, 0], z], axis=-1)
    bot = jnp.concatenate([z, st[:, :, 1]], axis=-1)
    return jnp.concatenate([top, bot], axis=-2)


def _pairs_to_state(st):
    nb = st.shape[0]
    s0 = st[:, :, :HEAD, :HEAD]
    s1 = st[:, :, HEAD:, HEAD:]
    out = jnp.stack([s0, s1], axis=2).reshape(nb, 2 * PAIRS, HEAD, HEAD)
    return jnp.swapaxes(out, -1, -2)


def _trunk(x, mods, kvmod, pos, prev_x, prev_wkv, past_k, past_v, w, nb, s):
    n = nb * s
    x = x.reshape(n, D_MODEL)
    mod0, mod1 = mods[0][:, None, :], mods[1][:, None, :]
    kvmod = kvmod[:, None, :]

    tm = min(s, 128)
    r, lw, k, v, a, b, g, last_x = _rwkv_proj(x, mod0, prev_x[:, None, :], w["rw"], nb, s, tm)
    z, st = _wkv(r, lw, k, v, a, b, g, _state_to_pairs(prev_wkv), w["rw"], nb, s)
    tm = min(n, 256)
    x = _out_proj(x, z, mod0[:, :, 2 * D_MODEL:3 * D_MODEL], w["rw"]["w_o"], w["rw"]["b_o"], nb, s, tm)
    tm_moe = min(n, 1024)
    x = _moe(x, mod0, w["norm_g"][0][1:2], w["moe_route"][0], w["moe_gu"][0], w["moe_down"][0],
             w["final_g"], nb, s, tm_moe, False)

    tm = min(s, 256)
    q, k_new, v_new = _qkv_proj(x, mod1, kvmod, w["at"], _rope_tables(pos), nb, s, tm)
    nc = s // CHUNK
    if past_k is None:
        arrays = [k_new] * 3 + [v_new] * 3
        maps = [lambda bi, c, d=d: (bi * nc + jnp.maximum(c - d, 0), 0) for d in (2, 1, 0)] * 2
    else:
        pk = past_k.astype(F32).reshape(nb * 2 * CHUNK, KV_W)
        pv = past_v.astype(F32).reshape(nb * 2 * CHUNK, KV_W)
        arrays = [pk, pk, k_new, pv, pv, v_new]
        maps = [lambda bi, c: (2 * bi, 0), lambda bi, c: (2 * bi + 1, 0), lambda bi, c: (bi, 0)] * 2
    o = _attention(q, arrays, maps, w["at"]["sinks"], nb, s, past_k is None)
    tm = min(n, 256)
    x = _out_proj(x, o, mod1[:, :, 2 * D_MODEL:3 * D_MODEL], w["at"]["w_o"], w["at"]["b_o"], nb, s, tm)
    y = _moe(x, mod1, w["norm_g"][1][1:2], w["moe_route"][1], w["moe_gu"][1], w["moe_down"][1],
             w["final_g"], nb, s, tm_moe, True)

    return (y.reshape(nb, s, D_MODEL), last_x.reshape(1, nb, D_MODEL), _pairs_to_state(st)[None],
            k_new.reshape(nb, s, N_KV, HEAD), v_new.reshape(nb, s, N_KV, HEAD))


def kernel(x_prompt, x_sample, state_shift, state_wkv, cache_k, cache_v, c_prompt, c_sample, ada_w, ada_b, norm_g, rw_mu, rw_w_rkv, rw_w0, rw_w1, rw_w2, rw_a0, rw_a1, rw_a2, rw_g1, rw_g2, rw_k_k, rw_k_a, rw_r_k, rw_lnx_w, rw_lnx_b, rw_w_o, kv_ada_w, kv_ada_b, kv_norm_g, w_kv, b_kv, at_w_q, at_b_q, at_sinks, at_w_o, at_b_o, moe_w_group, moe_w_expert, moe_w_gu, moe_w_down, final_norm_g):
    bp, sp, _ = x_prompt.shape
    bs, ss, _ = x_sample.shape
    row = lambda t: t.reshape(1, -1).astype(F32)

    c_all = jnp.concatenate([c_prompt, c_sample], axis=0)
    mods = _cond_linear(c_all, ada_w, ada_b[:, None, :])
    kvmods = _cond_linear(c_all, kv_ada_w[None], kv_ada_b[None, None, :])[0]

    lora_pad, gate_pad = LANES, 2 * LANES
    w = {
        "norm_g": norm_g,
        "final_g": row(final_norm_g),
        "rw": {
            "norm_g": norm_g[0, 0:1], "mu": rw_mu[0], "w_rkv": rw_w_rkv[0].astype(BF16),
            "w0": row(rw_w0[0]), "w1": _pad_cols(rw_w1[0], lora_pad).astype(BF16),
            "w2": _pad_rows(rw_w2[0], lora_pad).astype(BF16),
            "a0": row(rw_a0[0]), "a1": _pad_cols(rw_a1[0], lora_pad).astype(BF16),
            "a2": _pad_rows(rw_a2[0], lora_pad).astype(BF16),
            "g1": _pad_cols(rw_g1[0], gate_pad).astype(BF16), "g2": _pad_rows(rw_g2[0], gate_pad).astype(BF16),
            "k_k": row(rw_k_k[0]), "k_a": row(rw_k_a[0]), "r_k": row(rw_r_k[0]),
            "lnx_w": row(rw_lnx_w[0]), "lnx_b": row(rw_lnx_b[0]),
            "w_o": rw_w_o[0].astype(BF16), "b_o": jnp.zeros((1, D_MODEL), F32),
        },
        "at": {
            "norm_gq": norm_g[1, 0:1], "norm_gkv": row(kv_norm_g),
            "w_q": at_w_q[0].astype(BF16), "b_q": row(at_b_q[0]),
            "w_kv": w_kv.astype(BF16), "b_kv": row(b_kv),
            "sinks": at_sinks[0].astype(F32),
            "w_o": at_w_o[0].astype(BF16), "b_o": row(at_b_o[0]),
        },
        "moe_route": jnp.pad(jnp.concatenate([moe_w_group, moe_w_expert], axis=-1),
                             ((0, 0), (0, 0), (0, LANES - N_GROUPS - N_EXPERTS))),
        "moe_gu": moe_w_gu.astype(BF16),
        "moe_down": moe_w_down.astype(BF16),
    }

    pos_p = jnp.arange(sp, dtype=F32)
    pos_s = PAST_LEN + jnp.arange(ss, dtype=F32)
    zero_x = jnp.zeros((bp, D_MODEL), x_prompt.dtype)
    zero_wkv = jnp.zeros((bp,) + state_wkv.shape[2:], state_wkv.dtype)
    y_p, p_shift, p_wkv, p_k, p_v = _trunk(x_prompt, mods[:, :bp], kvmods[:bp], pos_p, zero_x, zero_wkv,
                                           None, None, w, bp, sp)
    y_s, s_shift, s_wkv, s_k, s_v = _trunk(x_sample, mods[:, bp:], kvmods[bp:], pos_s, state_shift[0],
                                           state_wkv[0], cache_k, cache_v, w, bs, ss)
    keep = min(2 * CHUNK, sp)
    return (y_p, y_s, p_shift, p_wkv.astype(state_wkv.dtype), p_k[:, sp - keep:], p_v[:, sp - keep:],
            s_shift, s_wkv.astype(state_wkv.dtype), s_k, s_v)
```

```python
import functools

import jax
import jax.numpy as jnp
from jax import lax
from jax.experimental import pallas as pl
from jax.experimental.pallas import tpu as pltpu

F32 = jnp.float32
BF16 = jnp.bfloat16

D_MODEL = 1024
LANES = 128
HEAD = 64
PAIRS = D_MODEL // LANES
CHUNK = 64
PAST_LEN = 4096
N_KV = 4
KV_W = N_KV * HEAD
ROPE_DIM = 16
ROPE_THETA = 500000.0
ATT_SCALE = HEAD ** -0.5
N_GROUPS = 4
EXP_PER_GROUP = 8
N_EXPERTS = N_GROUPS * EXP_PER_GROUP
D_EXPERT = D_MODEL // 4
ROUTE_LANE0 = N_GROUPS
RMS_EPS = 1e-6
GN_EPS = 64e-5
NEG_INF = -1e30
VMEM_LIMIT = 56 * 1024 * 1024


def _params(sem):
    return pltpu.CompilerParams(dimension_semantics=sem, vmem_limit_bytes=VMEM_LIMIT)


def _dot(a, b):
    return jnp.dot(a, b, preferred_element_type=F32)


def _dot_nt(a, b):
    return lax.dot_general(a, b, (((1,), (1,)), ((), ())), preferred_element_type=F32)


def _dot_tn(a, b):
    return lax.dot_general(a, b, (((0,), (0,)), ((), ())), preferred_element_type=F32)


def _split2(x):
    hi = x.astype(BF16)
    lo = (x - hi.astype(F32)).astype(BF16)
    return hi, lo


def _dot_x3(a, b, dot=_dot):
    ah, al = _split2(a)
    bh, bl = _split2(b)
    return dot(ah, bh) + dot(ah, bl) + dot(al, bh)


def _dot_lhs_x2(a, m):
    ah, al = _split2(a)
    return _dot(ah, m) + _dot(al, m)


def _sigmoid(x):
    return 1.0 / (1.0 + jnp.exp(-x))


def _rms(x):
    return x * lax.rsqrt(jnp.mean(x * x, axis=-1, keepdims=True) + RMS_EPS)


def _modulate(n, gain, shift, scale, nb):
    rows = n.shape[0]
    h = (n * gain).reshape(nb, rows // nb, D_MODEL)
    return (h * (1.0 + scale) + shift).reshape(rows, D_MODEL)


def _seg_matrix(scale):
    r = lax.broadcasted_iota(jnp.int32, (LANES, LANES), 0)
    c = lax.broadcasted_iota(jnp.int32, (LANES, LANES), 1)
    return jnp.where((r < HEAD) == (c < HEAD), scale, 0.0).astype(BF16)


def _const_spec(shape):
    nd = len(shape)
    return pl.BlockSpec(shape, lambda *_: (0,) * nd)


def _cond_kernel(c_ref, w_ref, b_ref, o_ref):
    c = c_ref[...]
    cs = (c * _sigmoid(c)).astype(BF16)
    o_ref[...] = _dot(cs, w_ref[...].astype(BF16)) + b_ref[...]


def _cond_linear(c, w, b, tn=512):
    nl, _, n = w.shape
    m = c.shape[0]
    return pl.pallas_call(
        _cond_kernel,
        grid=(nl, n // tn),
        in_specs=[
            pl.BlockSpec((m, D_MODEL), lambda l, j: (0, 0)),
            pl.BlockSpec((None, D_MODEL, tn), lambda l, j: (l, 0, j)),
            pl.BlockSpec((None, 1, tn), lambda l, j: (l, 0, j)),
        ],
        out_specs=pl.BlockSpec((None, m, tn), lambda l, j: (l, 0, j)),
        out_shape=jax.ShapeDtypeStruct((nl, m, n), F32),
        compiler_params=_params(("parallel", "parallel")),
        name="cond_linear",
    )(c, w, b)


def _rwkv_proj_kernel(x_ref, mod_ref, prev_ref, ng_ref, mu_ref, wrkv_ref, w0_ref, w1_ref, w2_ref,
                      a0_ref, a1_ref, a2_ref, g1_ref, g2_ref, kk_ref, ka_ref,
                      r_ref, lw_ref, k_ref, v_ref, a_ref, b_ref, g_ref, last_ref, carry_ref):
    tm = x_ref.shape[0]

    @pl.when(pl.program_id(1) == 0)
    def _():
        carry_ref[...] = prev_ref[...]

    shift = mod_ref[:, 0:D_MODEL].reshape(1, 1, D_MODEL)
    scale = mod_ref[:, D_MODEL:2 * D_MODEL].reshape(1, 1, D_MODEL)
    h = _modulate(_rms(x_ref[...]), ng_ref[...], shift, scale, 1)
    row = lax.broadcasted_iota(jnp.int32, (tm, 1), 0)
    shifted = jnp.where(row == 0, carry_ref[...], pltpu.roll(h, 1, axis=0))
    carry_ref[...] = h[tm - 1:tm, :]
    last_ref[...] = h[tm - 1:tm, :]
    dx = shifted - h

    def mix(n):
        return (h + dx * mu_ref[n:n + 1, :]).astype(BF16)

    r = _dot(mix(0), wrkv_ref[0])
    k = _dot(mix(1), wrkv_ref[1])
    v = _dot(mix(2), wrkv_ref[2])
    ww = _dot(jnp.tanh(_dot(mix(3), w1_ref[...])).astype(BF16), w2_ref[...])
    z = -(w0_ref[...] + ww)
    softplus = jnp.maximum(z, 0.0) + jnp.log(1.0 + jnp.exp(-jnp.abs(z)))
    lw = -jnp.exp(-softplus - 0.5)
    asig = _sigmoid(a0_ref[...] + _dot(_dot(mix(4), a1_ref[...]).astype(BF16), a2_ref[...]))
    g = _dot(_sigmoid(_dot(mix(5), g1_ref[...])).astype(BF16), g2_ref[...])

    kk = k * kk_ref[...]
    kk2 = kk * kk
    seg = _seg_matrix(1.0)
    ss = jnp.concatenate(
        [_dot_lhs_x2(kk2[:, p * LANES:(p + 1) * LANES], seg) for p in range(PAIRS)], axis=1)
    kkn = kk * lax.rsqrt(jnp.maximum(ss, 1e-24))

    r_ref[...] = r
    lw_ref[...] = lw
    k_ref[...] = k * (1.0 + (asig - 1.0) * ka_ref[...])
    v_ref[...] = v
    a_ref[...] = -kkn
    b_ref[...] = kkn * asig
    g_ref[...] = g


def _rwkv_proj(x, mod, prev_x, w, nb, s, tm):
    n = nb * s
    nt = s // tm
    row_spec = pl.BlockSpec((tm, D_MODEL), lambda b, j: (b * nt + j, 0))
    vec = _const_spec((1, D_MODEL))
    lora_w, lora_g = w["w1"].shape[1], w["g1"].shape[1]
    in_specs = [
        row_spec,
        pl.BlockSpec((None, 1, 6 * D_MODEL), lambda b, j: (b, 0, 0)),
        pl.BlockSpec((None, 1, D_MODEL), lambda b, j: (b, 0, 0)),
        vec, _const_spec((6, D_MODEL)), _const_spec((3, D_MODEL, D_MODEL)),
        vec, _const_spec((D_MODEL, lora_w)), _const_spec((lora_w, D_MODEL)),
        vec, _const_spec((D_MODEL, lora_w)), _const_spec((lora_w, D_MODEL)),
        _const_spec((D_MODEL, lora_g)), _const_spec((lora_g, D_MODEL)),
        vec, vec,
    ]
    act = jax.ShapeDtypeStruct((n, D_MODEL), F32)
    outs = pl.pallas_call(
        _rwkv_proj_kernel,
        grid=(nb, nt),
        in_specs=in_specs,
        out_specs=[row_spec] * 7 + [pl.BlockSpec((None, 1, D_MODEL), lambda b, j: (b, 0, 0))],
        out_shape=[act] * 7 + [jax.ShapeDtypeStruct((nb, 1, D_MODEL), F32)],
        scratch_shapes=[pltpu.VMEM((1, D_MODEL), F32)],
        compiler_params=_params(("parallel", "arbitrary")),
        name="rwkv_proj",
    )(x, mod, prev_x, w["norm_g"], w["mu"], w["w_rkv"], w["w0"], w["w1"], w["w2"],
      w["a0"], w["a1"], w["a2"], w["g1"], w["g2"], w["k_k"], w["k_a"])
    return outs


def _split3(x):
    hi = x.astype(BF16)
    rem = x - hi.astype(F32)
    mid = rem.astype(BF16)
    return hi, mid, (rem - mid.astype(F32)).astype(BF16)


def _wkv_kernel(r_ref, lw_ref, k_ref, v_ref, a_ref, b_ref, g_ref, s0_ref, rk_ref, lnw_ref, lnb_ref,
                z_ref, st_ref):
    @pl.when(pl.program_id(1) == 0)
    def _():
        st_ref[...] = s0_ref[...]

    b16 = lambda t: t.astype(BF16)
    pairs = range(PAIRS)
    blk = lambda t, p: t[:, p * LANES:(p + 1) * LANES]
    rows = lambda ts: jnp.concatenate(ts, axis=0)
    lo_cols = lax.broadcasted_iota(jnp.int32, (CHUNK, LANES), 1) < HEAD
    lo_cols2 = lax.broadcasted_iota(jnp.int32, (2 * CHUNK, LANES), 1) < HEAD
    rr = lax.broadcasted_iota(jnp.int32, (LANES, LANES), 0)
    cc = lax.broadcasted_iota(jnp.int32, (LANES, LANES), 1)
    head0 = lambda t: jnp.where(lo_cols if t.shape[0] == CHUNK else lo_cols2, t, 0.0)
    head1 = lambda t: jnp.where(lo_cols if t.shape[0] == CHUNK else lo_cols2, 0.0, t)
    blockdiag = lambda q: rows([head0(q), head1(q)])

    r, lw, k, v, a, b = (t[...] for t in (r_ref, lw_ref, k_ref, v_ref, a_ref, b_ref))
    ti = lax.broadcasted_iota(jnp.int32, (CHUNK, CHUNK), 0)
    tj = lax.broadcasted_iota(jnp.int32, (CHUNK, CHUNK), 1)
    tri = (tj <= ti).astype(BF16)
    cum = sum(_dot(tri, t) for t in _split3(lw))
    cl = cum[CHUNK - 1:CHUNK, :]
    e_neg = jnp.exp(-cum)
    e_end = jnp.exp(cl - cum)
    e_cl = jnp.exp(cl)
    at = b16(a * jnp.exp(cum - lw))
    rt = b16(r * jnp.exp(cum))
    bt = b * e_neg
    kt = k * e_neg
    bkh = rows([b16(b * e_end), b16(k * e_end)])
    v16 = b16(v)

    t_idx = rr & (CHUNK - 1)
    j_idx = cc & (CHUNK - 1)
    keep = (j_idx < t_idx) | ((rr >= CHUNK) & (j_idx == t_idx))
    g0, g1 = [], []
    for p in pairs:
        btp, ktp = blk(bt, p), blk(kt, p)
        w = b16(rows([head0(btp), head0(ktp), head1(ktp), head1(btp)]))
        g = _dot_nt(rows([blk(at, p), blk(rt, p)]), w)
        g0.append(jnp.where(keep, g[:, :LANES], 0.0))
        g1.append(jnp.where(keep, g[:, LANES:], 0.0))

    eye2 = jnp.where((lax.broadcasted_iota(jnp.int32, (CHUNK, LANES), 1) & (CHUNK - 1))
                     == lax.broadcasted_iota(jnp.int32, (CHUNK, LANES), 0), 1.0, 0.0)
    pc = [jnp.where(lo_cols, g0[p][:CHUNK], g1[p][:CHUNK]) for p in pairs]
    tc = [eye2 + pc[p] for p in pairs]
    q = [_dot(b16(pc[p]), b16(blockdiag(pc[p]))) for p in pairs]
    for _ in range(4):
        res = [_dot(b16(rows([tc[p], q[p]])), b16(blockdiag(q[p]))) for p in pairs]
        tc = [tc[p] + res[p][:CHUNK] for p in pairs]
        q = [res[p][CHUNK:] for p in pairs]
    tc = [b16(tc[p] + _dot(b16(tc[p]), b16(blockdiag(q[p])))) for p in pairs]

    akv = []
    for p in pairs:
        w_ak = b16(jnp.where(lo_cols, g1[p][:CHUNK], g0[p][:CHUNK]))
        vp = blk(v16, p)
        akv.append(_dot(w_ak, rows([head1(vp), head0(vp)])))

    s = [st_ref[p] for p in pairs]
    s16 = [b16(s[p]) for p in pairs]
    rhs = [_dot_nt(blk(at, p), s16[p]) + akv[p] for p in pairs]
    u = [_dot(tc[p], b16(rows([head0(rhs[p]), head1(rhs[p])]))) for p in pairs]
    u16 = [b16(u[p]) for p in pairs]

    ys = []
    bd_mask = (rr < HEAD) == (cc < HEAD)
    for p in pairs:
        vp = blk(v16, p)
        rbk = b16(jnp.concatenate([g0[p][CHUNK:], g1[p][CHUNK:]], axis=1))
        uv = rows([head0(u16[p]), head0(vp), head1(vp), head1(u16[p])])
        ys.append(_dot_nt(blk(rt, p), s16[p]) + _dot(rbk, uv))
        fresh = _dot_tn(rows([u16[p], vp]), blk(bkh, p))
        st_ref[p] = s[p] * blk(e_cl, p) + jnp.where(bd_mask, fresh, 0.0)

    avg = _seg_matrix(1.0 / HEAD)
    y = rows(ys)
    yc = y - _dot_lhs_x2(y, avg)
    var = _dot(b16(yc * yc), avg)
    yn = yc * lax.rsqrt(var + GN_EPS)
    rkk = r * k * rk_ref[...]
    bonus = _dot_lhs_x2(rows([blk(rkk, p) for p in pairs]), _seg_matrix(1.0))
    for p in pairs:
        cols = slice(p * LANES, (p + 1) * LANES)
        ynp = yn[p * CHUNK:(p + 1) * CHUNK] * lnw_ref[:, cols] + lnb_ref[:, cols]
        zp = (ynp + bonus[p * CHUNK:(p + 1) * CHUNK] * v[:, cols]) * g_ref[:, cols]
        z_ref[:, cols] = zp.astype(z_ref.dtype)


def _wkv(r, lw, k, v, a, b, g, st0, w, nb, s):
    n = nb * s
    nc = s // CHUNK
    blk = pl.BlockSpec((CHUNK, D_MODEL), lambda bi, c: (bi * nc + c, 0))
    vec = _const_spec((1, D_MODEL))
    st_spec = pl.BlockSpec((None, PAIRS, LANES, LANES), lambda bi, c: (bi, 0, 0, 0))
    return pl.pallas_call(
        _wkv_kernel,
        grid=(nb, nc),
        in_specs=[blk] * 7 + [st_spec, vec, vec, vec],
        out_specs=[blk, st_spec],
        out_shape=[jax.ShapeDtypeStruct((n, D_MODEL), BF16),
                   jax.ShapeDtypeStruct((nb, PAIRS, LANES, LANES), F32)],
        compiler_params=_params(("parallel", "arbitrary")),
        name="wkv",
    )(r, lw, k, v, a, b, g, st0, w["r_k"], w["lnx_w"], w["lnx_b"])


def _out_proj_kernel(x_ref, z_ref, gate_ref, w_ref, bias_ref, o_ref, *, nb):
    rows = x_ref.shape[0]
    y = _dot(z_ref[...], w_ref[...]) + bias_ref[...]
    y = (y.reshape(nb, rows // nb, D_MODEL) * gate_ref[...]).reshape(rows, D_MODEL)
    o_ref[...] = x_ref[...] + y


def _out_proj(x, z, gate, w, bias, nb, s, tm):
    n = nb * s
    tb = max(tm // s, 1)
    per_seq = max(s // tm, 1)
    row_spec = pl.BlockSpec((tm, D_MODEL), lambda i: (i, 0))
    return pl.pallas_call(
        functools.partial(_out_proj_kernel, nb=tb),
        grid=(n // tm,),
        in_specs=[row_spec, row_spec,
                  pl.BlockSpec((tb, 1, D_MODEL), lambda i: (i // per_seq, 0, 0)),
                  _const_spec((D_MODEL, D_MODEL)), _const_spec((1, D_MODEL))],
        out_specs=row_spec,
        out_shape=jax.ShapeDtypeStruct((n, D_MODEL), F32),
        compiler_params=_params(("parallel",)),
        name="out_proj",
    )(x, z, gate, w, bias)


def _route(logits):
    lane = lax.broadcasted_iota(jnp.int32, logits.shape, 1)
    big = jnp.int32(1 << 20)
    lg = jnp.where(lane < N_GROUPS, logits, -jnp.inf)
    gmax = jnp.max(lg, axis=-1, keepdims=True)
    gi = jnp.min(jnp.where(lg == gmax, lane, big), axis=-1, keepdims=True)
    gp = 1.0 / jnp.sum(jnp.exp(lg - gmax), axis=-1, keepdims=True)
    first = ROUTE_LANE0 + gi * EXP_PER_GROUP
    le = jnp.where((lane >= first) & (lane < first + EXP_PER_GROUP), logits, -jnp.inf)
    top1 = jnp.max(le, axis=-1, keepdims=True)
    i1 = jnp.min(jnp.where(le == top1, lane, big), axis=-1, keepdims=True)
    le2 = jnp.where(lane == i1, -jnp.inf, le)
    top2 = jnp.max(le2, axis=-1, keepdims=True)
    i2 = jnp.min(jnp.where(le2 == top2, lane, big), axis=-1, keepdims=True)
    e2 = jnp.exp(top2 - top1)
    w1 = gp / (1.0 + e2)
    return jnp.where(lane == i1, w1, 0.0) + jnp.where(lane == i2, w1 * e2, 0.0)


def _moe_kernel(x_ref, mod_ref, ng_ref, wr_ref, wgu_ref, wdn_ref, fg_ref, o_ref, h_scr, cw_scr, acc_scr,
                *, nb, final_norm):
    e = pl.program_id(1)
    rows = x_ref.shape[0]

    @pl.when(e == 0)
    def _():
        shift = mod_ref[:, :, 3 * D_MODEL:4 * D_MODEL]
        scale = mod_ref[:, :, 4 * D_MODEL:5 * D_MODEL]
        h = _modulate(_rms(x_ref[...]), ng_ref[...], shift, scale, nb)
        h_scr[...] = h.astype(BF16)
        cw_scr[...] = _route(_dot_x3(h, wr_ref[...]))
        acc_scr[...] = jnp.zeros_like(acc_scr)

    lane = lax.broadcasted_iota(jnp.int32, (rows, LANES), 1)
    cw = jnp.sum(jnp.where(lane == ROUTE_LANE0 + e, cw_scr[...], 0.0), axis=-1, keepdims=True)
    gu = _dot(h_scr[...], wgu_ref[...])
    gate = gu[:, :D_EXPERT]
    hid = gate * _sigmoid(gate) * gu[:, D_EXPERT:] * cw
    acc_scr[...] += _dot(hid.astype(BF16), wdn_ref[...])

    @pl.when(e == N_EXPERTS - 1)
    def _():
        g2 = mod_ref[:, :, 5 * D_MODEL:6 * D_MODEL]
        y = (acc_scr[...].reshape(nb, rows // nb, D_MODEL) * g2).reshape(rows, D_MODEL)
        out = x_ref[...] + y
        if final_norm:
            out = _rms(out) * fg_ref[...]
        o_ref[...] = out


def _moe(x, mod, norm_g, w_route, w_gu, w_down, final_g, nb, s, tm, final_norm):
    n = nb * s
    tb = max(tm // s, 1)
    per_seq = max(s // tm, 1)
    row_spec = pl.BlockSpec((tm, D_MODEL), lambda i, e: (i, 0))
    return pl.pallas_call(
        functools.partial(_moe_kernel, nb=tb, final_norm=final_norm),
        grid=(n // tm, N_EXPERTS),
        in_specs=[row_spec,
                  pl.BlockSpec((tb, 1, 6 * D_MODEL), lambda i, e: (i // per_seq, 0, 0)),
                  _const_spec((1, D_MODEL)), _const_spec((D_MODEL, LANES)),
                  pl.BlockSpec((None, D_MODEL, 2 * D_EXPERT), lambda i, e: (e, 0, 0)),
                  pl.BlockSpec((None, D_EXPERT, D_MODEL), lambda i, e: (e, 0, 0)),
                  _const_spec((1, D_MODEL))],
        out_specs=row_spec,
        out_shape=jax.ShapeDtypeStruct((n, D_MODEL), F32),
        scratch_shapes=[pltpu.VMEM((tm, D_MODEL), BF16), pltpu.VMEM((tm, LANES), F32),
                        pltpu.VMEM((tm, D_MODEL), F32)],
        compiler_params=_params(("parallel", "arbitrary")),
        name="moe",
    )(x, mod, norm_g, w_route, w_gu, w_down, final_g)


def _rope(x, cos, sin_lo, sin_hi):
    return x * cos + pltpu.roll(x, LANES - ROPE_DIM // 2, axis=1) * sin_lo + pltpu.roll(x, ROPE_DIM // 2, axis=1) * sin_hi


def _qkv_kernel(x_ref, mod_ref, kvmod_ref, ngq_ref, ngkv_ref, wq_ref, bq_ref, wkv_ref, bkv_ref,
                cos_ref, slo_ref, shi_ref, q_ref, k_ref, v_ref):
    n = _rms(x_ref[...])
    shift = mod_ref[:, 0:D_MODEL].reshape(1, 1, D_MODEL)
    scale = mod_ref[:, D_MODEL:2 * D_MODEL].reshape(1, 1, D_MODEL)
    hq = _modulate(n, ngq_ref[...], shift, scale, 1)
    kshift = kvmod_ref[:, 0:D_MODEL].reshape(1, 1, D_MODEL)
    kscale = kvmod_ref[:, D_MODEL:2 * D_MODEL].reshape(1, 1, D_MODEL)
    hkv = _modulate(n, ngkv_ref[...], kshift, kscale, 1)
    cos, slo, shi = cos_ref[...], slo_ref[...], shi_ref[...]
    q = _dot(hq.astype(BF16), wq_ref[...]) + bq_ref[...]
    for p in range(PAIRS):
        cols = slice(p * LANES, (p + 1) * LANES)
        q_ref[:, cols] = (_rope(q[:, cols], cos, slo, shi) * ATT_SCALE).astype(q_ref.dtype)
    kv = _dot(hkv.astype(BF16), wkv_ref[...]) + bkv_ref[...]
    for p in range(KV_W // LANES):
        cols = slice(p * LANES, (p + 1) * LANES)
        k_ref[:, cols] = _rope(kv[:, cols], cos, slo, shi)
    v_ref[...] = kv[:, KV_W:]


def _qkv_proj(x, mod, kvmod, w, tables, nb, s, tm):
    n = nb * s
    nt = s // tm
    row_spec = pl.BlockSpec((tm, D_MODEL), lambda b, j: (b * nt + j, 0))
    kv_spec = pl.BlockSpec((tm, KV_W), lambda b, j: (b * nt + j, 0))
    tab_spec = pl.BlockSpec((tm, LANES), lambda b, j: (j, 0))
    vec = _const_spec((1, D_MODEL))
    return pl.pallas_call(
        _qkv_kernel,
        grid=(nb, nt),
        in_specs=[row_spec,
                  pl.BlockSpec((None, 1, 6 * D_MODEL), lambda b, j: (b, 0, 0)),
                  pl.BlockSpec((None, 1, 2 * D_MODEL), lambda b, j: (b, 0, 0)),
                  vec, vec, _const_spec((D_MODEL, D_MODEL)), vec,
                  _const_spec((D_MODEL, 2 * KV_W)), _const_spec((1, 2 * KV_W)),
                  tab_spec, tab_spec, tab_spec],
        out_specs=[row_spec, kv_spec, kv_spec],
        out_shape=[jax.ShapeDtypeStruct((n, D_MODEL), BF16),
                   jax.ShapeDtypeStruct((n, KV_W), F32), jax.ShapeDtypeStruct((n, KV_W), F32)],
        compiler_params=_params(("parallel", "parallel")),
        name="qkv_proj",
    )(x, mod, kvmod, w["norm_gq"], w["norm_gkv"], w["w_q"], w["b_q"], w["w_kv"], w["b_kv"], *tables)


def _attn_kernel(sink_ref, q_ref, k0_ref, k1_ref, k2_ref, v0_ref, v1_ref, v2_ref, o_ref, *, banded):
    n = pl.program_id(1)
    k_all = jnp.concatenate([k0_ref[...], k1_ref[...], k2_ref[...]], axis=0)
    v_all = jnp.concatenate([v0_ref[...], v1_ref[...], v2_ref[...]], axis=0)
    lane = lax.broadcasted_iota(jnp.int32, (1, LANES), 1)
    lo = lane < HEAD
    col = lax.broadcasted_iota(jnp.int32, (1, 3 * CHUNK), 1)
    if banded:
        visible = (col >= 2 * CHUNK) | ((col >= CHUNK) & (n >= 1)) | (n >= 2)

    def halves(t, kv_head):
        blk = t[:, (kv_head // 2) * LANES:(kv_head // 2 + 1) * LANES]
        if kv_head % 2 == 0:
            t_lo = jnp.where(lo, blk, 0.0)
            t_hi = pltpu.roll(t_lo, HEAD, axis=1)
        else:
            t_hi = jnp.where(lo, 0.0, blk)
            t_lo = pltpu.roll(t_hi, HEAD, axis=1)
        return t_lo.astype(BF16), t_hi.astype(BF16)

    first_pair = lax.broadcasted_iota(jnp.int32, (2 * CHUNK, 1), 0) < CHUNK

    def probs(s, head_a, head_b):
        if banded:
            s = jnp.where(visible, s, NEG_INF)
        sink = jnp.where(first_pair, sink_ref[head_a], sink_ref[head_b])
        m = jnp.maximum(jnp.max(s, axis=-1, keepdims=True), sink)
        e = jnp.exp(s - m)
        return (e / (jnp.sum(e, axis=-1, keepdims=True) + jnp.exp(sink - m))).astype(BF16)

    scores, values = [], []
    for g in range(N_KV):
        k_lo, k_hi = halves(k_all, g)
        values.append(halves(v_all, g))
        q2 = jnp.concatenate([q_ref[:, (2 * g + j) * LANES:(2 * g + j + 1) * LANES] for j in range(2)], axis=0)
        scores.append((_dot_nt(q2, k_lo), _dot_nt(q2, k_hi)))
    for g in range(N_KV):
        o = (_dot(probs(scores[g][0], 4 * g, 4 * g + 2), values[g][0])
             + _dot(probs(scores[g][1], 4 * g + 1, 4 * g + 3), values[g][1]))
        for j in range(2):
            pair = 2 * g + j
            o_ref[:, pair * LANES:(pair + 1) * LANES] = o[j * CHUNK:(j + 1) * CHUNK].astype(o_ref.dtype)


def _attention(q, kv_arrays, kv_maps, sinks, nb, s, banded):
    n = nb * s
    nc = s // CHUNK
    q_spec = pl.BlockSpec((CHUNK, D_MODEL), lambda b, c: (b * nc + c, 0))
    kv_specs = [pl.BlockSpec((CHUNK, KV_W), m) for m in kv_maps]
    return pl.pallas_call(
        functools.partial(_attn_kernel, banded=banded),
        grid=(nb, nc),
        in_specs=[pl.BlockSpec(memory_space=pltpu.SMEM), q_spec] + kv_specs,
        out_specs=q_spec,
        out_shape=jax.ShapeDtypeStruct((n, D_MODEL), BF16),
        compiler_params=_params(("parallel", "parallel")),
        name="attention",
    )(sinks, q, *kv_arrays)


def _pad_cols(w, n):
    return jnp.pad(w, ((0, 0), (0, n - w.shape[1])))


def _pad_rows(w, n):
    return jnp.pad(w, ((0, n - w.shape[0]), (0, 0)))


def _rope_tables(pos):
    half = ROPE_DIM // 2
    inv = jnp.power(jnp.float32(ROPE_THETA), -jnp.arange(half, dtype=F32) * (2.0 / ROPE_DIM))
    ang = pos[:, None] * inv[None, :]
    cos, sin = jnp.cos(ang), jnp.sin(ang)
    rest = HEAD - ROPE_DIM
    ones = jnp.ones((pos.shape[0], rest), F32)
    zeros = jnp.zeros((pos.shape[0], rest), F32)
    z8 = jnp.zeros_like(sin)
    per_head = (jnp.concatenate([cos, cos, ones], axis=1),
                jnp.concatenate([-sin, z8, zeros], axis=1),
                jnp.concatenate([z8, sin, zeros], axis=1))
    return tuple(jnp.tile(t, (1, LANES // HEAD)) for t in per_head)


def _state_to_pairs(state):
    nb = state.shape[0]
    st = state.astype(F32).reshape(nb, PAIRS, 2, HEAD, HEAD)
    z = jnp.zeros_like(st[:, :, 0])
    top = jnp.concatenate([st[:, :, 0], z], axis=-1)
    bot = jnp.concatenate([z, st[:, :, 1]], axis=-1)
    return jnp.concatenate([top, bot], axis=-2)


def _pairs_to_state(st):
    nb = st.shape[0]
    s0 = st[:, :, :HEAD, :HEAD]
    s1 = st[:, :, HEAD:, HEAD:]
    return jnp.stack([s0, s1], axis=2).reshape(nb, 2 * PAIRS, HEAD, HEAD)


def _trunk(x, mods, kvmod, pos, prev_x, prev_wkv, past_k, past_v, w, nb, s):
    n = nb * s
    x = x.reshape(n, D_MODEL)
    mod0, mod1 = mods[0][:, None, :], mods[1][:, None, :]
    kvmod = kvmod[:, None, :]

    tm = min(s, 256)
    r, lw, k, v, a, b, g, last_x = _rwkv_proj(x, mod0, prev_x[:, None, :], w["rw"], nb, s, tm)
    z, st = _wkv(r, lw, k, v, a, b, g, _state_to_pairs(prev_wkv), w["rw"], nb, s)
    tm = min(n, 256)
    x = _out_proj(x, z, mod0[:, :, 2 * D_MODEL:3 * D_MODEL], w["rw"]["w_o"], w["rw"]["b_o"], nb, s, tm)
    tm_moe = min(n, 1024)
    x = _moe(x, mod0, w["norm_g"][0][1:2], w["moe_route"][0], w["moe_gu"][0], w["moe_down"][0],
             w["final_g"], nb, s, tm_moe, False)

    tm = min(s, 256)
    q, k_new, v_new = _qkv_proj(x, mod1, kvmod, w["at"], _rope_tables(pos), nb, s, tm)
    nc = s // CHUNK
    if past_k is None:
        arrays = [k_new] * 3 + [v_new] * 3
        maps = [lambda bi, c, d=d: (bi * nc + jnp.maximum(c - d, 0), 0) for d in (2, 1, 0)] * 2
    else:
        pk = past_k.astype(F32).reshape(nb * 2 * CHUNK, KV_W)
        pv = past_v.astype(F32).reshape(nb * 2 * CHUNK, KV_W)
        arrays = [pk, pk, k_new, pv, pv, v_new]
        maps = [lambda bi, c: (2 * bi, 0), lambda bi, c: (2 * bi + 1, 0), lambda bi, c: (bi, 0)] * 2
    o = _attention(q, arrays, maps, w["at"]["sinks"], nb, s, past_k is None)
    tm = min(n, 256)
    x = _out_proj(x, o, mod1[:, :, 2 * D_MODEL:3 * D_MODEL], w["at"]["w_o"], w["at"]["b_o"], nb, s, tm)
    y = _moe(x, mod1, w["norm_g"][1][1:2], w["moe_route"][1], w["moe_gu"][1], w["moe_down"][1],
             w["final_g"], nb, s, tm_moe, True)

    return (y.reshape(nb, s, D_MODEL), last_x.reshape(1, nb, D_MODEL), _pairs_to_state(st)[None],
            k_new.reshape(nb, s, N_KV, HEAD), v_new.reshape(nb, s, N_KV, HEAD))


def kernel(x_prompt, x_sample, state_shift, state_wkv, cache_k, cache_v, c_prompt, c_sample, ada_w, ada_b, norm_g, rw_mu, rw_w_rkv, rw_w0, rw_w1, rw_w2, rw_a0, rw_a1, rw_a2, rw_g1, rw_g2, rw_k_k, rw_k_a, rw_r_k, rw_lnx_w, rw_lnx_b, rw_w_o, kv_ada_w, kv_ada_b, kv_norm_g, w_kv, b_kv, at_w_q, at_b_q, at_sinks, at_w_o, at_b_o, moe_w_group, moe_w_expert, moe_w_gu, moe_w_down, final_norm_g):
    bp, sp, _ = x_prompt.shape
    bs, ss, _ = x_sample.shape
    row = lambda t: t.reshape(1, -1).astype(F32)

    c_all = jnp.concatenate([c_prompt, c_sample], axis=0)
    mods = _cond_linear(c_all, ada_w, ada_b[:, None, :])
    kvmods = _cond_linear(c_all, kv_ada_w[None], kv_ada_b[None, None, :])[0]

    lora_pad, gate_pad = LANES, 2 * LANES
    w = {
        "norm_g": norm_g,
        "final_g": row(final_norm_g),
        "rw": {
            "norm_g": norm_g[0, 0:1], "mu": rw_mu[0], "w_rkv": rw_w_rkv[0].astype(BF16),
            "w0": row(rw_w0[0]), "w1": _pad_cols(rw_w1[0], lora_pad).astype(BF16),
            "w2": _pad_rows(rw_w2[0], lora_pad).astype(BF16),
            "a0": row(rw_a0[0]), "a1": _pad_cols(rw_a1[0], lora_pad).astype(BF16),
            "a2": _pad_rows(rw_a2[0], lora_pad).astype(BF16),
            "g1": _pad_cols(rw_g1[0], gate_pad).astype(BF16), "g2": _pad_rows(rw_g2[0], gate_pad).astype(BF16),
            "k_k": row(rw_k_k[0]), "k_a": row(rw_k_a[0]), "r_k": row(rw_r_k[0]),
            "lnx_w": row(rw_lnx_w[0]), "lnx_b": row(rw_lnx_b[0]),
            "w_o": rw_w_o[0].astype(BF16), "b_o": jnp.zeros((1, D_MODEL), F32),
        },
        "at": {
            "norm_gq": norm_g[1, 0:1], "norm_gkv": row(kv_norm_g),
            "w_q": at_w_q[0].astype(BF16), "b_q": row(at_b_q[0]),
            "w_kv": w_kv.astype(BF16), "b_kv": row(b_kv),
            "sinks": at_sinks[0].astype(F32),
            "w_o": at_w_o[0].astype(BF16), "b_o": row(at_b_o[0]),
        },
        "moe_route": jnp.pad(jnp.concatenate([moe_w_group, moe_w_expert], axis=-1),
                             ((0, 0), (0, 0), (0, LANES - N_GROUPS - N_EXPERTS))),
        "moe_gu": moe_w_gu.astype(BF16),
        "moe_down": moe_w_down.astype(BF16),
    }

    pos_p = jnp.arange(sp, dtype=F32)
    pos_s = PAST_LEN + jnp.arange(ss, dtype=F32)
    zero_x = jnp.zeros((bp, D_MODEL), x_prompt.dtype)
    zero_wkv = jnp.zeros((bp,) + state_wkv.shape[2:], state_wkv.dtype)
    y_p, p_shift, p_wkv, p_k, p_v = _trunk(x_prompt, mods[:, :bp], kvmods[:bp], pos_p, zero_x, zero_wkv,
                                           None, None, w, bp, sp)
    y_s, s_shift, s_wkv, s_k, s_v = _trunk(x_sample, mods[:, bp:], kvmods[bp:], pos_s, state_shift[0],
                                           state_wkv[0], cache_k, cache_v, w, bs, ss)
    keep = min(2 * CHUNK, sp)
    return (y_p, y_s, p_shift, p_wkv.astype(state_wkv.dtype), p_k[:, sp - keep:], p_v[:, sp - keep:],
            s_shift, s_wkv.astype(state_wkv.dtype), s_k, s_v)
```

```python
import functools

import jax
import jax.numpy as jnp
from jax import lax
from jax.experimental import pallas as pl
from jax.experimental.pallas import tpu as pltpu

F32 = jnp.float32
BF16 = jnp.bfloat16

D_MODEL = 1024
LANES = 128
HEAD = 64
PAIRS = D_MODEL // LANES
CHUNK = 64
PAST_LEN = 4096
N_KV = 4
KV_W = N_KV * HEAD
ROPE_DIM = 16
ROPE_THETA = 500000.0
ATT_SCALE = HEAD ** -0.5
N_GROUPS = 4
EXP_PER_GROUP = 8
N_EXPERTS = N_GROUPS * EXP_PER_GROUP
D_EXPERT = D_MODEL // 4
ROUTE_LANE0 = N_GROUPS
RMS_EPS = 1e-6
GN_EPS = 64e-5
NEG_INF = -1e30
VMEM_LIMIT = 56 * 1024 * 1024


def _params(sem):
    return pltpu.CompilerParams(dimension_semantics=sem, vmem_limit_bytes=VMEM_LIMIT)


def _dot(a, b):
    return jnp.dot(a, b, preferred_element_type=F32)


def _dot_nt(a, b):
    return lax.dot_general(a, b, (((1,), (1,)), ((), ())), preferred_element_type=F32)


def _dot_tn(a, b):
    return lax.dot_general(a, b, (((0,), (0,)), ((), ())), preferred_element_type=F32)


def _split2(x):
    hi = x.astype(BF16)
    lo = (x - hi.astype(F32)).astype(BF16)
    return hi, lo


def _dot_x3(a, b, dot=_dot):
    ah, al = _split2(a)
    bh, bl = _split2(b)
    return dot(ah, bh) + dot(ah, bl) + dot(al, bh)


def _dot_lhs_x2(a, m):
    ah, al = _split2(a)
    return _dot(ah, m) + _dot(al, m)


def _sigmoid(x):
    return 1.0 / (1.0 + jnp.exp(-x))


def _rms(x):
    return x * lax.rsqrt(jnp.mean(x * x, axis=-1, keepdims=True) + RMS_EPS)


def _modulate(n, gain, shift, scale, nb):
    rows = n.shape[0]
    h = (n * gain).reshape(nb, rows // nb, D_MODEL)
    return (h * (1.0 + scale) + shift).reshape(rows, D_MODEL)


def _seg_matrix(scale):
    r = lax.broadcasted_iota(jnp.int32, (LANES, LANES), 0)
    c = lax.broadcasted_iota(jnp.int32, (LANES, LANES), 1)
    return jnp.where((r < HEAD) == (c < HEAD), scale, 0.0).astype(BF16)


def _const_spec(shape):
    nd = len(shape)
    return pl.BlockSpec(shape, lambda *_: (0,) * nd)


def _cond_kernel(c_ref, w_ref, b_ref, o_ref):
    c = c_ref[...]
    cs = (c * _sigmoid(c)).astype(BF16)
    o_ref[...] = _dot(cs, w_ref[...].astype(BF16)) + b_ref[...]


def _cond_linear(c, w, b, tn=512):
    nl, _, n = w.shape
    m = c.shape[0]
    return pl.pallas_call(
        _cond_kernel,
        grid=(nl, n // tn),
        in_specs=[
            pl.BlockSpec((m, D_MODEL), lambda l, j: (0, 0)),
            pl.BlockSpec((None, D_MODEL, tn), lambda l, j: (l, 0, j)),
            pl.BlockSpec((None, 1, tn), lambda l, j: (l, 0, j)),
        ],
        out_specs=pl.BlockSpec((None, m, tn), lambda l, j: (l, 0, j)),
        out_shape=jax.ShapeDtypeStruct((nl, m, n), F32),
        compiler_params=_params(("parallel", "parallel")),
        name="cond_linear",
    )(c, w, b)


def _rwkv_proj_kernel(x_ref, mod_ref, prev_ref, ng_ref, mu_ref, wrkv_ref, w0_ref, w1_ref, w2_ref,
                      a0_ref, a1_ref, a2_ref, g1_ref, g2_ref, kk_ref, ka_ref,
                      r_ref, lw_ref, k_ref, v_ref, a_ref, b_ref, g_ref, last_ref, carry_ref):
    tm = x_ref.shape[0]

    @pl.when(pl.program_id(1) == 0)
    def _():
        carry_ref[...] = prev_ref[...]

    shift = mod_ref[:, 0:D_MODEL].reshape(1, 1, D_MODEL)
    scale = mod_ref[:, D_MODEL:2 * D_MODEL].reshape(1, 1, D_MODEL)
    h = _modulate(_rms(x_ref[...]), ng_ref[...], shift, scale, 1)
    row = lax.broadcasted_iota(jnp.int32, (tm, 1), 0)
    shifted = jnp.where(row == 0, carry_ref[...], pltpu.roll(h, 1, axis=0))
    carry_ref[...] = h[tm - 1:tm, :]
    last_ref[...] = h[tm - 1:tm, :]
    dx = shifted - h

    def mix(n):
        return (h + dx * mu_ref[n:n + 1, :]).astype(BF16)

    r = _dot(mix(0), wrkv_ref[0])
    k = _dot(mix(1), wrkv_ref[1])
    v = _dot(mix(2), wrkv_ref[2])
    ww = _dot(jnp.tanh(_dot(mix(3), w1_ref[...])).astype(BF16), w2_ref[...])
    z = -(w0_ref[...] + ww)
    softplus = jnp.maximum(z, 0.0) + jnp.log(1.0 + jnp.exp(-jnp.abs(z)))
    lw = -jnp.exp(-softplus - 0.5)
    asig = _sigmoid(a0_ref[...] + _dot(_dot(mix(4), a1_ref[...]).astype(BF16), a2_ref[...]))
    g = _dot(_sigmoid(_dot(mix(5), g1_ref[...])).astype(BF16), g2_ref[...])

    kk = k * kk_ref[...]
    kk2 = kk * kk
    seg = _seg_matrix(1.0)
    ss = jnp.concatenate(
        [_dot_lhs_x2(kk2[:, p * LANES:(p + 1) * LANES], seg) for p in range(PAIRS)], axis=1)
    kkn = kk * lax.rsqrt(jnp.maximum(ss, 1e-24))

    r_ref[...] = r
    lw_ref[...] = lw
    k_ref[...] = k * (1.0 + (asig - 1.0) * ka_ref[...])
    v_ref[...] = v
    a_ref[...] = -kkn
    b_ref[...] = kkn * asig
    g_ref[...] = g


def _rwkv_proj(x, mod, prev_x, w, nb, s, tm):
    n = nb * s
    nt = s // tm
    row_spec = pl.BlockSpec((tm, D_MODEL), lambda b, j: (b * nt + j, 0))
    vec = _const_spec((1, D_MODEL))
    lora_w, lora_g = w["w1"].shape[1], w["g1"].shape[1]
    in_specs = [
        row_spec,
        pl.BlockSpec((None, 1, 6 * D_MODEL), lambda b, j: (b, 0, 0)),
        pl.BlockSpec((None, 1, D_MODEL), lambda b, j: (b, 0, 0)),
        vec, _const_spec((6, D_MODEL)), _const_spec((3, D_MODEL, D_MODEL)),
        vec, _const_spec((D_MODEL, lora_w)), _const_spec((lora_w, D_MODEL)),
        vec, _const_spec((D_MODEL, lora_w)), _const_spec((lora_w, D_MODEL)),
        _const_spec((D_MODEL, lora_g)), _const_spec((lora_g, D_MODEL)),
        vec, vec,
    ]
    act = jax.ShapeDtypeStruct((n, D_MODEL), F32)
    outs = pl.pallas_call(
        _rwkv_proj_kernel,
        grid=(nb, nt),
        in_specs=in_specs,
        out_specs=[row_spec] * 7 + [pl.BlockSpec((None, 1, D_MODEL), lambda b, j: (b, 0, 0))],
        out_shape=[act] * 7 + [jax.ShapeDtypeStruct((nb, 1, D_MODEL), F32)],
        scratch_shapes=[pltpu.VMEM((1, D_MODEL), F32)],
        compiler_params=_params(("parallel", "arbitrary")),
        name="rwkv_proj",
    )(x, mod, prev_x, w["norm_g"], w["mu"], w["w_rkv"], w["w0"], w["w1"], w["w2"],
      w["a0"], w["a1"], w["a2"], w["g1"], w["g2"], w["k_k"], w["k_a"])
    return outs


def _split3(x):
    hi = x.astype(BF16)
    rem = x - hi.astype(F32)
    mid = rem.astype(BF16)
    return hi, mid, (rem - mid.astype(F32)).astype(BF16)


def _wkv_kernel(r_ref, lw_ref, k_ref, v_ref, a_ref, b_ref, g_ref, s0_ref, rk_ref, lnw_ref, lnb_ref,
                z_ref, st_ref):
    @pl.when(pl.program_id(1) == 0)
    def _():
        st_ref[...] = s0_ref[...]

    b16 = lambda t: t.astype(BF16)
    pairs = range(PAIRS)
    blk = lambda t, p: t[:, p * LANES:(p + 1) * LANES]
    rows = lambda ts: jnp.concatenate(ts, axis=0)
    lo_cols = lax.broadcasted_iota(jnp.int32, (CHUNK, LANES), 1) < HEAD
    lo_cols2 = lax.broadcasted_iota(jnp.int32, (2 * CHUNK, LANES), 1) < HEAD
    rr = lax.broadcasted_iota(jnp.int32, (LANES, LANES), 0)
    cc = lax.broadcasted_iota(jnp.int32, (LANES, LANES), 1)
    head0 = lambda t: jnp.where(lo_cols if t.shape[0] == CHUNK else lo_cols2, t, 0.0)
    head1 = lambda t: jnp.where(lo_cols if t.shape[0] == CHUNK else lo_cols2, 0.0, t)
    blockdiag = lambda q: rows([head0(q), head1(q)])

    r, lw, k, v, a, b = (t[...] for t in (r_ref, lw_ref, k_ref, v_ref, a_ref, b_ref))
    ti = lax.broadcasted_iota(jnp.int32, (CHUNK, CHUNK), 0)
    tj = lax.broadcasted_iota(jnp.int32, (CHUNK, CHUNK), 1)
    tri = (tj <= ti).astype(BF16)
    cum = sum(_dot(tri, t) for t in _split3(lw))
    cl = cum[CHUNK - 1:CHUNK, :]
    e_neg = jnp.exp(-cum)
    e_end = jnp.exp(cl - cum)
    e_cl = jnp.exp(cl)
    at = b16(a * jnp.exp(cum - lw))
    rt = b16(r * jnp.exp(cum))
    bt = b * e_neg
    kt = k * e_neg
    bkh = rows([b16(b * e_end), b16(k * e_end)])
    v16 = b16(v)

    t_idx = rr & (CHUNK - 1)
    j_idx = cc & (CHUNK - 1)
    keep = (j_idx < t_idx) | ((rr >= CHUNK) & (j_idx == t_idx))
    g0, g1 = [], []
    for p in pairs:
        btp, ktp = blk(bt, p), blk(kt, p)
        w = b16(rows([head0(btp), head0(ktp), head1(ktp), head1(btp)]))
        g = _dot_nt(rows([blk(at, p), blk(rt, p)]), w)
        g0.append(jnp.where(keep, g[:, :LANES], 0.0))
        g1.append(jnp.where(keep, g[:, LANES:], 0.0))

    eye2 = jnp.where((lax.broadcasted_iota(jnp.int32, (CHUNK, LANES), 1) & (CHUNK - 1))
                     == lax.broadcasted_iota(jnp.int32, (CHUNK, LANES), 0), 1.0, 0.0)
    pc = [jnp.where(lo_cols, g0[p][:CHUNK], g1[p][:CHUNK]) for p in pairs]
    tc = [eye2 + pc[p] for p in pairs]
    q = [_dot(b16(pc[p]), b16(blockdiag(pc[p]))) for p in pairs]
    for _ in range(4):
        res = [_dot(b16(rows([tc[p], q[p]])), b16(blockdiag(q[p]))) for p in pairs]
        tc = [tc[p] + res[p][:CHUNK] for p in pairs]
        q = [res[p][CHUNK:] for p in pairs]
    tc = [b16(tc[p] + _dot(b16(tc[p]), b16(blockdiag(q[p])))) for p in pairs]

    akv = []
    for p in pairs:
        w_ak = b16(jnp.where(lo_cols, g1[p][:CHUNK], g0[p][:CHUNK]))
        vp = blk(v16, p)
        akv.append(_dot(w_ak, rows([head1(vp), head0(vp)])))

    s = [st_ref[p] for p in pairs]
    s16 = [b16(s[p]) for p in pairs]
    rhs = [_dot_nt(blk(at, p), s16[p]) + akv[p] for p in pairs]
    u = [_dot(tc[p], b16(rows([head0(rhs[p]), head1(rhs[p])]))) for p in pairs]
    u16 = [b16(u[p]) for p in pairs]

    ys = []
    bd_mask = (rr < HEAD) == (cc < HEAD)
    for p in pairs:
        vp = blk(v16, p)
        rbk = b16(jnp.concatenate([g0[p][CHUNK:], g1[p][CHUNK:]], axis=1))
        uv = rows([head0(u16[p]), head0(vp), head1(vp), head1(u16[p])])
        ys.append(_dot_nt(blk(rt, p), s16[p]) + _dot(rbk, uv))
        fresh = _dot_tn(rows([u16[p], vp]), blk(bkh, p))
        st_ref[p] = s[p] * blk(e_cl, p) + jnp.where(bd_mask, fresh, 0.0)

    avg = _seg_matrix(1.0 / HEAD)
    y = rows(ys)
    yc = y - _dot_lhs_x2(y, avg)
    var = _dot(b16(yc * yc), avg)
    yn = yc * lax.rsqrt(var + GN_EPS)
    rkk = r * k * rk_ref[...]
    bonus = _dot_lhs_x2(rows([blk(rkk, p) for p in pairs]), _seg_matrix(1.0))
    for p in pairs:
        cols = slice(p * LANES, (p + 1) * LANES)
        ynp = yn[p * CHUNK:(p + 1) * CHUNK] * lnw_ref[:, cols] + lnb_ref[:, cols]
        zp = (ynp + bonus[p * CHUNK:(p + 1) * CHUNK] * v[:, cols]) * g_ref[:, cols]
        z_ref[:, cols] = zp.astype(z_ref.dtype)


def _wkv(r, lw, k, v, a, b, g, st0, w, nb, s):
    n = nb * s
    nc = s // CHUNK
    blk = pl.BlockSpec((CHUNK, D_MODEL), lambda bi, c: (bi * nc + c, 0))
    vec = _const_spec((1, D_MODEL))
    st_spec = pl.BlockSpec((None, PAIRS, LANES, LANES), lambda bi, c: (bi, 0, 0, 0))
    return pl.pallas_call(
        _wkv_kernel,
        grid=(nb, nc),
        in_specs=[blk] * 7 + [st_spec, vec, vec, vec],
        out_specs=[blk, st_spec],
        out_shape=[jax.ShapeDtypeStruct((n, D_MODEL), BF16),
                   jax.ShapeDtypeStruct((nb, PAIRS, LANES, LANES), F32)],
        compiler_params=_params(("parallel", "arbitrary")),
        name="wkv",
    )(r, lw, k, v, a, b, g, st0, w["r_k"], w["lnx_w"], w["lnx_b"])


def _out_proj_kernel(x_ref, z_ref, gate_ref, w_ref, bias_ref, o_ref, *, nb):
    rows = x_ref.shape[0]
    y = _dot(z_ref[...], w_ref[...]) + bias_ref[...]
    y = (y.reshape(nb, rows // nb, D_MODEL) * gate_ref[...]).reshape(rows, D_MODEL)
    o_ref[...] = x_ref[...] + y


def _out_proj(x, z, gate, w, bias, nb, s, tm):
    n = nb * s
    tb = max(tm // s, 1)
    per_seq = max(s // tm, 1)
    row_spec = pl.BlockSpec((tm, D_MODEL), lambda i: (i, 0))
    return pl.pallas_call(
        functools.partial(_out_proj_kernel, nb=tb),
        grid=(n // tm,),
        in_specs=[row_spec, row_spec,
                  pl.BlockSpec((tb, 1, D_MODEL), lambda i: (i // per_seq, 0, 0)),
                  _const_spec((D_MODEL, D_MODEL)), _const_spec((1, D_MODEL))],
        out_specs=row_spec,
        out_shape=jax.ShapeDtypeStruct((n, D_MODEL), F32),
        compiler_params=_params(("parallel",)),
        name="out_proj",
    )(x, z, gate, w, bias)


def _top2(logits):
    lane = lax.broadcasted_iota(jnp.int32, logits.shape, 1)
    big = jnp.int32(1 << 20)
    lg = jnp.where(lane < N_GROUPS, logits, -jnp.inf)
    gmax = jnp.max(lg, axis=-1, keepdims=True)
    gi = jnp.min(jnp.where(lg == gmax, lane, big), axis=-1, keepdims=True)
    gp = 1.0 / jnp.sum(jnp.exp(lg - gmax), axis=-1, keepdims=True)
    first = ROUTE_LANE0 + gi * EXP_PER_GROUP
    le = jnp.where((lane >= first) & (lane < first + EXP_PER_GROUP), logits, -jnp.inf)
    top1 = jnp.max(le, axis=-1, keepdims=True)
    i1 = jnp.min(jnp.where(le == top1, lane, big), axis=-1, keepdims=True)
    le2 = jnp.where(lane == i1, -jnp.inf, le)
    top2 = jnp.max(le2, axis=-1, keepdims=True)
    i2 = jnp.min(jnp.where(le2 == top2, lane, big), axis=-1, keepdims=True)
    e2 = jnp.exp(top2 - top1)
    w1 = gp / (1.0 + e2)
    return i1, i2, w1, w1 * e2


def _route(logits):
    lane = lax.broadcasted_iota(jnp.int32, logits.shape, 1)
    i1, i2, w1, w2 = _top2(logits)
    return jnp.where(lane == i1, w1, 0.0) + jnp.where(lane == i2, w2, 0.0)


def _moe_kernel(x_ref, mod_ref, ng_ref, wr_ref, wgu_ref, wdn_ref, fg_ref, o_ref, h_scr, cw_scr, acc_scr,
                *, nb, final_norm):
    e = pl.program_id(1)
    rows = x_ref.shape[0]

    @pl.when(e == 0)
    def _():
        shift = mod_ref[:, :, 3 * D_MODEL:4 * D_MODEL]
        scale = mod_ref[:, :, 4 * D_MODEL:5 * D_MODEL]
        h = _modulate(_rms(x_ref[...]), ng_ref[...], shift, scale, nb)
        h_scr[...] = h.astype(BF16)
        cw_scr[...] = _route(_dot_x3(h, wr_ref[...]))
        acc_scr[...] = jnp.zeros_like(acc_scr)

    lane = lax.broadcasted_iota(jnp.int32, (rows, LANES), 1)
    cw = jnp.sum(jnp.where(lane == ROUTE_LANE0 + e, cw_scr[...], 0.0), axis=-1, keepdims=True)
    gu = _dot(h_scr[...], wgu_ref[...])
    gate = gu[:, :D_EXPERT]
    hid = gate * _sigmoid(gate) * gu[:, D_EXPERT:] * cw
    acc_scr[...] += _dot(hid.astype(BF16), wdn_ref[...])

    @pl.when(e == N_EXPERTS - 1)
    def _():
        g2 = mod_ref[:, :, 5 * D_MODEL:6 * D_MODEL]
        y = (acc_scr[...].reshape(nb, rows // nb, D_MODEL) * g2).reshape(rows, D_MODEL)
        out = x_ref[...] + y
        if final_norm:
            out = _rms(out) * fg_ref[...]
        o_ref[...] = out


def _moe(x, mod, norm_g, w_route, w_gu, w_down, final_g, nb, s, tm, final_norm):
    n = nb * s
    tb = max(tm // s, 1)
    per_seq = max(s // tm, 1)
    row_spec = pl.BlockSpec((tm, D_MODEL), lambda i, e: (i, 0))
    return pl.pallas_call(
        functools.partial(_moe_kernel, nb=tb, final_norm=final_norm),
        grid=(n // tm, N_EXPERTS),
        in_specs=[row_spec,
                  pl.BlockSpec((tb, 1, 6 * D_MODEL), lambda i, e: (i // per_seq, 0, 0)),
                  _const_spec((1, D_MODEL)), _const_spec((D_MODEL, LANES)),
                  pl.BlockSpec((None, D_MODEL, 2 * D_EXPERT), lambda i, e: (e, 0, 0)),
                  pl.BlockSpec((None, D_EXPERT, D_MODEL), lambda i, e: (e, 0, 0)),
                  _const_spec((1, D_MODEL))],
        out_specs=row_spec,
        out_shape=jax.ShapeDtypeStruct((n, D_MODEL), F32),
        scratch_shapes=[pltpu.VMEM((tm, D_MODEL), BF16), pltpu.VMEM((tm, LANES), F32),
                        pltpu.VMEM((tm, D_MODEL), F32)],
        compiler_params=_params(("parallel", "arbitrary")),
        name="moe",
    )(x, mod, norm_g, w_route, w_gu, w_down, final_g)


PAIRS_PER_GROUP = EXP_PER_GROUP * (EXP_PER_GROUP - 1) // 2
N_BUCKETS = N_GROUPS * PAIRS_PER_GROUP
BUCKET_TILE = 128
ROW_W = D_MODEL + LANES
META_BUCKET, META_RANK, META_W_LO, META_W_HI = 0, 1, 2, 3
ROUTER_TILE = 512
MOVE_TILE = 256
DMA_UNROLL = 8
SPARSE_MIN_TOKENS = 4096


def _moe_norm(x_ref, mod_ref, ng_ref, nb):
    shift = mod_ref[:, :, 3 * D_MODEL:4 * D_MODEL]
    scale = mod_ref[:, :, 4 * D_MODEL:5 * D_MODEL]
    return _modulate(_rms(x_ref[...]), ng_ref[...], shift, scale, nb)


def _router_kernel(x_ref, mod_ref, ng_ref, wr_ref, meta_ref, cnt_ref, *, nb):
    @pl.when(pl.program_id(0) == 0)
    def _():
        cnt_ref[...] = jnp.zeros_like(cnt_ref)

    rows = x_ref.shape[0]
    h = _moe_norm(x_ref, mod_ref, ng_ref, nb)
    i1, i2, w1, w2 = _top2(_dot_x3(h, wr_ref[...]))
    lo = jnp.minimum(i1, i2) - ROUTE_LANE0
    hi = jnp.maximum(i1, i2) - ROUTE_LANE0
    first_is_lo = i1 < i2
    a = lo & (EXP_PER_GROUP - 1)
    b = hi & (EXP_PER_GROUP - 1)
    group = lo >> (EXP_PER_GROUP.bit_length() - 1)
    bucket = group * PAIRS_PER_GROUP + ((a * (2 * EXP_PER_GROUP - 1 - a)) >> 1) + (b - a - 1)

    lane = lax.broadcasted_iota(jnp.int32, (rows, LANES), 1)
    mine = lane == bucket
    onehot = jnp.where(mine, 1.0, 0.0)
    earlier = (lax.broadcasted_iota(jnp.int32, (rows, rows), 1)
               < lax.broadcasted_iota(jnp.int32, (rows, rows), 0)).astype(BF16)
    seen = cnt_ref[...]
    before = _dot(earlier, onehot.astype(BF16)) + seen
    rank = jnp.sum(jnp.where(mine, before, 0.0), axis=-1, keepdims=True)
    cnt_ref[...] = seen + jnp.sum(onehot, axis=0, keepdims=True)

    meta = jnp.where(lane == META_BUCKET, bucket.astype(F32), 0.0)
    meta = jnp.where(lane == META_RANK, rank, meta)
    meta = jnp.where(lane == META_W_LO, jnp.where(first_is_lo, w1, w2), meta)
    meta_ref[...] = jnp.where(lane == META_W_HI, jnp.where(first_is_lo, w2, w1), meta)


def _dispatch_kernel(pos_ref, x_ref, mod_ref, ng_ref, meta_ref, xs_ref, buf, sem, *, nb):
    i = pl.program_id(0)
    rows = x_ref.shape[0]
    slot = i % 2
    buf[slot, :, 0:D_MODEL] = _moe_norm(x_ref, mod_ref, ng_ref, nb)
    buf[slot, :, D_MODEL:ROW_W] = meta_ref[...]

    def send(r, carry):
        pltpu.make_async_copy(buf.at[slot, pl.ds(r, 1), :],
                              xs_ref.at[pl.ds(pos_ref[i * rows + r], 1), :], sem.at[slot]).start()
        return carry

    lax.fori_loop(0, rows, send, 0, unroll=DMA_UNROLL)

    def wait_all(sl):
        pltpu.make_async_copy(buf.at[sl], xs_ref.at[pl.ds(0, rows), :], sem.at[sl]).wait()

    @pl.when(i >= 1)
    def _():
        wait_all(1 - slot)

    @pl.when(i == pl.num_programs(0) - 1)
    def _():
        wait_all(slot)


def _experts_kernel(tile_ref, bucket_ref, lo_ref, hi_ref, used_ref, xs_ref, gu_lo_ref, gu_hi_ref,
                    dn_lo_ref, dn_hi_ref, ys_ref):
    k = pl.program_id(0)

    @pl.when(k < used_ref[0])
    def _():
        rows = xs_ref.shape[0]
        x16 = xs_ref[:, 0:D_MODEL].astype(BF16)
        meta = xs_ref[:, D_MODEL:ROW_W]
        lane = lax.broadcasted_iota(jnp.int32, (rows, LANES), 1)
        column = lambda c: jnp.sum(jnp.where(lane == c, meta, 0.0), axis=-1, keepdims=True)
        member = column(META_BUCKET) == bucket_ref[k].astype(F32)

        def expert(gu_ref, dn_ref, col):
            gu = _dot(x16, gu_ref[...])
            gate = gu[:, :D_EXPERT]
            hid = gate * _sigmoid(gate) * gu[:, D_EXPERT:] * jnp.where(member, column(col), 0.0)
            return _dot(hid.astype(BF16), dn_ref[...])

        y = expert(gu_lo_ref, dn_lo_ref, META_W_LO) + expert(gu_hi_ref, dn_hi_ref, META_W_HI)
        new_tile = (k == 0) | (tile_ref[k] != tile_ref[jnp.maximum(k - 1, 0)])

        @pl.when(new_tile)
        def _():
            ys_ref[...] = y

        @pl.when(jnp.logical_not(new_tile))
        def _():
            ys_ref[...] += y


def _combine_kernel(pos_ref, x_ref, mod_ref, fg_ref, ys_ref, o_ref, buf, sem, *, nb, final_norm):
    i = pl.program_id(0)
    rows = x_ref.shape[0]
    slot = i % 2

    def fetch(step, sl):
        def one(r, carry):
            pltpu.make_async_copy(ys_ref.at[pl.ds(pos_ref[step * rows + r], 1), :],
                                  buf.at[sl, pl.ds(r, 1), :], sem.at[sl]).start()
            return carry
        lax.fori_loop(0, rows, one, 0, unroll=DMA_UNROLL)

    @pl.when(i == 0)
    def _():
        fetch(0, 0)

    pltpu.make_async_copy(ys_ref.at[pl.ds(0, rows), :], buf.at[slot], sem.at[slot]).wait()

    @pl.when(i + 1 < pl.num_programs(0))
    def _():
        fetch(i + 1, 1 - slot)

    g2 = mod_ref[:, :, 5 * D_MODEL:6 * D_MODEL]
    y = (buf[slot].reshape(nb, rows // nb, D_MODEL) * g2).reshape(rows, D_MODEL)
    out = x_ref[...] + y
    if final_norm:
        out = _rms(out) * fg_ref[...]
    o_ref[...] = out


def _bucket_experts():
    lo, hi = [], []
    for g in range(N_GROUPS):
        for a in range(EXP_PER_GROUP):
            for b in range(a + 1, EXP_PER_GROUP):
                lo.append(g * EXP_PER_GROUP + a)
                hi.append(g * EXP_PER_GROUP + b)
    return jnp.asarray(lo, jnp.int32), jnp.asarray(hi, jnp.int32)


def _moe_sparse(x, mod, norm_g, w_route, w_gu, w_down, final_g, nb, s, final_norm):
    n = nb * s
    move_params = pltpu.CompilerParams(dimension_semantics=("arbitrary",), vmem_limit_bytes=VMEM_LIMIT,
                                       disable_bounds_checks=True)

    def token_specs(tm):
        tb, per_seq = max(tm // s, 1), max(s // tm, 1)
        row = pl.BlockSpec((tm, D_MODEL), lambda i, *_: (i, 0))
        modspec = pl.BlockSpec((tb, 1, 6 * D_MODEL), lambda i, *_: (i // per_seq, 0, 0))
        meta = pl.BlockSpec((tm, LANES), lambda i, *_: (i, 0))
        return tb, row, modspec, meta

    vec = pl.BlockSpec((1, D_MODEL), lambda *_: (0, 0))
    tb, row, modspec, metaspec = token_specs(ROUTER_TILE)
    meta, counts = pl.pallas_call(
        functools.partial(_router_kernel, nb=tb),
        grid=(n // ROUTER_TILE,),
        in_specs=[row, modspec, vec, pl.BlockSpec((D_MODEL, LANES), lambda i: (0, 0))],
        out_specs=[metaspec, pl.BlockSpec((1, LANES), lambda i: (0, 0))],
        out_shape=[jax.ShapeDtypeStruct((n, LANES), F32), jax.ShapeDtypeStruct((1, LANES), F32)],
        compiler_params=_params(("arbitrary",)),
        name="moe_router",
    )(x, mod, norm_g, w_route)

    counts = counts[0, :N_BUCKETS].astype(jnp.int32)
    end = jnp.cumsum(counts)
    start = end - counts
    first_tile = start // BUCKET_TILE
    per_bucket = jnp.where(counts > 0, (end - 1) // BUCKET_TILE - first_tile + 1, 0)
    item_end = jnp.cumsum(per_bucket)
    used = item_end[-1:]
    n_items = n // BUCKET_TILE + N_BUCKETS
    k = jnp.minimum(jnp.arange(n_items, dtype=jnp.int32), used - 1)
    item_bucket = jnp.searchsorted(item_end, k, side="right").astype(jnp.int32)
    item_tile = first_tile[item_bucket] + k - (item_end - per_bucket)[item_bucket]
    bucket_lo, bucket_hi = _bucket_experts()
    item_lo, item_hi = bucket_lo[item_bucket], bucket_hi[item_bucket]
    pos = start[meta[:, META_BUCKET].astype(jnp.int32)] + meta[:, META_RANK].astype(jnp.int32)

    tb, row, modspec, metaspec = token_specs(MOVE_TILE)
    any_spec = pl.BlockSpec(memory_space=pl.ANY)
    xs = pl.pallas_call(
        functools.partial(_dispatch_kernel, nb=tb),
        grid_spec=pltpu.PrefetchScalarGridSpec(
            num_scalar_prefetch=1, grid=(n // MOVE_TILE,),
            in_specs=[row, modspec, vec, metaspec], out_specs=any_spec,
            scratch_shapes=[pltpu.VMEM((2, MOVE_TILE, ROW_W), F32), pltpu.SemaphoreType.DMA((2,))]),
        out_shape=jax.ShapeDtypeStruct((n, ROW_W), F32),
        compiler_params=move_params,
        name="moe_dispatch",
    )(pos, x, mod, norm_g, meta)

    tile_of = lambda k, tile, *_: (tile[k], 0)
    ys = pl.pallas_call(
        _experts_kernel,
        grid_spec=pltpu.PrefetchScalarGridSpec(
            num_scalar_prefetch=5, grid=(n_items,),
            in_specs=[pl.BlockSpec((BUCKET_TILE, ROW_W), tile_of),
                      pl.BlockSpec((None, D_MODEL, 2 * D_EXPERT), lambda k, t, b, lo, hi, nu: (lo[k], 0, 0)),
                      pl.BlockSpec((None, D_MODEL, 2 * D_EXPERT), lambda k, t, b, lo, hi, nu: (hi[k], 0, 0)),
                      pl.BlockSpec((None, D_EXPERT, D_MODEL), lambda k, t, b, lo, hi, nu: (lo[k], 0, 0)),
                      pl.BlockSpec((None, D_EXPERT, D_MODEL), lambda k, t, b, lo, hi, nu: (hi[k], 0, 0))],
            out_specs=pl.BlockSpec((BUCKET_TILE, D_MODEL), tile_of)),
        out_shape=jax.ShapeDtypeStruct((n, D_MODEL), F32),
        compiler_params=_params(("arbitrary",)),
        name="moe_experts",
    )(item_tile, item_bucket, item_lo, item_hi, used, xs, w_gu, w_gu, w_down, w_down)

    return pl.pallas_call(
        functools.partial(_combine_kernel, nb=tb, final_norm=final_norm),
        grid_spec=pltpu.PrefetchScalarGridSpec(
            num_scalar_prefetch=1, grid=(n // MOVE_TILE,),
            in_specs=[row, modspec, vec, any_spec], out_specs=row,
            scratch_shapes=[pltpu.VMEM((2, MOVE_TILE, D_MODEL), F32), pltpu.SemaphoreType.DMA((2,))]),
        out_shape=jax.ShapeDtypeStruct((n, D_MODEL), F32),
        compiler_params=move_params,
        name="moe_combine",
    )(pos, x, mod, final_g, ys)


def _rope(x, cos, sin_lo, sin_hi):
    return x * cos + pltpu.roll(x, LANES - ROPE_DIM // 2, axis=1) * sin_lo + pltpu.roll(x, ROPE_DIM // 2, axis=1) * sin_hi


def _qkv_kernel(x_ref, mod_ref, kvmod_ref, ngq_ref, ngkv_ref, wq_ref, bq_ref, wkv_ref, bkv_ref,
                cos_ref, slo_ref, shi_ref, q_ref, k_ref, v_ref):
    n = _rms(x_ref[...])
    shift = mod_ref[:, 0:D_MODEL].reshape(1, 1, D_MODEL)
    scale = mod_ref[:, D_MODEL:2 * D_MODEL].reshape(1, 1, D_MODEL)
    hq = _modulate(n, ngq_ref[...], shift, scale, 1)
    kshift = kvmod_ref[:, 0:D_MODEL].reshape(1, 1, D_MODEL)
    kscale = kvmod_ref[:, D_MODEL:2 * D_MODEL].reshape(1, 1, D_MODEL)
    hkv = _modulate(n, ngkv_ref[...], kshift, kscale, 1)
    cos, slo, shi = cos_ref[...], slo_ref[...], shi_ref[...]
    q = _dot(hq.astype(BF16), wq_ref[...]) + bq_ref[...]
    for p in range(PAIRS):
        cols = slice(p * LANES, (p + 1) * LANES)
        q_ref[:, cols] = (_rope(q[:, cols], cos, slo, shi) * ATT_SCALE).astype(q_ref.dtype)
    kv = _dot(hkv.astype(BF16), wkv_ref[...]) + bkv_ref[...]
    for p in range(KV_W // LANES):
        cols = slice(p * LANES, (p + 1) * LANES)
        k_ref[:, cols] = _rope(kv[:, cols], cos, slo, shi)
    v_ref[...] = kv[:, KV_W:]


def _qkv_proj(x, mod, kvmod, w, tables, nb, s, tm):
    n = nb * s
    nt = s // tm
    row_spec = pl.BlockSpec((tm, D_MODEL), lambda b, j: (b * nt + j, 0))
    kv_spec = pl.BlockSpec((tm, KV_W), lambda b, j: (b * nt + j, 0))
    tab_spec = pl.BlockSpec((tm, LANES), lambda b, j: (j, 0))
    vec = _const_spec((1, D_MODEL))
    return pl.pallas_call(
        _qkv_kernel,
        grid=(nb, nt),
        in_specs=[row_spec,
                  pl.BlockSpec((None, 1, 6 * D_MODEL), lambda b, j: (b, 0, 0)),
                  pl.BlockSpec((None, 1, 2 * D_MODEL), lambda b, j: (b, 0, 0)),
                  vec, vec, _const_spec((D_MODEL, D_MODEL)), vec,
                  _const_spec((D_MODEL, 2 * KV_W)), _const_spec((1, 2 * KV_W)),
                  tab_spec, tab_spec, tab_spec],
        out_specs=[row_spec, kv_spec, kv_spec],
        out_shape=[jax.ShapeDtypeStruct((n, D_MODEL), BF16),
                   jax.ShapeDtypeStruct((n, KV_W), F32), jax.ShapeDtypeStruct((n, KV_W), F32)],
        compiler_params=_params(("parallel", "parallel")),
        name="qkv_proj",
    )(x, mod, kvmod, w["norm_gq"], w["norm_gkv"], w["w_q"], w["b_q"], w["w_kv"], w["b_kv"], *tables)


def _attn_kernel(sink_ref, q_ref, k0_ref, k1_ref, k2_ref, v0_ref, v1_ref, v2_ref, o_ref, *, banded):
    n = pl.program_id(1)
    k_all = jnp.concatenate([k0_ref[...], k1_ref[...], k2_ref[...]], axis=0)
    v_all = jnp.concatenate([v0_ref[...], v1_ref[...], v2_ref[...]], axis=0)
    lane = lax.broadcasted_iota(jnp.int32, (1, LANES), 1)
    lo = lane < HEAD
    col = lax.broadcasted_iota(jnp.int32, (1, 3 * CHUNK), 1)
    if banded:
        visible = (col >= 2 * CHUNK) | ((col >= CHUNK) & (n >= 1)) | (n >= 2)

    def halves(t, kv_head):
        blk = t[:, (kv_head // 2) * LANES:(kv_head // 2 + 1) * LANES]
        if kv_head % 2 == 0:
            t_lo = jnp.where(lo, blk, 0.0)
            t_hi = pltpu.roll(t_lo, HEAD, axis=1)
        else:
            t_hi = jnp.where(lo, 0.0, blk)
            t_lo = pltpu.roll(t_hi, HEAD, axis=1)
        return t_lo.astype(BF16), t_hi.astype(BF16)

    first_pair = lax.broadcasted_iota(jnp.int32, (2 * CHUNK, 1), 0) < CHUNK

    def probs(s, head_a, head_b):
        if banded:
            s = jnp.where(visible, s, NEG_INF)
        sink = jnp.where(first_pair, sink_ref[head_a], sink_ref[head_b])
        m = jnp.maximum(jnp.max(s, axis=-1, keepdims=True), sink)
        e = jnp.exp(s - m)
        return (e / (jnp.sum(e, axis=-1, keepdims=True) + jnp.exp(sink - m))).astype(BF16)

    scores, values = [], []
    for g in range(N_KV):
        k_lo, k_hi = halves(k_all, g)
        values.append(halves(v_all, g))
        q2 = jnp.concatenate([q_ref[:, (2 * g + j) * LANES:(2 * g + j + 1) * LANES] for j in range(2)], axis=0)
        scores.append((_dot_nt(q2, k_lo), _dot_nt(q2, k_hi)))
    for g in range(N_KV):
        o = (_dot(probs(scores[g][0], 4 * g, 4 * g + 2), values[g][0])
             + _dot(probs(scores[g][1], 4 * g + 1, 4 * g + 3), values[g][1]))
        for j in range(2):
            pair = 2 * g + j
            o_ref[:, pair * LANES:(pair + 1) * LANES] = o[j * CHUNK:(j + 1) * CHUNK].astype(o_ref.dtype)


def _attention(q, kv_arrays, kv_maps, sinks, nb, s, banded):
    n = nb * s
    nc = s // CHUNK
    q_spec = pl.BlockSpec((CHUNK, D_MODEL), lambda b, c: (b * nc + c, 0))
    kv_specs = [pl.BlockSpec((CHUNK, KV_W), m) for m in kv_maps]
    return pl.pallas_call(
        functools.partial(_attn_kernel, banded=banded),
        grid=(nb, nc),
        in_specs=[pl.BlockSpec(memory_space=pltpu.SMEM), q_spec] + kv_specs,
        out_specs=q_spec,
        out_shape=jax.ShapeDtypeStruct((n, D_MODEL), BF16),
        compiler_params=_params(("parallel", "parallel")),
        name="attention",
    )(sinks, q, *kv_arrays)


def _pad_cols(w, n):
    return jnp.pad(w, ((0, 0), (0, n - w.shape[1])))


def _pad_rows(w, n):
    return jnp.pad(w, ((0, n - w.shape[0]), (0, 0)))


def _rope_tables(pos):
    half = ROPE_DIM // 2
    inv = jnp.power(jnp.float32(ROPE_THETA), -jnp.arange(half, dtype=F32) * (2.0 / ROPE_DIM))
    ang = pos[:, None] * inv[None, :]
    cos, sin = jnp.cos(ang), jnp.sin(ang)
    rest = HEAD - ROPE_DIM
    ones = jnp.ones((pos.shape[0], rest), F32)
    zeros = jnp.zeros((pos.shape[0], rest), F32)
    z8 = jnp.zeros_like(sin)
    per_head = (jnp.concatenate([cos, cos, ones], axis=1),
                jnp.concatenate([-sin, z8, zeros], axis=1),
                jnp.concatenate([z8, sin, zeros], axis=1))
    return tuple(jnp.tile(t, (1, LANES // HEAD)) for t in per_head)


def _state_to_pairs(state):
    nb = state.shape[0]
    st = state.astype(F32).reshape(nb, PAIRS, 2, HEAD, HEAD)
    z = jnp.zeros_like(st[:, :, 0])
    top = jnp.concatenate([st[:, :, 0], z], axis=-1)
    bot = jnp.concatenate([z, st[:, :, 1]], axis=-1)
    return jnp.concatenate([top, bot], axis=-2)


def _pairs_to_state(st):
    nb = st.shape[0]
    s0 = st[:, :, :HEAD, :HEAD]
    s1 = st[:, :, HEAD:, HEAD:]
    return jnp.stack([s0, s1], axis=2).reshape(nb, 2 * PAIRS, HEAD, HEAD)


def _trunk(x, mods, kvmod, pos, prev_x, prev_wkv, past_k, past_v, w, nb, s):
    n = nb * s
    x = x.reshape(n, D_MODEL)
    mod0, mod1 = mods[0][:, None, :], mods[1][:, None, :]
    kvmod = kvmod[:, None, :]

    tm = min(s, 256)
    r, lw, k, v, a, b, g, last_x = _rwkv_proj(x, mod0, prev_x[:, None, :], w["rw"], nb, s, tm)
    z, st = _wkv(r, lw, k, v, a, b, g, _state_to_pairs(prev_wkv), w["rw"], nb, s)
    tm = min(n, 256)
    x = _out_proj(x, z, mod0[:, :, 2 * D_MODEL:3 * D_MODEL], w["rw"]["w_o"], w["rw"]["b_o"], nb, s, tm)
    def moe_layer(x, mod, l):
        args = (x, mod, w["norm_g"][l][1:2], w["moe_route"][l], w["moe_gu"][l], w["moe_down"][l], w["final_g"], nb, s)
        if n >= SPARSE_MIN_TOKENS:
            return _moe_sparse(*args, l == 1)
        return _moe(*args, min(n, 1024), l == 1)

    x = moe_layer(x, mod0, 0)

    tm = min(s, 256)
    q, k_new, v_new = _qkv_proj(x, mod1, kvmod, w["at"], _rope_tables(pos), nb, s, tm)
    nc = s // CHUNK
    if past_k is None:
        arrays = [k_new] * 3 + [v_new] * 3
        maps = [lambda bi, c, d=d: (bi * nc + jnp.maximum(c - d, 0), 0) for d in (2, 1, 0)] * 2
    else:
        pk = past_k.astype(F32).reshape(nb * 2 * CHUNK, KV_W)
        pv = past_v.astype(F32).reshape(nb * 2 * CHUNK, KV_W)
        arrays = [pk, pk, k_new, pv, pv, v_new]
        maps = [lambda bi, c: (2 * bi, 0), lambda bi, c: (2 * bi + 1, 0), lambda bi, c: (bi, 0)] * 2
    o = _attention(q, arrays, maps, w["at"]["sinks"], nb, s, past_k is None)
    tm = min(n, 256)
    x = _out_proj(x, o, mod1[:, :, 2 * D_MODEL:3 * D_MODEL], w["at"]["w_o"], w["at"]["b_o"], nb, s, tm)
    y = moe_layer(x, mod1, 1)

    return (y.reshape(nb, s, D_MODEL), last_x.reshape(1, nb, D_MODEL), _pairs_to_state(st)[None],
            k_new.reshape(nb, s, N_KV, HEAD), v_new.reshape(nb, s, N_KV, HEAD))


def kernel(x_prompt, x_sample, state_shift, state_wkv, cache_k, cache_v, c_prompt, c_sample, ada_w, ada_b, norm_g, rw_mu, rw_w_rkv, rw_w0, rw_w1, rw_w2, rw_a0, rw_a1, rw_a2, rw_g1, rw_g2, rw_k_k, rw_k_a, rw_r_k, rw_lnx_w, rw_lnx_b, rw_w_o, kv_ada_w, kv_ada_b, kv_norm_g, w_kv, b_kv, at_w_q, at_b_q, at_sinks, at_w_o, at_b_o, moe_w_group, moe_w_expert, moe_w_gu, moe_w_down, final_norm_g):
    bp, sp, _ = x_prompt.shape
    bs, ss, _ = x_sample.shape
    row = lambda t: t.reshape(1, -1).astype(F32)

    c_all = jnp.concatenate([c_prompt, c_sample], axis=0)
    mods = _cond_linear(c_all, ada_w, ada_b[:, None, :])
    kvmods = _cond_linear(c_all, kv_ada_w[None], kv_ada_b[None, None, :])[0]

    lora_pad, gate_pad = LANES, 2 * LANES
    w = {
        "norm_g": norm_g,
        "final_g": row(final_norm_g),
        "rw": {
            "norm_g": norm_g[0, 0:1], "mu": rw_mu[0], "w_rkv": rw_w_rkv[0].astype(BF16),
            "w0": row(rw_w0[0]), "w1": _pad_cols(rw_w1[0], lora_pad).astype(BF16),
            "w2": _pad_rows(rw_w2[0], lora_pad).astype(BF16),
            "a0": row(rw_a0[0]), "a1": _pad_cols(rw_a1[0], lora_pad).astype(BF16),
            "a2": _pad_rows(rw_a2[0], lora_pad).astype(BF16),
            "g1": _pad_cols(rw_g1[0], gate_pad).astype(BF16), "g2": _pad_rows(rw_g2[0], gate_pad).astype(BF16),
            "k_k": row(rw_k_k[0]), "k_a": row(rw_k_a[0]), "r_k": row(rw_r_k[0]),
            "lnx_w": row(rw_lnx_w[0]), "lnx_b": row(rw_lnx_b[0]),
            "w_o": rw_w_o[0].astype(BF16), "b_o": jnp.zeros((1, D_MODEL), F32),
        },
        "at": {
            "norm_gq": norm_g[1, 0:1], "norm_gkv": row(kv_norm_g),
            "w_q": at_w_q[0].astype(BF16), "b_q": row(at_b_q[0]),
            "w_kv": w_kv.astype(BF16), "b_kv": row(b_kv),
            "sinks": at_sinks[0].astype(F32),
            "w_o": at_w_o[0].astype(BF16), "b_o": row(at_b_o[0]),
        },
        "moe_route": jnp.pad(jnp.concatenate([moe_w_group, moe_w_expert], axis=-1),
                             ((0, 0), (0, 0), (0, LANES - N_GROUPS - N_EXPERTS))),
        "moe_gu": moe_w_gu.astype(BF16),
        "moe_down": moe_w_down.astype(BF16),
    }

    pos_p = jnp.arange(sp, dtype=F32)
    pos_s = PAST_LEN + jnp.arange(ss, dtype=F32)
    zero_x = jnp.zeros((bp, D_MODEL), x_prompt.dtype)
    zero_wkv = jnp.zeros((bp,) + state_wkv.shape[2:], state_wkv.dtype)
    y_p, p_shift, p_wkv, p_k, p_v = _trunk(x_prompt, mods[:, :bp], kvmods[:bp], pos_p, zero_x, zero_wkv,
                                           None, None, w, bp, sp)
    y_s, s_shift, s_wkv, s_k, s_v = _trunk(x_sample, mods[:, bp:], kvmods[bp:], pos_s, state_shift[0],
                                           state_wkv[0], cache_k, cache_v, w, bs, ss)
    keep = min(2 * CHUNK, sp)
    return (y_p, y_s, p_shift, p_wkv.astype(state_wkv.dtype), p_k[:, sp - keep:], p_v[:, sp - keep:],
            s_shift, s_wkv.astype(state_wkv.dtype), s_k, s_v)
```

```python
import functools

import jax
import jax.numpy as jnp
from jax import lax
from jax.experimental import pallas as pl
from jax.experimental.pallas import tpu as pltpu

F32 = jnp.float32
BF16 = jnp.bfloat16

D_MODEL = 1024
LANES = 128
HEAD = 64
PAIRS = D_MODEL // LANES
CHUNK = 64
PAST_LEN = 4096
N_KV = 4
KV_W = N_KV * HEAD
ROPE_DIM = 16
ROPE_THETA = 500000.0
ATT_SCALE = HEAD ** -0.5
N_GROUPS = 4
EXP_PER_GROUP = 8
N_EXPERTS = N_GROUPS * EXP_PER_GROUP
D_EXPERT = D_MODEL // 4
ROUTE_LANE0 = N_GROUPS
RMS_EPS = 1e-6
GN_EPS = 64e-5
NEG_INF = -1e30
VMEM_LIMIT = 56 * 1024 * 1024


def _params(sem):
    return pltpu.CompilerParams(dimension_semantics=sem, vmem_limit_bytes=VMEM_LIMIT)


def _dot(a, b):
    return jnp.dot(a, b, preferred_element_type=F32)


def _dot_nt(a, b):
    return lax.dot_general(a, b, (((1,), (1,)), ((), ())), preferred_element_type=F32)


def _dot_tn(a, b):
    return lax.dot_general(a, b, (((0,), (0,)), ((), ())), preferred_element_type=F32)


def _split2(x):
    hi = x.astype(BF16)
    lo = (x - hi.astype(F32)).astype(BF16)
    return hi, lo


def _dot_x3(a, b, dot=_dot):
    ah, al = _split2(a)
    bh, bl = _split2(b)
    return dot(ah, bh) + dot(ah, bl) + dot(al, bh)


def _dot_lhs_x2(a, m):
    ah, al = _split2(a)
    return _dot(ah, m) + _dot(al, m)


def _sigmoid(x):
    return 1.0 / (1.0 + jnp.exp(-x))


def _rms(x):
    return x * lax.rsqrt(jnp.mean(x * x, axis=-1, keepdims=True) + RMS_EPS)


def _modulate(n, gain, shift, scale, nb):
    rows = n.shape[0]
    h = (n * gain).reshape(nb, rows // nb, D_MODEL)
    return (h * (1.0 + scale) + shift).reshape(rows, D_MODEL)


def _seg_matrix(scale):
    r = lax.broadcasted_iota(jnp.int32, (LANES, LANES), 0)
    c = lax.broadcasted_iota(jnp.int32, (LANES, LANES), 1)
    return jnp.where((r < HEAD) == (c < HEAD), scale, 0.0).astype(BF16)


def _const_spec(shape):
    nd = len(shape)
    return pl.BlockSpec(shape, lambda *_: (0,) * nd)


def _cond_kernel(c_ref, w_ref, b_ref, o_ref):
    c = c_ref[...]
    cs = (c * _sigmoid(c)).astype(BF16)
    o_ref[...] = _dot(cs, w_ref[...].astype(BF16)) + b_ref[...]


def _cond_linear(c, w, b, tn=512):
    nl, _, n = w.shape
    m = c.shape[0]
    return pl.pallas_call(
        _cond_kernel,
        grid=(nl, n // tn),
        in_specs=[
            pl.BlockSpec((m, D_MODEL), lambda l, j: (0, 0)),
            pl.BlockSpec((None, D_MODEL, tn), lambda l, j: (l, 0, j)),
            pl.BlockSpec((None, 1, tn), lambda l, j: (l, 0, j)),
        ],
        out_specs=pl.BlockSpec((None, m, tn), lambda l, j: (l, 0, j)),
        out_shape=jax.ShapeDtypeStruct((nl, m, n), F32),
        compiler_params=_params(("parallel", "parallel")),
        name="cond_linear",
    )(c, w, b)


def _rwkv_proj_kernel(x_ref, mod_ref, prev_ref, ng_ref, mu_ref, wrkv_ref, w0_ref, w1_ref, w2_ref,
                      a0_ref, a1_ref, a2_ref, g1_ref, g2_ref, kk_ref, ka_ref,
                      r_ref, lw_ref, k_ref, v_ref, a_ref, b_ref, g_ref, last_ref, carry_ref):
    tm = x_ref.shape[0]

    @pl.when(pl.program_id(1) == 0)
    def _():
        carry_ref[...] = prev_ref[...]

    shift = mod_ref[:, 0:D_MODEL].reshape(1, 1, D_MODEL)
    scale = mod_ref[:, D_MODEL:2 * D_MODEL].reshape(1, 1, D_MODEL)
    h = _modulate(_rms(x_ref[...]), ng_ref[...], shift, scale, 1)
    row = lax.broadcasted_iota(jnp.int32, (tm, 1), 0)
    shifted = jnp.where(row == 0, carry_ref[...], pltpu.roll(h, 1, axis=0))
    carry_ref[...] = h[tm - 1:tm, :]
    last_ref[...] = h[tm - 1:tm, :]
    dx = shifted - h

    def mix(n):
        return (h + dx * mu_ref[n:n + 1, :]).astype(BF16)

    r = _dot(mix(0), wrkv_ref[0])
    k = _dot(mix(1), wrkv_ref[1])
    v = _dot(mix(2), wrkv_ref[2])
    ww = _dot(jnp.tanh(_dot(mix(3), w1_ref[...])).astype(BF16), w2_ref[...])
    z = -(w0_ref[...] + ww)
    softplus = jnp.maximum(z, 0.0) + jnp.log(1.0 + jnp.exp(-jnp.abs(z)))
    lw = -jnp.exp(-softplus - 0.5)
    asig = _sigmoid(a0_ref[...] + _dot(_dot(mix(4), a1_ref[...]).astype(BF16), a2_ref[...]))
    g = _dot(_sigmoid(_dot(mix(5), g1_ref[...])).astype(BF16), g2_ref[...])

    kk = k * kk_ref[...]
    kk2 = kk * kk
    seg = _seg_matrix(1.0)
    ss = jnp.concatenate(
        [_dot_lhs_x2(kk2[:, p * LANES:(p + 1) * LANES], seg) for p in range(PAIRS)], axis=1)
    kkn = kk * lax.rsqrt(jnp.maximum(ss, 1e-24))

    r_ref[...] = r.astype(r_ref.dtype)
    lw_ref[...] = lw
    k_ref[...] = (k * (1.0 + (asig - 1.0) * ka_ref[...])).astype(k_ref.dtype)
    v_ref[...] = v.astype(v_ref.dtype)
    a_ref[...] = (-kkn).astype(a_ref.dtype)
    b_ref[...] = (kkn * asig).astype(b_ref.dtype)
    g_ref[...] = g.astype(g_ref.dtype)


def _rwkv_proj(x, mod, prev_x, w, nb, s, tm):
    n = nb * s
    nt = s // tm
    row_spec = pl.BlockSpec((tm, D_MODEL), lambda b, j: (b * nt + j, 0))
    vec = _const_spec((1, D_MODEL))
    lora_w, lora_g = w["w1"].shape[1], w["g1"].shape[1]
    in_specs = [
        row_spec,
        pl.BlockSpec((None, 1, 6 * D_MODEL), lambda b, j: (b, 0, 0)),
        pl.BlockSpec((None, 1, D_MODEL), lambda b, j: (b, 0, 0)),
        vec, _const_spec((6, D_MODEL)), _const_spec((3, D_MODEL, D_MODEL)),
        vec, _const_spec((D_MODEL, lora_w)), _const_spec((lora_w, D_MODEL)),
        vec, _const_spec((D_MODEL, lora_w)), _const_spec((lora_w, D_MODEL)),
        _const_spec((D_MODEL, lora_g)), _const_spec((lora_g, D_MODEL)),
        vec, vec,
    ]
    act = lambda dt: jax.ShapeDtypeStruct((n, D_MODEL), dt)
    out_types = [act(BF16), act(F32)] + [act(BF16)] * 5
    outs = pl.pallas_call(
        _rwkv_proj_kernel,
        grid=(nb, nt),
        in_specs=in_specs,
        out_specs=[row_spec] * 7 + [pl.BlockSpec((None, 1, D_MODEL), lambda b, j: (b, 0, 0))],
        out_shape=out_types + [jax.ShapeDtypeStruct((nb, 1, D_MODEL), F32)],
        scratch_shapes=[pltpu.VMEM((1, D_MODEL), F32)],
        compiler_params=_params(("parallel", "arbitrary")),
        name="rwkv_proj",
    )(x, mod, prev_x, w["norm_g"], w["mu"], w["w_rkv"], w["w0"], w["w1"], w["w2"],
      w["a0"], w["a1"], w["a2"], w["g1"], w["g2"], w["k_k"], w["k_a"])
    return outs


def _split3(x):
    hi = x.astype(BF16)
    rem = x - hi.astype(F32)
    mid = rem.astype(BF16)
    return hi, mid, (rem - mid.astype(F32)).astype(BF16)


def _wkv_kernel(r_ref, lw_ref, k_ref, v_ref, a_ref, b_ref, g_ref, s0_ref, rk_ref, lnw_ref, lnb_ref,
                z_ref, st_ref):
    @pl.when(pl.program_id(1) == 0)
    def _():
        st_ref[...] = s0_ref[...]

    b16 = lambda t: t.astype(BF16)
    pairs = range(PAIRS)
    blk = lambda t, p: t[:, p * LANES:(p + 1) * LANES]
    rows = lambda ts: jnp.concatenate(ts, axis=0)
    lo_cols = lax.broadcasted_iota(jnp.int32, (CHUNK, LANES), 1) < HEAD
    lo_cols2 = lax.broadcasted_iota(jnp.int32, (2 * CHUNK, LANES), 1) < HEAD
    rr = lax.broadcasted_iota(jnp.int32, (LANES, LANES), 0)
    cc = lax.broadcasted_iota(jnp.int32, (LANES, LANES), 1)
    head0 = lambda t: jnp.where(lo_cols if t.shape[0] == CHUNK else lo_cols2, t, 0.0)
    head1 = lambda t: jnp.where(lo_cols if t.shape[0] == CHUNK else lo_cols2, 0.0, t)
    blockdiag = lambda q: rows([head0(q), head1(q)])

    r, lw, k, v, a, b = (t[...].astype(F32) for t in (r_ref, lw_ref, k_ref, v_ref, a_ref, b_ref))
    ti = lax.broadcasted_iota(jnp.int32, (CHUNK, CHUNK), 0)
    tj = lax.broadcasted_iota(jnp.int32, (CHUNK, CHUNK), 1)
    tri = (tj <= ti).astype(BF16)
    cum = sum(_dot(tri, t) for t in _split3(lw))
    cl = cum[CHUNK - 1:CHUNK, :]
    e_neg = jnp.exp(-cum)
    e_end = jnp.exp(cl - cum)
    e_cl = jnp.exp(cl)
    at = b16(a * jnp.exp(cum - lw))
    rt = b16(r * jnp.exp(cum))
    bt = b * e_neg
    kt = k * e_neg
    bkh = rows([b16(b * e_end), b16(k * e_end)])
    v16 = b16(v)

    t_idx = rr & (CHUNK - 1)
    j_idx = cc & (CHUNK - 1)
    keep = (j_idx < t_idx) | ((rr >= CHUNK) & (j_idx == t_idx))
    g0, g1 = [], []
    for p in pairs:
        btp, ktp = blk(bt, p), blk(kt, p)
        w = b16(rows([head0(btp), head0(ktp), head1(ktp), head1(btp)]))
        g = _dot_nt(rows([blk(at, p), blk(rt, p)]), w)
        g0.append(jnp.where(keep, g[:, :LANES], 0.0))
        g1.append(jnp.where(keep, g[:, LANES:], 0.0))

    eye2 = jnp.where((lax.broadcasted_iota(jnp.int32, (CHUNK, LANES), 1) & (CHUNK - 1))
                     == lax.broadcasted_iota(jnp.int32, (CHUNK, LANES), 0), 1.0, 0.0)
    pc = [jnp.where(lo_cols, g0[p][:CHUNK], g1[p][:CHUNK]) for p in pairs]
    tc = [eye2 + pc[p] for p in pairs]
    q = [_dot(b16(pc[p]), b16(blockdiag(pc[p]))) for p in pairs]
    for _ in range(4):
        res = [_dot(b16(rows([tc[p], q[p]])), b16(blockdiag(q[p]))) for p in pairs]
        tc = [tc[p] + res[p][:CHUNK] for p in pairs]
        q = [res[p][CHUNK:] for p in pairs]
    tc = [b16(tc[p] + _dot(b16(tc[p]), b16(blockdiag(q[p])))) for p in pairs]

    akv = []
    for p in pairs:
        w_ak = b16(jnp.where(lo_cols, g1[p][:CHUNK], g0[p][:CHUNK]))
        vp = blk(v16, p)
        akv.append(_dot(w_ak, rows([head1(vp), head0(vp)])))

    s = [st_ref[p] for p in pairs]
    s16 = [b16(s[p]) for p in pairs]
    rhs = [_dot_nt(blk(at, p), s16[p]) + akv[p] for p in pairs]
    u = [_dot(tc[p], b16(rows([head0(rhs[p]), head1(rhs[p])]))) for p in pairs]
    u16 = [b16(u[p]) for p in pairs]

    ys = []
    bd_mask = (rr < HEAD) == (cc < HEAD)
    for p in pairs:
        vp = blk(v16, p)
        rbk = b16(jnp.concatenate([g0[p][CHUNK:], g1[p][CHUNK:]], axis=1))
        uv = rows([head0(u16[p]), head0(vp), head1(vp), head1(u16[p])])
        ys.append(_dot_nt(blk(rt, p), s16[p]) + _dot(rbk, uv))
        fresh = _dot_tn(rows([u16[p], vp]), blk(bkh, p))
        st_ref[p] = s[p] * blk(e_cl, p) + jnp.where(bd_mask, fresh, 0.0)

    avg = _seg_matrix(1.0 / HEAD)
    y = rows(ys)
    yc = y - _dot_lhs_x2(y, avg)
    var = _dot(b16(yc * yc), avg)
    yn = yc * lax.rsqrt(var + GN_EPS)
    rkk = r * k * rk_ref[...]
    bonus = _dot_lhs_x2(rows([blk(rkk, p) for p in pairs]), _seg_matrix(1.0))
    for p in pairs:
        cols = slice(p * LANES, (p + 1) * LANES)
        ynp = yn[p * CHUNK:(p + 1) * CHUNK] * lnw_ref[:, cols] + lnb_ref[:, cols]
        zp = (ynp + bonus[p * CHUNK:(p + 1) * CHUNK] * v[:, cols]) * g_ref[:, cols].astype(F32)
        z_ref[:, cols] = zp.astype(z_ref.dtype)


def _wkv(r, lw, k, v, a, b, g, st0, w, nb, s):
    n = nb * s
    nc = s // CHUNK
    blk = pl.BlockSpec((CHUNK, D_MODEL), lambda bi, c: (bi * nc + c, 0))
    vec = _const_spec((1, D_MODEL))
    st_spec = pl.BlockSpec((None, PAIRS, LANES, LANES), lambda bi, c: (bi, 0, 0, 0))
    return pl.pallas_call(
        _wkv_kernel,
        grid=(nb, nc),
        in_specs=[blk] * 7 + [st_spec, vec, vec, vec],
        out_specs=[blk, st_spec],
        out_shape=[jax.ShapeDtypeStruct((n, D_MODEL), BF16),
                   jax.ShapeDtypeStruct((nb, PAIRS, LANES, LANES), F32)],
        compiler_params=_params(("parallel", "arbitrary")),
        name="wkv",
    )(r, lw, k, v, a, b, g, st0, w["r_k"], w["lnx_w"], w["lnx_b"])


def _out_proj_kernel(x_ref, z_ref, gate_ref, w_ref, bias_ref, o_ref, *, nb):
    rows = x_ref.shape[0]
    y = _dot(z_ref[...], w_ref[...]) + bias_ref[...]
    y = (y.reshape(nb, rows // nb, D_MODEL) * gate_ref[...]).reshape(rows, D_MODEL)
    o_ref[...] = x_ref[...] + y


def _out_proj(x, z, gate, w, bias, nb, s, tm):
    n = nb * s
    tb = max(tm // s, 1)
    per_seq = max(s // tm, 1)
    row_spec = pl.BlockSpec((tm, D_MODEL), lambda i: (i, 0))
    return pl.pallas_call(
        functools.partial(_out_proj_kernel, nb=tb),
        grid=(n // tm,),
        in_specs=[row_spec, row_spec,
                  pl.BlockSpec((tb, 1, D_MODEL), lambda i: (i // per_seq, 0, 0)),
                  _const_spec((D_MODEL, D_MODEL)), _const_spec((1, D_MODEL))],
        out_specs=row_spec,
        out_shape=jax.ShapeDtypeStruct((n, D_MODEL), F32),
        compiler_params=_params(("parallel",)),
        name="out_proj",
    )(x, z, gate, w, bias)


def _top2(logits):
    lane = lax.broadcasted_iota(jnp.int32, logits.shape, 1).astype(F32)
    first_of = lambda hit: jnp.min(jnp.where(hit, lane, float(LANES)), axis=-1, keepdims=True)
    lg = jnp.where(lane < N_GROUPS, logits, -jnp.inf)
    gmax = jnp.max(lg, axis=-1, keepdims=True)
    gi = first_of(lg == gmax)
    gp = 1.0 / jnp.sum(jnp.exp(lg - gmax), axis=-1, keepdims=True)
    first = ROUTE_LANE0 + gi * EXP_PER_GROUP
    le = jnp.where((lane >= first) & (lane < first + EXP_PER_GROUP), logits, -jnp.inf)
    top1 = jnp.max(le, axis=-1, keepdims=True)
    i1 = first_of(le == top1)
    le2 = jnp.where(lane == i1, -jnp.inf, le)
    top2 = jnp.max(le2, axis=-1, keepdims=True)
    i2 = first_of(le2 == top2)
    e2 = jnp.exp(top2 - top1)
    w1 = gp / (1.0 + e2)
    return i1, i2, w1, w1 * e2


def _route(logits):
    lane = lax.broadcasted_iota(jnp.int32, logits.shape, 1).astype(F32)
    i1, i2, w1, w2 = _top2(logits)
    return jnp.where(lane == i1, w1, 0.0) + jnp.where(lane == i2, w2, 0.0)


def _moe_kernel(x_ref, mod_ref, ng_ref, wr_ref, wgu_ref, wdn_ref, fg_ref, o_ref, h_scr, cw_scr, acc_scr,
                *, nb, final_norm):
    e = pl.program_id(1)
    rows = x_ref.shape[0]

    @pl.when(e == 0)
    def _():
        shift = mod_ref[:, :, 3 * D_MODEL:4 * D_MODEL]
        scale = mod_ref[:, :, 4 * D_MODEL:5 * D_MODEL]
        h = _modulate(_rms(x_ref[...]), ng_ref[...], shift, scale, nb)
        h_scr[...] = h.astype(BF16)
        cw_scr[...] = _route(_dot_x3(h, wr_ref[...]))
        acc_scr[...] = jnp.zeros_like(acc_scr)

    lane = lax.broadcasted_iota(jnp.int32, (rows, LANES), 1)
    cw = jnp.sum(jnp.where(lane == ROUTE_LANE0 + e, cw_scr[...], 0.0), axis=-1, keepdims=True)
    gu = _dot(h_scr[...], wgu_ref[...])
    gate = gu[:, :D_EXPERT]
    hid = gate * _sigmoid(gate) * gu[:, D_EXPERT:] * cw
    acc_scr[...] += _dot(hid.astype(BF16), wdn_ref[...])

    @pl.when(e == N_EXPERTS - 1)
    def _():
        g2 = mod_ref[:, :, 5 * D_MODEL:6 * D_MODEL]
        y = (acc_scr[...].reshape(nb, rows // nb, D_MODEL) * g2).reshape(rows, D_MODEL)
        out = x_ref[...] + y
        if final_norm:
            out = _rms(out) * fg_ref[...]
        o_ref[...] = out


def _moe(x, mod, norm_g, w_route, w_gu, w_down, final_g, nb, s, tm, final_norm):
    n = nb * s
    tb = max(tm // s, 1)
    per_seq = max(s // tm, 1)
    row_spec = pl.BlockSpec((tm, D_MODEL), lambda i, e: (i, 0))
    return pl.pallas_call(
        functools.partial(_moe_kernel, nb=tb, final_norm=final_norm),
        grid=(n // tm, N_EXPERTS),
        in_specs=[row_spec,
                  pl.BlockSpec((tb, 1, 6 * D_MODEL), lambda i, e: (i // per_seq, 0, 0)),
                  _const_spec((1, D_MODEL)), _const_spec((D_MODEL, LANES)),
                  pl.BlockSpec((None, D_MODEL, 2 * D_EXPERT), lambda i, e: (e, 0, 0)),
                  pl.BlockSpec((None, D_EXPERT, D_MODEL), lambda i, e: (e, 0, 0)),
                  _const_spec((1, D_MODEL))],
        out_specs=row_spec,
        out_shape=jax.ShapeDtypeStruct((n, D_MODEL), F32),
        scratch_shapes=[pltpu.VMEM((tm, D_MODEL), BF16), pltpu.VMEM((tm, LANES), F32),
                        pltpu.VMEM((tm, D_MODEL), F32)],
        compiler_params=_params(("parallel", "arbitrary")),
        name="moe",
    )(x, mod, norm_g, w_route, w_gu, w_down, final_g)


PAIRS_PER_GROUP = EXP_PER_GROUP * (EXP_PER_GROUP - 1) // 2
N_BUCKETS = N_GROUPS * PAIRS_PER_GROUP
BUCKET_TILE = 256
ROW_W = D_MODEL + LANES
META_BUCKET, META_RANK, META_W_LO, META_W_HI = 0, 1, 2, 3
ROUTER_TILE = 512
MOVE_TILE = 256
DMA_UNROLL = 8
SPARSE_MIN_TOKENS = 4096


def _moe_norm(x_ref, mod_ref, ng_ref, nb):
    shift = mod_ref[:, :, 3 * D_MODEL:4 * D_MODEL]
    scale = mod_ref[:, :, 4 * D_MODEL:5 * D_MODEL]
    return _modulate(_rms(x_ref[...]), ng_ref[...], shift, scale, nb)


def _router_kernel(x_ref, mod_ref, ng_ref, wr_ref, earlier_ref, meta_ref, cnt_ref, *, nb):
    @pl.when(pl.program_id(0) == 0)
    def _():
        cnt_ref[...] = jnp.zeros_like(cnt_ref)

    rows = x_ref.shape[0]
    h = _moe_norm(x_ref, mod_ref, ng_ref, nb)
    i1, i2, w1, w2 = _top2(_dot_x3(h, wr_ref[...]))
    lo = (jnp.minimum(i1, i2) - ROUTE_LANE0).astype(jnp.int32)
    hi = (jnp.maximum(i1, i2) - ROUTE_LANE0).astype(jnp.int32)
    first_is_lo = i1 < i2
    a = lo & (EXP_PER_GROUP - 1)
    b = hi & (EXP_PER_GROUP - 1)
    group = lo >> (EXP_PER_GROUP.bit_length() - 1)
    bucket = group * PAIRS_PER_GROUP + ((a * (2 * EXP_PER_GROUP - 1 - a)) >> 1) + (b - a - 1)

    lane = lax.broadcasted_iota(jnp.int32, (rows, LANES), 1)
    mine = lane == bucket
    onehot = jnp.where(mine, 1.0, 0.0)
    seen = cnt_ref[...]
    before = _dot(earlier_ref[...], onehot.astype(BF16)) + seen
    rank = jnp.sum(jnp.where(mine, before, 0.0), axis=-1, keepdims=True)
    cnt_ref[...] = seen + jnp.sum(onehot, axis=0, keepdims=True)

    meta = jnp.where(lane == META_BUCKET, bucket.astype(F32), 0.0)
    meta = jnp.where(lane == META_RANK, rank, meta)
    meta = jnp.where(lane == META_W_LO, jnp.where(first_is_lo, w1, w2), meta)
    meta_ref[...] = jnp.where(lane == META_W_HI, jnp.where(first_is_lo, w2, w1), meta)


def _positions_kernel(meta_ref, start_ref, pos_ref):
    meta = meta_ref[...]
    lane = lax.broadcasted_iota(jnp.int32, meta.shape, 1)
    column = lambda c: jnp.sum(jnp.where(lane == c, meta, 0.0), axis=-1, keepdims=True)
    value = jnp.where(lane == column(META_BUCKET).astype(jnp.int32), start_ref[...] + column(META_RANK), 0.0)
    ones = jnp.ones((8, LANES), BF16)
    pos = sum(_dot_nt(ones, piece) for piece in _split3(value))
    pos_ref[...] = pos[0:1, :].astype(jnp.int32)


def _dispatch_kernel(pos_ref, x_ref, mod_ref, ng_ref, meta_ref, xs_ref, buf, sem, *, nb):
    i = pl.program_id(0)
    rows = x_ref.shape[0]
    slot = i % 2
    buf[slot, :, 0:D_MODEL] = _moe_norm(x_ref, mod_ref, ng_ref, nb)
    buf[slot, :, D_MODEL:ROW_W] = meta_ref[...]

    def send(r, carry):
        pltpu.make_async_copy(buf.at[slot, pl.ds(r, 1), :],
                              xs_ref.at[pl.ds(pos_ref[i * rows + r], 1), :], sem.at[slot]).start()
        return carry

    lax.fori_loop(0, rows, send, 0, unroll=DMA_UNROLL)

    def wait_all(sl):
        pltpu.make_async_copy(buf.at[sl], xs_ref.at[pl.ds(0, rows), :], sem.at[sl]).wait()

    @pl.when(i >= 1)
    def _():
        wait_all(1 - slot)

    @pl.when(i == pl.num_programs(0) - 1)
    def _():
        wait_all(slot)


def _experts_kernel(tile_ref, bucket_ref, lo_ref, hi_ref, used_ref, xs_ref, gu_lo_ref, gu_hi_ref,
                    dn_lo_ref, dn_hi_ref, ys_ref):
    k = pl.program_id(0)

    @pl.when(k < used_ref[0])
    def _():
        rows = xs_ref.shape[0]
        x16 = xs_ref[:, 0:D_MODEL].astype(BF16)
        meta = xs_ref[:, D_MODEL:ROW_W]
        lane = lax.broadcasted_iota(jnp.int32, (rows, LANES), 1)
        column = lambda c: jnp.sum(jnp.where(lane == c, meta, 0.0), axis=-1, keepdims=True)
        member = column(META_BUCKET) == bucket_ref[k].astype(F32)

        def expert(gu_ref, dn_ref, col):
            gu = _dot(x16, gu_ref[...])
            gate = gu[:, :D_EXPERT]
            hid = gate * _sigmoid(gate) * gu[:, D_EXPERT:] * jnp.where(member, column(col), 0.0)
            return _dot(hid.astype(BF16), dn_ref[...])

        y = expert(gu_lo_ref, dn_lo_ref, META_W_LO) + expert(gu_hi_ref, dn_hi_ref, META_W_HI)
        new_tile = (k == 0) | (tile_ref[k] != tile_ref[jnp.maximum(k - 1, 0)])

        @pl.when(new_tile)
        def _():
            ys_ref[...] = y

        @pl.when(jnp.logical_not(new_tile))
        def _():
            ys_ref[...] += y


def _combine_kernel(pos_ref, x_ref, mod_ref, fg_ref, ys_ref, o_ref, buf, sem, *, nb, final_norm):
    i = pl.program_id(0)
    rows = x_ref.shape[0]
    slot = i % 2

    def fetch(step, sl):
        def one(r, carry):
            pltpu.make_async_copy(ys_ref.at[pl.ds(pos_ref[step * rows + r], 1), :],
                                  buf.at[sl, pl.ds(r, 1), :], sem.at[sl]).start()
            return carry
        lax.fori_loop(0, rows, one, 0, unroll=DMA_UNROLL)

    @pl.when(i == 0)
    def _():
        fetch(0, 0)

    pltpu.make_async_copy(ys_ref.at[pl.ds(0, rows), :], buf.at[slot], sem.at[slot]).wait()

    @pl.when(i + 1 < pl.num_programs(0))
    def _():
        fetch(i + 1, 1 - slot)

    g2 = mod_ref[:, :, 5 * D_MODEL:6 * D_MODEL]
    y = (buf[slot].reshape(nb, rows // nb, D_MODEL) * g2).reshape(rows, D_MODEL)
    out = x_ref[...] + y
    if final_norm:
        out = _rms(out) * fg_ref[...]
    o_ref[...] = out


def _bucket_experts():
    lo, hi = [], []
    for g in range(N_GROUPS):
        for a in range(EXP_PER_GROUP):
            for b in range(a + 1, EXP_PER_GROUP):
                lo.append(g * EXP_PER_GROUP + a)
                hi.append(g * EXP_PER_GROUP + b)
    return jnp.asarray(lo, jnp.int32), jnp.asarray(hi, jnp.int32)


def _moe_sparse(x, mod, norm_g, w_route, w_gu, w_down, final_g, nb, s, final_norm):
    n = nb * s
    move_params = pltpu.CompilerParams(dimension_semantics=("arbitrary",), vmem_limit_bytes=VMEM_LIMIT,
                                       disable_bounds_checks=True)

    def token_specs(tm):
        tb, per_seq = max(tm // s, 1), max(s // tm, 1)
        row = pl.BlockSpec((tm, D_MODEL), lambda i, *_: (i, 0))
        modspec = pl.BlockSpec((tb, 1, 6 * D_MODEL), lambda i, *_: (i // per_seq, 0, 0))
        meta = pl.BlockSpec((tm, LANES), lambda i, *_: (i, 0))
        return tb, row, modspec, meta

    vec = pl.BlockSpec((1, D_MODEL), lambda *_: (0, 0))
    tb, row, modspec, metaspec = token_specs(ROUTER_TILE)
    meta, counts = pl.pallas_call(
        functools.partial(_router_kernel, nb=tb),
        grid=(n // ROUTER_TILE,),
        in_specs=[row, modspec, vec, pl.BlockSpec((D_MODEL, LANES), lambda i: (0, 0)),
                  pl.BlockSpec((ROUTER_TILE, ROUTER_TILE), lambda i: (0, 0))],
        out_specs=[metaspec, pl.BlockSpec((1, LANES), lambda i: (0, 0))],
        out_shape=[jax.ShapeDtypeStruct((n, LANES), F32), jax.ShapeDtypeStruct((1, LANES), F32)],
        compiler_params=_params(("arbitrary",)),
        name="moe_router",
    )(x, mod, norm_g, w_route, jnp.tril(jnp.ones((ROUTER_TILE, ROUTER_TILE), BF16), -1))

    counts = counts[0, :N_BUCKETS].astype(jnp.int32)
    end = jnp.cumsum(counts)
    start = end - counts
    first_tile = start // BUCKET_TILE
    per_bucket = jnp.where(counts > 0, (end - 1) // BUCKET_TILE - first_tile + 1, 0)
    item_end = jnp.cumsum(per_bucket)
    used = item_end[-1:]
    n_items = n // BUCKET_TILE + N_BUCKETS
    k = jnp.minimum(jnp.arange(n_items, dtype=jnp.int32), used - 1)
    item_bucket = jnp.sum(item_end[None, :] <= k[:, None], axis=1, dtype=jnp.int32)
    item_tile = first_tile[item_bucket] + k - (item_end - per_bucket)[item_bucket]
    bucket_lo, bucket_hi = _bucket_experts()
    item_lo, item_hi = bucket_lo[item_bucket], bucket_hi[item_bucket]
    start_row = jnp.pad(start, (0, LANES - N_BUCKETS)).astype(F32)[None, :]
    pos = pl.pallas_call(
        _positions_kernel,
        grid=(n // ROUTER_TILE,),
        in_specs=[metaspec, pl.BlockSpec((1, LANES), lambda i: (0, 0))],
        out_specs=pl.BlockSpec((1, ROUTER_TILE), lambda i: (0, i)),
        out_shape=jax.ShapeDtypeStruct((1, n), jnp.int32),
        compiler_params=_params(("parallel",)),
        name="moe_positions",
    )(meta, start_row).reshape(n)

    tb, row, modspec, metaspec = token_specs(MOVE_TILE)
    any_spec = pl.BlockSpec(memory_space=pl.ANY)
    xs = pl.pallas_call(
        functools.partial(_dispatch_kernel, nb=tb),
        grid_spec=pltpu.PrefetchScalarGridSpec(
            num_scalar_prefetch=1, grid=(n // MOVE_TILE,),
            in_specs=[row, modspec, vec, metaspec], out_specs=any_spec,
            scratch_shapes=[pltpu.VMEM((2, MOVE_TILE, ROW_W), F32), pltpu.SemaphoreType.DMA((2,))]),
        out_shape=jax.ShapeDtypeStruct((n, ROW_W), F32),
        compiler_params=move_params,
        name="moe_dispatch",
    )(pos, x, mod, norm_g, meta)

    tile_of = lambda k, tile, *_: (tile[k], 0)
    ys = pl.pallas_call(
        _experts_kernel,
        grid_spec=pltpu.PrefetchScalarGridSpec(
            num_scalar_prefetch=5, grid=(n_items,),
            in_specs=[pl.BlockSpec((BUCKET_TILE, ROW_W), tile_of),
                      pl.BlockSpec((None, D_MODEL, 2 * D_EXPERT), lambda k, t, b, lo, hi, nu: (lo[k], 0, 0)),
                      pl.BlockSpec((None, D_MODEL, 2 * D_EXPERT), lambda k, t, b, lo, hi, nu: (hi[k], 0, 0)),
                      pl.BlockSpec((None, D_EXPERT, D_MODEL), lambda k, t, b, lo, hi, nu: (lo[k], 0, 0)),
                      pl.BlockSpec((None, D_EXPERT, D_MODEL), lambda k, t, b, lo, hi, nu: (hi[k], 0, 0))],
            out_specs=pl.BlockSpec((BUCKET_TILE, D_MODEL), tile_of)),
        out_shape=jax.ShapeDtypeStruct((n, D_MODEL), F32),
        compiler_params=_params(("arbitrary",)),
        name="moe_experts",
    )(item_tile, item_bucket, item_lo, item_hi, used, xs, w_gu, w_gu, w_down, w_down)

    return pl.pallas_call(
        functools.partial(_combine_kernel, nb=tb, final_norm=final_norm),
        grid_spec=pltpu.PrefetchScalarGridSpec(
            num_scalar_prefetch=1, grid=(n // MOVE_TILE,),
            in_specs=[row, modspec, vec, any_spec], out_specs=row,
            scratch_shapes=[pltpu.VMEM((2, MOVE_TILE, D_MODEL), F32), pltpu.SemaphoreType.DMA((2,))]),
        out_shape=jax.ShapeDtypeStruct((n, D_MODEL), F32),
        compiler_params=move_params,
        name="moe_combine",
    )(pos, x, mod, final_g, ys)


def _rope(x, cos, sin_lo, sin_hi):
    return x * cos + pltpu.roll(x, LANES - ROPE_DIM // 2, axis=1) * sin_lo + pltpu.roll(x, ROPE_DIM // 2, axis=1) * sin_hi


def _qkv_kernel(x_ref, mod_ref, kvmod_ref, ngq_ref, ngkv_ref, wq_ref, bq_ref, wkv_ref, bkv_ref,
                cos_ref, slo_ref, shi_ref, q_ref, k_ref, v_ref):
    n = _rms(x_ref[...])
    shift = mod_ref[:, 0:D_MODEL].reshape(1, 1, D_MODEL)
    scale = mod_ref[:, D_MODEL:2 * D_MODEL].reshape(1, 1, D_MODEL)
    hq = _modulate(n, ngq_ref[...], shift, scale, 1)
    kshift = kvmod_ref[:, 0:D_MODEL].reshape(1, 1, D_MODEL)
    kscale = kvmod_ref[:, D_MODEL:2 * D_MODEL].reshape(1, 1, D_MODEL)
    hkv = _modulate(n, ngkv_ref[...], kshift, kscale, 1)
    cos, slo, shi = cos_ref[...], slo_ref[...], shi_ref[...]
    q = _dot(hq.astype(BF16), wq_ref[...]) + bq_ref[...]
    for p in range(PAIRS):
        cols = slice(p * LANES, (p + 1) * LANES)
        q_ref[:, cols] = (_rope(q[:, cols], cos, slo, shi) * ATT_SCALE).astype(q_ref.dtype)
    kv = _dot(hkv.astype(BF16), wkv_ref[...]) + bkv_ref[...]
    for p in range(KV_W // LANES):
        cols = slice(p * LANES, (p + 1) * LANES)
        k_ref[:, cols] = _rope(kv[:, cols], cos, slo, shi)
    v_ref[...] = kv[:, KV_W:]


def _qkv_proj(x, mod, kvmod, w, tables, nb, s, tm):
    n = nb * s
    nt = s // tm
    row_spec = pl.BlockSpec((tm, D_MODEL), lambda b, j: (b * nt + j, 0))
    kv_spec = pl.BlockSpec((tm, KV_W), lambda b, j: (b * nt + j, 0))
    tab_spec = pl.BlockSpec((tm, LANES), lambda b, j: (j, 0))
    vec = _const_spec((1, D_MODEL))
    return pl.pallas_call(
        _qkv_kernel,
        grid=(nb, nt),
        in_specs=[row_spec,
                  pl.BlockSpec((None, 1, 6 * D_MODEL), lambda b, j: (b, 0, 0)),
                  pl.BlockSpec((None, 1, 2 * D_MODEL), lambda b, j: (b, 0, 0)),
                  vec, vec, _const_spec((D_MODEL, D_MODEL)), vec,
                  _const_spec((D_MODEL, 2 * KV_W)), _const_spec((1, 2 * KV_W)),
                  tab_spec, tab_spec, tab_spec],
        out_specs=[row_spec, kv_spec, kv_spec],
        out_shape=[jax.ShapeDtypeStruct((n, D_MODEL), BF16),
                   jax.ShapeDtypeStruct((n, KV_W), F32), jax.ShapeDtypeStruct((n, KV_W), F32)],
        compiler_params=_params(("parallel", "parallel")),
        name="qkv_proj",
    )(x, mod, kvmod, w["norm_gq"], w["norm_gkv"], w["w_q"], w["b_q"], w["w_kv"], w["b_kv"], *tables)


def _attn_kernel(sink_ref, q_ref, k0_ref, k1_ref, k2_ref, v0_ref, v1_ref, v2_ref, o_ref, *, banded):
    n = pl.program_id(1)
    k_all = jnp.concatenate([k0_ref[...], k1_ref[...], k2_ref[...]], axis=0)
    v_all = jnp.concatenate([v0_ref[...], v1_ref[...], v2_ref[...]], axis=0)
    lane = lax.broadcasted_iota(jnp.int32, (1, LANES), 1)
    lo = lane < HEAD
    col = lax.broadcasted_iota(jnp.int32, (1, 3 * CHUNK), 1)
    if banded:
        visible = (col >= 2 * CHUNK) | ((col >= CHUNK) & (n >= 1)) | (n >= 2)

    def halves(t, kv_head):
        blk = t[:, (kv_head // 2) * LANES:(kv_head // 2 + 1) * LANES]
        if kv_head % 2 == 0:
            t_lo = jnp.where(lo, blk, 0.0)
            t_hi = pltpu.roll(t_lo, HEAD, axis=1)
        else:
            t_hi = jnp.where(lo, 0.0, blk)
            t_lo = pltpu.roll(t_hi, HEAD, axis=1)
        return t_lo.astype(BF16), t_hi.astype(BF16)

    first_pair = lax.broadcasted_iota(jnp.int32, (2 * CHUNK, 1), 0) < CHUNK

    def probs(s, head_a, head_b):
        if banded:
            s = jnp.where(visible, s, NEG_INF)
        sink = jnp.where(first_pair, sink_ref[head_a], sink_ref[head_b])
        m = jnp.maximum(jnp.max(s, axis=-1, keepdims=True), sink)
        e = jnp.exp(s - m)
        return (e / (jnp.sum(e, axis=-1, keepdims=True) + jnp.exp(sink - m))).astype(BF16)

    scores, values = [], []
    for g in range(N_KV):
        k_lo, k_hi = halves(k_all, g)
        values.append(halves(v_all, g))
        q2 = jnp.concatenate([q_ref[:, (2 * g + j) * LANES:(2 * g + j + 1) * LANES] for j in range(2)], axis=0)
        scores.append((_dot_nt(q2, k_lo), _dot_nt(q2, k_hi)))
    for g in range(N_KV):
        o = (_dot(probs(scores[g][0], 4 * g, 4 * g + 2), values[g][0])
             + _dot(probs(scores[g][1], 4 * g + 1, 4 * g + 3), values[g][1]))
        for j in range(2):
            pair = 2 * g + j
            o_ref[:, pair * LANES:(pair + 1) * LANES] = o[j * CHUNK:(j + 1) * CHUNK].astype(o_ref.dtype)


def _attention(q, kv_arrays, kv_maps, sinks, nb, s, banded):
    n = nb * s
    nc = s // CHUNK
    q_spec = pl.BlockSpec((CHUNK, D_MODEL), lambda b, c: (b * nc + c, 0))
    kv_specs = [pl.BlockSpec((CHUNK, KV_W), m) for m in kv_maps]
    return pl.pallas_call(
        functools.partial(_attn_kernel, banded=banded),
        grid=(nb, nc),
        in_specs=[pl.BlockSpec(memory_space=pltpu.SMEM), q_spec] + kv_specs,
        out_specs=q_spec,
        out_shape=jax.ShapeDtypeStruct((n, D_MODEL), BF16),
        compiler_params=_params(("parallel", "parallel")),
        name="attention",
    )(sinks, q, *kv_arrays)


def _pad_cols(w, n):
    return jnp.pad(w, ((0, 0), (0, n - w.shape[1])))


def _pad_rows(w, n):
    return jnp.pad(w, ((0, n - w.shape[0]), (0, 0)))


def _rope_tables(pos):
    half = ROPE_DIM // 2
    inv = jnp.power(jnp.float32(ROPE_THETA), -jnp.arange(half, dtype=F32) * (2.0 / ROPE_DIM))
    ang = pos[:, None] * inv[None, :]
    cos, sin = jnp.cos(ang), jnp.sin(ang)
    rest = HEAD - ROPE_DIM
    ones = jnp.ones((pos.shape[0], rest), F32)
    zeros = jnp.zeros((pos.shape[0], rest), F32)
    z8 = jnp.zeros_like(sin)
    per_head = (jnp.concatenate([cos, cos, ones], axis=1),
                jnp.concatenate([-sin, z8, zeros], axis=1),
                jnp.concatenate([z8, sin, zeros], axis=1))
    return tuple(jnp.tile(t, (1, LANES // HEAD)) for t in per_head)


def _state_to_pairs(state):
    nb = state.shape[0]
    st = state.astype(F32).reshape(nb, PAIRS, 2, HEAD, HEAD)
    z = jnp.zeros_like(st[:, :, 0])
    top = jnp.concatenate([st[:, :, 0], z], axis=-1)
    bot = jnp.concatenate([z, st[:, :, 1]], axis=-1)
    return jnp.concatenate([top, bot], axis=-2)


def _pairs_to_state(st):
    nb = st.shape[0]
    s0 = st[:, :, :HEAD, :HEAD]
    s1 = st[:, :, HEAD:, HEAD:]
    return jnp.stack([s0, s1], axis=2).reshape(nb, 2 * PAIRS, HEAD, HEAD)


def _trunk(x, mods, kvmod, pos, prev_x, prev_wkv, past_k, past_v, w, nb, s):
    n = nb * s
    x = x.reshape(n, D_MODEL)
    mod0, mod1 = mods[0][:, None, :], mods[1][:, None, :]
    kvmod = kvmod[:, None, :]

    tm = min(s, 256)
    r, lw, k, v, a, b, g, last_x = _rwkv_proj(x, mod0, prev_x[:, None, :], w["rw"], nb, s, tm)
    z, st = _wkv(r, lw, k, v, a, b, g, _state_to_pairs(prev_wkv), w["rw"], nb, s)
    tm = min(n, 256)
    x = _out_proj(x, z, mod0[:, :, 2 * D_MODEL:3 * D_MODEL], w["rw"]["w_o"], w["rw"]["b_o"], nb, s, tm)
    def moe_layer(x, mod, l):
        args = (x, mod, w["norm_g"][l][1:2], w["moe_route"][l], w["moe_gu"][l], w["moe_down"][l], w["final_g"], nb, s)
        if n >= SPARSE_MIN_TOKENS:
            return _moe_sparse(*args, l == 1)
        return _moe(*args, min(n, 1024), l == 1)

    x = moe_layer(x, mod0, 0)

    tm = min(s, 256)
    q, k_new, v_new = _qkv_proj(x, mod1, kvmod, w["at"], _rope_tables(pos), nb, s, tm)
    nc = s // CHUNK
    if past_k is None:
        arrays = [k_new] * 3 + [v_new] * 3
        maps = [lambda bi, c, d=d: (bi * nc + jnp.maximum(c - d, 0), 0) for d in (2, 1, 0)] * 2
    else:
        pk = past_k.astype(F32).reshape(nb * 2 * CHUNK, KV_W)
        pv = past_v.astype(F32).reshape(nb * 2 * CHUNK, KV_W)
        arrays = [pk, pk, k_new, pv, pv, v_new]
        maps = [lambda bi, c: (2 * bi, 0), lambda bi, c: (2 * bi + 1, 0), lambda bi, c: (bi, 0)] * 2
    o = _attention(q, arrays, maps, w["at"]["sinks"], nb, s, past_k is None)
    tm = min(n, 256)
    x = _out_proj(x, o, mod1[:, :, 2 * D_MODEL:3 * D_MODEL], w["at"]["w_o"], w["at"]["b_o"], nb, s, tm)
    y = moe_layer(x, mod1, 1)

    return (y.reshape(nb, s, D_MODEL), last_x.reshape(1, nb, D_MODEL), _pairs_to_state(st)[None],
            k_new.reshape(nb, s, N_KV, HEAD), v_new.reshape(nb, s, N_KV, HEAD))


def kernel(x_prompt, x_sample, state_shift, state_wkv, cache_k, cache_v, c_prompt, c_sample, ada_w, ada_b, norm_g, rw_mu, rw_w_rkv, rw_w0, rw_w1, rw_w2, rw_a0, rw_a1, rw_a2, rw_g1, rw_g2, rw_k_k, rw_k_a, rw_r_k, rw_lnx_w, rw_lnx_b, rw_w_o, kv_ada_w, kv_ada_b, kv_norm_g, w_kv, b_kv, at_w_q, at_b_q, at_sinks, at_w_o, at_b_o, moe_w_group, moe_w_expert, moe_w_gu, moe_w_down, final_norm_g):
    bp, sp, _ = x_prompt.shape
    bs, ss, _ = x_sample.shape
    row = lambda t: t.reshape(1, -1).astype(F32)

    c_all = jnp.concatenate([c_prompt, c_sample], axis=0)
    mods = _cond_linear(c_all, ada_w, ada_b[:, None, :])
    kvmods = _cond_linear(c_all, kv_ada_w[None], kv_ada_b[None, None, :])[0]

    lora_pad, gate_pad = LANES, 2 * LANES
    w = {
        "norm_g": norm_g,
        "final_g": row(final_norm_g),
        "rw": {
            "norm_g": norm_g[0, 0:1], "mu": rw_mu[0], "w_rkv": rw_w_rkv[0].astype(BF16),
            "w0": row(rw_w0[0]), "w1": _pad_cols(rw_w1[0], lora_pad).astype(BF16),
            "w2": _pad_rows(rw_w2[0], lora_pad).astype(BF16),
            "a0": row(rw_a0[0]), "a1": _pad_cols(rw_a1[0], lora_pad).astype(BF16),
            "a2": _pad_rows(rw_a2[0], lora_pad).astype(BF16),
            "g1": _pad_cols(rw_g1[0], gate_pad).astype(BF16), "g2": _pad_rows(rw_g2[0], gate_pad).astype(BF16),
            "k_k": row(rw_k_k[0]), "k_a": row(rw_k_a[0]), "r_k": row(rw_r_k[0]),
            "lnx_w": row(rw_lnx_w[0]), "lnx_b": row(rw_lnx_b[0]),
            "w_o": rw_w_o[0].astype(BF16), "b_o": jnp.zeros((1, D_MODEL), F32),
        },
        "at": {
            "norm_gq": norm_g[1, 0:1], "norm_gkv": row(kv_norm_g),
            "w_q": at_w_q[0].astype(BF16), "b_q": row(at_b_q[0]),
            "w_kv": w_kv.astype(BF16), "b_kv": row(b_kv),
            "sinks": at_sinks[0].astype(F32),
            "w_o": at_w_o[0].astype(BF16), "b_o": row(at_b_o[0]),
        },
        "moe_route": jnp.pad(jnp.concatenate([moe_w_group, moe_w_expert], axis=-1),
                             ((0, 0), (0, 0), (0, LANES - N_GROUPS - N_EXPERTS))),
        "moe_gu": moe_w_gu.astype(BF16),
        "moe_down": moe_w_down.astype(BF16),
    }

    pos_p = jnp.arange(sp, dtype=F32)
    pos_s = PAST_LEN + jnp.arange(ss, dtype=F32)
    zero_x = jnp.zeros((bp, D_MODEL), x_prompt.dtype)
    zero_wkv = jnp.zeros((bp,) + state_wkv.shape[2:], state_wkv.dtype)
    y_p, p_shift, p_wkv, p_k, p_v = _trunk(x_prompt, mods[:, :bp], kvmods[:bp], pos_p, zero_x, zero_wkv,
                                           None, None, w, bp, sp)
    y_s, s_shift, s_wkv, s_k, s_v = _trunk(x_sample, mods[:, bp:], kvmods[bp:], pos_s, state_shift[0],
                                           state_wkv[0], cache_k, cache_v, w, bs, ss)
    keep = min(2 * CHUNK, sp)
    return (y_p, y_s, p_shift, p_wkv.astype(state_wkv.dtype), p_k[:, sp - keep:], p_v[:, sp - keep:],
            s_shift, s_wkv.astype(state_wkv.dtype), s_k, s_v)
```

```python
import functools
import math

import jax
import jax.numpy as jnp
from jax import lax
from jax.experimental import pallas as pl
from jax.experimental.pallas import tpu as pltpu

F32 = jnp.float32
BF16 = jnp.bfloat16

D_MODEL = 1024
LANES = 128
HEAD = 64
PAIRS = D_MODEL // LANES
CHUNK = 64
WKV_SEQS = 4
ATTN_SEQS = 2
PAST_LEN = 4096
N_KV = 4
KV_W = N_KV * HEAD
ROPE_DIM = 16
ROPE_THETA = 500000.0
ATT_SCALE = HEAD ** -0.5
N_GROUPS = 4
EXP_PER_GROUP = 8
N_EXPERTS = N_GROUPS * EXP_PER_GROUP
D_EXPERT = D_MODEL // 4
ROUTE_LANE0 = N_GROUPS
RMS_EPS = 1e-6
GN_EPS = 64e-5
NEG_INF = -1e30
VMEM_LIMIT = 56 * 1024 * 1024


def _params(sem):
    return pltpu.CompilerParams(dimension_semantics=sem, vmem_limit_bytes=VMEM_LIMIT)


def _dot(a, b):
    return jnp.dot(a, b, preferred_element_type=F32)


def _dot_nt(a, b):
    return lax.dot_general(a, b, (((1,), (1,)), ((), ())), preferred_element_type=F32)


def _dot_tn(a, b):
    return lax.dot_general(a, b, (((0,), (0,)), ((), ())), preferred_element_type=F32)


def _split2(x):
    hi = x.astype(BF16)
    lo = (x - hi.astype(F32)).astype(BF16)
    return hi, lo


def _dot_x3(a, b, dot=_dot):
    ah, al = _split2(a)
    bh, bl = _split2(b)
    return dot(ah, bh) + dot(ah, bl) + dot(al, bh)


def _dot_lhs_x2(a, m):
    ah, al = _split2(a)
    return _dot(ah, m) + _dot(al, m)


def _sigmoid(x):
    return 1.0 / (1.0 + jnp.exp(-x))


def _rms(x):
    return x * lax.rsqrt(jnp.mean(x * x, axis=-1, keepdims=True) + RMS_EPS)


def _modulate(n, gain, shift, scale, nb):
    rows = n.shape[0]
    h = (n * gain).reshape(nb, rows // nb, D_MODEL)
    return (h * (1.0 + scale) + shift).reshape(rows, D_MODEL)


def _seg_matrix(scale):
    r = lax.broadcasted_iota(jnp.int32, (LANES, LANES), 0)
    c = lax.broadcasted_iota(jnp.int32, (LANES, LANES), 1)
    return jnp.where((r < HEAD) == (c < HEAD), scale, 0.0).astype(BF16)


def _const_spec(shape):
    nd = len(shape)
    return pl.BlockSpec(shape, lambda *_: (0,) * nd)


def _cond_kernel(c_ref, w_ref, b_ref, o_ref):
    c = c_ref[...]
    cs = (c * _sigmoid(c)).astype(BF16)
    o_ref[...] = _dot(cs, w_ref[...].astype(BF16)) + b_ref[...]


def _cond_linear(c, w, b, tn=512):
    nl, _, n = w.shape
    m = c.shape[0]
    return pl.pallas_call(
        _cond_kernel,
        grid=(nl, n // tn),
        in_specs=[
            pl.BlockSpec((m, D_MODEL), lambda l, j: (0, 0)),
            pl.BlockSpec((None, D_MODEL, tn), lambda l, j: (l, 0, j)),
            pl.BlockSpec((None, 1, tn), lambda l, j: (l, 0, j)),
        ],
        out_specs=pl.BlockSpec((None, m, tn), lambda l, j: (l, 0, j)),
        out_shape=jax.ShapeDtypeStruct((nl, m, n), F32),
        compiler_params=_params(("parallel", "parallel")),
        name="cond_linear",
    )(c, w, b)


def _rwkv_proj_kernel(x_ref, mod_ref, prev_ref, ng_ref, mu_ref, wrkv_ref, w0_ref, w1_ref, w2_ref,
                      a0_ref, a1_ref, a2_ref, g1_ref, g2_ref, kk_ref, ka_ref,
                      r_ref, lw_ref, k_ref, v_ref, a_ref, b_ref, g_ref, last_ref, carry_ref):
    tm = x_ref.shape[0]

    @pl.when(pl.program_id(1) == 0)
    def _():
        carry_ref[...] = prev_ref[...]

    shift = mod_ref[:, 0:D_MODEL].reshape(1, 1, D_MODEL)
    scale = mod_ref[:, D_MODEL:2 * D_MODEL].reshape(1, 1, D_MODEL)
    h = _modulate(_rms(x_ref[...]), ng_ref[...], shift, scale, 1)
    row = lax.broadcasted_iota(jnp.int32, (tm, 1), 0)
    shifted = jnp.where(row == 0, carry_ref[...], pltpu.roll(h, 1, axis=0))
    carry_ref[...] = h[tm - 1:tm, :]
    last_ref[...] = h[tm - 1:tm, :]
    dx = shifted - h

    def mix(n):
        return (h + dx * mu_ref[n:n + 1, :]).astype(BF16)

    r = _dot(mix(0), wrkv_ref[0])
    k = _dot(mix(1), wrkv_ref[1])
    v = _dot(mix(2), wrkv_ref[2])
    ww = _dot(jnp.tanh(_dot(mix(3), w1_ref[...])).astype(BF16), w2_ref[...])
    z = -(w0_ref[...] + ww)
    softplus = jnp.maximum(z, 0.0) + jnp.log(1.0 + jnp.exp(-jnp.abs(z)))
    lw = -jnp.exp(-softplus - 0.5)
    asig = _sigmoid(a0_ref[...] + _dot(_dot(mix(4), a1_ref[...]).astype(BF16), a2_ref[...]))
    g = _dot(_sigmoid(_dot(mix(5), g1_ref[...])).astype(BF16), g2_ref[...])

    kk = k * kk_ref[...]
    kk2 = kk * kk
    seg = _seg_matrix(1.0)
    ss = jnp.concatenate(
        [_dot(kk2[:, p * LANES:(p + 1) * LANES].astype(BF16), seg) for p in range(PAIRS)], axis=1)
    kkn = kk * lax.rsqrt(jnp.maximum(ss, 1e-24))

    r_ref[...] = r.astype(r_ref.dtype)
    lw_ref[...] = lw
    k_ref[...] = (k * (1.0 + (asig - 1.0) * ka_ref[...])).astype(k_ref.dtype)
    v_ref[...] = v.astype(v_ref.dtype)
    a_ref[...] = (-kkn).astype(a_ref.dtype)
    b_ref[...] = (kkn * asig).astype(b_ref.dtype)
    g_ref[...] = g.astype(g_ref.dtype)


def _rwkv_proj(x, mod, prev_x, w, nb, s, tm):
    n = nb * s
    nt = s // tm
    row_spec = pl.BlockSpec((tm, D_MODEL), lambda b, j: (b * nt + j, 0))
    vec = _const_spec((1, D_MODEL))
    lora_w, lora_g = w["w1"].shape[1], w["g1"].shape[1]
    in_specs = [
        row_spec,
        pl.BlockSpec((None, 1, 6 * D_MODEL), lambda b, j: (b, 0, 0)),
        pl.BlockSpec((None, 1, D_MODEL), lambda b, j: (b, 0, 0)),
        vec, _const_spec((6, D_MODEL)), _const_spec((3, D_MODEL, D_MODEL)),
        vec, _const_spec((D_MODEL, lora_w)), _const_spec((lora_w, D_MODEL)),
        vec, _const_spec((D_MODEL, lora_w)), _const_spec((lora_w, D_MODEL)),
        _const_spec((D_MODEL, lora_g)), _const_spec((lora_g, D_MODEL)),
        vec, vec,
    ]
    act = lambda dt: jax.ShapeDtypeStruct((n, D_MODEL), dt)
    out_types = [act(BF16), act(F32)] + [act(BF16)] * 5
    outs = pl.pallas_call(
        _rwkv_proj_kernel,
        grid=(nb, nt),
        in_specs=in_specs,
        out_specs=[row_spec] * 7 + [pl.BlockSpec((None, 1, D_MODEL), lambda b, j: (b, 0, 0))],
        out_shape=out_types + [jax.ShapeDtypeStruct((nb, 1, D_MODEL), F32)],
        scratch_shapes=[pltpu.VMEM((1, D_MODEL), F32)],
        compiler_params=_params(("parallel", "arbitrary")),
        name="rwkv_proj",
    )(x, mod, prev_x, w["norm_g"], w["mu"], w["w_rkv"], w["w0"], w["w1"], w["w2"],
      w["a0"], w["a1"], w["a2"], w["g1"], w["g2"], w["k_k"], w["k_a"])
    return outs


def _split3(x):
    hi = x.astype(BF16)
    rem = x - hi.astype(F32)
    mid = rem.astype(BF16)
    return hi, mid, (rem - mid.astype(F32)).astype(BF16)


def _wkv_kernel(r_ref, lw_ref, k_ref, v_ref, a_ref, b_ref, g_ref, s0_ref, rk_ref, lnw_ref, lnb_ref,
                z_ref, st_ref):
    @pl.when(pl.program_id(1) == 0)
    def _():
        st_ref[...] = s0_ref[...]

    nseq = r_ref.shape[0]
    wide = lambda ref: jnp.concatenate([ref[i] for i in range(nseq)], axis=1)
    per_seq = lambda ref: jnp.concatenate([ref[...]] * nseq, axis=1)
    b16 = lambda t: t.astype(BF16)
    pairs = range(nseq * PAIRS)
    blk = lambda t, p: t[:, p * LANES:(p + 1) * LANES]
    rows = lambda ts: jnp.concatenate(ts, axis=0)
    lo_cols = lax.broadcasted_iota(jnp.int32, (CHUNK, LANES), 1) < HEAD
    lo_cols2 = lax.broadcasted_iota(jnp.int32, (2 * CHUNK, LANES), 1) < HEAD
    rr = lax.broadcasted_iota(jnp.int32, (LANES, LANES), 0)
    cc = lax.broadcasted_iota(jnp.int32, (LANES, LANES), 1)
    head0 = lambda t: jnp.where(lo_cols if t.shape[0] == CHUNK else lo_cols2, t, 0.0)
    head1 = lambda t: jnp.where(lo_cols if t.shape[0] == CHUNK else lo_cols2, 0.0, t)
    blockdiag = lambda q: rows([head0(q), head1(q)])

    r, lw, k, v, a, b = (wide(t).astype(F32) for t in (r_ref, lw_ref, k_ref, v_ref, a_ref, b_ref))
    ti = lax.broadcasted_iota(jnp.int32, (CHUNK, CHUNK), 0)
    tj = lax.broadcasted_iota(jnp.int32, (CHUNK, CHUNK), 1)
    tri = (tj <= ti).astype(BF16)
    cum = sum(_dot(tri, t) for t in _split3(lw))
    cl = cum[CHUNK - 1:CHUNK, :]
    e_neg = jnp.exp(-cum)
    e_end = jnp.exp(cl - cum)
    e_cl = jnp.exp(cl)
    at = b16(a * jnp.exp(cum - lw))
    rt = b16(r * jnp.exp(cum))
    bt = b * e_neg
    kt = k * e_neg
    bkh = rows([b16(b * e_end), b16(k * e_end)])
    v16 = b16(v)

    t_idx = rr & (CHUNK - 1)
    j_idx = cc & (CHUNK - 1)
    keep = (j_idx < t_idx) | ((rr >= CHUNK) & (j_idx == t_idx))
    g0, g1 = [], []
    for p in pairs:
        btp, ktp = blk(bt, p), blk(kt, p)
        w = b16(rows([head0(btp), head0(ktp), head1(ktp), head1(btp)]))
        g = _dot_nt(rows([blk(at, p), blk(rt, p)]), w)
        g0.append(jnp.where(keep, g[:, :LANES], 0.0))
        g1.append(jnp.where(keep, g[:, LANES:], 0.0))

    eye2 = jnp.where((lax.broadcasted_iota(jnp.int32, (CHUNK, LANES), 1) & (CHUNK - 1))
                     == lax.broadcasted_iota(jnp.int32, (CHUNK, LANES), 0), 1.0, 0.0)
    pc = [jnp.where(lo_cols, g0[p][:CHUNK], g1[p][:CHUNK]) for p in pairs]
    tc = [eye2 + pc[p] for p in pairs]
    q = [_dot(b16(pc[p]), b16(blockdiag(pc[p]))) for p in pairs]
    for _ in range(4):
        res = [_dot(b16(rows([tc[p], q[p]])), b16(blockdiag(q[p]))) for p in pairs]
        tc = [tc[p] + res[p][:CHUNK] for p in pairs]
        q = [res[p][CHUNK:] for p in pairs]
    tc = [b16(tc[p] + _dot(b16(tc[p]), b16(blockdiag(q[p])))) for p in pairs]

    akv = []
    for p in pairs:
        w_ak = b16(jnp.where(lo_cols, g1[p][:CHUNK], g0[p][:CHUNK]))
        vp = blk(v16, p)
        akv.append(_dot(w_ak, rows([head1(vp), head0(vp)])))

    s = [st_ref[p // PAIRS, p % PAIRS] for p in pairs]
    s16 = [b16(s[p]) for p in pairs]
    rhs = [_dot_nt(blk(at, p), s16[p]) + akv[p] for p in pairs]
    u = [_dot(tc[p], b16(rows([head0(rhs[p]), head1(rhs[p])]))) for p in pairs]
    u16 = [b16(u[p]) for p in pairs]

    ys = []
    bd_mask = (rr < HEAD) == (cc < HEAD)
    for p in pairs:
        vp = blk(v16, p)
        rbk = b16(jnp.concatenate([g0[p][CHUNK:], g1[p][CHUNK:]], axis=1))
        uv = rows([head0(u16[p]), head0(vp), head1(vp), head1(u16[p])])
        ys.append(_dot_nt(blk(rt, p), s16[p]) + _dot(rbk, uv))
        fresh = _dot_tn(rows([u16[p], vp]), blk(bkh, p))
        st_ref[p // PAIRS, p % PAIRS] = s[p] * blk(e_cl, p) + jnp.where(bd_mask, fresh, 0.0)

    avg = _seg_matrix(1.0 / HEAD)
    y = rows(ys)
    yc = y - _dot_lhs_x2(y, avg)
    var = _dot(b16(yc * yc), avg)
    yn = yc * lax.rsqrt(var + GN_EPS)
    rkk = r * k * per_seq(rk_ref)
    bonus = _dot_lhs_x2(rows([blk(rkk, p) for p in pairs]), _seg_matrix(1.0))
    for p in pairs:
        seq, cols = p // PAIRS, slice((p % PAIRS) * LANES, (p % PAIRS + 1) * LANES)
        ynp = yn[p * CHUNK:(p + 1) * CHUNK] * lnw_ref[:, cols] + lnb_ref[:, cols]
        zp = (ynp + bonus[p * CHUNK:(p + 1) * CHUNK] * blk(v, p)) * g_ref[seq, :, cols].astype(F32)
        z_ref[seq, :, cols] = zp.astype(z_ref.dtype)


def _wkv(r, lw, k, v, a, b, g, st0, w, nb, s):
    nseq = math.gcd(nb, WKV_SEQS)
    blk = pl.BlockSpec((nseq, CHUNK, D_MODEL), lambda bi, c: (bi, c, 0))
    vec = _const_spec((1, D_MODEL))
    st_spec = pl.BlockSpec((nseq, PAIRS, LANES, LANES), lambda bi, c: (bi, 0, 0, 0))
    return pl.pallas_call(
        _wkv_kernel,
        grid=(nb // nseq, s // CHUNK),
        in_specs=[blk] * 7 + [st_spec, vec, vec, vec],
        out_specs=[blk, st_spec],
        out_shape=[jax.ShapeDtypeStruct((nb, s, D_MODEL), BF16),
                   jax.ShapeDtypeStruct((nb, PAIRS, LANES, LANES), F32)],
        compiler_params=_params(("parallel", "arbitrary")),
        name="wkv",
    )(r, lw, k, v, a, b, g, st0, w["r_k"], w["lnx_w"], w["lnx_b"])


def _out_proj_kernel(x_ref, z_ref, gate_ref, w_ref, bias_ref, o_ref, *, nb):
    rows = x_ref.shape[0]
    y = _dot(z_ref[...], w_ref[...]) + bias_ref[...]
    y = (y.reshape(nb, rows // nb, D_MODEL) * gate_ref[...]).reshape(rows, D_MODEL)
    o_ref[...] = x_ref[...] + y


def _out_proj(x, z, gate, w, bias, nb, s, tm):
    n = nb * s
    tb = max(tm // s, 1)
    per_seq = max(s // tm, 1)
    row_spec = pl.BlockSpec((tm, D_MODEL), lambda i: (i, 0))
    return pl.pallas_call(
        functools.partial(_out_proj_kernel, nb=tb),
        grid=(n // tm,),
        in_specs=[row_spec, row_spec,
                  pl.BlockSpec((tb, 1, D_MODEL), lambda i: (i // per_seq, 0, 0)),
                  _const_spec((D_MODEL, D_MODEL)), _const_spec((1, D_MODEL))],
        out_specs=row_spec,
        out_shape=jax.ShapeDtypeStruct((n, D_MODEL), F32),
        compiler_params=_params(("parallel",)),
        name="out_proj",
    )(x, z, gate, w, bias)


def _top2(logits):
    lane = lax.broadcasted_iota(jnp.int32, logits.shape, 1).astype(F32)
    first_of = lambda hit: jnp.min(jnp.where(hit, lane, float(LANES)), axis=-1, keepdims=True)
    lg = jnp.where(lane < N_GROUPS, logits, -jnp.inf)
    gmax = jnp.max(lg, axis=-1, keepdims=True)
    gi = first_of(lg == gmax)
    gp = 1.0 / jnp.sum(jnp.exp(lg - gmax), axis=-1, keepdims=True)
    first = ROUTE_LANE0 + gi * EXP_PER_GROUP
    le = jnp.where((lane >= first) & (lane < first + EXP_PER_GROUP), logits, -jnp.inf)
    top1 = jnp.max(le, axis=-1, keepdims=True)
    i1 = first_of(le == top1)
    le2 = jnp.where(lane == i1, -jnp.inf, le)
    top2 = jnp.max(le2, axis=-1, keepdims=True)
    i2 = first_of(le2 == top2)
    e2 = jnp.exp(top2 - top1)
    w1 = gp / (1.0 + e2)
    return i1, i2, w1, w1 * e2


def _route(logits):
    lane = lax.broadcasted_iota(jnp.int32, logits.shape, 1).astype(F32)
    i1, i2, w1, w2 = _top2(logits)
    return jnp.where(lane == i1, w1, 0.0) + jnp.where(lane == i2, w2, 0.0)


def _moe_kernel(x_ref, mod_ref, ng_ref, wr_ref, wgu_ref, wdn_ref, fg_ref, o_ref, h_scr, cw_scr, acc_scr,
                *, nb, final_norm):
    e = pl.program_id(1)
    rows = x_ref.shape[0]

    @pl.when(e == 0)
    def _():
        shift = mod_ref[:, :, 3 * D_MODEL:4 * D_MODEL]
        scale = mod_ref[:, :, 4 * D_MODEL:5 * D_MODEL]
        h = _modulate(_rms(x_ref[...]), ng_ref[...], shift, scale, nb)
        h_scr[...] = h.astype(BF16)
        cw_scr[...] = _route(_dot_x3(h, wr_ref[...]))
        acc_scr[...] = jnp.zeros_like(acc_scr)

    lane = lax.broadcasted_iota(jnp.int32, (rows, LANES), 1)
    cw = jnp.sum(jnp.where(lane == ROUTE_LANE0 + e, cw_scr[...], 0.0), axis=-1, keepdims=True)
    gu = _dot(h_scr[...], wgu_ref[...])
    gate = gu[:, :D_EXPERT]
    hid = gate * _sigmoid(gate) * gu[:, D_EXPERT:] * cw
    acc_scr[...] += _dot(hid.astype(BF16), wdn_ref[...])

    @pl.when(e == N_EXPERTS - 1)
    def _():
        g2 = mod_ref[:, :, 5 * D_MODEL:6 * D_MODEL]
        y = (acc_scr[...].reshape(nb, rows // nb, D_MODEL) * g2).reshape(rows, D_MODEL)
        out = x_ref[...] + y
        if final_norm:
            out = _rms(out) * fg_ref[...]
        o_ref[...] = out


def _moe(x, mod, norm_g, w_route, w_gu, w_down, final_g, nb, s, tm, final_norm):
    n = nb * s
    tb = max(tm // s, 1)
    per_seq = max(s // tm, 1)
    row_spec = pl.BlockSpec((tm, D_MODEL), lambda i, e: (i, 0))
    return pl.pallas_call(
        functools.partial(_moe_kernel, nb=tb, final_norm=final_norm),
        grid=(n // tm, N_EXPERTS),
        in_specs=[row_spec,
                  pl.BlockSpec((tb, 1, 6 * D_MODEL), lambda i, e: (i // per_seq, 0, 0)),
                  _const_spec((1, D_MODEL)), _const_spec((D_MODEL, LANES)),
                  pl.BlockSpec((None, D_MODEL, 2 * D_EXPERT), lambda i, e: (e, 0, 0)),
                  pl.BlockSpec((None, D_EXPERT, D_MODEL), lambda i, e: (e, 0, 0)),
                  _const_spec((1, D_MODEL))],
        out_specs=row_spec,
        out_shape=jax.ShapeDtypeStruct((n, D_MODEL), F32),
        scratch_shapes=[pltpu.VMEM((tm, D_MODEL), BF16), pltpu.VMEM((tm, LANES), F32),
                        pltpu.VMEM((tm, D_MODEL), F32)],
        compiler_params=_params(("parallel", "arbitrary")),
        name="moe",
    )(x, mod, norm_g, w_route, w_gu, w_down, final_g)


PAIRS_PER_GROUP = EXP_PER_GROUP * (EXP_PER_GROUP - 1) // 2
N_BUCKETS = N_GROUPS * PAIRS_PER_GROUP
BUCKET_TILE = 256
ROW_W = D_MODEL + LANES
META_BUCKET, META_RANK, META_W_LO, META_W_HI = 0, 1, 2, 3
ROUTER_TILE = 512
MOVE_TILE = 256
DMA_UNROLL = 8
SPARSE_MIN_TOKENS = 4096


def _moe_norm(x_ref, mod_ref, ng_ref, nb):
    shift = mod_ref[:, :, 3 * D_MODEL:4 * D_MODEL]
    scale = mod_ref[:, :, 4 * D_MODEL:5 * D_MODEL]
    return _modulate(_rms(x_ref[...]), ng_ref[...], shift, scale, nb)


def _router_kernel(x_ref, mod_ref, ng_ref, wr_ref, earlier_ref, meta_ref, cnt_ref, *, nb):
    @pl.when(pl.program_id(0) == 0)
    def _():
        cnt_ref[...] = jnp.zeros_like(cnt_ref)

    rows = x_ref.shape[0]
    h = _moe_norm(x_ref, mod_ref, ng_ref, nb)
    i1, i2, w1, w2 = _top2(_dot_x3(h, wr_ref[...]))
    lo = (jnp.minimum(i1, i2) - ROUTE_LANE0).astype(jnp.int32)
    hi = (jnp.maximum(i1, i2) - ROUTE_LANE0).astype(jnp.int32)
    first_is_lo = i1 < i2
    a = lo & (EXP_PER_GROUP - 1)
    b = hi & (EXP_PER_GROUP - 1)
    group = lo >> (EXP_PER_GROUP.bit_length() - 1)
    bucket = group * PAIRS_PER_GROUP + ((a * (2 * EXP_PER_GROUP - 1 - a)) >> 1) + (b - a - 1)

    lane = lax.broadcasted_iota(jnp.int32, (rows, LANES), 1)
    mine = lane == bucket
    onehot = jnp.where(mine, 1.0, 0.0)
    seen = cnt_ref[...]
    before = _dot(earlier_ref[...], onehot.astype(BF16)) + seen
    rank = jnp.sum(jnp.where(mine, before, 0.0), axis=-1, keepdims=True)
    cnt_ref[...] = seen + jnp.sum(onehot, axis=0, keepdims=True)

    meta = jnp.where(lane == META_BUCKET, bucket.astype(F32), 0.0)
    meta = jnp.where(lane == META_RANK, rank, meta)
    meta = jnp.where(lane == META_W_LO, jnp.where(first_is_lo, w1, w2), meta)
    meta_ref[...] = jnp.where(lane == META_W_HI, jnp.where(first_is_lo, w2, w1), meta)


def _positions_kernel(meta_ref, start_ref, pos_ref):
    meta = meta_ref[...]
    lane = lax.broadcasted_iota(jnp.int32, meta.shape, 1)
    column = lambda c: jnp.sum(jnp.where(lane == c, meta, 0.0), axis=-1, keepdims=True)
    value = jnp.where(lane == column(META_BUCKET).astype(jnp.int32), start_ref[...] + column(META_RANK), 0.0)
    ones = jnp.ones((8, LANES), BF16)
    pos = sum(_dot_nt(ones, piece) for piece in _split3(value))
    pos_ref[...] = pos[0:1, :].astype(jnp.int32)


def _dispatch_kernel(pos_ref, x_ref, mod_ref, ng_ref, meta_ref, xs_ref, buf, sem, *, nb):
    i = pl.program_id(0)
    rows = x_ref.shape[0]
    slot = i % 2
    buf[slot, :, 0:D_MODEL] = _moe_norm(x_ref, mod_ref, ng_ref, nb)
    buf[slot, :, D_MODEL:ROW_W] = meta_ref[...]

    def send(r, carry):
        pltpu.make_async_copy(buf.at[slot, pl.ds(r, 1), :],
                              xs_ref.at[pl.ds(pos_ref[i * rows + r], 1), :], sem.at[slot]).start()
        return carry

    lax.fori_loop(0, rows, send, 0, unroll=DMA_UNROLL)

    def wait_all(sl):
        pltpu.make_async_copy(buf.at[sl], xs_ref.at[pl.ds(0, rows), :], sem.at[sl]).wait()

    @pl.when(i >= 1)
    def _():
        wait_all(1 - slot)

    @pl.when(i == pl.num_programs(0) - 1)
    def _():
        wait_all(slot)


def _experts_kernel(tile_ref, bucket_ref, lo_ref, hi_ref, used_ref, xs_ref, gu_lo_ref, gu_hi_ref,
                    dn_lo_ref, dn_hi_ref, ys_ref):
    k = pl.program_id(0)

    @pl.when(k < used_ref[0])
    def _():
        rows = xs_ref.shape[0]
        x16 = xs_ref[:, 0:D_MODEL].astype(BF16)
        meta = xs_ref[:, D_MODEL:ROW_W]
        lane = lax.broadcasted_iota(jnp.int32, (rows, LANES), 1)
        column = lambda c: jnp.sum(jnp.where(lane == c, meta, 0.0), axis=-1, keepdims=True)
        member = column(META_BUCKET) == bucket_ref[k].astype(F32)

        def expert(gu_ref, dn_ref, col):
            gu = _dot(x16, gu_ref[...])
            gate = gu[:, :D_EXPERT]
            hid = gate * _sigmoid(gate) * gu[:, D_EXPERT:] * jnp.where(member, column(col), 0.0)
            return _dot(hid.astype(BF16), dn_ref[...])

        y = expert(gu_lo_ref, dn_lo_ref, META_W_LO) + expert(gu_hi_ref, dn_hi_ref, META_W_HI)
        new_tile = (k == 0) | (tile_ref[k] != tile_ref[jnp.maximum(k - 1, 0)])

        @pl.when(new_tile)
        def _():
            ys_ref[...] = y

        @pl.when(jnp.logical_not(new_tile))
        def _():
            ys_ref[...] += y


def _combine_kernel(pos_ref, x_ref, mod_ref, fg_ref, ys_ref, o_ref, buf, sem, *, nb, final_norm):
    i = pl.program_id(0)
    rows = x_ref.shape[0]
    slot = i % 2

    def fetch(step, sl):
        def one(r, carry):
            pltpu.make_async_copy(ys_ref.at[pl.ds(pos_ref[step * rows + r], 1), :],
                                  buf.at[sl, pl.ds(r, 1), :], sem.at[sl]).start()
            return carry
        lax.fori_loop(0, rows, one, 0, unroll=DMA_UNROLL)

    @pl.when(i == 0)
    def _():
        fetch(0, 0)

    pltpu.make_async_copy(ys_ref.at[pl.ds(0, rows), :], buf.at[slot], sem.at[slot]).wait()

    @pl.when(i + 1 < pl.num_programs(0))
    def _():
        fetch(i + 1, 1 - slot)

    g2 = mod_ref[:, :, 5 * D_MODEL:6 * D_MODEL]
    y = (buf[slot].reshape(nb, rows // nb, D_MODEL) * g2).reshape(rows, D_MODEL)
    out = x_ref[...] + y
    if final_norm:
        out = _rms(out) * fg_ref[...]
    o_ref[...] = out


def _bucket_experts():
    lo, hi = [], []
    for g in range(N_GROUPS):
        for a in range(EXP_PER_GROUP):
            for b in range(a + 1, EXP_PER_GROUP):
                lo.append(g * EXP_PER_GROUP + a)
                hi.append(g * EXP_PER_GROUP + b)
    return jnp.asarray(lo, jnp.int32), jnp.asarray(hi, jnp.int32)


def _moe_sparse(x, mod, norm_g, w_route, w_gu, w_down, final_g, nb, s, final_norm):
    n = nb * s
    move_params = pltpu.CompilerParams(dimension_semantics=("arbitrary",), vmem_limit_bytes=VMEM_LIMIT,
                                       disable_bounds_checks=True)

    def token_specs(tm):
        tb, per_seq = max(tm // s, 1), max(s // tm, 1)
        row = pl.BlockSpec((tm, D_MODEL), lambda i, *_: (i, 0))
        modspec = pl.BlockSpec((tb, 1, 6 * D_MODEL), lambda i, *_: (i // per_seq, 0, 0))
        meta = pl.BlockSpec((tm, LANES), lambda i, *_: (i, 0))
        return tb, row, modspec, meta

    vec = pl.BlockSpec((1, D_MODEL), lambda *_: (0, 0))
    tb, row, modspec, metaspec = token_specs(ROUTER_TILE)
    meta, counts = pl.pallas_call(
        functools.partial(_router_kernel, nb=tb),
        grid=(n // ROUTER_TILE,),
        in_specs=[row, modspec, vec, pl.BlockSpec((D_MODEL, LANES), lambda i: (0, 0)),
                  pl.BlockSpec((ROUTER_TILE, ROUTER_TILE), lambda i: (0, 0))],
        out_specs=[metaspec, pl.BlockSpec((1, LANES), lambda i: (0, 0))],
        out_shape=[jax.ShapeDtypeStruct((n, LANES), F32), jax.ShapeDtypeStruct((1, LANES), F32)],
        compiler_params=_params(("arbitrary",)),
        name="moe_router",
    )(x, mod, norm_g, w_route, jnp.tril(jnp.ones((ROUTER_TILE, ROUTER_TILE), BF16), -1))

    counts = counts[0, :N_BUCKETS].astype(jnp.int32)
    end = jnp.cumsum(counts)
    start = end - counts
    first_tile = start // BUCKET_TILE
    per_bucket = jnp.where(counts > 0, (end - 1) // BUCKET_TILE - first_tile + 1, 0)
    item_end = jnp.cumsum(per_bucket)
    used = item_end[-1:]
    n_items = n // BUCKET_TILE + N_BUCKETS
    k = jnp.minimum(jnp.arange(n_items, dtype=jnp.int32), used - 1)
    item_bucket = jnp.sum(item_end[None, :] <= k[:, None], axis=1, dtype=jnp.int32)
    item_tile = first_tile[item_bucket] + k - (item_end - per_bucket)[item_bucket]
    bucket_lo, bucket_hi = _bucket_experts()
    item_lo, item_hi = bucket_lo[item_bucket], bucket_hi[item_bucket]
    start_row = jnp.pad(start, (0, LANES - N_BUCKETS)).astype(F32)[None, :]
    pos = pl.pallas_call(
        _positions_kernel,
        grid=(n // ROUTER_TILE,),
        in_specs=[metaspec, pl.BlockSpec((1, LANES), lambda i: (0, 0))],
        out_specs=pl.BlockSpec((1, ROUTER_TILE), lambda i: (0, i)),
        out_shape=jax.ShapeDtypeStruct((1, n), jnp.int32),
        compiler_params=_params(("parallel",)),
        name="moe_positions",
    )(meta, start_row).reshape(n)

    tb, row, modspec, metaspec = token_specs(MOVE_TILE)
    any_spec = pl.BlockSpec(memory_space=pl.ANY)
    xs = pl.pallas_call(
        functools.partial(_dispatch_kernel, nb=tb),
        grid_spec=pltpu.PrefetchScalarGridSpec(
            num_scalar_prefetch=1, grid=(n // MOVE_TILE,),
            in_specs=[row, modspec, vec, metaspec], out_specs=any_spec,
            scratch_shapes=[pltpu.VMEM((2, MOVE_TILE, ROW_W), F32), pltpu.SemaphoreType.DMA((2,))]),
        out_shape=jax.ShapeDtypeStruct((n, ROW_W), F32),
        compiler_params=move_params,
        name="moe_dispatch",
    )(pos, x, mod, norm_g, meta)

    tile_of = lambda k, tile, *_: (tile[k], 0)
    ys = pl.pallas_call(
        _experts_kernel,
        grid_spec=pltpu.PrefetchScalarGridSpec(
            num_scalar_prefetch=5, grid=(n_items,),
            in_specs=[pl.BlockSpec((BUCKET_TILE, ROW_W), tile_of),
                      pl.BlockSpec((None, D_MODEL, 2 * D_EXPERT), lambda k, t, b, lo, hi, nu: (lo[k], 0, 0)),
                      pl.BlockSpec((None, D_MODEL, 2 * D_EXPERT), lambda k, t, b, lo, hi, nu: (hi[k], 0, 0)),
                      pl.BlockSpec((None, D_EXPERT, D_MODEL), lambda k, t, b, lo, hi, nu: (lo[k], 0, 0)),
                      pl.BlockSpec((None, D_EXPERT, D_MODEL), lambda k, t, b, lo, hi, nu: (hi[k], 0, 0))],
            out_specs=pl.BlockSpec((BUCKET_TILE, D_MODEL), tile_of)),
        out_shape=jax.ShapeDtypeStruct((n, D_MODEL), F32),
        compiler_params=_params(("arbitrary",)),
        name="moe_experts",
    )(item_tile, item_bucket, item_lo, item_hi, used, xs, w_gu, w_gu, w_down, w_down)

    return pl.pallas_call(
        functools.partial(_combine_kernel, nb=tb, final_norm=final_norm),
        grid_spec=pltpu.PrefetchScalarGridSpec(
            num_scalar_prefetch=1, grid=(n // MOVE_TILE,),
            in_specs=[row, modspec, vec, any_spec], out_specs=row,
            scratch_shapes=[pltpu.VMEM((2, MOVE_TILE, D_MODEL), F32), pltpu.SemaphoreType.DMA((2,))]),
        out_shape=jax.ShapeDtypeStruct((n, D_MODEL), F32),
        compiler_params=move_params,
        name="moe_combine",
    )(pos, x, mod, final_g, ys)


def _rope(x, cos, sin_lo, sin_hi):
    return x * cos + pltpu.roll(x, LANES - ROPE_DIM // 2, axis=1) * sin_lo + pltpu.roll(x, ROPE_DIM // 2, axis=1) * sin_hi


def _qkv_kernel(x_ref, mod_ref, kvmod_ref, ngq_ref, ngkv_ref, wq_ref, bq_ref, wkv_ref, bkv_ref,
                cos_ref, slo_ref, shi_ref, q_ref, k_ref, v_ref):
    n = _rms(x_ref[...])
    shift = mod_ref[:, 0:D_MODEL].reshape(1, 1, D_MODEL)
    scale = mod_ref[:, D_MODEL:2 * D_MODEL].reshape(1, 1, D_MODEL)
    hq = _modulate(n, ngq_ref[...], shift, scale, 1)
    kshift = kvmod_ref[:, 0:D_MODEL].reshape(1, 1, D_MODEL)
    kscale = kvmod_ref[:, D_MODEL:2 * D_MODEL].reshape(1, 1, D_MODEL)
    hkv = _modulate(n, ngkv_ref[...], kshift, kscale, 1)
    cos, slo, shi = cos_ref[...], slo_ref[...], shi_ref[...]
    q = _dot(hq.astype(BF16), wq_ref[...]) + bq_ref[...]
    for p in range(PAIRS):
        cols = slice(p * LANES, (p + 1) * LANES)
        q_ref[:, cols] = (_rope(q[:, cols], cos, slo, shi) * ATT_SCALE).astype(q_ref.dtype)
    kv = _dot(hkv.astype(BF16), wkv_ref[...]) + bkv_ref[...]
    for p in range(KV_W // LANES):
        cols = slice(p * LANES, (p + 1) * LANES)
        k_ref[:, cols] = _rope(kv[:, cols], cos, slo, shi)
    v_ref[...] = kv[:, KV_W:]


def _qkv_proj(x, mod, kvmod, w, tables, nb, s, tm):
    n = nb * s
    nt = s // tm
    row_spec = pl.BlockSpec((tm, D_MODEL), lambda b, j: (b * nt + j, 0))
    kv_spec = pl.BlockSpec((tm, KV_W), lambda b, j: (b * nt + j, 0))
    tab_spec = pl.BlockSpec((tm, LANES), lambda b, j: (j, 0))
    vec = _const_spec((1, D_MODEL))
    return pl.pallas_call(
        _qkv_kernel,
        grid=(nb, nt),
        in_specs=[row_spec,
                  pl.BlockSpec((None, 1, 6 * D_MODEL), lambda b, j: (b, 0, 0)),
                  pl.BlockSpec((None, 1, 2 * D_MODEL), lambda b, j: (b, 0, 0)),
                  vec, vec, _const_spec((D_MODEL, D_MODEL)), vec,
                  _const_spec((D_MODEL, 2 * KV_W)), _const_spec((1, 2 * KV_W)),
                  tab_spec, tab_spec, tab_spec],
        out_specs=[row_spec, kv_spec, kv_spec],
        out_shape=[jax.ShapeDtypeStruct((n, D_MODEL), BF16),
                   jax.ShapeDtypeStruct((n, KV_W), F32), jax.ShapeDtypeStruct((n, KV_W), F32)],
        compiler_params=_params(("parallel", "parallel")),
        name="qkv_proj",
    )(x, mod, kvmod, w["norm_gq"], w["norm_gkv"], w["w_q"], w["b_q"], w["w_kv"], w["b_kv"], *tables)


def _attn_kernel(sink_ref, q_ref, k0_ref, k1_ref, k2_ref, v0_ref, v1_ref, v2_ref, o_ref, *, banded):
    n = pl.program_id(1)
    seqs = range(q_ref.shape[0])
    k_all = [jnp.concatenate([k0_ref[i], k1_ref[i], k2_ref[i]], axis=0) for i in seqs]
    v_all = [jnp.concatenate([v0_ref[i], v1_ref[i], v2_ref[i]], axis=0) for i in seqs]
    lane = lax.broadcasted_iota(jnp.int32, (1, LANES), 1)
    lo = lane < HEAD
    col = lax.broadcasted_iota(jnp.int32, (1, 3 * CHUNK), 1)
    if banded:
        visible = (col >= 2 * CHUNK) | ((col >= CHUNK) & (n >= 1)) | (n >= 2)

    def halves(t, kv_head):
        blk = t[:, (kv_head // 2) * LANES:(kv_head // 2 + 1) * LANES]
        if kv_head % 2 == 0:
            t_lo = jnp.where(lo, blk, 0.0)
            t_hi = pltpu.roll(t_lo, HEAD, axis=1)
        else:
            t_hi = jnp.where(lo, 0.0, blk)
            t_lo = pltpu.roll(t_hi, HEAD, axis=1)
        return t_lo.astype(BF16), t_hi.astype(BF16)

    first_pair = lax.broadcasted_iota(jnp.int32, (2 * CHUNK, 1), 0) < CHUNK

    def probs(s, head_a, head_b):
        if banded:
            s = jnp.where(visible, s, NEG_INF)
        sink = jnp.where(first_pair, sink_ref[head_a], sink_ref[head_b])
        m = jnp.maximum(jnp.max(s, axis=-1, keepdims=True), sink)
        e = jnp.exp(s - m)
        return (e / (jnp.sum(e, axis=-1, keepdims=True) + jnp.exp(sink - m))).astype(BF16)

    units = [(i, g) for i in seqs for g in range(N_KV)]
    scores, values = [], []
    for i, g in units:
        k_lo, k_hi = halves(k_all[i], g)
        values.append(halves(v_all[i], g))
        q2 = jnp.concatenate([q_ref[i, :, (2 * g + j) * LANES:(2 * g + j + 1) * LANES] for j in range(2)], axis=0)
        scores.append((_dot_nt(q2, k_lo), _dot_nt(q2, k_hi)))
    for (i, g), (s_lo, s_hi), (v_lo, v_hi) in zip(units, scores, values):
        o = _dot(probs(s_lo, 4 * g, 4 * g + 2), v_lo) + _dot(probs(s_hi, 4 * g + 1, 4 * g + 3), v_hi)
        for j in range(2):
            pair = 2 * g + j
            o_ref[i, :, pair * LANES:(pair + 1) * LANES] = o[j * CHUNK:(j + 1) * CHUNK].astype(o_ref.dtype)


def _attention(q, kv_arrays, kv_chunks, sinks, nb, s, banded):
    nseq = math.gcd(nb, ATTN_SEQS)
    q_spec = pl.BlockSpec((nseq, CHUNK, D_MODEL), lambda b, c: (b, c, 0))
    kv_specs = [pl.BlockSpec((nseq, CHUNK, KV_W), lambda b, c, f=f: (b, f(c), 0)) for f in kv_chunks]
    return pl.pallas_call(
        functools.partial(_attn_kernel, banded=banded),
        grid=(nb // nseq, s // CHUNK),
        in_specs=[pl.BlockSpec(memory_space=pltpu.SMEM), q_spec] + kv_specs,
        out_specs=q_spec,
        out_shape=jax.ShapeDtypeStruct((nb, s, D_MODEL), BF16),
        compiler_params=_params(("parallel", "parallel")),
        name="attention",
    )(sinks, q, *kv_arrays)


def _pad_cols(w, n):
    return jnp.pad(w, ((0, 0), (0, n - w.shape[1])))


def _pad_rows(w, n):
    return jnp.pad(w, ((0, n - w.shape[0]), (0, 0)))


def _rope_tables(pos):
    half = ROPE_DIM // 2
    inv = jnp.power(jnp.float32(ROPE_THETA), -jnp.arange(half, dtype=F32) * (2.0 / ROPE_DIM))
    ang = pos[:, None] * inv[None, :]
    cos, sin = jnp.cos(ang), jnp.sin(ang)
    rest = HEAD - ROPE_DIM
    ones = jnp.ones((pos.shape[0], rest), F32)
    zeros = jnp.zeros((pos.shape[0], rest), F32)
    z8 = jnp.zeros_like(sin)
    per_head = (jnp.concatenate([cos, cos, ones], axis=1),
                jnp.concatenate([-sin, z8, zeros], axis=1),
                jnp.concatenate([z8, sin, zeros], axis=1))
    return tuple(jnp.tile(t, (1, LANES // HEAD)) for t in per_head)


def _state_to_pairs(state):
    nb = state.shape[0]
    st = state.astype(F32).reshape(nb, PAIRS, 2, HEAD, HEAD)
    z = jnp.zeros_like(st[:, :, 0])
    top = jnp.concatenate([st[:, :, 0], z], axis=-1)
    bot = jnp.concatenate([z, st[:, :, 1]], axis=-1)
    return jnp.concatenate([top, bot], axis=-2)


def _pairs_to_state(st):
    nb = st.shape[0]
    s0 = st[:, :, :HEAD, :HEAD]
    s1 = st[:, :, HEAD:, HEAD:]
    return jnp.stack([s0, s1], axis=2).reshape(nb, 2 * PAIRS, HEAD, HEAD)


def _trunk(x, mods, kvmod, pos, prev_x, prev_wkv, past_k, past_v, w, nb, s):
    n = nb * s
    x = x.reshape(n, D_MODEL)
    mod0, mod1 = mods[0][:, None, :], mods[1][:, None, :]
    kvmod = kvmod[:, None, :]

    tm = min(s, 256)
    r, lw, k, v, a, b, g, last_x = _rwkv_proj(x, mod0, prev_x[:, None, :], w["rw"], nb, s, tm)
    seqs = lambda t: t.reshape(nb, s, D_MODEL)
    z, st = _wkv(*(seqs(t) for t in (r, lw, k, v, a, b, g)), _state_to_pairs(prev_wkv), w["rw"], nb, s)
    z = z.reshape(n, D_MODEL)
    tm = min(n, 256)
    x = _out_proj(x, z, mod0[:, :, 2 * D_MODEL:3 * D_MODEL], w["rw"]["w_o"], w["rw"]["b_o"], nb, s, tm)
    def moe_layer(x, mod, l):
        args = (x, mod, w["norm_g"][l][1:2], w["moe_route"][l], w["moe_gu"][l], w["moe_down"][l], w["final_g"], nb, s)
        if n >= SPARSE_MIN_TOKENS:
            return _moe_sparse(*args, l == 1)
        return _moe(*args, min(n, 1024), l == 1)

    x = moe_layer(x, mod0, 0)

    tm = min(s, 256)
    q, k_new, v_new = _qkv_proj(x, mod1, kvmod, w["at"], _rope_tables(pos), nb, s, tm)
    k_seq, v_seq = k_new.reshape(nb, s, KV_W), v_new.reshape(nb, s, KV_W)
    if past_k is None:
        arrays = [k_seq] * 3 + [v_seq] * 3
        chunks = [lambda c, d=d: jnp.maximum(c - d, 0) for d in (2, 1, 0)] * 2
    else:
        pk = past_k.astype(F32).reshape(nb, 2 * CHUNK, KV_W)
        pv = past_v.astype(F32).reshape(nb, 2 * CHUNK, KV_W)
        arrays = [pk, pk, k_seq, pv, pv, v_seq]
        chunks = [lambda c: 0, lambda c: 1, lambda c: 0] * 2
    o = _attention(q.reshape(nb, s, D_MODEL), arrays, chunks, w["at"]["sinks"], nb, s, past_k is None)
    o = o.reshape(n, D_MODEL)
    tm = min(n, 256)
    x = _out_proj(x, o, mod1[:, :, 2 * D_MODEL:3 * D_MODEL], w["at"]["w_o"], w["at"]["b_o"], nb, s, tm)
    y = moe_layer(x, mod1, 1)

    return (y.reshape(nb, s, D_MODEL), last_x.reshape(1, nb, D_MODEL), _pairs_to_state(st)[None],
            k_new.reshape(nb, s, N_KV, HEAD), v_new.reshape(nb, s, N_KV, HEAD))


def kernel(x_prompt, x_sample, state_shift, state_wkv, cache_k, cache_v, c_prompt, c_sample, ada_w, ada_b, norm_g, rw_mu, rw_w_rkv, rw_w0, rw_w1, rw_w2, rw_a0, rw_a1, rw_a2, rw_g1, rw_g2, rw_k_k, rw_k_a, rw_r_k, rw_lnx_w, rw_lnx_b, rw_w_o, kv_ada_w, kv_ada_b, kv_norm_g, w_kv, b_kv, at_w_q, at_b_q, at_sinks, at_w_o, at_b_o, moe_w_group, moe_w_expert, moe_w_gu, moe_w_down, final_norm_g):
    bp, sp, _ = x_prompt.shape
    bs, ss, _ = x_sample.shape
    row = lambda t: t.reshape(1, -1).astype(F32)

    c_all = jnp.concatenate([c_prompt, c_sample], axis=0)
    mods = _cond_linear(c_all, ada_w, ada_b[:, None, :])
    kvmods = _cond_linear(c_all, kv_ada_w[None], kv_ada_b[None, None, :])[0]

    lora_pad, gate_pad = LANES, 2 * LANES
    w = {
        "norm_g": norm_g,
        "final_g": row(final_norm_g),
        "rw": {
            "norm_g": norm_g[0, 0:1], "mu": rw_mu[0], "w_rkv": rw_w_rkv[0].astype(BF16),
            "w0": row(rw_w0[0]), "w1": _pad_cols(rw_w1[0], lora_pad).astype(BF16),
            "w2": _pad_rows(rw_w2[0], lora_pad).astype(BF16),
            "a0": row(rw_a0[0]), "a1": _pad_cols(rw_a1[0], lora_pad).astype(BF16),
            "a2": _pad_rows(rw_a2[0], lora_pad).astype(BF16),
            "g1": _pad_cols(rw_g1[0], gate_pad).astype(BF16), "g2": _pad_rows(rw_g2[0], gate_pad).astype(BF16),
            "k_k": row(rw_k_k[0]), "k_a": row(rw_k_a[0]), "r_k": row(rw_r_k[0]),
            "lnx_w": row(rw_lnx_w[0]), "lnx_b": row(rw_lnx_b[0]),
            "w_o": rw_w_o[0].astype(BF16), "b_o": jnp.zeros((1, D_MODEL), F32),
        },
        "at": {
            "norm_gq": norm_g[1, 0:1], "norm_gkv": row(kv_norm_g),
            "w_q": at_w_q[0].astype(BF16), "b_q": row(at_b_q[0]),
            "w_kv": w_kv.astype(BF16), "b_kv": row(b_kv),
            "sinks": at_sinks[0].astype(F32),
            "w_o": at_w_o[0].astype(BF16), "b_o": row(at_b_o[0]),
        },
        "moe_route": jnp.pad(jnp.concatenate([moe_w_group, moe_w_expert], axis=-1),
                             ((0, 0), (0, 0), (0, LANES - N_GROUPS - N_EXPERTS))),
        "moe_gu": moe_w_gu.astype(BF16),
        "moe_down": moe_w_down.astype(BF16),
    }

    pos_p = jnp.arange(sp, dtype=F32)
    pos_s = PAST_LEN + jnp.arange(ss, dtype=F32)
    zero_x = jnp.zeros((bp, D_MODEL), x_prompt.dtype)
    zero_wkv = jnp.zeros((bp,) + state_wkv.shape[2:], state_wkv.dtype)
    y_p, p_shift, p_wkv, p_k, p_v = _trunk(x_prompt, mods[:, :bp], kvmods[:bp], pos_p, zero_x, zero_wkv,
                                           None, None, w, bp, sp)
    y_s, s_shift, s_wkv, s_k, s_v = _trunk(x_sample, mods[:, bp:], kvmods[bp:], pos_s, state_shift[0],
                                           state_wkv[0], cache_k, cache_v, w, bs, ss)
    keep = min(2 * CHUNK, sp)
    return (y_p, y_s, p_shift, p_wkv.astype(state_wkv.dtype), p_k[:, sp - keep:], p_v[:, sp - keep:],
            s_shift, s_wkv.astype(state_wkv.dtype), s_k, s_v)
```

```python
import functools
import math

import jax
import jax.numpy as jnp
from jax import lax
from jax.experimental import pallas as pl
from jax.experimental.pallas import tpu as pltpu

F32 = jnp.float32
BF16 = jnp.bfloat16

D_MODEL = 1024
LANES = 128
HEAD = 64
PAIRS = D_MODEL // LANES
CHUNK = 64
WKV_SEQS = 4
ATTN_SEQS = 2
PAST_LEN = 4096
N_KV = 4
KV_W = N_KV * HEAD
ROPE_DIM = 16
ROPE_THETA = 500000.0
ATT_SCALE = HEAD ** -0.5
N_GROUPS = 4
EXP_PER_GROUP = 8
N_EXPERTS = N_GROUPS * EXP_PER_GROUP
D_EXPERT = D_MODEL // 4
ROUTE_LANE0 = N_GROUPS
RMS_EPS = 1e-6
GN_EPS = 64e-5
NEG_INF = -1e30
VMEM_LIMIT = 56 * 1024 * 1024


def _params(sem):
    return pltpu.CompilerParams(dimension_semantics=sem, vmem_limit_bytes=VMEM_LIMIT)


def _dot(a, b):
    return jnp.dot(a, b, preferred_element_type=F32)


def _dot_nt(a, b):
    return lax.dot_general(a, b, (((1,), (1,)), ((), ())), preferred_element_type=F32)


def _dot_tn(a, b):
    return lax.dot_general(a, b, (((0,), (0,)), ((), ())), preferred_element_type=F32)


def _split2(x):
    hi = x.astype(BF16)
    lo = (x - hi.astype(F32)).astype(BF16)
    return hi, lo


def _dot_x3(a, b, dot=_dot):
    ah, al = _split2(a)
    bh, bl = _split2(b)
    return dot(ah, bh) + dot(ah, bl) + dot(al, bh)


def _sigmoid(x):
    return 1.0 / (1.0 + jnp.exp(-x))


def _rms(x):
    return x * lax.rsqrt(jnp.mean(x * x, axis=-1, keepdims=True) + RMS_EPS)


def _modulate(n, gain, shift, scale, nb):
    rows = n.shape[0]
    h = (n * gain).reshape(nb, rows // nb, D_MODEL)
    return (h * (1.0 + scale) + shift).reshape(rows, D_MODEL)


def _const_spec(shape):
    nd = len(shape)
    return pl.BlockSpec(shape, lambda *_: (0,) * nd)


def _cond_kernel(c_ref, w_ref, b_ref, o_ref):
    c = c_ref[...]
    cs = (c * _sigmoid(c)).astype(BF16)
    o_ref[...] = _dot(cs, w_ref[...].astype(BF16)) + b_ref[...]


def _cond_linear(c, w, b, tn=512):
    nl, _, n = w.shape
    m = c.shape[0]
    return pl.pallas_call(
        _cond_kernel,
        grid=(nl, n // tn),
        in_specs=[
            pl.BlockSpec((m, D_MODEL), lambda l, j: (0, 0)),
            pl.BlockSpec((None, D_MODEL, tn), lambda l, j: (l, 0, j)),
            pl.BlockSpec((None, 1, tn), lambda l, j: (l, 0, j)),
        ],
        out_specs=pl.BlockSpec((None, m, tn), lambda l, j: (l, 0, j)),
        out_shape=jax.ShapeDtypeStruct((nl, m, n), F32),
        compiler_params=_params(("parallel", "parallel")),
        name="cond_linear",
    )(c, w, b)


def _rwkv_proj_kernel(x_ref, mod_ref, prev_ref, ng_ref, mu_ref, wrkv_ref, w0_ref, w1_ref, w2_ref,
                      a0_ref, a1_ref, a2_ref, g1_ref, g2_ref, kk_ref, ka_ref,
                      r_ref, lw_ref, k_ref, v_ref, a_ref, b_ref, g_ref, last_ref, carry_ref):
    tm = x_ref.shape[0]

    @pl.when(pl.program_id(1) == 0)
    def _():
        carry_ref[...] = prev_ref[...]

    shift = mod_ref[:, 0:D_MODEL].reshape(1, 1, D_MODEL)
    scale = mod_ref[:, D_MODEL:2 * D_MODEL].reshape(1, 1, D_MODEL)
    h = _modulate(_rms(x_ref[...]), ng_ref[...], shift, scale, 1)
    row = lax.broadcasted_iota(jnp.int32, (tm, 1), 0)
    shifted = jnp.where(row == 0, carry_ref[...], pltpu.roll(h, 1, axis=0))
    carry_ref[...] = h[tm - 1:tm, :]
    last_ref[...] = h[tm - 1:tm, :]
    dx = shifted - h

    def mix(n):
        return (h + dx * mu_ref[n:n + 1, :]).astype(BF16)

    r = _dot(mix(0), wrkv_ref[0])
    k = _dot(mix(1), wrkv_ref[1])
    v = _dot(mix(2), wrkv_ref[2])
    ww = _dot(jnp.tanh(_dot(mix(3), w1_ref[...])).astype(BF16), w2_ref[...])
    z = -(w0_ref[...] + ww)
    softplus = jnp.maximum(z, 0.0) + jnp.log(1.0 + jnp.exp(-jnp.abs(z)))
    lw = -jnp.exp(-softplus - 0.5)
    asig = _sigmoid(a0_ref[...] + _dot(_dot(mix(4), a1_ref[...]).astype(BF16), a2_ref[...]))
    g = _dot(_sigmoid(_dot(mix(5), g1_ref[...])).astype(BF16), g2_ref[...])

    kk = k * kk_ref[...]
    kk2 = kk * kk
    lo_cols = lax.broadcasted_iota(jnp.int32, (tm, LANES), 1) < HEAD

    def head_sums(t):
        lo_sum = jnp.sum(jnp.where(lo_cols, t, 0.0), axis=-1, keepdims=True)
        hi_sum = jnp.sum(jnp.where(lo_cols, 0.0, t), axis=-1, keepdims=True)
        return jnp.where(lo_cols, lo_sum, hi_sum)

    ss = jnp.concatenate([head_sums(kk2[:, p * LANES:(p + 1) * LANES]) for p in range(PAIRS)], axis=1)
    kkn = kk * lax.rsqrt(jnp.maximum(ss, 1e-24))

    r_ref[...] = r.astype(r_ref.dtype)
    lw_ref[...] = lw
    k_ref[...] = (k * (1.0 + (asig - 1.0) * ka_ref[...])).astype(k_ref.dtype)
    v_ref[...] = v.astype(v_ref.dtype)
    a_ref[...] = (-kkn).astype(a_ref.dtype)
    b_ref[...] = (kkn * asig).astype(b_ref.dtype)
    g_ref[...] = g.astype(g_ref.dtype)


def _rwkv_proj(x, mod, prev_x, w, nb, s, tm):
    n = nb * s
    nt = s // tm
    row_spec = pl.BlockSpec((tm, D_MODEL), lambda b, j: (b * nt + j, 0))
    vec = _const_spec((1, D_MODEL))
    lora_w, lora_g = w["w1"].shape[1], w["g1"].shape[1]
    in_specs = [
        row_spec,
        pl.BlockSpec((None, 1, 6 * D_MODEL), lambda b, j: (b, 0, 0)),
        pl.BlockSpec((None, 1, D_MODEL), lambda b, j: (b, 0, 0)),
        vec, _const_spec((6, D_MODEL)), _const_spec((3, D_MODEL, D_MODEL)),
        vec, _const_spec((D_MODEL, lora_w)), _const_spec((lora_w, D_MODEL)),
        vec, _const_spec((D_MODEL, lora_w)), _const_spec((lora_w, D_MODEL)),
        _const_spec((D_MODEL, lora_g)), _const_spec((lora_g, D_MODEL)),
        vec, vec,
    ]
    act = lambda dt: jax.ShapeDtypeStruct((n, D_MODEL), dt)
    out_types = [act(BF16), act(F32)] + [act(BF16)] * 5
    outs = pl.pallas_call(
        _rwkv_proj_kernel,
        grid=(nb, nt),
        in_specs=in_specs,
        out_specs=[row_spec] * 7 + [pl.BlockSpec((None, 1, D_MODEL), lambda b, j: (b, 0, 0))],
        out_shape=out_types + [jax.ShapeDtypeStruct((nb, 1, D_MODEL), F32)],
        scratch_shapes=[pltpu.VMEM((1, D_MODEL), F32)],
        compiler_params=_params(("parallel", "arbitrary")),
        name="rwkv_proj",
    )(x, mod, prev_x, w["norm_g"], w["mu"], w["w_rkv"], w["w0"], w["w1"], w["w2"],
      w["a0"], w["a1"], w["a2"], w["g1"], w["g2"], w["k_k"], w["k_a"])
    return outs


def _split3(x):
    hi = x.astype(BF16)
    rem = x - hi.astype(F32)
    mid = rem.astype(BF16)
    return hi, mid, (rem - mid.astype(F32)).astype(BF16)


def _wkv_kernel(r_ref, lw_ref, k_ref, v_ref, a_ref, b_ref, g_ref, s0_ref, rk_ref, lnw_ref, lnb_ref,
                z_ref, st_ref):
    @pl.when(pl.program_id(1) == 0)
    def _():
        st_ref[...] = s0_ref[...]

    nseq = r_ref.shape[0]
    wide = lambda ref: jnp.concatenate([ref[i] for i in range(nseq)], axis=1)
    per_seq = lambda ref: jnp.concatenate([ref[...]] * nseq, axis=1)
    b16 = lambda t: t.astype(BF16)
    pairs = range(nseq * PAIRS)
    blk = lambda t, p: t[:, p * LANES:(p + 1) * LANES]
    rows = lambda ts: jnp.concatenate(ts, axis=0)
    lo_cols = lax.broadcasted_iota(jnp.int32, (CHUNK, LANES), 1) < HEAD
    rr = lax.broadcasted_iota(jnp.int32, (LANES, LANES), 0)
    cc = lax.broadcasted_iota(jnp.int32, (LANES, LANES), 1)
    head0 = lambda t: jnp.where(lo_cols, t, 0.0)
    head1 = lambda t: jnp.where(lo_cols, 0.0, t)
    blockdiag = lambda q: rows([head0(q), head1(q)])

    r, lw, k, v, a, b = (wide(t).astype(F32) for t in (r_ref, lw_ref, k_ref, v_ref, a_ref, b_ref))
    ti = lax.broadcasted_iota(jnp.int32, (CHUNK, CHUNK), 0)
    tj = lax.broadcasted_iota(jnp.int32, (CHUNK, CHUNK), 1)
    tri = (tj <= ti).astype(BF16)
    cum = sum(_dot(tri, t) for t in _split3(lw))
    cl = cum[CHUNK - 1:CHUNK, :]
    e_neg = jnp.exp(-cum)
    e_end = jnp.exp(cl - cum)
    e_cl = jnp.exp(cl)
    at = b16(a * jnp.exp(cum - lw))
    rt = b16(r * jnp.exp(cum))
    bt = b * e_neg
    kt = k * e_neg
    bkh = rows([b16(b * e_end), b16(k * e_end)])
    v16 = b16(v)

    t_idx = rr & (CHUNK - 1)
    j_idx = cc & (CHUNK - 1)
    keep = (j_idx < t_idx) | ((rr >= CHUNK) & (j_idx == t_idx))
    g0, g1 = [], []
    for p in pairs:
        btp, ktp = blk(bt, p), blk(kt, p)
        w = b16(rows([head0(btp), head0(ktp), head1(ktp), head1(btp)]))
        g = _dot_nt(rows([blk(at, p), blk(rt, p)]), w)
        g0.append(jnp.where(keep, g[:, :LANES], 0.0))
        g1.append(jnp.where(keep, g[:, LANES:], 0.0))

    eye2 = jnp.where((lax.broadcasted_iota(jnp.int32, (CHUNK, LANES), 1) & (CHUNK - 1))
                     == lax.broadcasted_iota(jnp.int32, (CHUNK, LANES), 0), 1.0, 0.0)
    pc = [jnp.where(lo_cols, g0[p][:CHUNK], g1[p][:CHUNK]) for p in pairs]
    tc = [eye2 + pc[p] for p in pairs]
    q = [_dot(b16(pc[p]), b16(blockdiag(pc[p]))) for p in pairs]
    for _ in range(4):
        res = [_dot(b16(rows([tc[p], q[p]])), b16(blockdiag(q[p]))) for p in pairs]
        tc = [tc[p] + res[p][:CHUNK] for p in pairs]
        q = [res[p][CHUNK:] for p in pairs]
    tc = [b16(tc[p] + _dot(b16(tc[p]), b16(blockdiag(q[p])))) for p in pairs]

    akv = []
    for p in pairs:
        w_ak = b16(jnp.where(lo_cols, g1[p][:CHUNK], g0[p][:CHUNK]))
        vp = blk(v16, p)
        akv.append(_dot(w_ak, rows([head1(vp), head0(vp)])))

    s = [st_ref[p // PAIRS, p % PAIRS] for p in pairs]
    s16 = [b16(s[p]) for p in pairs]
    rhs = [_dot_nt(blk(at, p), s16[p]) + akv[p] for p in pairs]
    u = [_dot(tc[p], b16(rows([head0(rhs[p]), head1(rhs[p])]))) for p in pairs]
    u16 = [b16(u[p]) for p in pairs]

    ys = []
    bd_mask = (rr < HEAD) == (cc < HEAD)
    for p in pairs:
        vp = blk(v16, p)
        rbk = b16(jnp.concatenate([g0[p][CHUNK:], g1[p][CHUNK:]], axis=1))
        uv = rows([head0(u16[p]), head0(vp), head1(vp), head1(u16[p])])
        ys.append(_dot_nt(blk(rt, p), s16[p]) + _dot(rbk, uv))
        fresh = _dot_tn(rows([u16[p], vp]), blk(bkh, p))
        st_ref[p // PAIRS, p % PAIRS] = s[p] * blk(e_cl, p) + jnp.where(bd_mask, fresh, 0.0)

    def head_sums(t):
        lo_sum = jnp.sum(head0(t), axis=-1, keepdims=True)
        hi_sum = jnp.sum(head1(t), axis=-1, keepdims=True)
        return jnp.where(lo_cols, lo_sum, hi_sum)

    rkk = r * k * per_seq(rk_ref)
    for p in pairs:
        seq, cols = p // PAIRS, slice((p % PAIRS) * LANES, (p % PAIRS + 1) * LANES)
        yc = ys[p] - head_sums(ys[p]) * (1.0 / HEAD)
        yn = yc * lax.rsqrt(head_sums(yc * yc) * (1.0 / HEAD) + GN_EPS)
        zp = (yn * lnw_ref[:, cols] + lnb_ref[:, cols] + head_sums(blk(rkk, p)) * blk(v, p)) * g_ref[seq, :, cols].astype(F32)
        z_ref[seq, :, cols] = zp.astype(z_ref.dtype)


def _wkv(r, lw, k, v, a, b, g, st0, w, nb, s):
    nseq = math.gcd(nb, WKV_SEQS)
    blk = pl.BlockSpec((nseq, CHUNK, D_MODEL), lambda bi, c: (bi, c, 0))
    vec = _const_spec((1, D_MODEL))
    st_spec = pl.BlockSpec((nseq, PAIRS, LANES, LANES), lambda bi, c: (bi, 0, 0, 0))
    return pl.pallas_call(
        _wkv_kernel,
        grid=(nb // nseq, s // CHUNK),
        in_specs=[blk] * 7 + [st_spec, vec, vec, vec],
        out_specs=[blk, st_spec],
        out_shape=[jax.ShapeDtypeStruct((nb, s, D_MODEL), BF16),
                   jax.ShapeDtypeStruct((nb, PAIRS, LANES, LANES), F32)],
        compiler_params=_params(("parallel", "arbitrary")),
        name="wkv",
    )(r, lw, k, v, a, b, g, st0, w["r_k"], w["lnx_w"], w["lnx_b"])


def _out_proj_kernel(x_ref, z_ref, gate_ref, w_ref, bias_ref, o_ref, *, nb):
    rows = x_ref.shape[0]
    y = _dot(z_ref[...], w_ref[...]) + bias_ref[...]
    y = (y.reshape(nb, rows // nb, D_MODEL) * gate_ref[...]).reshape(rows, D_MODEL)
    o_ref[...] = x_ref[...] + y


def _out_proj(x, z, gate, w, bias, nb, s, tm):
    n = nb * s
    tb = max(tm // s, 1)
    per_seq = max(s // tm, 1)
    row_spec = pl.BlockSpec((tm, D_MODEL), lambda i: (i, 0))
    return pl.pallas_call(
        functools.partial(_out_proj_kernel, nb=tb),
        grid=(n // tm,),
        in_specs=[row_spec, row_spec,
                  pl.BlockSpec((tb, 1, D_MODEL), lambda i: (i // per_seq, 0, 0)),
                  _const_spec((D_MODEL, D_MODEL)), _const_spec((1, D_MODEL))],
        out_specs=row_spec,
        out_shape=jax.ShapeDtypeStruct((n, D_MODEL), F32),
        compiler_params=_params(("parallel",)),
        name="out_proj",
    )(x, z, gate, w, bias)


def _top2(logits):
    lane = lax.broadcasted_iota(jnp.int32, logits.shape, 1).astype(F32)
    first_of = lambda hit: jnp.min(jnp.where(hit, lane, float(LANES)), axis=-1, keepdims=True)
    lg = jnp.where(lane < N_GROUPS, logits, -jnp.inf)
    gmax = jnp.max(lg, axis=-1, keepdims=True)
    gi = first_of(lg == gmax)
    gp = 1.0 / jnp.sum(jnp.exp(lg - gmax), axis=-1, keepdims=True)
    first = ROUTE_LANE0 + gi * EXP_PER_GROUP
    le = jnp.where((lane >= first) & (lane < first + EXP_PER_GROUP), logits, -jnp.inf)
    top1 = jnp.max(le, axis=-1, keepdims=True)
    i1 = first_of(le == top1)
    le2 = jnp.where(lane == i1, -jnp.inf, le)
    top2 = jnp.max(le2, axis=-1, keepdims=True)
    i2 = first_of(le2 == top2)
    e2 = jnp.exp(top2 - top1)
    w1 = gp / (1.0 + e2)
    return i1, i2, w1, w1 * e2


def _route(logits):
    lane = lax.broadcasted_iota(jnp.int32, logits.shape, 1).astype(F32)
    i1, i2, w1, w2 = _top2(logits)
    return jnp.where(lane == i1, w1, 0.0) + jnp.where(lane == i2, w2, 0.0)


def _moe_kernel(x_ref, mod_ref, ng_ref, wr_ref, wgu_ref, wdn_ref, fg_ref, o_ref, h_scr, cw_scr, acc_scr,
                *, nb, final_norm):
    e = pl.program_id(1)
    rows = x_ref.shape[0]

    @pl.when(e == 0)
    def _():
        shift = mod_ref[:, :, 3 * D_MODEL:4 * D_MODEL]
        scale = mod_ref[:, :, 4 * D_MODEL:5 * D_MODEL]
        h = _modulate(_rms(x_ref[...]), ng_ref[...], shift, scale, nb)
        h_scr[...] = h.astype(BF16)
        cw_scr[...] = _route(_dot_x3(h, wr_ref[...]))
        acc_scr[...] = jnp.zeros_like(acc_scr)

    lane = lax.broadcasted_iota(jnp.int32, (rows, LANES), 1)
    cw = jnp.sum(jnp.where(lane == ROUTE_LANE0 + e, cw_scr[...], 0.0), axis=-1, keepdims=True)
    gu = _dot(h_scr[...], wgu_ref[...])
    gate = gu[:, :D_EXPERT]
    hid = gate * _sigmoid(gate) * gu[:, D_EXPERT:] * cw
    acc_scr[...] += _dot(hid.astype(BF16), wdn_ref[...])

    @pl.when(e == N_EXPERTS - 1)
    def _():
        g2 = mod_ref[:, :, 5 * D_MODEL:6 * D_MODEL]
        y = (acc_scr[...].reshape(nb, rows // nb, D_MODEL) * g2).reshape(rows, D_MODEL)
        out = x_ref[...] + y
        if final_norm:
            out = _rms(out) * fg_ref[...]
        o_ref[...] = out


def _moe(x, mod, norm_g, w_route, w_gu, w_down, final_g, nb, s, tm, final_norm):
    n = nb * s
    tb = max(tm // s, 1)
    per_seq = max(s // tm, 1)
    row_spec = pl.BlockSpec((tm, D_MODEL), lambda i, e: (i, 0))
    return pl.pallas_call(
        functools.partial(_moe_kernel, nb=tb, final_norm=final_norm),
        grid=(n // tm, N_EXPERTS),
        in_specs=[row_spec,
                  pl.BlockSpec((tb, 1, 6 * D_MODEL), lambda i, e: (i // per_seq, 0, 0)),
                  _const_spec((1, D_MODEL)), _const_spec((D_MODEL, LANES)),
                  pl.BlockSpec((None, D_MODEL, 2 * D_EXPERT), lambda i, e: (e, 0, 0)),
                  pl.BlockSpec((None, D_EXPERT, D_MODEL), lambda i, e: (e, 0, 0)),
                  _const_spec((1, D_MODEL))],
        out_specs=row_spec,
        out_shape=jax.ShapeDtypeStruct((n, D_MODEL), F32),
        scratch_shapes=[pltpu.VMEM((tm, D_MODEL), BF16), pltpu.VMEM((tm, LANES), F32),
                        pltpu.VMEM((tm, D_MODEL), F32)],
        compiler_params=_params(("parallel", "arbitrary")),
        name="moe",
    )(x, mod, norm_g, w_route, w_gu, w_down, final_g)


PAIRS_PER_GROUP = EXP_PER_GROUP * (EXP_PER_GROUP - 1) // 2
N_BUCKETS = N_GROUPS * PAIRS_PER_GROUP
BUCKET_TILE = 256
ROW_W = D_MODEL + LANES
META_BUCKET, META_RANK, META_W_LO, META_W_HI = 0, 1, 2, 3
ROUTER_TILE = 512
MOVE_TILE = 256
DMA_UNROLL = 8
SPARSE_MIN_TOKENS = 4096


def _moe_norm(x_ref, mod_ref, ng_ref, nb):
    shift = mod_ref[:, :, 3 * D_MODEL:4 * D_MODEL]
    scale = mod_ref[:, :, 4 * D_MODEL:5 * D_MODEL]
    return _modulate(_rms(x_ref[...]), ng_ref[...], shift, scale, nb)


def _router_kernel(x_ref, mod_ref, ng_ref, wr_ref, earlier_ref, meta_ref, cnt_ref, *, nb):
    @pl.when(pl.program_id(0) == 0)
    def _():
        cnt_ref[...] = jnp.zeros_like(cnt_ref)

    rows = x_ref.shape[0]
    h = _moe_norm(x_ref, mod_ref, ng_ref, nb)
    i1, i2, w1, w2 = _top2(_dot_x3(h, wr_ref[...]))
    lo = (jnp.minimum(i1, i2) - ROUTE_LANE0).astype(jnp.int32)
    hi = (jnp.maximum(i1, i2) - ROUTE_LANE0).astype(jnp.int32)
    first_is_lo = i1 < i2
    a = lo & (EXP_PER_GROUP - 1)
    b = hi & (EXP_PER_GROUP - 1)
    group = lo >> (EXP_PER_GROUP.bit_length() - 1)
    bucket = group * PAIRS_PER_GROUP + ((a * (2 * EXP_PER_GROUP - 1 - a)) >> 1) + (b - a - 1)

    lane = lax.broadcasted_iota(jnp.int32, (rows, LANES), 1)
    mine = lane == bucket
    onehot = jnp.where(mine, 1.0, 0.0)
    seen = cnt_ref[...]
    before = _dot(earlier_ref[...], onehot.astype(BF16)) + seen
    rank = jnp.sum(jnp.where(mine, before, 0.0), axis=-1, keepdims=True)
    cnt_ref[...] = seen + jnp.sum(onehot, axis=0, keepdims=True)

    meta = jnp.where(lane == META_BUCKET, bucket.astype(F32), 0.0)
    meta = jnp.where(lane == META_RANK, rank, meta)
    meta = jnp.where(lane == META_W_LO, jnp.where(first_is_lo, w1, w2), meta)
    meta_ref[...] = jnp.where(lane == META_W_HI, jnp.where(first_is_lo, w2, w1), meta)


def _positions_kernel(meta_ref, start_ref, pos_ref):
    meta = meta_ref[...]
    lane = lax.broadcasted_iota(jnp.int32, meta.shape, 1)
    column = lambda c: jnp.sum(jnp.where(lane == c, meta, 0.0), axis=-1, keepdims=True)
    value = jnp.where(lane == column(META_BUCKET).astype(jnp.int32), start_ref[...] + column(META_RANK), 0.0)
    ones = jnp.ones((8, LANES), BF16)
    pos = sum(_dot_nt(ones, piece) for piece in _split3(value))
    pos_ref[...] = pos[0:1, :].astype(jnp.int32)


def _dispatch_kernel(pos_ref, x_ref, mod_ref, ng_ref, meta_ref, xs_ref, buf, sem, *, nb):
    i = pl.program_id(0)
    rows = x_ref.shape[0]
    slot = i % 2
    buf[slot, :, 0:D_MODEL] = _moe_norm(x_ref, mod_ref, ng_ref, nb)
    buf[slot, :, D_MODEL:ROW_W] = meta_ref[...]

    def send(r, carry):
        pltpu.make_async_copy(buf.at[slot, pl.ds(r, 1), :],
                              xs_ref.at[pl.ds(pos_ref[i * rows + r], 1), :], sem.at[slot]).start()
        return carry

    lax.fori_loop(0, rows, send, 0, unroll=DMA_UNROLL)

    def wait_all(sl):
        pltpu.make_async_copy(buf.at[sl], xs_ref.at[pl.ds(0, rows), :], sem.at[sl]).wait()

    @pl.when(i >= 1)
    def _():
        wait_all(1 - slot)

    @pl.when(i == pl.num_programs(0) - 1)
    def _():
        wait_all(slot)


def _experts_kernel(tile_ref, bucket_ref, lo_ref, hi_ref, used_ref, xs_ref, gu_lo_ref, gu_hi_ref,
                    dn_lo_ref, dn_hi_ref, ys_ref):
    k = pl.program_id(0)

    @pl.when(k < used_ref[0])
    def _():
        rows = xs_ref.shape[0]
        x16 = xs_ref[:, 0:D_MODEL].astype(BF16)
        meta = xs_ref[:, D_MODEL:ROW_W]
        lane = lax.broadcasted_iota(jnp.int32, (rows, LANES), 1)
        column = lambda c: jnp.sum(jnp.where(lane == c, meta, 0.0), axis=-1, keepdims=True)
        member = column(META_BUCKET) == bucket_ref[k].astype(F32)

        def expert(gu_ref, dn_ref, col):
            gu = _dot(x16, gu_ref[...])
            gate = gu[:, :D_EXPERT]
            hid = gate * _sigmoid(gate) * gu[:, D_EXPERT:] * jnp.where(member, column(col), 0.0)
            return _dot(hid.astype(BF16), dn_ref[...])

        y = expert(gu_lo_ref, dn_lo_ref, META_W_LO) + expert(gu_hi_ref, dn_hi_ref, META_W_HI)
        new_tile = (k == 0) | (tile_ref[k] != tile_ref[jnp.maximum(k - 1, 0)])

        @pl.when(new_tile)
        def _():
            ys_ref[...] = y

        @pl.when(jnp.logical_not(new_tile))
        def _():
            ys_ref[...] += y


def _combine_kernel(pos_ref, x_ref, mod_ref, fg_ref, ys_ref, o_ref, buf, sem, *, nb, final_norm):
    i = pl.program_id(0)
    rows = x_ref.shape[0]
    slot = i % 2

    def fetch(step, sl):
        def one(r, carry):
            pltpu.make_async_copy(ys_ref.at[pl.ds(pos_ref[step * rows + r], 1), :],
                                  buf.at[sl, pl.ds(r, 1), :], sem.at[sl]).start()
            return carry
        lax.fori_loop(0, rows, one, 0, unroll=DMA_UNROLL)

    @pl.when(i == 0)
    def _():
        fetch(0, 0)

    pltpu.make_async_copy(ys_ref.at[pl.ds(0, rows), :], buf.at[slot], sem.at[slot]).wait()

    @pl.when(i + 1 < pl.num_programs(0))
    def _():
        fetch(i + 1, 1 - slot)

    g2 = mod_ref[:, :, 5 * D_MODEL:6 * D_MODEL]
    y = (buf[slot].reshape(nb, rows // nb, D_MODEL) * g2).reshape(rows, D_MODEL)
    out = x_ref[...] + y
    if final_norm:
        out = _rms(out) * fg_ref[...]
    o_ref[...] = out


def _bucket_experts():
    lo, hi = [], []
    for g in range(N_GROUPS):
        for a in range(EXP_PER_GROUP):
            for b in range(a + 1, EXP_PER_GROUP):
                lo.append(g * EXP_PER_GROUP + a)
                hi.append(g * EXP_PER_GROUP + b)
    return jnp.asarray(lo, jnp.int32), jnp.asarray(hi, jnp.int32)


def _moe_sparse(x, mod, norm_g, w_route, w_gu, w_down, final_g, nb, s, final_norm):
    n = nb * s
    move_params = pltpu.CompilerParams(dimension_semantics=("arbitrary",), vmem_limit_bytes=VMEM_LIMIT,
                                       disable_bounds_checks=True)

    def token_specs(tm):
        tb, per_seq = max(tm // s, 1), max(s // tm, 1)
        row = pl.BlockSpec((tm, D_MODEL), lambda i, *_: (i, 0))
        modspec = pl.BlockSpec((tb, 1, 6 * D_MODEL), lambda i, *_: (i // per_seq, 0, 0))
        meta = pl.BlockSpec((tm, LANES), lambda i, *_: (i, 0))
        return tb, row, modspec, meta

    vec = pl.BlockSpec((1, D_MODEL), lambda *_: (0, 0))
    tb, row, modspec, metaspec = token_specs(ROUTER_TILE)
    meta, counts = pl.pallas_call(
        functools.partial(_router_kernel, nb=tb),
        grid=(n // ROUTER_TILE,),
        in_specs=[row, modspec, vec, pl.BlockSpec((D_MODEL, LANES), lambda i: (0, 0)),
                  pl.BlockSpec((ROUTER_TILE, ROUTER_TILE), lambda i: (0, 0))],
        out_specs=[metaspec, pl.BlockSpec((1, LANES), lambda i: (0, 0))],
        out_shape=[jax.ShapeDtypeStruct((n, LANES), F32), jax.ShapeDtypeStruct((1, LANES), F32)],
        compiler_params=_params(("arbitrary",)),
        name="moe_router",
    )(x, mod, norm_g, w_route, jnp.tril(jnp.ones((ROUTER_TILE, ROUTER_TILE), BF16), -1))

    counts = counts[0, :N_BUCKETS].astype(jnp.int32)
    end = jnp.cumsum(counts)
    start = end - counts
    first_tile = start // BUCKET_TILE
    per_bucket = jnp.where(counts > 0, (end - 1) // BUCKET_TILE - first_tile + 1, 0)
    item_end = jnp.cumsum(per_bucket)
    used = item_end[-1:]
    n_items = n // BUCKET_TILE + N_BUCKETS
    k = jnp.minimum(jnp.arange(n_items, dtype=jnp.int32), used - 1)
    item_bucket = jnp.sum(item_end[None, :] <= k[:, None], axis=1, dtype=jnp.int32)
    item_tile = first_tile[item_bucket] + k - (item_end - per_bucket)[item_bucket]
    bucket_lo, bucket_hi = _bucket_experts()
    item_lo, item_hi = bucket_lo[item_bucket], bucket_hi[item_bucket]
    start_row = jnp.pad(start, (0, LANES - N_BUCKETS)).astype(F32)[None, :]
    pos = pl.pallas_call(
        _positions_kernel,
        grid=(n // ROUTER_TILE,),
        in_specs=[metaspec, pl.BlockSpec((1, LANES), lambda i: (0, 0))],
        out_specs=pl.BlockSpec((1, ROUTER_TILE), lambda i: (0, i)),
        out_shape=jax.ShapeDtypeStruct((1, n), jnp.int32),
        compiler_params=_params(("parallel",)),
        name="moe_positions",
    )(meta, start_row).reshape(n)

    tb, row, modspec, metaspec = token_specs(MOVE_TILE)
    any_spec = pl.BlockSpec(memory_space=pl.ANY)
    xs = pl.pallas_call(
        functools.partial(_dispatch_kernel, nb=tb),
        grid_spec=pltpu.PrefetchScalarGridSpec(
            num_scalar_prefetch=1, grid=(n // MOVE_TILE,),
            in_specs=[row, modspec, vec, metaspec], out_specs=any_spec,
            scratch_shapes=[pltpu.VMEM((2, MOVE_TILE, ROW_W), F32), pltpu.SemaphoreType.DMA((2,))]),
        out_shape=jax.ShapeDtypeStruct((n, ROW_W), F32),
        compiler_params=move_params,
        name="moe_dispatch",
    )(pos, x, mod, norm_g, meta)

    tile_of = lambda k, tile, *_: (tile[k], 0)
    ys = pl.pallas_call(
        _experts_kernel,
        grid_spec=pltpu.PrefetchScalarGridSpec(
            num_scalar_prefetch=5, grid=(n_items,),
            in_specs=[pl.BlockSpec((BUCKET_TILE, ROW_W), tile_of),
                      pl.BlockSpec((None, D_MODEL, 2 * D_EXPERT), lambda k, t, b, lo, hi, nu: (lo[k], 0, 0)),
                      pl.BlockSpec((None, D_MODEL, 2 * D_EXPERT), lambda k, t, b, lo, hi, nu: (hi[k], 0, 0)),
                      pl.BlockSpec((None, D_EXPERT, D_MODEL), lambda k, t, b, lo, hi, nu: (lo[k], 0, 0)),
                      pl.BlockSpec((None, D_EXPERT, D_MODEL), lambda k, t, b, lo, hi, nu: (hi[k], 0, 0))],
            out_specs=pl.BlockSpec((BUCKET_TILE, D_MODEL), tile_of)),
        out_shape=jax.ShapeDtypeStruct((n, D_MODEL), F32),
        compiler_params=_params(("arbitrary",)),
        name="moe_experts",
    )(item_tile, item_bucket, item_lo, item_hi, used, xs, w_gu, w_gu, w_down, w_down)

    return pl.pallas_call(
        functools.partial(_combine_kernel, nb=tb, final_norm=final_norm),
        grid_spec=pltpu.PrefetchScalarGridSpec(
            num_scalar_prefetch=1, grid=(n // MOVE_TILE,),
            in_specs=[row, modspec, vec, any_spec], out_specs=row,
            scratch_shapes=[pltpu.VMEM((2, MOVE_TILE, D_MODEL), F32), pltpu.SemaphoreType.DMA((2,))]),
        out_shape=jax.ShapeDtypeStruct((n, D_MODEL), F32),
        compiler_params=move_params,
        name="moe_combine",
    )(pos, x, mod, final_g, ys)


def _rope(x, cos, sin_lo, sin_hi):
    return x * cos + pltpu.roll(x, LANES - ROPE_DIM // 2, axis=1) * sin_lo + pltpu.roll(x, ROPE_DIM // 2, axis=1) * sin_hi


def _qkv_kernel(x_ref, mod_ref, kvmod_ref, ngq_ref, ngkv_ref, wq_ref, bq_ref, wkv_ref, bkv_ref,
                cos_ref, slo_ref, shi_ref, q_ref, k_ref, v_ref):
    n = _rms(x_ref[...])
    shift = mod_ref[:, 0:D_MODEL].reshape(1, 1, D_MODEL)
    scale = mod_ref[:, D_MODEL:2 * D_MODEL].reshape(1, 1, D_MODEL)
    hq = _modulate(n, ngq_ref[...], shift, scale, 1)
    kshift = kvmod_ref[:, 0:D_MODEL].reshape(1, 1, D_MODEL)
    kscale = kvmod_ref[:, D_MODEL:2 * D_MODEL].reshape(1, 1, D_MODEL)
    hkv = _modulate(n, ngkv_ref[...], kshift, kscale, 1)
    cos, slo, shi = cos_ref[...], slo_ref[...], shi_ref[...]
    q = _dot(hq.astype(BF16), wq_ref[...]) + bq_ref[...]
    for p in range(PAIRS):
        cols = slice(p * LANES, (p + 1) * LANES)
        q_ref[:, cols] = (_rope(q[:, cols], cos, slo, shi) * ATT_SCALE).astype(q_ref.dtype)
    kv = _dot(hkv.astype(BF16), wkv_ref[...]) + bkv_ref[...]
    for p in range(KV_W // LANES):
        cols = slice(p * LANES, (p + 1) * LANES)
        k_ref[:, cols] = _rope(kv[:, cols], cos, slo, shi)
    v_ref[...] = kv[:, KV_W:]


def _qkv_proj(x, mod, kvmod, w, tables, nb, s, tm):
    n = nb * s
    nt = s // tm
    row_spec = pl.BlockSpec((tm, D_MODEL), lambda b, j: (b * nt + j, 0))
    kv_spec = pl.BlockSpec((tm, KV_W), lambda b, j: (b * nt + j, 0))
    tab_spec = pl.BlockSpec((tm, LANES), lambda b, j: (j, 0))
    vec = _const_spec((1, D_MODEL))
    return pl.pallas_call(
        _qkv_kernel,
        grid=(nb, nt),
        in_specs=[row_spec,
                  pl.BlockSpec((None, 1, 6 * D_MODEL), lambda b, j: (b, 0, 0)),
                  pl.BlockSpec((None, 1, 2 * D_MODEL), lambda b, j: (b, 0, 0)),
                  vec, vec, _const_spec((D_MODEL, D_MODEL)), vec,
                  _const_spec((D_MODEL, 2 * KV_W)), _const_spec((1, 2 * KV_W)),
                  tab_spec, tab_spec, tab_spec],
        out_specs=[row_spec, kv_spec, kv_spec],
        out_shape=[jax.ShapeDtypeStruct((n, D_MODEL), BF16),
                   jax.ShapeDtypeStruct((n, KV_W), F32), jax.ShapeDtypeStruct((n, KV_W), F32)],
        compiler_params=_params(("parallel", "parallel")),
        name="qkv_proj",
    )(x, mod, kvmod, w["norm_gq"], w["norm_gkv"], w["w_q"], w["b_q"], w["w_kv"], w["b_kv"], *tables)


def _attn_kernel(sink_ref, q_ref, k0_ref, k1_ref, k2_ref, v0_ref, v1_ref, v2_ref, o_ref, *, banded):
    n = pl.program_id(1)
    seqs = range(q_ref.shape[0])
    k_all = [jnp.concatenate([k0_ref[i], k1_ref[i], k2_ref[i]], axis=0) for i in seqs]
    v_all = [jnp.concatenate([v0_ref[i], v1_ref[i], v2_ref[i]], axis=0) for i in seqs]
    lane = lax.broadcasted_iota(jnp.int32, (1, LANES), 1)
    lo = lane < HEAD
    col = lax.broadcasted_iota(jnp.int32, (1, 3 * CHUNK), 1)
    if banded:
        visible = (col >= 2 * CHUNK) | ((col >= CHUNK) & (n >= 1)) | (n >= 2)

    def halves(t, kv_head):
        blk = t[:, (kv_head // 2) * LANES:(kv_head // 2 + 1) * LANES]
        if kv_head % 2 == 0:
            t_lo = jnp.where(lo, blk, 0.0)
            t_hi = pltpu.roll(t_lo, HEAD, axis=1)
        else:
            t_hi = jnp.where(lo, 0.0, blk)
            t_lo = pltpu.roll(t_hi, HEAD, axis=1)
        return t_lo.astype(BF16), t_hi.astype(BF16)

    first_pair = lax.broadcasted_iota(jnp.int32, (2 * CHUNK, 1), 0) < CHUNK

    def probs(s, head_a, head_b):
        if banded:
            s = jnp.where(visible, s, NEG_INF)
        sink = jnp.where(first_pair, sink_ref[head_a], sink_ref[head_b])
        m = jnp.maximum(jnp.max(s, axis=-1, keepdims=True), sink)
        e = jnp.exp(s - m)
        return (e / (jnp.sum(e, axis=-1, keepdims=True) + jnp.exp(sink - m))).astype(BF16)

    units = [(i, g) for i in seqs for g in range(N_KV)]
    scores, values = [], []
    for i, g in units:
        k_lo, k_hi = halves(k_all[i], g)
        values.append(halves(v_all[i], g))
        q2 = jnp.concatenate([q_ref[i, :, (2 * g + j) * LANES:(2 * g + j + 1) * LANES] for j in range(2)], axis=0)
        scores.append((_dot_nt(q2, k_lo), _dot_nt(q2, k_hi)))
    for (i, g), (s_lo, s_hi), (v_lo, v_hi) in zip(units, scores, values):
        o = _dot(probs(s_lo, 4 * g, 4 * g + 2), v_lo) + _dot(probs(s_hi, 4 * g + 1, 4 * g + 3), v_hi)
        for j in range(2):
            pair = 2 * g + j
            o_ref[i, :, pair * LANES:(pair + 1) * LANES] = o[j * CHUNK:(j + 1) * CHUNK].astype(o_ref.dtype)


def _attention(q, kv_arrays, kv_chunks, sinks, nb, s, banded):
    nseq = math.gcd(nb, ATTN_SEQS)
    q_spec = pl.BlockSpec((nseq, CHUNK, D_MODEL), lambda b, c: (b, c, 0))
    kv_specs = [pl.BlockSpec((nseq, CHUNK, KV_W), lambda b, c, f=f: (b, f(c), 0)) for f in kv_chunks]
    return pl.pallas_call(
        functools.partial(_attn_kernel, banded=banded),
        grid=(nb // nseq, s // CHUNK),
        in_specs=[pl.BlockSpec(memory_space=pltpu.SMEM), q_spec] + kv_specs,
        out_specs=q_spec,
        out_shape=jax.ShapeDtypeStruct((nb, s, D_MODEL), BF16),
        compiler_params=_params(("parallel", "parallel")),
        name="attention",
    )(sinks, q, *kv_arrays)


def _pad_cols(w, n):
    return jnp.pad(w, ((0, 0), (0, n - w.shape[1])))


def _pad_rows(w, n):
    return jnp.pad(w, ((0, n - w.shape[0]), (0, 0)))


def _rope_tables(pos):
    half = ROPE_DIM // 2
    inv = jnp.power(jnp.float32(ROPE_THETA), -jnp.arange(half, dtype=F32) * (2.0 / ROPE_DIM))
    ang = pos[:, None] * inv[None, :]
    cos, sin = jnp.cos(ang), jnp.sin(ang)
    rest = HEAD - ROPE_DIM
    ones = jnp.ones((pos.shape[0], rest), F32)
    zeros = jnp.zeros((pos.shape[0], rest), F32)
    z8 = jnp.zeros_like(sin)
    per_head = (jnp.concatenate([cos, cos, ones], axis=1),
                jnp.concatenate([-sin, z8, zeros], axis=1),
                jnp.concatenate([z8, sin, zeros], axis=1))
    return tuple(jnp.tile(t, (1, LANES // HEAD)) for t in per_head)


def _state_to_pairs(state):
    nb = state.shape[0]
    st = state.astype(F32).reshape(nb, PAIRS, 2, HEAD, HEAD)
    z = jnp.zeros_like(st[:, :, 0])
    top = jnp.concatenate([st[:, :, 0], z], axis=-1)
    bot = jnp.concatenate([z, st[:, :, 1]], axis=-1)
    return jnp.concatenate([top, bot], axis=-2)


def _pairs_to_state(st):
    nb = st.shape[0]
    s0 = st[:, :, :HEAD, :HEAD]
    s1 = st[:, :, HEAD:, HEAD:]
    return jnp.stack([s0, s1], axis=2).reshape(nb, 2 * PAIRS, HEAD, HEAD)


def _trunk(x, mods, kvmod, pos, prev_x, prev_wkv, past_k, past_v, w, nb, s):
    n = nb * s
    x = x.reshape(n, D_MODEL)
    mod0, mod1 = mods[0][:, None, :], mods[1][:, None, :]
    kvmod = kvmod[:, None, :]

    tm = min(s, 256)
    r, lw, k, v, a, b, g, last_x = _rwkv_proj(x, mod0, prev_x[:, None, :], w["rw"], nb, s, tm)
    seqs = lambda t: t.reshape(nb, s, D_MODEL)
    z, st = _wkv(*(seqs(t) for t in (r, lw, k, v, a, b, g)), _state_to_pairs(prev_wkv), w["rw"], nb, s)
    z = z.reshape(n, D_MODEL)
    tm = min(n, 512)
    x = _out_proj(x, z, mod0[:, :, 2 * D_MODEL:3 * D_MODEL], w["rw"]["w_o"], w["rw"]["b_o"], nb, s, tm)
    def moe_layer(x, mod, l):
        args = (x, mod, w["norm_g"][l][1:2], w["moe_route"][l], w["moe_gu"][l], w["moe_down"][l], w["final_g"], nb, s)
        if n >= SPARSE_MIN_TOKENS:
            return _moe_sparse(*args, l == 1)
        return _moe(*args, min(n, 1024), l == 1)

    x = moe_layer(x, mod0, 0)

    tm = min(s, 256)
    q, k_new, v_new = _qkv_proj(x, mod1, kvmod, w["at"], _rope_tables(pos), nb, s, tm)
    k_seq, v_seq = k_new.reshape(nb, s, KV_W), v_new.reshape(nb, s, KV_W)
    if past_k is None:
        arrays = [k_seq] * 3 + [v_seq] * 3
        chunks = [lambda c, d=d: jnp.maximum(c - d, 0) for d in (2, 1, 0)] * 2
    else:
        pk = past_k.astype(F32).reshape(nb, 2 * CHUNK, KV_W)
        pv = past_v.astype(F32).reshape(nb, 2 * CHUNK, KV_W)
        arrays = [pk, pk, k_seq, pv, pv, v_seq]
        chunks = [lambda c: 0, lambda c: 1, lambda c: 0] * 2
    o = _attention(q.reshape(nb, s, D_MODEL), arrays, chunks, w["at"]["sinks"], nb, s, past_k is None)
    o = o.reshape(n, D_MODEL)
    tm = min(n, 512)
    x = _out_proj(x, o, mod1[:, :, 2 * D_MODEL:3 * D_MODEL], w["at"]["w_o"], w["at"]["b_o"], nb, s, tm)
    y = moe_layer(x, mod1, 1)

    return y.reshape(nb, s, D_MODEL), last_x.reshape(1, nb, D_MODEL), _pairs_to_state(st)[None], k_seq, v_seq


def kernel(x_prompt, x_sample, state_shift, state_wkv, cache_k, cache_v, c_prompt, c_sample, ada_w, ada_b, norm_g, rw_mu, rw_w_rkv, rw_w0, rw_w1, rw_w2, rw_a0, rw_a1, rw_a2, rw_g1, rw_g2, rw_k_k, rw_k_a, rw_r_k, rw_lnx_w, rw_lnx_b, rw_w_o, kv_ada_w, kv_ada_b, kv_norm_g, w_kv, b_kv, at_w_q, at_b_q, at_sinks, at_w_o, at_b_o, moe_w_group, moe_w_expert, moe_w_gu, moe_w_down, final_norm_g):
    bp, sp, _ = x_prompt.shape
    bs, ss, _ = x_sample.shape
    row = lambda t: t.reshape(1, -1).astype(F32)

    c_all = jnp.concatenate([c_prompt, c_sample], axis=0)
    mods = _cond_linear(c_all, ada_w, ada_b[:, None, :])
    kvmods = _cond_linear(c_all, kv_ada_w[None], kv_ada_b[None, None, :])[0]

    lora_pad, gate_pad = LANES, 2 * LANES
    w = {
        "norm_g": norm_g,
        "final_g": row(final_norm_g),
        "rw": {
            "norm_g": norm_g[0, 0:1], "mu": rw_mu[0], "w_rkv": rw_w_rkv[0].astype(BF16),
            "w0": row(rw_w0[0]), "w1": _pad_cols(rw_w1[0], lora_pad).astype(BF16),
            "w2": _pad_rows(rw_w2[0], lora_pad).astype(BF16),
            "a0": row(rw_a0[0]), "a1": _pad_cols(rw_a1[0], lora_pad).astype(BF16),
            "a2": _pad_rows(rw_a2[0], lora_pad).astype(BF16),
            "g1": _pad_cols(rw_g1[0], gate_pad).astype(BF16), "g2": _pad_rows(rw_g2[0], gate_pad).astype(BF16),
            "k_k": row(rw_k_k[0]), "k_a": row(rw_k_a[0]), "r_k": row(rw_r_k[0]),
            "lnx_w": row(rw_lnx_w[0]), "lnx_b": row(rw_lnx_b[0]),
            "w_o": rw_w_o[0].astype(BF16), "b_o": jnp.zeros((1, D_MODEL), F32),
        },
        "at": {
            "norm_gq": norm_g[1, 0:1], "norm_gkv": row(kv_norm_g),
            "w_q": at_w_q[0].astype(BF16), "b_q": row(at_b_q[0]),
            "w_kv": w_kv.astype(BF16), "b_kv": row(b_kv),
            "sinks": at_sinks[0].astype(F32),
            "w_o": at_w_o[0].astype(BF16), "b_o": row(at_b_o[0]),
        },
        "moe_route": jnp.pad(jnp.concatenate([moe_w_group, moe_w_expert], axis=-1),
                             ((0, 0), (0, 0), (0, LANES - N_GROUPS - N_EXPERTS))),
        "moe_gu": [moe_w_gu[l].astype(BF16) for l in range(2)],
        "moe_down": [moe_w_down[l].astype(BF16) for l in range(2)],
    }

    pos_p = jnp.arange(sp, dtype=F32)
    pos_s = PAST_LEN + jnp.arange(ss, dtype=F32)
    zero_x = jnp.zeros((bp, D_MODEL), x_prompt.dtype)
    zero_wkv = jnp.zeros((bp,) + state_wkv.shape[2:], state_wkv.dtype)
    y_p, p_shift, p_wkv, p_k, p_v = _trunk(x_prompt, mods[:, :bp], kvmods[:bp], pos_p, zero_x, zero_wkv,
                                           None, None, w, bp, sp)
    y_s, s_shift, s_wkv, s_k, s_v = _trunk(x_sample, mods[:, bp:], kvmods[bp:], pos_s, state_shift[0],
                                           state_wkv[0], cache_k, cache_v, w, bs, ss)
    keep = min(2 * CHUNK, sp)
    heads = lambda t: t.reshape(t.shape[0], t.shape[1], N_KV, HEAD)
    return (y_p, y_s, p_shift, p_wkv.astype(state_wkv.dtype), heads(p_k[:, sp - keep:]), heads(p_v[:, sp - keep:]),
            s_shift, s_wkv.astype(state_wkv.dtype), heads(s_k), heads(s_v))
```

```python
import functools
import math

import jax
import jax.numpy as jnp
from jax import lax
from jax.experimental import pallas as pl
from jax.experimental.pallas import tpu as pltpu

F32 = jnp.float32
BF16 = jnp.bfloat16

D_MODEL = 1024
LANES = 128
HEAD = 64
PAIRS = D_MODEL // LANES
CHUNK = 64
WKV_SEQS = 4
ATTN_SEQS = 2
PAST_LEN = 4096
N_KV = 4
KV_W = N_KV * HEAD
ROPE_DIM = 16
ROPE_THETA = 500000.0
ATT_SCALE = HEAD ** -0.5
N_GROUPS = 4
EXP_PER_GROUP = 8
N_EXPERTS = N_GROUPS * EXP_PER_GROUP
D_EXPERT = D_MODEL // 4
ROUTE_LANE0 = N_GROUPS
RMS_EPS = 1e-6
GN_EPS = 64e-5
NEG_INF = -1e30
VMEM_LIMIT = 56 * 1024 * 1024


def _params(sem):
    return pltpu.CompilerParams(dimension_semantics=sem, vmem_limit_bytes=VMEM_LIMIT)


def _dot(a, b):
    return jnp.dot(a, b, preferred_element_type=F32)


def _dot_nt(a, b):
    return lax.dot_general(a, b, (((1,), (1,)), ((), ())), preferred_element_type=F32)


def _dot_tn(a, b):
    return lax.dot_general(a, b, (((0,), (0,)), ((), ())), preferred_element_type=F32)


def _split2(x):
    hi = x.astype(BF16)
    lo = (x - hi.astype(F32)).astype(BF16)
    return hi, lo


def _dot_x3(a, b, dot=_dot):
    ah, al = _split2(a)
    bh, bl = _split2(b)
    return dot(ah, bh) + dot(ah, bl) + dot(al, bh)


def _sigmoid(x):
    return 1.0 / (1.0 + jnp.exp(-x))


def _rms(x):
    return x * lax.rsqrt(jnp.mean(x * x, axis=-1, keepdims=True) + RMS_EPS)


def _modulate(n, gain, shift, scale, nb):
    rows = n.shape[0]
    h = (n * gain).reshape(nb, rows // nb, D_MODEL)
    return (h * (1.0 + scale) + shift).reshape(rows, D_MODEL)


def _const_spec(shape):
    nd = len(shape)
    return pl.BlockSpec(shape, lambda *_: (0,) * nd)


def _cond_kernel(c_ref, w_ref, b_ref, o_ref):
    c = c_ref[...]
    cs = (c * _sigmoid(c)).astype(BF16)
    o_ref[...] = _dot(cs, w_ref[...].astype(BF16)) + b_ref[...]


def _cond_linear(c, w, b, tn=512):
    nl, _, n = w.shape
    m = c.shape[0]
    return pl.pallas_call(
        _cond_kernel,
        grid=(nl, n // tn),
        in_specs=[
            pl.BlockSpec((m, D_MODEL), lambda l, j: (0, 0)),
            pl.BlockSpec((None, D_MODEL, tn), lambda l, j: (l, 0, j)),
            pl.BlockSpec((None, 1, tn), lambda l, j: (l, 0, j)),
        ],
        out_specs=pl.BlockSpec((None, m, tn), lambda l, j: (l, 0, j)),
        out_shape=jax.ShapeDtypeStruct((nl, m, n), F32),
        compiler_params=_params(("parallel", "parallel")),
        name="cond_linear",
    )(c, w, b)


def _rwkv_proj_kernel(x_ref, mod_ref, prev_ref, ng_ref, mu_ref, wrkv_ref, w0_ref, w1_ref, w2_ref,
                      a0_ref, a1_ref, a2_ref, g1_ref, g2_ref, kk_ref, ka_ref,
                      r_ref, lw_ref, k_ref, v_ref, a_ref, b_ref, g_ref, last_ref, carry_ref):
    tm = x_ref.shape[0]

    @pl.when(pl.program_id(1) == 0)
    def _():
        carry_ref[...] = prev_ref[...]

    shift = mod_ref[:, 0:D_MODEL].reshape(1, 1, D_MODEL)
    scale = mod_ref[:, D_MODEL:2 * D_MODEL].reshape(1, 1, D_MODEL)
    h = _modulate(_rms(x_ref[...]), ng_ref[...], shift, scale, 1)
    row = lax.broadcasted_iota(jnp.int32, (tm, 1), 0)
    shifted = jnp.where(row == 0, carry_ref[...], pltpu.roll(h, 1, axis=0))
    carry_ref[...] = h[tm - 1:tm, :]
    last_ref[...] = h[tm - 1:tm, :]
    dx = shifted - h

    def mix(n):
        return (h + dx * mu_ref[n:n + 1, :]).astype(BF16)

    r = _dot(mix(0), wrkv_ref[0])
    k = _dot(mix(1), wrkv_ref[1])
    v = _dot(mix(2), wrkv_ref[2])
    ww = _dot(jnp.tanh(_dot(mix(3), w1_ref[...])).astype(BF16), w2_ref[...])
    z = -(w0_ref[...] + ww)
    softplus = jnp.maximum(z, 0.0) + jnp.log(1.0 + jnp.exp(-jnp.abs(z)))
    lw = -jnp.exp(-softplus - 0.5)
    asig = _sigmoid(a0_ref[...] + _dot(_dot(mix(4), a1_ref[...]).astype(BF16), a2_ref[...]))
    g = _dot(_sigmoid(_dot(mix(5), g1_ref[...])).astype(BF16), g2_ref[...])

    kk = k * kk_ref[...]
    kk2 = kk * kk
    lo_cols = lax.broadcasted_iota(jnp.int32, (tm, LANES), 1) < HEAD

    def head_sums(t):
        lo_sum = jnp.sum(jnp.where(lo_cols, t, 0.0), axis=-1, keepdims=True)
        hi_sum = jnp.sum(jnp.where(lo_cols, 0.0, t), axis=-1, keepdims=True)
        return jnp.where(lo_cols, lo_sum, hi_sum)

    ss = jnp.concatenate([head_sums(kk2[:, p * LANES:(p + 1) * LANES]) for p in range(PAIRS)], axis=1)
    kkn = kk * lax.rsqrt(jnp.maximum(ss, 1e-24))

    r_ref[...] = r.astype(r_ref.dtype)
    lw_ref[...] = lw
    k_ref[...] = (k * (1.0 + (asig - 1.0) * ka_ref[...])).astype(k_ref.dtype)
    v_ref[...] = v.astype(v_ref.dtype)
    a_ref[...] = (-kkn).astype(a_ref.dtype)
    b_ref[...] = (kkn * asig).astype(b_ref.dtype)
    g_ref[...] = g.astype(g_ref.dtype)


def _rwkv_proj(x, mod, prev_x, w, nb, s, tm):
    n = nb * s
    nt = s // tm
    row_spec = pl.BlockSpec((tm, D_MODEL), lambda b, j: (b * nt + j, 0))
    vec = _const_spec((1, D_MODEL))
    lora_w, lora_g = w["w1"].shape[1], w["g1"].shape[1]
    in_specs = [
        row_spec,
        pl.BlockSpec((None, 1, 6 * D_MODEL), lambda b, j: (b, 0, 0)),
        pl.BlockSpec((None, 1, D_MODEL), lambda b, j: (b, 0, 0)),
        vec, _const_spec((6, D_MODEL)), _const_spec((3, D_MODEL, D_MODEL)),
        vec, _const_spec((D_MODEL, lora_w)), _const_spec((lora_w, D_MODEL)),
        vec, _const_spec((D_MODEL, lora_w)), _const_spec((lora_w, D_MODEL)),
        _const_spec((D_MODEL, lora_g)), _const_spec((lora_g, D_MODEL)),
        vec, vec,
    ]
    act = lambda dt: jax.ShapeDtypeStruct((n, D_MODEL), dt)
    out_types = [act(BF16), act(F32)] + [act(BF16)] * 5
    outs = pl.pallas_call(
        _rwkv_proj_kernel,
        grid=(nb, nt),
        in_specs=in_specs,
        out_specs=[row_spec] * 7 + [pl.BlockSpec((None, 1, D_MODEL), lambda b, j: (b, 0, 0))],
        out_shape=out_types + [jax.ShapeDtypeStruct((nb, 1, D_MODEL), F32)],
        scratch_shapes=[pltpu.VMEM((1, D_MODEL), F32)],
        compiler_params=_params(("parallel", "arbitrary")),
        name="rwkv_proj",
    )(x, mod, prev_x, w["norm_g"], w["mu"], w["w_rkv"], w["w0"], w["w1"], w["w2"],
      w["a0"], w["a1"], w["a2"], w["g1"], w["g2"], w["k_k"], w["k_a"])
    return outs


def _split3(x):
    hi = x.astype(BF16)
    rem = x - hi.astype(F32)
    mid = rem.astype(BF16)
    return hi, mid, (rem - mid.astype(F32)).astype(BF16)


def _wkv_kernel(r_ref, lw_ref, k_ref, v_ref, a_ref, b_ref, g_ref, s0_ref, rk_ref, lnw_ref, lnb_ref,
                z_ref, st_ref):
    @pl.when(pl.program_id(1) == 0)
    def _():
        st_ref[...] = s0_ref[...]

    nseq = r_ref.shape[0]
    wide = lambda ref: jnp.concatenate([ref[i] for i in range(nseq)], axis=1)
    per_seq = lambda ref: jnp.concatenate([ref[...]] * nseq, axis=1)
    b16 = lambda t: t.astype(BF16)
    pairs = range(nseq * PAIRS)
    blk = lambda t, p: t[:, p * LANES:(p + 1) * LANES]
    rows = lambda ts: jnp.concatenate(ts, axis=0)
    lo_cols = lax.broadcasted_iota(jnp.int32, (CHUNK, LANES), 1) < HEAD
    rr = lax.broadcasted_iota(jnp.int32, (LANES, LANES), 0)
    cc = lax.broadcasted_iota(jnp.int32, (LANES, LANES), 1)
    head0 = lambda t: jnp.where(lo_cols, t, 0.0)
    head1 = lambda t: jnp.where(lo_cols, 0.0, t)
    blockdiag = lambda q: rows([head0(q), head1(q)])

    r, lw, k, v, a, b = (wide(t).astype(F32) for t in (r_ref, lw_ref, k_ref, v_ref, a_ref, b_ref))
    ti = lax.broadcasted_iota(jnp.int32, (CHUNK, CHUNK), 0)
    tj = lax.broadcasted_iota(jnp.int32, (CHUNK, CHUNK), 1)
    tri = (tj <= ti).astype(BF16)
    cum = sum(_dot(tri, t) for t in _split3(lw))
    cl = cum[CHUNK - 1:CHUNK, :]
    e_neg = jnp.exp(-cum)
    e_end = jnp.exp(cl - cum)
    e_cl = jnp.exp(cl)
    at = b16(a * jnp.exp(cum - lw))
    rt = b16(r * jnp.exp(cum))
    bt = b * e_neg
    kt = k * e_neg
    bkh = rows([b16(b * e_end), b16(k * e_end)])
    v16 = b16(v)

    t_idx = rr & (CHUNK - 1)
    j_idx = cc & (CHUNK - 1)
    keep = (j_idx < t_idx) | ((rr >= CHUNK) & (j_idx == t_idx))
    g0, g1 = [], []
    for p in pairs:
        btp, ktp = blk(bt, p), blk(kt, p)
        w = b16(rows([head0(btp), head0(ktp), head1(ktp), head1(btp)]))
        g = _dot_nt(rows([blk(at, p), blk(rt, p)]), w)
        g0.append(jnp.where(keep, g[:, :LANES], 0.0))
        g1.append(jnp.where(keep, g[:, LANES:], 0.0))

    eye2 = jnp.where((lax.broadcasted_iota(jnp.int32, (CHUNK, LANES), 1) & (CHUNK - 1))
                     == lax.broadcasted_iota(jnp.int32, (CHUNK, LANES), 0), 1.0, 0.0)
    pc = [jnp.where(lo_cols, g0[p][:CHUNK], g1[p][:CHUNK]) for p in pairs]
    tc = [eye2 + pc[p] for p in pairs]
    q = [_dot(b16(pc[p]), b16(blockdiag(pc[p]))) for p in pairs]
    for _ in range(4):
        res = [_dot(b16(rows([tc[p], q[p]])), b16(blockdiag(q[p]))) for p in pairs]
        tc = [tc[p] + res[p][:CHUNK] for p in pairs]
        q = [res[p][CHUNK:] for p in pairs]
    tc = [b16(tc[p] + _dot(b16(tc[p]), b16(blockdiag(q[p])))) for p in pairs]

    akv = []
    for p in pairs:
        w_ak = b16(jnp.where(lo_cols, g1[p][:CHUNK], g0[p][:CHUNK]))
        vp = blk(v16, p)
        akv.append(_dot(w_ak, rows([head1(vp), head0(vp)])))

    s = [st_ref[p // PAIRS, p % PAIRS] for p in pairs]
    s16 = [b16(s[p]) for p in pairs]
    rhs = [_dot_nt(blk(at, p), s16[p]) + akv[p] for p in pairs]
    u = [_dot(tc[p], b16(rows([head0(rhs[p]), head1(rhs[p])]))) for p in pairs]
    u16 = [b16(u[p]) for p in pairs]

    ys = []
    bd_mask = (rr < HEAD) == (cc < HEAD)
    for p in pairs:
        vp = blk(v16, p)
        rbk = b16(jnp.concatenate([g0[p][CHUNK:], g1[p][CHUNK:]], axis=1))
        uv = rows([head0(u16[p]), head0(vp), head1(vp), head1(u16[p])])
        ys.append(_dot_nt(blk(rt, p), s16[p]) + _dot(rbk, uv))
        fresh = _dot_tn(rows([u16[p], vp]), blk(bkh, p))
        st_ref[p // PAIRS, p % PAIRS] = s[p] * blk(e_cl, p) + jnp.where(bd_mask, fresh, 0.0)

    def head_sums(t):
        lo_sum = jnp.sum(head0(t), axis=-1, keepdims=True)
        hi_sum = jnp.sum(head1(t), axis=-1, keepdims=True)
        return jnp.where(lo_cols, lo_sum, hi_sum)

    rkk = r * k * per_seq(rk_ref)
    for p in pairs:
        seq, cols = p // PAIRS, slice((p % PAIRS) * LANES, (p % PAIRS + 1) * LANES)
        yc = ys[p] - head_sums(ys[p]) * (1.0 / HEAD)
        yn = yc * lax.rsqrt(head_sums(yc * yc) * (1.0 / HEAD) + GN_EPS)
        zp = (yn * lnw_ref[:, cols] + lnb_ref[:, cols] + head_sums(blk(rkk, p)) * blk(v, p)) * g_ref[seq, :, cols].astype(F32)
        z_ref[seq, :, cols] = zp.astype(z_ref.dtype)


def _wkv(r, lw, k, v, a, b, g, st0, w, nb, s):
    nseq = math.gcd(nb, WKV_SEQS)
    blk = pl.BlockSpec((nseq, CHUNK, D_MODEL), lambda bi, c: (bi, c, 0))
    vec = _const_spec((1, D_MODEL))
    st_spec = pl.BlockSpec((nseq, PAIRS, LANES, LANES), lambda bi, c: (bi, 0, 0, 0))
    return pl.pallas_call(
        _wkv_kernel,
        grid=(nb // nseq, s // CHUNK),
        in_specs=[blk] * 7 + [st_spec, vec, vec, vec],
        out_specs=[blk, st_spec],
        out_shape=[jax.ShapeDtypeStruct((nb, s, D_MODEL), BF16),
                   jax.ShapeDtypeStruct((nb, PAIRS, LANES, LANES), F32)],
        compiler_params=_params(("parallel", "arbitrary")),
        name="wkv",
    )(r, lw, k, v, a, b, g, st0, w["r_k"], w["lnx_w"], w["lnx_b"])


def _out_proj_kernel(x_ref, z_ref, gate_ref, w_ref, bias_ref, o_ref, *, nb):
    rows = x_ref.shape[0]
    y = _dot(z_ref[...], w_ref[...]) + bias_ref[...]
    y = (y.reshape(nb, rows // nb, D_MODEL) * gate_ref[...]).reshape(rows, D_MODEL)
    o_ref[...] = x_ref[...] + y


def _out_proj(x, z, gate, w, bias, nb, s, tm):
    n = nb * s
    tb = max(tm // s, 1)
    per_seq = max(s // tm, 1)
    row_spec = pl.BlockSpec((tm, D_MODEL), lambda i: (i, 0))
    return pl.pallas_call(
        functools.partial(_out_proj_kernel, nb=tb),
        grid=(n // tm,),
        in_specs=[row_spec, row_spec,
                  pl.BlockSpec((tb, 1, D_MODEL), lambda i: (i // per_seq, 0, 0)),
                  _const_spec((D_MODEL, D_MODEL)), _const_spec((1, D_MODEL))],
        out_specs=row_spec,
        out_shape=jax.ShapeDtypeStruct((n, D_MODEL), F32),
        compiler_params=_params(("parallel",)),
        name="out_proj",
    )(x, z, gate, w, bias)


def _top2(logits):
    lane = lax.broadcasted_iota(jnp.int32, logits.shape, 1).astype(F32)
    first_of = lambda hit: jnp.min(jnp.where(hit, lane, float(LANES)), axis=-1, keepdims=True)
    lg = jnp.where(lane < N_GROUPS, logits, -jnp.inf)
    gmax = jnp.max(lg, axis=-1, keepdims=True)
    gi = first_of(lg == gmax)
    gp = 1.0 / jnp.sum(jnp.exp(lg - gmax), axis=-1, keepdims=True)
    first = ROUTE_LANE0 + gi * EXP_PER_GROUP
    le = jnp.where((lane >= first) & (lane < first + EXP_PER_GROUP), logits, -jnp.inf)
    top1 = jnp.max(le, axis=-1, keepdims=True)
    i1 = first_of(le == top1)
    le2 = jnp.where(lane == i1, -jnp.inf, le)
    top2 = jnp.max(le2, axis=-1, keepdims=True)
    i2 = first_of(le2 == top2)
    e2 = jnp.exp(top2 - top1)
    w1 = gp / (1.0 + e2)
    return i1, i2, w1, w1 * e2


def _route(logits):
    lane = lax.broadcasted_iota(jnp.int32, logits.shape, 1).astype(F32)
    i1, i2, w1, w2 = _top2(logits)
    return jnp.where(lane == i1, w1, 0.0) + jnp.where(lane == i2, w2, 0.0)


def _moe_kernel(x_ref, mod_ref, ng_ref, wr_ref, wgu_ref, wdn_ref, fg_ref, o_ref, h_scr, cw_scr, acc_scr,
                *, nb, final_norm):
    e = pl.program_id(1)
    rows = x_ref.shape[0]

    @pl.when(e == 0)
    def _():
        shift = mod_ref[:, :, 3 * D_MODEL:4 * D_MODEL]
        scale = mod_ref[:, :, 4 * D_MODEL:5 * D_MODEL]
        h = _modulate(_rms(x_ref[...]), ng_ref[...], shift, scale, nb)
        h_scr[...] = h.astype(BF16)
        cw_scr[...] = _route(_dot_x3(h, wr_ref[...]))
        acc_scr[...] = jnp.zeros_like(acc_scr)

    lane = lax.broadcasted_iota(jnp.int32, (rows, LANES), 1)
    cw = jnp.sum(jnp.where(lane == ROUTE_LANE0 + e, cw_scr[...], 0.0), axis=-1, keepdims=True)
    gu = _dot(h_scr[...], wgu_ref[...])
    gate = gu[:, :D_EXPERT]
    hid = gate * _sigmoid(gate) * gu[:, D_EXPERT:] * cw
    acc_scr[...] += _dot(hid.astype(BF16), wdn_ref[...])

    @pl.when(e == N_EXPERTS - 1)
    def _():
        g2 = mod_ref[:, :, 5 * D_MODEL:6 * D_MODEL]
        y = (acc_scr[...].reshape(nb, rows // nb, D_MODEL) * g2).reshape(rows, D_MODEL)
        out = x_ref[...] + y
        if final_norm:
            out = _rms(out) * fg_ref[...]
        o_ref[...] = out


def _moe(x, mod, norm_g, w_route, w_gu, w_down, final_g, nb, s, layer, tm, final_norm):
    n = nb * s
    tb = max(tm // s, 1)
    per_seq = max(s // tm, 1)
    row_spec = pl.BlockSpec((tm, D_MODEL), lambda i, e: (i, 0))
    return pl.pallas_call(
        functools.partial(_moe_kernel, nb=tb, final_norm=final_norm),
        grid=(n // tm, N_EXPERTS),
        in_specs=[row_spec,
                  pl.BlockSpec((tb, 1, 6 * D_MODEL), lambda i, e: (i // per_seq, 0, 0)),
                  _const_spec((1, D_MODEL)), _const_spec((D_MODEL, LANES)),
                  pl.BlockSpec((None, None, D_MODEL, 2 * D_EXPERT), lambda i, e: (layer, e, 0, 0)),
                  pl.BlockSpec((None, None, D_EXPERT, D_MODEL), lambda i, e: (layer, e, 0, 0)),
                  _const_spec((1, D_MODEL))],
        out_specs=row_spec,
        out_shape=jax.ShapeDtypeStruct((n, D_MODEL), F32),
        scratch_shapes=[pltpu.VMEM((tm, D_MODEL), BF16), pltpu.VMEM((tm, LANES), F32),
                        pltpu.VMEM((tm, D_MODEL), F32)],
        compiler_params=_params(("parallel", "arbitrary")),
        name="moe",
    )(x, mod, norm_g, w_route, w_gu, w_down, final_g)


PAIRS_PER_GROUP = EXP_PER_GROUP * (EXP_PER_GROUP - 1) // 2
N_BUCKETS = N_GROUPS * PAIRS_PER_GROUP
BUCKET_TILE = 256
ROW_W = D_MODEL + LANES
META_BUCKET, META_RANK, META_W_LO, META_W_HI = 0, 1, 2, 3
ROUTER_TILE = 512
MOVE_TILE = 512
DMA_UNROLL = 8
SPARSE_MIN_TOKENS = 4096


def _moe_norm(x_ref, mod_ref, ng_ref, nb):
    shift = mod_ref[:, :, 3 * D_MODEL:4 * D_MODEL]
    scale = mod_ref[:, :, 4 * D_MODEL:5 * D_MODEL]
    return _modulate(_rms(x_ref[...]), ng_ref[...], shift, scale, nb)


def _router_kernel(x_ref, mod_ref, ng_ref, wr_ref, earlier_ref, meta_ref, cnt_ref, *, nb):
    @pl.when(pl.program_id(0) == 0)
    def _():
        cnt_ref[...] = jnp.zeros_like(cnt_ref)

    rows = x_ref.shape[0]
    h = _moe_norm(x_ref, mod_ref, ng_ref, nb)
    i1, i2, w1, w2 = _top2(_dot_x3(h, wr_ref[...]))
    lo = (jnp.minimum(i1, i2) - ROUTE_LANE0).astype(jnp.int32)
    hi = (jnp.maximum(i1, i2) - ROUTE_LANE0).astype(jnp.int32)
    first_is_lo = i1 < i2
    a = lo & (EXP_PER_GROUP - 1)
    b = hi & (EXP_PER_GROUP - 1)
    group = lo >> (EXP_PER_GROUP.bit_length() - 1)
    bucket = group * PAIRS_PER_GROUP + ((a * (2 * EXP_PER_GROUP - 1 - a)) >> 1) + (b - a - 1)

    lane = lax.broadcasted_iota(jnp.int32, (rows, LANES), 1)
    mine = lane == bucket
    onehot = jnp.where(mine, 1.0, 0.0)
    seen = cnt_ref[...]
    before = _dot(earlier_ref[...], onehot.astype(BF16)) + seen
    rank = jnp.sum(jnp.where(mine, before, 0.0), axis=-1, keepdims=True)
    cnt_ref[...] = seen + jnp.sum(onehot, axis=0, keepdims=True)

    meta = jnp.where(lane == META_BUCKET, bucket.astype(F32), 0.0)
    meta = jnp.where(lane == META_RANK, rank, meta)
    meta = jnp.where(lane == META_W_LO, jnp.where(first_is_lo, w1, w2), meta)
    meta_ref[...] = jnp.where(lane == META_W_HI, jnp.where(first_is_lo, w2, w1), meta)


def _positions_kernel(meta_ref, start_ref, pos_ref):
    meta = meta_ref[...]
    lane = lax.broadcasted_iota(jnp.int32, meta.shape, 1)
    column = lambda c: jnp.sum(jnp.where(lane == c, meta, 0.0), axis=-1, keepdims=True)
    value = jnp.where(lane == column(META_BUCKET).astype(jnp.int32), start_ref[...] + column(META_RANK), 0.0)
    ones = jnp.ones((8, LANES), BF16)
    pos = sum(_dot_nt(ones, piece) for piece in _split3(value))
    pos_ref[...] = pos[0:1, :].astype(jnp.int32)


def _dispatch_kernel(pos_ref, x_ref, mod_ref, ng_ref, meta_ref, xs_ref, buf, sem, *, nb):
    i = pl.program_id(0)
    rows = x_ref.shape[0]
    slot = i % 2
    buf[slot, :, 0:D_MODEL] = _moe_norm(x_ref, mod_ref, ng_ref, nb)
    buf[slot, :, D_MODEL:ROW_W] = meta_ref[...]

    def send(r, carry):
        pltpu.make_async_copy(buf.at[slot, pl.ds(r, 1), :],
                              xs_ref.at[pl.ds(pos_ref[i * rows + r], 1), :], sem.at[slot]).start()
        return carry

    lax.fori_loop(0, rows, send, 0, unroll=DMA_UNROLL)

    def wait_all(sl):
        pltpu.make_async_copy(buf.at[sl], xs_ref.at[pl.ds(0, rows), :], sem.at[sl]).wait()

    @pl.when(i >= 1)
    def _():
        wait_all(1 - slot)

    @pl.when(i == pl.num_programs(0) - 1)
    def _():
        wait_all(slot)


def _experts_kernel(tile_ref, bucket_ref, lo_ref, hi_ref, used_ref, xs_ref, gu_lo_ref, gu_hi_ref,
                    dn_lo_ref, dn_hi_ref, ys_ref):
    k = pl.program_id(0)

    @pl.when(k < used_ref[0])
    def _():
        rows = xs_ref.shape[0]
        x16 = xs_ref[:, 0:D_MODEL].astype(BF16)
        meta = xs_ref[:, D_MODEL:ROW_W]
        lane = lax.broadcasted_iota(jnp.int32, (rows, LANES), 1)
        column = lambda c: jnp.sum(jnp.where(lane == c, meta, 0.0), axis=-1, keepdims=True)
        member = column(META_BUCKET) == bucket_ref[k].astype(F32)

        def expert(gu_ref, dn_ref, col):
            gu = _dot(x16, gu_ref[...])
            gate = gu[:, :D_EXPERT]
            hid = gate * _sigmoid(gate) * gu[:, D_EXPERT:] * jnp.where(member, column(col), 0.0)
            return _dot(hid.astype(BF16), dn_ref[...])

        y = expert(gu_lo_ref, dn_lo_ref, META_W_LO) + expert(gu_hi_ref, dn_hi_ref, META_W_HI)
        new_tile = (k == 0) | (tile_ref[k] != tile_ref[jnp.maximum(k - 1, 0)])

        @pl.when(new_tile)
        def _():
            ys_ref[...] = y

        @pl.when(jnp.logical_not(new_tile))
        def _():
            ys_ref[...] += y


def _combine_kernel(pos_ref, x_ref, mod_ref, fg_ref, ys_ref, o_ref, buf, sem, *, nb, final_norm):
    i = pl.program_id(0)
    rows = x_ref.shape[0]
    slot = i % 2

    def fetch(step, sl):
        def one(r, carry):
            pltpu.make_async_copy(ys_ref.at[pl.ds(pos_ref[step * rows + r], 1), :],
                                  buf.at[sl, pl.ds(r, 1), :], sem.at[sl]).start()
            return carry
        lax.fori_loop(0, rows, one, 0, unroll=DMA_UNROLL)

    @pl.when(i == 0)
    def _():
        fetch(0, 0)

    pltpu.make_async_copy(ys_ref.at[pl.ds(0, rows), :], buf.at[slot], sem.at[slot]).wait()

    @pl.when(i + 1 < pl.num_programs(0))
    def _():
        fetch(i + 1, 1 - slot)

    g2 = mod_ref[:, :, 5 * D_MODEL:6 * D_MODEL]
    y = (buf[slot].reshape(nb, rows // nb, D_MODEL) * g2).reshape(rows, D_MODEL)
    out = x_ref[...] + y
    if final_norm:
        out = _rms(out) * fg_ref[...]
    o_ref[...] = out


def _bucket_experts():
    lo, hi = [], []
    for g in range(N_GROUPS):
        for a in range(EXP_PER_GROUP):
            for b in range(a + 1, EXP_PER_GROUP):
                lo.append(g * EXP_PER_GROUP + a)
                hi.append(g * EXP_PER_GROUP + b)
    return jnp.asarray(lo, jnp.int32), jnp.asarray(hi, jnp.int32)


def _moe_sparse(x, mod, norm_g, w_route, w_gu, w_down, final_g, nb, s, layer, final_norm):
    n = nb * s
    move_params = pltpu.CompilerParams(dimension_semantics=("arbitrary",), vmem_limit_bytes=VMEM_LIMIT,
                                       disable_bounds_checks=True)

    def token_specs(tm):
        tb, per_seq = max(tm // s, 1), max(s // tm, 1)
        row = pl.BlockSpec((tm, D_MODEL), lambda i, *_: (i, 0))
        modspec = pl.BlockSpec((tb, 1, 6 * D_MODEL), lambda i, *_: (i // per_seq, 0, 0))
        meta = pl.BlockSpec((tm, LANES), lambda i, *_: (i, 0))
        return tb, row, modspec, meta

    vec = pl.BlockSpec((1, D_MODEL), lambda *_: (0, 0))
    tb, row, modspec, metaspec = token_specs(ROUTER_TILE)
    meta, counts = pl.pallas_call(
        functools.partial(_router_kernel, nb=tb),
        grid=(n // ROUTER_TILE,),
        in_specs=[row, modspec, vec, pl.BlockSpec((D_MODEL, LANES), lambda i: (0, 0)),
                  pl.BlockSpec((ROUTER_TILE, ROUTER_TILE), lambda i: (0, 0))],
        out_specs=[metaspec, pl.BlockSpec((1, LANES), lambda i: (0, 0))],
        out_shape=[jax.ShapeDtypeStruct((n, LANES), F32), jax.ShapeDtypeStruct((1, LANES), F32)],
        compiler_params=_params(("arbitrary",)),
        name="moe_router",
    )(x, mod, norm_g, w_route, jnp.tril(jnp.ones((ROUTER_TILE, ROUTER_TILE), BF16), -1))

    counts = counts[0, :N_BUCKETS].astype(jnp.int32)
    end = jnp.cumsum(counts)
    start = end - counts
    first_tile = start // BUCKET_TILE
    per_bucket = jnp.where(counts > 0, (end - 1) // BUCKET_TILE - first_tile + 1, 0)
    item_end = jnp.cumsum(per_bucket)
    used = item_end[-1:]
    n_items = n // BUCKET_TILE + N_BUCKETS
    k = jnp.minimum(jnp.arange(n_items, dtype=jnp.int32), used - 1)
    item_bucket = jnp.sum(item_end[None, :] <= k[:, None], axis=1, dtype=jnp.int32)
    item_tile = first_tile[item_bucket] + k - (item_end - per_bucket)[item_bucket]
    bucket_lo, bucket_hi = _bucket_experts()
    item_lo, item_hi = bucket_lo[item_bucket], bucket_hi[item_bucket]
    start_row = jnp.pad(start, (0, LANES - N_BUCKETS)).astype(F32)[None, :]
    pos = pl.pallas_call(
        _positions_kernel,
        grid=(n // ROUTER_TILE,),
        in_specs=[metaspec, pl.BlockSpec((1, LANES), lambda i: (0, 0))],
        out_specs=pl.BlockSpec((1, ROUTER_TILE), lambda i: (0, i)),
        out_shape=jax.ShapeDtypeStruct((1, n), jnp.int32),
        compiler_params=_params(("parallel",)),
        name="moe_positions",
    )(meta, start_row).reshape(n)

    tb, row, modspec, metaspec = token_specs(MOVE_TILE)
    any_spec = pl.BlockSpec(memory_space=pl.ANY)
    xs = pl.pallas_call(
        functools.partial(_dispatch_kernel, nb=tb),
        grid_spec=pltpu.PrefetchScalarGridSpec(
            num_scalar_prefetch=1, grid=(n // MOVE_TILE,),
            in_specs=[row, modspec, vec, metaspec], out_specs=any_spec,
            scratch_shapes=[pltpu.VMEM((2, MOVE_TILE, ROW_W), F32), pltpu.SemaphoreType.DMA((2,))]),
        out_shape=jax.ShapeDtypeStruct((n, ROW_W), F32),
        compiler_params=move_params,
        name="moe_dispatch",
    )(pos, x, mod, norm_g, meta)

    tile_of = lambda k, tile, *_: (tile[k], 0)
    ys = pl.pallas_call(
        _experts_kernel,
        grid_spec=pltpu.PrefetchScalarGridSpec(
            num_scalar_prefetch=5, grid=(n_items,),
            in_specs=[pl.BlockSpec((BUCKET_TILE, ROW_W), tile_of),
                      pl.BlockSpec((None, None, D_MODEL, 2 * D_EXPERT), lambda k, t, b, lo, hi, nu: (layer, lo[k], 0, 0)),
                      pl.BlockSpec((None, None, D_MODEL, 2 * D_EXPERT), lambda k, t, b, lo, hi, nu: (layer, hi[k], 0, 0)),
                      pl.BlockSpec((None, None, D_EXPERT, D_MODEL), lambda k, t, b, lo, hi, nu: (layer, lo[k], 0, 0)),
                      pl.BlockSpec((None, None, D_EXPERT, D_MODEL), lambda k, t, b, lo, hi, nu: (layer, hi[k], 0, 0))],
            out_specs=pl.BlockSpec((BUCKET_TILE, D_MODEL), tile_of)),
        out_shape=jax.ShapeDtypeStruct((n, D_MODEL), F32),
        compiler_params=_params(("arbitrary",)),
        name="moe_experts",
    )(item_tile, item_bucket, item_lo, item_hi, used, xs, w_gu, w_gu, w_down, w_down)

    return pl.pallas_call(
        functools.partial(_combine_kernel, nb=tb, final_norm=final_norm),
        grid_spec=pltpu.PrefetchScalarGridSpec(
            num_scalar_prefetch=1, grid=(n // MOVE_TILE,),
            in_specs=[row, modspec, vec, any_spec], out_specs=row,
            scratch_shapes=[pltpu.VMEM((2, MOVE_TILE, D_MODEL), F32), pltpu.SemaphoreType.DMA((2,))]),
        out_shape=jax.ShapeDtypeStruct((n, D_MODEL), F32),
        compiler_params=move_params,
        name="moe_combine",
    )(pos, x, mod, final_g, ys)


def _rope(x, cos, sin_lo, sin_hi):
    return x * cos + pltpu.roll(x, LANES - ROPE_DIM // 2, axis=1) * sin_lo + pltpu.roll(x, ROPE_DIM // 2, axis=1) * sin_hi


def _qkv_kernel(x_ref, mod_ref, kvmod_ref, ngq_ref, ngkv_ref, wq_ref, bq_ref, wkv_ref, bkv_ref,
                cos_ref, slo_ref, shi_ref, q_ref, k_ref, v_ref):
    n = _rms(x_ref[...])
    shift = mod_ref[:, 0:D_MODEL].reshape(1, 1, D_MODEL)
    scale = mod_ref[:, D_MODEL:2 * D_MODEL].reshape(1, 1, D_MODEL)
    hq = _modulate(n, ngq_ref[...], shift, scale, 1)
    kshift = kvmod_ref[:, 0:D_MODEL].reshape(1, 1, D_MODEL)
    kscale = kvmod_ref[:, D_MODEL:2 * D_MODEL].reshape(1, 1, D_MODEL)
    hkv = _modulate(n, ngkv_ref[...], kshift, kscale, 1)
    cos, slo, shi = cos_ref[...], slo_ref[...], shi_ref[...]
    q = _dot(hq.astype(BF16), wq_ref[...]) + bq_ref[...]
    for p in range(PAIRS):
        cols = slice(p * LANES, (p + 1) * LANES)
        q_ref[:, cols] = (_rope(q[:, cols], cos, slo, shi) * ATT_SCALE).astype(q_ref.dtype)
    kv = _dot(hkv.astype(BF16), wkv_ref[...]) + bkv_ref[...]
    for p in range(KV_W // LANES):
        cols = slice(p * LANES, (p + 1) * LANES)
        k_ref[:, cols] = _rope(kv[:, cols], cos, slo, shi)
    v_ref[...] = kv[:, KV_W:]


def _qkv_proj(x, mod, kvmod, w, tables, nb, s, tm):
    n = nb * s
    nt = s // tm
    row_spec = pl.BlockSpec((tm, D_MODEL), lambda b, j: (b * nt + j, 0))
    kv_spec = pl.BlockSpec((tm, KV_W), lambda b, j: (b * nt + j, 0))
    tab_spec = pl.BlockSpec((tm, LANES), lambda b, j: (j, 0))
    vec = _const_spec((1, D_MODEL))
    return pl.pallas_call(
        _qkv_kernel,
        grid=(nb, nt),
        in_specs=[row_spec,
                  pl.BlockSpec((None, 1, 6 * D_MODEL), lambda b, j: (b, 0, 0)),
                  pl.BlockSpec((None, 1, 2 * D_MODEL), lambda b, j: (b, 0, 0)),
                  vec, vec, _const_spec((D_MODEL, D_MODEL)), vec,
                  _const_spec((D_MODEL, 2 * KV_W)), _const_spec((1, 2 * KV_W)),
                  tab_spec, tab_spec, tab_spec],
        out_specs=[row_spec, kv_spec, kv_spec],
        out_shape=[jax.ShapeDtypeStruct((n, D_MODEL), BF16),
                   jax.ShapeDtypeStruct((n, KV_W), F32), jax.ShapeDtypeStruct((n, KV_W), F32)],
        compiler_params=_params(("parallel", "parallel")),
        name="qkv_proj",
    )(x, mod, kvmod, w["norm_gq"], w["norm_gkv"], w["w_q"], w["b_q"], w["w_kv"], w["b_kv"], *tables)


def _attn_kernel(sink_ref, q_ref, k0_ref, k1_ref, k2_ref, v0_ref, v1_ref, v2_ref, o_ref, *, banded):
    n = pl.program_id(1)
    seqs = range(q_ref.shape[0])
    k_all = [jnp.concatenate([k0_ref[i], k1_ref[i], k2_ref[i]], axis=0) for i in seqs]
    v_all = [jnp.concatenate([v0_ref[i], v1_ref[i], v2_ref[i]], axis=0) for i in seqs]
    lane = lax.broadcasted_iota(jnp.int32, (1, LANES), 1)
    lo = lane < HEAD
    col = lax.broadcasted_iota(jnp.int32, (1, 3 * CHUNK), 1)
    if banded:
        visible = (col >= 2 * CHUNK) | ((col >= CHUNK) & (n >= 1)) | (n >= 2)

    def halves(t, kv_head):
        blk = t[:, (kv_head // 2) * LANES:(kv_head // 2 + 1) * LANES]
        if kv_head % 2 == 0:
            t_lo = jnp.where(lo, blk, 0.0)
            t_hi = pltpu.roll(t_lo, HEAD, axis=1)
        else:
            t_hi = jnp.where(lo, 0.0, blk)
            t_lo = pltpu.roll(t_hi, HEAD, axis=1)
        return t_lo.astype(BF16), t_hi.astype(BF16)

    first_pair = lax.broadcasted_iota(jnp.int32, (2 * CHUNK, 1), 0) < CHUNK

    def probs(s, head_a, head_b):
        if banded:
            s = jnp.where(visible, s, NEG_INF)
        sink = jnp.where(first_pair, sink_ref[head_a], sink_ref[head_b])
        m = jnp.maximum(jnp.max(s, axis=-1, keepdims=True), sink)
        e = jnp.exp(s - m)
        return (e / (jnp.sum(e, axis=-1, keepdims=True) + jnp.exp(sink - m))).astype(BF16)

    units = [(i, g) for i in seqs for g in range(N_KV)]
    scores, values = [], []
    for i, g in units:
        k_lo, k_hi = halves(k_all[i], g)
        values.append(halves(v_all[i], g))
        q2 = jnp.concatenate([q_ref[i, :, (2 * g + j) * LANES:(2 * g + j + 1) * LANES] for j in range(2)], axis=0)
        scores.append((_dot_nt(q2, k_lo), _dot_nt(q2, k_hi)))
    for (i, g), (s_lo, s_hi), (v_lo, v_hi) in zip(units, scores, values):
        o = _dot(probs(s_lo, 4 * g, 4 * g + 2), v_lo) + _dot(probs(s_hi, 4 * g + 1, 4 * g + 3), v_hi)
        for j in range(2):
            pair = 2 * g + j
            o_ref[i, :, pair * LANES:(pair + 1) * LANES] = o[j * CHUNK:(j + 1) * CHUNK].astype(o_ref.dtype)


def _attention(q, kv_arrays, kv_chunks, sinks, nb, s, banded):
    nseq = math.gcd(nb, ATTN_SEQS)
    q_spec = pl.BlockSpec((nseq, CHUNK, D_MODEL), lambda b, c: (b, c, 0))
    kv_specs = [pl.BlockSpec((nseq, CHUNK, KV_W), lambda b, c, f=f: (b, f(c), 0)) for f in kv_chunks]
    return pl.pallas_call(
        functools.partial(_attn_kernel, banded=banded),
        grid=(nb // nseq, s // CHUNK),
        in_specs=[pl.BlockSpec(memory_space=pltpu.SMEM), q_spec] + kv_specs,
        out_specs=q_spec,
        out_shape=jax.ShapeDtypeStruct((nb, s, D_MODEL), BF16),
        compiler_params=_params(("parallel", "parallel")),
        name="attention",
    )(sinks, q, *kv_arrays)


def _pad_cols(w, n):
    return jnp.pad(w, ((0, 0), (0, n - w.shape[1])))


def _pad_rows(w, n):
    return jnp.pad(w, ((0, n - w.shape[0]), (0, 0)))


def _rope_tables(pos):
    half = ROPE_DIM // 2
    inv = jnp.power(jnp.float32(ROPE_THETA), -jnp.arange(half, dtype=F32) * (2.0 / ROPE_DIM))
    ang = pos[:, None] * inv[None, :]
    cos, sin = jnp.cos(ang), jnp.sin(ang)
    rest = HEAD - ROPE_DIM
    ones = jnp.ones((pos.shape[0], rest), F32)
    zeros = jnp.zeros((pos.shape[0], rest), F32)
    z8 = jnp.zeros_like(sin)
    per_head = (jnp.concatenate([cos, cos, ones], axis=1),
                jnp.concatenate([-sin, z8, zeros], axis=1),
                jnp.concatenate([z8, sin, zeros], axis=1))
    return tuple(jnp.tile(t, (1, LANES // HEAD)) for t in per_head)


def _state_to_pairs(state):
    nb = state.shape[0]
    st = state.astype(F32).reshape(nb, PAIRS, 2, HEAD, HEAD)
    z = jnp.zeros_like(st[:, :, 0])
    top = jnp.concatenate([st[:, :, 0], z], axis=-1)
    bot = jnp.concatenate([z, st[:, :, 1]], axis=-1)
    return jnp.concatenate([top, bot], axis=-2)


def _pairs_to_state(st):
    nb = st.shape[0]
    s0 = st[:, :, :HEAD, :HEAD]
    s1 = st[:, :, HEAD:, HEAD:]
    return jnp.stack([s0, s1], axis=2).reshape(nb, 2 * PAIRS, HEAD, HEAD)


def _trunk(x, mods, kvmod, pos, prev_x, prev_wkv, past_k, past_v, w, nb, s):
    n = nb * s
    x = x.reshape(n, D_MODEL)
    mod0, mod1 = mods[0][:, None, :], mods[1][:, None, :]
    kvmod = kvmod[:, None, :]

    tm = min(s, 256)
    r, lw, k, v, a, b, g, last_x = _rwkv_proj(x, mod0, prev_x[:, None, :], w["rw"], nb, s, tm)
    seqs = lambda t: t.reshape(nb, s, D_MODEL)
    z, st = _wkv(*(seqs(t) for t in (r, lw, k, v, a, b, g)), _state_to_pairs(prev_wkv), w["rw"], nb, s)
    z = z.reshape(n, D_MODEL)
    tm = min(n, 512)
    x = _out_proj(x, z, mod0[:, :, 2 * D_MODEL:3 * D_MODEL], w["rw"]["w_o"], w["rw"]["b_o"], nb, s, tm)
    def moe_layer(x, mod, l):
        args = (x, mod, w["norm_g"][l][1:2], w["moe_route"][l], w["moe_gu"], w["moe_down"], w["final_g"], nb, s, l)
        if n >= SPARSE_MIN_TOKENS:
            return _moe_sparse(*args, l == 1)
        return _moe(*args, min(n, 1024), l == 1)

    x = moe_layer(x, mod0, 0)

    tm = min(s, 256)
    q, k_new, v_new = _qkv_proj(x, mod1, kvmod, w["at"], _rope_tables(pos), nb, s, tm)
    k_seq, v_seq = k_new.reshape(nb, s, KV_W), v_new.reshape(nb, s, KV_W)
    if past_k is None:
        arrays = [k_seq] * 3 + [v_seq] * 3
        chunks = [lambda c, d=d: jnp.maximum(c - d, 0) for d in (2, 1, 0)] * 2
    else:
        pk = past_k.astype(F32).reshape(nb, 2 * CHUNK, KV_W)
        pv = past_v.astype(F32).reshape(nb, 2 * CHUNK, KV_W)
        arrays = [pk, pk, k_seq, pv, pv, v_seq]
        chunks = [lambda c: 0, lambda c: 1, lambda c: 0] * 2
    o = _attention(q.reshape(nb, s, D_MODEL), arrays, chunks, w["at"]["sinks"], nb, s, past_k is None)
    o = o.reshape(n, D_MODEL)
    tm = min(n, 512)
    x = _out_proj(x, o, mod1[:, :, 2 * D_MODEL:3 * D_MODEL], w["at"]["w_o"], w["at"]["b_o"], nb, s, tm)
    y = moe_layer(x, mod1, 1)

    return y.reshape(nb, s, D_MODEL), last_x.reshape(1, nb, D_MODEL), _pairs_to_state(st)[None], k_seq, v_seq


def kernel(x_prompt, x_sample, state_shift, state_wkv, cache_k, cache_v, c_prompt, c_sample, ada_w, ada_b, norm_g, rw_mu, rw_w_rkv, rw_w0, rw_w1, rw_w2, rw_a0, rw_a1, rw_a2, rw_g1, rw_g2, rw_k_k, rw_k_a, rw_r_k, rw_lnx_w, rw_lnx_b, rw_w_o, kv_ada_w, kv_ada_b, kv_norm_g, w_kv, b_kv, at_w_q, at_b_q, at_sinks, at_w_o, at_b_o, moe_w_group, moe_w_expert, moe_w_gu, moe_w_down, final_norm_g):
    bp, sp, _ = x_prompt.shape
    bs, ss, _ = x_sample.shape
    row = lambda t: t.reshape(1, -1).astype(F32)

    c_all = jnp.concatenate([c_prompt, c_sample], axis=0)
    mods = _cond_linear(c_all, ada_w, ada_b[:, None, :])
    kvmods = _cond_linear(c_all, kv_ada_w[None], kv_ada_b[None, None, :])[0]

    lora_pad, gate_pad = LANES, 2 * LANES
    w = {
        "norm_g": norm_g,
        "final_g": row(final_norm_g),
        "rw": {
            "norm_g": norm_g[0, 0:1], "mu": rw_mu[0], "w_rkv": rw_w_rkv[0].astype(BF16),
            "w0": row(rw_w0[0]), "w1": _pad_cols(rw_w1[0], lora_pad).astype(BF16),
            "w2": _pad_rows(rw_w2[0], lora_pad).astype(BF16),
            "a0": row(rw_a0[0]), "a1": _pad_cols(rw_a1[0], lora_pad).astype(BF16),
            "a2": _pad_rows(rw_a2[0], lora_pad).astype(BF16),
            "g1": _pad_cols(rw_g1[0], gate_pad).astype(BF16), "g2": _pad_rows(rw_g2[0], gate_pad).astype(BF16),
            "k_k": row(rw_k_k[0]), "k_a": row(rw_k_a[0]), "r_k": row(rw_r_k[0]),
            "lnx_w": row(rw_lnx_w[0]), "lnx_b": row(rw_lnx_b[0]),
            "w_o": rw_w_o[0].astype(BF16), "b_o": jnp.zeros((1, D_MODEL), F32),
        },
        "at": {
            "norm_gq": norm_g[1, 0:1], "norm_gkv": row(kv_norm_g),
            "w_q": at_w_q[0].astype(BF16), "b_q": row(at_b_q[0]),
            "w_kv": w_kv.astype(BF16), "b_kv": row(b_kv),
            "sinks": at_sinks[0].astype(F32),
            "w_o": at_w_o[0].astype(BF16), "b_o": row(at_b_o[0]),
        },
        "moe_route": jnp.pad(jnp.concatenate([moe_w_group, moe_w_expert], axis=-1),
                             ((0, 0), (0, 0), (0, LANES - N_GROUPS - N_EXPERTS))),
        "moe_gu": moe_w_gu.astype(BF16),
        "moe_down": moe_w_down.astype(BF16),
    }

    pos_p = jnp.arange(sp, dtype=F32)
    pos_s = PAST_LEN + jnp.arange(ss, dtype=F32)
    zero_x = jnp.zeros((bp, D_MODEL), x_prompt.dtype)
    zero_wkv = jnp.zeros((bp,) + state_wkv.shape[2:], state_wkv.dtype)
    y_p, p_shift, p_wkv, p_k, p_v = _trunk(x_prompt, mods[:, :bp], kvmods[:bp], pos_p, zero_x, zero_wkv,
                                           None, None, w, bp, sp)
    y_s, s_shift, s_wkv, s_k, s_v = _trunk(x_sample, mods[:, bp:], kvmods[bp:], pos_s, state_shift[0],
                                           state_wkv[0], cache_k, cache_v, w, bs, ss)
    keep = min(2 * CHUNK, sp)
    heads = lambda t: t.reshape(t.shape[0], t.shape[1], N_KV, HEAD)
    return (y_p, y_s, p_shift, p_wkv.astype(state_wkv.dtype), heads(p_k[:, sp - keep:]), heads(p_v[:, sp - keep:]),
            s_shift, s_wkv.astype(state_wkv.dtype), heads(s_k), heads(s_v))
```

```python
import functools
import math

import jax
import jax.numpy as jnp
from jax import lax
from jax.experimental import pallas as pl
from jax.experimental.pallas import tpu as pltpu

F32 = jnp.float32
BF16 = jnp.bfloat16

D_MODEL = 1024
LANES = 128
HEAD = 64
PAIRS = D_MODEL // LANES
CHUNK = 64
WKV_SEQS = 4
ATTN_SEQS = 2
PAST_LEN = 4096
N_KV = 4
KV_W = N_KV * HEAD
ROPE_DIM = 16
ROPE_THETA = 500000.0
ATT_SCALE = HEAD ** -0.5
N_GROUPS = 4
EXP_PER_GROUP = 8
N_EXPERTS = N_GROUPS * EXP_PER_GROUP
D_EXPERT = D_MODEL // 4
ROUTE_LANE0 = N_GROUPS
RMS_EPS = 1e-6
GN_EPS = 64e-5
NEG_INF = -1e30
VMEM_LIMIT = 56 * 1024 * 1024


def _params(sem):
    return pltpu.CompilerParams(dimension_semantics=sem, vmem_limit_bytes=VMEM_LIMIT)


def _dot(a, b):
    return jnp.dot(a, b, preferred_element_type=F32)


def _dot_nt(a, b):
    return lax.dot_general(a, b, (((1,), (1,)), ((), ())), preferred_element_type=F32)


def _dot_tn(a, b):
    return lax.dot_general(a, b, (((0,), (0,)), ((), ())), preferred_element_type=F32)


def _split2(x):
    hi = x.astype(BF16)
    lo = (x - hi.astype(F32)).astype(BF16)
    return hi, lo


def _dot_x3(a, b, dot=_dot):
    ah, al = _split2(a)
    bh, bl = _split2(b)
    return dot(ah, bh) + dot(ah, bl) + dot(al, bh)


def _sigmoid(x):
    return 1.0 / (1.0 + jnp.exp(-x))


def _rms(x):
    return x * lax.rsqrt(jnp.mean(x * x, axis=-1, keepdims=True) + RMS_EPS)


def _modulate(n, gain, shift, scale, nb):
    rows = n.shape[0]
    h = (n * gain).reshape(nb, rows // nb, D_MODEL)
    return (h * (1.0 + scale) + shift).reshape(rows, D_MODEL)


def _const_spec(shape):
    nd = len(shape)
    return pl.BlockSpec(shape, lambda *_: (0,) * nd)


def _cond_kernel(c_ref, w_ref, b_ref, o_ref):
    c = c_ref[...]
    cs = (c * _sigmoid(c)).astype(BF16)
    o_ref[...] = _dot(cs, w_ref[...].astype(BF16)) + b_ref[...]


def _cond_linear(c, w, b, tn=512):
    nl, _, n = w.shape
    m = c.shape[0]
    return pl.pallas_call(
        _cond_kernel,
        grid=(nl, n // tn),
        in_specs=[
            pl.BlockSpec((m, D_MODEL), lambda l, j: (0, 0)),
            pl.BlockSpec((None, D_MODEL, tn), lambda l, j: (l, 0, j)),
            pl.BlockSpec((None, 1, tn), lambda l, j: (l, 0, j)),
        ],
        out_specs=pl.BlockSpec((None, m, tn), lambda l, j: (l, 0, j)),
        out_shape=jax.ShapeDtypeStruct((nl, m, n), F32),
        compiler_params=_params(("parallel", "parallel")),
        name="cond_linear",
    )(c, w, b)


def _rwkv_proj_kernel(x_ref, mod_ref, prev_ref, ng_ref, mu_ref, wrkv_ref, w0_ref, w1_ref, w2_ref,
                      a0_ref, a1_ref, a2_ref, g1_ref, g2_ref, kk_ref, ka_ref,
                      r_ref, lw_ref, k_ref, v_ref, a_ref, b_ref, g_ref, last_ref, carry_ref):
    tm = x_ref.shape[0]

    @pl.when(pl.program_id(1) == 0)
    def _():
        carry_ref[...] = prev_ref[...]

    shift = mod_ref[:, 0:D_MODEL].reshape(1, 1, D_MODEL)
    scale = mod_ref[:, D_MODEL:2 * D_MODEL].reshape(1, 1, D_MODEL)
    h = _modulate(_rms(x_ref[...]), ng_ref[...], shift, scale, 1)
    row = lax.broadcasted_iota(jnp.int32, (tm, 1), 0)
    shifted = jnp.where(row == 0, carry_ref[...], pltpu.roll(h, 1, axis=0))
    carry_ref[...] = h[tm - 1:tm, :]
    last_ref[...] = h[tm - 1:tm, :]
    dx = shifted - h

    def mix(n):
        return (h + dx * mu_ref[n:n + 1, :]).astype(BF16)

    r = _dot(mix(0), wrkv_ref[0])
    k = _dot(mix(1), wrkv_ref[1])
    v = _dot(mix(2), wrkv_ref[2])
    ww = _dot(jnp.tanh(_dot(mix(3), w1_ref[...])).astype(BF16), w2_ref[...])
    z = -(w0_ref[...] + ww)
    softplus = jnp.maximum(z, 0.0) + jnp.log(1.0 + jnp.exp(-jnp.abs(z)))
    lw = -jnp.exp(-softplus - 0.5)
    asig = _sigmoid(a0_ref[...] + _dot(_dot(mix(4), a1_ref[...]).astype(BF16), a2_ref[...]))
    g = _dot(_sigmoid(_dot(mix(5), g1_ref[...])).astype(BF16), g2_ref[...])

    kk = k * kk_ref[...]
    kk2 = kk * kk
    lo_cols = lax.broadcasted_iota(jnp.int32, (tm, LANES), 1) < HEAD

    def head_sums(t):
        lo_sum = jnp.sum(jnp.where(lo_cols, t, 0.0), axis=-1, keepdims=True)
        hi_sum = jnp.sum(jnp.where(lo_cols, 0.0, t), axis=-1, keepdims=True)
        return jnp.where(lo_cols, lo_sum, hi_sum)

    ss = jnp.concatenate([head_sums(kk2[:, p * LANES:(p + 1) * LANES]) for p in range(PAIRS)], axis=1)
    kkn = kk * lax.rsqrt(jnp.maximum(ss, 1e-24))

    r_ref[...] = r.astype(r_ref.dtype)
    lw_ref[...] = lw
    k_ref[...] = (k * (1.0 + (asig - 1.0) * ka_ref[...])).astype(k_ref.dtype)
    v_ref[...] = v.astype(v_ref.dtype)
    a_ref[...] = (-kkn).astype(a_ref.dtype)
    b_ref[...] = (kkn * asig).astype(b_ref.dtype)
    g_ref[...] = g.astype(g_ref.dtype)


def _rwkv_proj(x, mod, prev_x, w, nb, s, tm):
    n = nb * s
    nt = s // tm
    row_spec = pl.BlockSpec((tm, D_MODEL), lambda b, j: (b * nt + j, 0))
    vec = _const_spec((1, D_MODEL))
    lora_w, lora_g = w["w1"].shape[1], w["g1"].shape[1]
    in_specs = [
        row_spec,
        pl.BlockSpec((None, 1, 6 * D_MODEL), lambda b, j: (b, 0, 0)),
        pl.BlockSpec((None, 1, D_MODEL), lambda b, j: (b, 0, 0)),
        vec, _const_spec((6, D_MODEL)), _const_spec((3, D_MODEL, D_MODEL)),
        vec, _const_spec((D_MODEL, lora_w)), _const_spec((lora_w, D_MODEL)),
        vec, _const_spec((D_MODEL, lora_w)), _const_spec((lora_w, D_MODEL)),
        _const_spec((D_MODEL, lora_g)), _const_spec((lora_g, D_MODEL)),
        vec, vec,
    ]
    act = lambda dt: jax.ShapeDtypeStruct((n, D_MODEL), dt)
    out_types = [act(BF16), act(F32)] + [act(BF16)] * 5
    outs = pl.pallas_call(
        _rwkv_proj_kernel,
        grid=(nb, nt),
        in_specs=in_specs,
        out_specs=[row_spec] * 7 + [pl.BlockSpec((None, 1, D_MODEL), lambda b, j: (b, 0, 0))],
        out_shape=out_types + [jax.ShapeDtypeStruct((nb, 1, D_MODEL), F32)],
        scratch_shapes=[pltpu.VMEM((1, D_MODEL), F32)],
        compiler_params=_params(("parallel", "arbitrary")),
        name="rwkv_proj",
    )(x, mod, prev_x, w["norm_g"], w["mu"], w["w_rkv"], w["w0"], w["w1"], w["w2"],
      w["a0"], w["a1"], w["a2"], w["g1"], w["g2"], w["k_k"], w["k_a"])
    return outs


def _split3(x):
    hi = x.astype(BF16)
    rem = x - hi.astype(F32)
    mid = rem.astype(BF16)
    return hi, mid, (rem - mid.astype(F32)).astype(BF16)


def _wkv_kernel(r_ref, lw_ref, k_ref, v_ref, a_ref, b_ref, g_ref, s0_ref, rk_ref, lnw_ref, lnb_ref,
                z_ref, st_ref):
    @pl.when(pl.program_id(1) == 0)
    def _():
        st_ref[...] = s0_ref[...]

    nseq = r_ref.shape[0]
    wide = lambda ref: jnp.concatenate([ref[i] for i in range(nseq)], axis=1)
    per_seq = lambda ref: jnp.concatenate([ref[...]] * nseq, axis=1)
    b16 = lambda t: t.astype(BF16)
    pairs = range(nseq * PAIRS)
    blk = lambda t, p: t[:, p * LANES:(p + 1) * LANES]
    rows = lambda ts: jnp.concatenate(ts, axis=0)
    lo_cols = lax.broadcasted_iota(jnp.int32, (CHUNK, LANES), 1) < HEAD
    rr = lax.broadcasted_iota(jnp.int32, (LANES, LANES), 0)
    cc = lax.broadcasted_iota(jnp.int32, (LANES, LANES), 1)
    head0 = lambda t: jnp.where(lo_cols, t, 0.0)
    head1 = lambda t: jnp.where(lo_cols, 0.0, t)
    blockdiag = lambda q: rows([head0(q), head1(q)])

    r, lw, k, v, a, b = (wide(t).astype(F32) for t in (r_ref, lw_ref, k_ref, v_ref, a_ref, b_ref))
    ti = lax.broadcasted_iota(jnp.int32, (CHUNK, CHUNK), 0)
    tj = lax.broadcasted_iota(jnp.int32, (CHUNK, CHUNK), 1)
    tri = (tj <= ti).astype(BF16)
    cum = sum(_dot(tri, t) for t in _split3(lw))
    cl = cum[CHUNK - 1:CHUNK, :]
    e_neg = jnp.exp(-cum)
    e_end = jnp.exp(cl - cum)
    e_cl = jnp.exp(cl)
    at = b16(a * jnp.exp(cum - lw))
    rt = b16(r * jnp.exp(cum))
    bt = b * e_neg
    kt = k * e_neg
    bkh = rows([b16(b * e_end), b16(k * e_end)])
    v16 = b16(v)

    t_idx = rr & (CHUNK - 1)
    j_idx = cc & (CHUNK - 1)
    keep = (j_idx < t_idx) | ((rr >= CHUNK) & (j_idx == t_idx))
    g0, g1 = [], []
    for p in pairs:
        btp, ktp = blk(bt, p), blk(kt, p)
        w = b16(rows([head0(btp), head0(ktp), head1(ktp), head1(btp)]))
        g = _dot_nt(rows([blk(at, p), blk(rt, p)]), w)
        g0.append(jnp.where(keep, g[:, :LANES], 0.0))
        g1.append(jnp.where(keep, g[:, LANES:], 0.0))

    eye2 = jnp.where((lax.broadcasted_iota(jnp.int32, (CHUNK, LANES), 1) & (CHUNK - 1))
                     == lax.broadcasted_iota(jnp.int32, (CHUNK, LANES), 0), 1.0, 0.0)
    pc = [jnp.where(lo_cols, g0[p][:CHUNK], g1[p][:CHUNK]) for p in pairs]
    tc = [eye2 + pc[p] for p in pairs]
    q = [_dot(b16(pc[p]), b16(blockdiag(pc[p]))) for p in pairs]
    for _ in range(4):
        res = [_dot(b16(rows([tc[p], q[p]])), b16(blockdiag(q[p]))) for p in pairs]
        tc = [tc[p] + res[p][:CHUNK] for p in pairs]
        q = [res[p][CHUNK:] for p in pairs]
    tc = [b16(tc[p] + _dot(b16(tc[p]), b16(blockdiag(q[p])))) for p in pairs]

    akv = []
    for p in pairs:
        w_ak = b16(jnp.where(lo_cols, g1[p][:CHUNK], g0[p][:CHUNK]))
        vp = blk(v16, p)
        akv.append(_dot(w_ak, rows([head1(vp), head0(vp)])))

    s = [st_ref[p // PAIRS, p % PAIRS] for p in pairs]
    s16 = [b16(s[p]) for p in pairs]
    rhs = [_dot_nt(blk(at, p), s16[p]) + akv[p] for p in pairs]
    u = [_dot(tc[p], b16(rows([head0(rhs[p]), head1(rhs[p])]))) for p in pairs]
    u16 = [b16(u[p]) for p in pairs]

    ys = []
    bd_mask = (rr < HEAD) == (cc < HEAD)
    for p in pairs:
        vp = blk(v16, p)
        rbk = b16(jnp.concatenate([g0[p][CHUNK:], g1[p][CHUNK:]], axis=1))
        uv = rows([head0(u16[p]), head0(vp), head1(vp), head1(u16[p])])
        ys.append(_dot_nt(blk(rt, p), s16[p]) + _dot(rbk, uv))
        fresh = _dot_tn(rows([u16[p], vp]), blk(bkh, p))
        st_ref[p // PAIRS, p % PAIRS] = s[p] * blk(e_cl, p) + jnp.where(bd_mask, fresh, 0.0)

    def head_sums(t):
        lo_sum = jnp.sum(head0(t), axis=-1, keepdims=True)
        hi_sum = jnp.sum(head1(t), axis=-1, keepdims=True)
        return jnp.where(lo_cols, lo_sum, hi_sum)

    rkk = r * k * per_seq(rk_ref)
    for p in pairs:
        seq, cols = p // PAIRS, slice((p % PAIRS) * LANES, (p % PAIRS + 1) * LANES)
        yc = ys[p] - head_sums(ys[p]) * (1.0 / HEAD)
        yn = yc * lax.rsqrt(head_sums(yc * yc) * (1.0 / HEAD) + GN_EPS)
        zp = (yn * lnw_ref[:, cols] + lnb_ref[:, cols] + head_sums(blk(rkk, p)) * blk(v, p)) * g_ref[seq, :, cols].astype(F32)
        z_ref[seq, :, cols] = zp.astype(z_ref.dtype)


def _wkv(r, lw, k, v, a, b, g, st0, w, nb, s):
    nseq = math.gcd(nb, WKV_SEQS)
    blk = pl.BlockSpec((nseq, CHUNK, D_MODEL), lambda bi, c: (bi, c, 0))
    vec = _const_spec((1, D_MODEL))
    st_spec = pl.BlockSpec((nseq, PAIRS, LANES, LANES), lambda bi, c: (bi, 0, 0, 0))
    return pl.pallas_call(
        _wkv_kernel,
        grid=(nb // nseq, s // CHUNK),
        in_specs=[blk] * 7 + [st_spec, vec, vec, vec],
        out_specs=[blk, st_spec],
        out_shape=[jax.ShapeDtypeStruct((nb, s, D_MODEL), BF16),
                   jax.ShapeDtypeStruct((nb, PAIRS, LANES, LANES), F32)],
        compiler_params=_params(("parallel", "arbitrary")),
        name="wkv",
    )(r, lw, k, v, a, b, g, st0, w["r_k"], w["lnx_w"], w["lnx_b"])


def _out_proj_kernel(x_ref, z_ref, gate_ref, w_ref, bias_ref, o_ref, *, nb):
    rows = x_ref.shape[0]
    y = _dot(z_ref[...], w_ref[...]) + bias_ref[...]
    y = (y.reshape(nb, rows // nb, D_MODEL) * gate_ref[...]).reshape(rows, D_MODEL)
    o_ref[...] = x_ref[...] + y


def _out_proj(x, z, gate, w, bias, nb, s, tm):
    n = nb * s
    tb = max(tm // s, 1)
    per_seq = max(s // tm, 1)
    row_spec = pl.BlockSpec((tm, D_MODEL), lambda i: (i, 0))
    return pl.pallas_call(
        functools.partial(_out_proj_kernel, nb=tb),
        grid=(n // tm,),
        in_specs=[row_spec, row_spec,
                  pl.BlockSpec((tb, 1, D_MODEL), lambda i: (i // per_seq, 0, 0)),
                  _const_spec((D_MODEL, D_MODEL)), _const_spec((1, D_MODEL))],
        out_specs=row_spec,
        out_shape=jax.ShapeDtypeStruct((n, D_MODEL), F32),
        compiler_params=_params(("parallel",)),
        name="out_proj",
    )(x, z, gate, w, bias)


def _top2(logits):
    lane = lax.broadcasted_iota(jnp.int32, logits.shape, 1).astype(F32)
    first_of = lambda hit: jnp.min(jnp.where(hit, lane, float(LANES)), axis=-1, keepdims=True)
    lg = jnp.where(lane < N_GROUPS, logits, -jnp.inf)
    gmax = jnp.max(lg, axis=-1, keepdims=True)
    gi = first_of(lg == gmax)
    gp = 1.0 / jnp.sum(jnp.exp(lg - gmax), axis=-1, keepdims=True)
    first = ROUTE_LANE0 + gi * EXP_PER_GROUP
    le = jnp.where((lane >= first) & (lane < first + EXP_PER_GROUP), logits, -jnp.inf)
    top1 = jnp.max(le, axis=-1, keepdims=True)
    i1 = first_of(le == top1)
    le2 = jnp.where(lane == i1, -jnp.inf, le)
    top2 = jnp.max(le2, axis=-1, keepdims=True)
    i2 = first_of(le2 == top2)
    e2 = jnp.exp(top2 - top1)
    w1 = gp / (1.0 + e2)
    return i1, i2, w1, w1 * e2


def _route(logits):
    lane = lax.broadcasted_iota(jnp.int32, logits.shape, 1).astype(F32)
    i1, i2, w1, w2 = _top2(logits)
    return jnp.where(lane == i1, w1, 0.0) + jnp.where(lane == i2, w2, 0.0)


def _moe_kernel(x_ref, mod_ref, ng_ref, wr_ref, wgu_ref, wdn_ref, fg_ref, o_ref, h_scr, cw_scr, acc_scr,
                *, nb, final_norm):
    e = pl.program_id(1)
    rows = x_ref.shape[0]

    @pl.when(e == 0)
    def _():
        shift = mod_ref[:, :, 3 * D_MODEL:4 * D_MODEL]
        scale = mod_ref[:, :, 4 * D_MODEL:5 * D_MODEL]
        h = _modulate(_rms(x_ref[...]), ng_ref[...], shift, scale, nb)
        h_scr[...] = h.astype(BF16)
        cw_scr[...] = _route(_dot_x3(h, wr_ref[...]))
        acc_scr[...] = jnp.zeros_like(acc_scr)

    lane = lax.broadcasted_iota(jnp.int32, (rows, LANES), 1)
    cw = jnp.sum(jnp.where(lane == ROUTE_LANE0 + e, cw_scr[...], 0.0), axis=-1, keepdims=True)
    gu = _dot(h_scr[...], wgu_ref[...])
    gate = gu[:, :D_EXPERT]
    hid = gate * _sigmoid(gate) * gu[:, D_EXPERT:] * cw
    acc_scr[...] += _dot(hid.astype(BF16), wdn_ref[...])

    @pl.when(e == N_EXPERTS - 1)
    def _():
        g2 = mod_ref[:, :, 5 * D_MODEL:6 * D_MODEL]
        y = (acc_scr[...].reshape(nb, rows // nb, D_MODEL) * g2).reshape(rows, D_MODEL)
        out = x_ref[...] + y
        if final_norm:
            out = _rms(out) * fg_ref[...]
        o_ref[...] = out


def _moe(x, mod, norm_g, w_route, w_gu, w_down, final_g, nb, s, layer, tm, final_norm):
    n = nb * s
    tb = max(tm // s, 1)
    per_seq = max(s // tm, 1)
    row_spec = pl.BlockSpec((tm, D_MODEL), lambda i, e: (i, 0))
    return pl.pallas_call(
        functools.partial(_moe_kernel, nb=tb, final_norm=final_norm),
        grid=(n // tm, N_EXPERTS),
        in_specs=[row_spec,
                  pl.BlockSpec((tb, 1, 6 * D_MODEL), lambda i, e: (i // per_seq, 0, 0)),
                  _const_spec((1, D_MODEL)), _const_spec((D_MODEL, LANES)),
                  pl.BlockSpec((None, None, D_MODEL, 2 * D_EXPERT), lambda i, e: (layer, e, 0, 0)),
                  pl.BlockSpec((None, None, D_EXPERT, D_MODEL), lambda i, e: (layer, e, 0, 0)),
                  _const_spec((1, D_MODEL))],
        out_specs=row_spec,
        out_shape=jax.ShapeDtypeStruct((n, D_MODEL), F32),
        scratch_shapes=[pltpu.VMEM((tm, D_MODEL), BF16), pltpu.VMEM((tm, LANES), F32),
                        pltpu.VMEM((tm, D_MODEL), F32)],
        compiler_params=_params(("parallel", "arbitrary")),
        name="moe",
    )(x, mod, norm_g, w_route, w_gu, w_down, final_g)


PAIRS_PER_GROUP = EXP_PER_GROUP * (EXP_PER_GROUP - 1) // 2
N_BUCKETS = N_GROUPS * PAIRS_PER_GROUP
BUCKET_TILE = 256
ROW_W = D_MODEL + LANES
META_BUCKET, META_RANK, META_W_LO, META_W_HI = 0, 1, 2, 3
ROUTER_TILE = 512
MOVE_TILE = 512
DMA_UNROLL = 8
SPARSE_MIN_TOKENS = 4096


def _moe_norm(x, mod_ref, ng_ref, nb):
    shift = mod_ref[:, :, 3 * D_MODEL:4 * D_MODEL]
    scale = mod_ref[:, :, 4 * D_MODEL:5 * D_MODEL]
    return _modulate(_rms(x), ng_ref[...], shift, scale, nb)


def _router_kernel(x_ref, z_ref, wo_ref, bo_ref, mod_ref, ng_ref, wr_ref, earlier_ref,
                   x1_ref, meta_ref, cnt_ref, *, nb):
    @pl.when(pl.program_id(0) == 0)
    def _():
        cnt_ref[...] = jnp.zeros_like(cnt_ref)

    rows = x_ref.shape[0]
    y = _dot(z_ref[...], wo_ref[...]) + bo_ref[...]
    gate = mod_ref[:, :, 2 * D_MODEL:3 * D_MODEL]
    x1 = x_ref[...] + (y.reshape(nb, rows // nb, D_MODEL) * gate).reshape(rows, D_MODEL)
    x1_ref[...] = x1
    h = _moe_norm(x1, mod_ref, ng_ref, nb)
    i1, i2, w1, w2 = _top2(_dot_x3(h, wr_ref[...]))
    lo = (jnp.minimum(i1, i2) - ROUTE_LANE0).astype(jnp.int32)
    hi = (jnp.maximum(i1, i2) - ROUTE_LANE0).astype(jnp.int32)
    first_is_lo = i1 < i2
    a = lo & (EXP_PER_GROUP - 1)
    b = hi & (EXP_PER_GROUP - 1)
    group = lo >> (EXP_PER_GROUP.bit_length() - 1)
    bucket = group * PAIRS_PER_GROUP + ((a * (2 * EXP_PER_GROUP - 1 - a)) >> 1) + (b - a - 1)

    lane = lax.broadcasted_iota(jnp.int32, (rows, LANES), 1)
    mine = lane == bucket
    onehot = jnp.where(mine, 1.0, 0.0)
    seen = cnt_ref[...]
    before = _dot(earlier_ref[...], onehot.astype(BF16)) + seen
    rank = jnp.sum(jnp.where(mine, before, 0.0), axis=-1, keepdims=True)
    cnt_ref[...] = seen + jnp.sum(onehot, axis=0, keepdims=True)

    meta = jnp.where(lane == META_BUCKET, bucket.astype(F32), 0.0)
    meta = jnp.where(lane == META_RANK, rank, meta)
    meta = jnp.where(lane == META_W_LO, jnp.where(first_is_lo, w1, w2), meta)
    meta_ref[...] = jnp.where(lane == META_W_HI, jnp.where(first_is_lo, w2, w1), meta)


def _positions_kernel(meta_ref, start_ref, pos_ref):
    meta = meta_ref[...]
    lane = lax.broadcasted_iota(jnp.int32, meta.shape, 1)
    column = lambda c: jnp.sum(jnp.where(lane == c, meta, 0.0), axis=-1, keepdims=True)
    value = jnp.where(lane == column(META_BUCKET).astype(jnp.int32), start_ref[...] + column(META_RANK), 0.0)
    ones = jnp.ones((8, LANES), BF16)
    pos = sum(_dot_nt(ones, piece) for piece in _split3(value))
    pos_ref[...] = pos[0:1, :].astype(jnp.int32)


def _dispatch_kernel(pos_ref, x_ref, mod_ref, ng_ref, meta_ref, xs_ref, buf, sem, *, nb):
    i = pl.program_id(0)
    rows = x_ref.shape[0]
    slot = i % 2
    buf[slot, :, 0:D_MODEL] = _moe_norm(x_ref[...], mod_ref, ng_ref, nb)
    buf[slot, :, D_MODEL:ROW_W] = meta_ref[...]

    def send(r, carry):
        pltpu.make_async_copy(buf.at[slot, pl.ds(r, 1), :],
                              xs_ref.at[pl.ds(pos_ref[i * rows + r], 1), :], sem.at[slot]).start()
        return carry

    lax.fori_loop(0, rows, send, 0, unroll=DMA_UNROLL)

    def wait_all(sl):
        pltpu.make_async_copy(buf.at[sl], xs_ref.at[pl.ds(0, rows), :], sem.at[sl]).wait()

    @pl.when(i >= 1)
    def _():
        wait_all(1 - slot)

    @pl.when(i == pl.num_programs(0) - 1)
    def _():
        wait_all(slot)


def _experts_kernel(tile_ref, bucket_ref, lo_ref, hi_ref, used_ref, xs_ref, gu_lo_ref, gu_hi_ref,
                    dn_lo_ref, dn_hi_ref, ys_ref):
    k = pl.program_id(0)

    @pl.when(k < used_ref[0])
    def _():
        rows = xs_ref.shape[0]
        x16 = xs_ref[:, 0:D_MODEL].astype(BF16)
        meta = xs_ref[:, D_MODEL:ROW_W]
        lane = lax.broadcasted_iota(jnp.int32, (rows, LANES), 1)
        column = lambda c: jnp.sum(jnp.where(lane == c, meta, 0.0), axis=-1, keepdims=True)
        member = column(META_BUCKET) == bucket_ref[k].astype(F32)

        def expert(gu_ref, dn_ref, col):
            gu = _dot(x16, gu_ref[...])
            gate = gu[:, :D_EXPERT]
            hid = gate * _sigmoid(gate) * gu[:, D_EXPERT:] * jnp.where(member, column(col), 0.0)
            return _dot(hid.astype(BF16), dn_ref[...])

        y = expert(gu_lo_ref, dn_lo_ref, META_W_LO) + expert(gu_hi_ref, dn_hi_ref, META_W_HI)
        new_tile = (k == 0) | (tile_ref[k] != tile_ref[jnp.maximum(k - 1, 0)])

        @pl.when(new_tile)
        def _():
            ys_ref[...] = y

        @pl.when(jnp.logical_not(new_tile))
        def _():
            ys_ref[...] += y


def _combine_kernel(pos_ref, x_ref, mod_ref, fg_ref, ys_ref, o_ref, buf, sem, *, nb, final_norm):
    i = pl.program_id(0)
    rows = x_ref.shape[0]
    slot = i % 2

    def fetch(step, sl):
        def one(r, carry):
            pltpu.make_async_copy(ys_ref.at[pl.ds(pos_ref[step * rows + r], 1), :],
                                  buf.at[sl, pl.ds(r, 1), :], sem.at[sl]).start()
            return carry
        lax.fori_loop(0, rows, one, 0, unroll=DMA_UNROLL)

    @pl.when(i == 0)
    def _():
        fetch(0, 0)

    pltpu.make_async_copy(ys_ref.at[pl.ds(0, rows), :], buf.at[slot], sem.at[slot]).wait()

    @pl.when(i + 1 < pl.num_programs(0))
    def _():
        fetch(i + 1, 1 - slot)

    g2 = mod_ref[:, :, 5 * D_MODEL:6 * D_MODEL]
    y = (buf[slot].reshape(nb, rows // nb, D_MODEL) * g2).reshape(rows, D_MODEL)
    out = x_ref[...] + y
    if final_norm:
        out = _rms(out) * fg_ref[...]
    o_ref[...] = out


def _bucket_experts():
    lo, hi = [], []
    for g in range(N_GROUPS):
        for a in range(EXP_PER_GROUP):
            for b in range(a + 1, EXP_PER_GROUP):
                lo.append(g * EXP_PER_GROUP + a)
                hi.append(g * EXP_PER_GROUP + b)
    return jnp.asarray(lo, jnp.int32), jnp.asarray(hi, jnp.int32)


def _moe_sparse(x, z, w_o, b_o, mod, norm_g, w_route, w_gu, w_down, final_g, nb, s, layer, final_norm):
    n = nb * s
    move_params = pltpu.CompilerParams(dimension_semantics=("arbitrary",), vmem_limit_bytes=VMEM_LIMIT,
                                       disable_bounds_checks=True)

    def token_specs(tm):
        tb, per_seq = max(tm // s, 1), max(s // tm, 1)
        row = pl.BlockSpec((tm, D_MODEL), lambda i, *_: (i, 0))
        modspec = pl.BlockSpec((tb, 1, 6 * D_MODEL), lambda i, *_: (i // per_seq, 0, 0))
        meta = pl.BlockSpec((tm, LANES), lambda i, *_: (i, 0))
        return tb, row, modspec, meta

    vec = pl.BlockSpec((1, D_MODEL), lambda *_: (0, 0))
    tb, row, modspec, metaspec = token_specs(ROUTER_TILE)
    x, meta, counts = pl.pallas_call(
        functools.partial(_router_kernel, nb=tb),
        grid=(n // ROUTER_TILE,),
        in_specs=[row, row, pl.BlockSpec((D_MODEL, D_MODEL), lambda i: (0, 0)), vec,
                  modspec, vec, pl.BlockSpec((D_MODEL, LANES), lambda i: (0, 0)),
                  pl.BlockSpec((ROUTER_TILE, ROUTER_TILE), lambda i: (0, 0))],
        out_specs=[row, metaspec, pl.BlockSpec((1, LANES), lambda i: (0, 0))],
        out_shape=[jax.ShapeDtypeStruct((n, D_MODEL), F32), jax.ShapeDtypeStruct((n, LANES), F32),
                   jax.ShapeDtypeStruct((1, LANES), F32)],
        compiler_params=_params(("arbitrary",)),
        name="moe_router",
    )(x, z, w_o, b_o, mod, norm_g, w_route, jnp.tril(jnp.ones((ROUTER_TILE, ROUTER_TILE), BF16), -1))

    counts = counts[0, :N_BUCKETS].astype(jnp.int32)
    end = jnp.cumsum(counts)
    start = end - counts
    first_tile = start // BUCKET_TILE
    per_bucket = jnp.where(counts > 0, (end - 1) // BUCKET_TILE - first_tile + 1, 0)
    item_end = jnp.cumsum(per_bucket)
    used = item_end[-1:]
    n_items = n // BUCKET_TILE + N_BUCKETS
    k = jnp.minimum(jnp.arange(n_items, dtype=jnp.int32), used - 1)
    item_bucket = jnp.sum(item_end[None, :] <= k[:, None], axis=1, dtype=jnp.int32)
    item_tile = first_tile[item_bucket] + k - (item_end - per_bucket)[item_bucket]
    bucket_lo, bucket_hi = _bucket_experts()
    item_lo, item_hi = bucket_lo[item_bucket], bucket_hi[item_bucket]
    start_row = jnp.pad(start, (0, LANES - N_BUCKETS)).astype(F32)[None, :]
    pos = pl.pallas_call(
        _positions_kernel,
        grid=(n // ROUTER_TILE,),
        in_specs=[metaspec, pl.BlockSpec((1, LANES), lambda i: (0, 0))],
        out_specs=pl.BlockSpec((1, ROUTER_TILE), lambda i: (0, i)),
        out_shape=jax.ShapeDtypeStruct((1, n), jnp.int32),
        compiler_params=_params(("parallel",)),
        name="moe_positions",
    )(meta, start_row).reshape(n)

    tb, row, modspec, metaspec = token_specs(MOVE_TILE)
    any_spec = pl.BlockSpec(memory_space=pl.ANY)
    xs = pl.pallas_call(
        functools.partial(_dispatch_kernel, nb=tb),
        grid_spec=pltpu.PrefetchScalarGridSpec(
            num_scalar_prefetch=1, grid=(n // MOVE_TILE,),
            in_specs=[row, modspec, vec, metaspec], out_specs=any_spec,
            scratch_shapes=[pltpu.VMEM((2, MOVE_TILE, ROW_W), F32), pltpu.SemaphoreType.DMA((2,))]),
        out_shape=jax.ShapeDtypeStruct((n, ROW_W), F32),
        compiler_params=move_params,
        name="moe_dispatch",
    )(pos, x, mod, norm_g, meta)

    tile_of = lambda k, tile, *_: (tile[k], 0)
    ys = pl.pallas_call(
        _experts_kernel,
        grid_spec=pltpu.PrefetchScalarGridSpec(
            num_scalar_prefetch=5, grid=(n_items,),
            in_specs=[pl.BlockSpec((BUCKET_TILE, ROW_W), tile_of),
                      pl.BlockSpec((None, None, D_MODEL, 2 * D_EXPERT), lambda k, t, b, lo, hi, nu: (layer, lo[k], 0, 0)),
                      pl.BlockSpec((None, None, D_MODEL, 2 * D_EXPERT), lambda k, t, b, lo, hi, nu: (layer, hi[k], 0, 0)),
                      pl.BlockSpec((None, None, D_EXPERT, D_MODEL), lambda k, t, b, lo, hi, nu: (layer, lo[k], 0, 0)),
                      pl.BlockSpec((None, None, D_EXPERT, D_MODEL), lambda k, t, b, lo, hi, nu: (layer, hi[k], 0, 0))],
            out_specs=pl.BlockSpec((BUCKET_TILE, D_MODEL), tile_of)),
        out_shape=jax.ShapeDtypeStruct((n, D_MODEL), F32),
        compiler_params=_params(("arbitrary",)),
        name="moe_experts",
    )(item_tile, item_bucket, item_lo, item_hi, used, xs, w_gu, w_gu, w_down, w_down)

    return pl.pallas_call(
        functools.partial(_combine_kernel, nb=tb, final_norm=final_norm),
        grid_spec=pltpu.PrefetchScalarGridSpec(
            num_scalar_prefetch=1, grid=(n // MOVE_TILE,),
            in_specs=[row, modspec, vec, any_spec], out_specs=row,
            scratch_shapes=[pltpu.VMEM((2, MOVE_TILE, D_MODEL), F32), pltpu.SemaphoreType.DMA((2,))]),
        out_shape=jax.ShapeDtypeStruct((n, D_MODEL), F32),
        compiler_params=move_params,
        name="moe_combine",
    )(pos, x, mod, final_g, ys)


def _rope(x, cos, sin_lo, sin_hi):
    return x * cos + pltpu.roll(x, LANES - ROPE_DIM // 2, axis=1) * sin_lo + pltpu.roll(x, ROPE_DIM // 2, axis=1) * sin_hi


def _qkv_kernel(x_ref, mod_ref, kvmod_ref, ngq_ref, ngkv_ref, wq_ref, bq_ref, wkv_ref, bkv_ref,
                cos_ref, slo_ref, shi_ref, q_ref, k_ref, v_ref):
    n = _rms(x_ref[...])
    shift = mod_ref[:, 0:D_MODEL].reshape(1, 1, D_MODEL)
    scale = mod_ref[:, D_MODEL:2 * D_MODEL].reshape(1, 1, D_MODEL)
    hq = _modulate(n, ngq_ref[...], shift, scale, 1)
    kshift = kvmod_ref[:, 0:D_MODEL].reshape(1, 1, D_MODEL)
    kscale = kvmod_ref[:, D_MODEL:2 * D_MODEL].reshape(1, 1, D_MODEL)
    hkv = _modulate(n, ngkv_ref[...], kshift, kscale, 1)
    cos, slo, shi = cos_ref[...], slo_ref[...], shi_ref[...]
    q = _dot(hq.astype(BF16), wq_ref[...]) + bq_ref[...]
    for p in range(PAIRS):
        cols = slice(p * LANES, (p + 1) * LANES)
        q_ref[:, cols] = (_rope(q[:, cols], cos, slo, shi) * ATT_SCALE).astype(q_ref.dtype)
    kv = _dot(hkv.astype(BF16), wkv_ref[...]) + bkv_ref[...]
    for p in range(KV_W // LANES):
        cols = slice(p * LANES, (p + 1) * LANES)
        k_ref[:, cols] = _rope(kv[:, cols], cos, slo, shi)
    v_ref[...] = kv[:, KV_W:]


def _qkv_proj(x, mod, kvmod, w, tables, nb, s, tm):
    n = nb * s
    nt = s // tm
    row_spec = pl.BlockSpec((tm, D_MODEL), lambda b, j: (b * nt + j, 0))
    kv_spec = pl.BlockSpec((tm, KV_W), lambda b, j: (b * nt + j, 0))
    tab_spec = pl.BlockSpec((tm, LANES), lambda b, j: (j, 0))
    vec = _const_spec((1, D_MODEL))
    return pl.pallas_call(
        _qkv_kernel,
        grid=(nb, nt),
        in_specs=[row_spec,
                  pl.BlockSpec((None, 1, 6 * D_MODEL), lambda b, j: (b, 0, 0)),
                  pl.BlockSpec((None, 1, 2 * D_MODEL), lambda b, j: (b, 0, 0)),
                  vec, vec, _const_spec((D_MODEL, D_MODEL)), vec,
                  _const_spec((D_MODEL, 2 * KV_W)), _const_spec((1, 2 * KV_W)),
                  tab_spec, tab_spec, tab_spec],
        out_specs=[row_spec, kv_spec, kv_spec],
        out_shape=[jax.ShapeDtypeStruct((n, D_MODEL), BF16),
                   jax.ShapeDtypeStruct((n, KV_W), F32), jax.ShapeDtypeStruct((n, KV_W), F32)],
        compiler_params=_params(("parallel", "parallel")),
        name="qkv_proj",
    )(x, mod, kvmod, w["norm_gq"], w["norm_gkv"], w["w_q"], w["b_q"], w["w_kv"], w["b_kv"], *tables)


def _attn_kernel(sink_ref, q_ref, k0_ref, k1_ref, k2_ref, v0_ref, v1_ref, v2_ref, o_ref, *, banded):
    n = pl.program_id(1)
    seqs = range(q_ref.shape[0])
    k_all = [jnp.concatenate([k0_ref[i], k1_ref[i], k2_ref[i]], axis=0) for i in seqs]
    v_all = [jnp.concatenate([v0_ref[i], v1_ref[i], v2_ref[i]], axis=0) for i in seqs]
    lane = lax.broadcasted_iota(jnp.int32, (1, LANES), 1)
    lo = lane < HEAD
    col = lax.broadcasted_iota(jnp.int32, (1, 3 * CHUNK), 1)
    if banded:
        visible = (col >= 2 * CHUNK) | ((col >= CHUNK) & (n >= 1)) | (n >= 2)

    def halves(t, kv_head):
        blk = t[:, (kv_head // 2) * LANES:(kv_head // 2 + 1) * LANES]
        if kv_head % 2 == 0:
            t_lo = jnp.where(lo, blk, 0.0)
            t_hi = pltpu.roll(t_lo, HEAD, axis=1)
        else:
            t_hi = jnp.where(lo, 0.0, blk)
            t_lo = pltpu.roll(t_hi, HEAD, axis=1)
        return t_lo.astype(BF16), t_hi.astype(BF16)

    first_pair = lax.broadcasted_iota(jnp.int32, (2 * CHUNK, 1), 0) < CHUNK

    def probs(s, head_a, head_b):
        if banded:
            s = jnp.where(visible, s, NEG_INF)
        sink = jnp.where(first_pair, sink_ref[head_a], sink_ref[head_b])
        m = jnp.maximum(jnp.max(s, axis=-1, keepdims=True), sink)
        e = jnp.exp(s - m)
        return (e / (jnp.sum(e, axis=-1, keepdims=True) + jnp.exp(sink - m))).astype(BF16)

    units = [(i, g) for i in seqs for g in range(N_KV)]
    scores, values = [], []
    for i, g in units:
        k_lo, k_hi = halves(k_all[i], g)
        values.append(halves(v_all[i], g))
        q2 = jnp.concatenate([q_ref[i, :, (2 * g + j) * LANES:(2 * g + j + 1) * LANES] for j in range(2)], axis=0)
        scores.append((_dot_nt(q2, k_lo), _dot_nt(q2, k_hi)))
    for (i, g), (s_lo, s_hi), (v_lo, v_hi) in zip(units, scores, values):
        o = _dot(probs(s_lo, 4 * g, 4 * g + 2), v_lo) + _dot(probs(s_hi, 4 * g + 1, 4 * g + 3), v_hi)
        for j in range(2):
            pair = 2 * g + j
            o_ref[i, :, pair * LANES:(pair + 1) * LANES] = o[j * CHUNK:(j + 1) * CHUNK].astype(o_ref.dtype)


def _attention(q, kv_arrays, kv_chunks, sinks, nb, s, banded):
    nseq = math.gcd(nb, ATTN_SEQS)
    q_spec = pl.BlockSpec((nseq, CHUNK, D_MODEL), lambda b, c: (b, c, 0))
    kv_specs = [pl.BlockSpec((nseq, CHUNK, KV_W), lambda b, c, f=f: (b, f(c), 0)) for f in kv_chunks]
    return pl.pallas_call(
        functools.partial(_attn_kernel, banded=banded),
        grid=(nb // nseq, s // CHUNK),
        in_specs=[pl.BlockSpec(memory_space=pltpu.SMEM), q_spec] + kv_specs,
        out_specs=q_spec,
        out_shape=jax.ShapeDtypeStruct((nb, s, D_MODEL), BF16),
        compiler_params=_params(("parallel", "parallel")),
        name="attention",
    )(sinks, q, *kv_arrays)


def _pad_cols(w, n):
    return jnp.pad(w, ((0, 0), (0, n - w.shape[1])))


def _pad_rows(w, n):
    return jnp.pad(w, ((0, n - w.shape[0]), (0, 0)))


def _rope_tables(pos):
    half = ROPE_DIM // 2
    inv = jnp.power(jnp.float32(ROPE_THETA), -jnp.arange(half, dtype=F32) * (2.0 / ROPE_DIM))
    ang = pos[:, None] * inv[None, :]
    cos, sin = jnp.cos(ang), jnp.sin(ang)
    rest = HEAD - ROPE_DIM
    ones = jnp.ones((pos.shape[0], rest), F32)
    zeros = jnp.zeros((pos.shape[0], rest), F32)
    z8 = jnp.zeros_like(sin)
    per_head = (jnp.concatenate([cos, cos, ones], axis=1),
                jnp.concatenate([-sin, z8, zeros], axis=1),
                jnp.concatenate([z8, sin, zeros], axis=1))
    return tuple(jnp.tile(t, (1, LANES // HEAD)) for t in per_head)


def _state_to_pairs(state):
    nb = state.shape[0]
    st = state.astype(F32).reshape(nb, PAIRS, 2, HEAD, HEAD)
    z = jnp.zeros_like(st[:, :, 0])
    top = jnp.concatenate([st[:, :, 0], z], axis=-1)
    bot = jnp.concatenate([z, st[:, :, 1]], axis=-1)
    return jnp.concatenate([top, bot], axis=-2)


def _pairs_to_state(st):
    nb = st.shape[0]
    s0 = st[:, :, :HEAD, :HEAD]
    s1 = st[:, :, HEAD:, HEAD:]
    return jnp.stack([s0, s1], axis=2).reshape(nb, 2 * PAIRS, HEAD, HEAD)


def _trunk(x, mods, kvmod, pos, prev_x, prev_wkv, past_k, past_v, w, nb, s):
    n = nb * s
    x = x.reshape(n, D_MODEL)
    mod0, mod1 = mods[0][:, None, :], mods[1][:, None, :]
    kvmod = kvmod[:, None, :]

    tm = min(s, 256)
    r, lw, k, v, a, b, g, last_x = _rwkv_proj(x, mod0, prev_x[:, None, :], w["rw"], nb, s, tm)
    seqs = lambda t: t.reshape(nb, s, D_MODEL)
    z, st = _wkv(*(seqs(t) for t in (r, lw, k, v, a, b, g)), _state_to_pairs(prev_wkv), w["rw"], nb, s)
    z = z.reshape(n, D_MODEL)

    def residual_moe(x, z, mixer, mod, l):
        args = (mod, w["norm_g"][l][1:2], w["moe_route"][l], w["moe_gu"], w["moe_down"], w["final_g"], nb, s, l)
        if n >= SPARSE_MIN_TOKENS:
            return _moe_sparse(x, z, mixer["w_o"], mixer["b_o"], *args, l == 1)
        x = _out_proj(x, z, mod[:, :, 2 * D_MODEL:3 * D_MODEL], mixer["w_o"], mixer["b_o"], nb, s, min(n, 512))
        return _moe(x, *args, min(n, 1024), l == 1)

    x = residual_moe(x, z, w["rw"], mod0, 0)

    tm = min(s, 256)
    q, k_new, v_new = _qkv_proj(x, mod1, kvmod, w["at"], _rope_tables(pos), nb, s, tm)
    k_seq, v_seq = k_new.reshape(nb, s, KV_W), v_new.reshape(nb, s, KV_W)
    if past_k is None:
        arrays = [k_seq] * 3 + [v_seq] * 3
        chunks = [lambda c, d=d: jnp.maximum(c - d, 0) for d in (2, 1, 0)] * 2
    else:
        pk = past_k.astype(F32).reshape(nb, 2 * CHUNK, KV_W)
        pv = past_v.astype(F32).reshape(nb, 2 * CHUNK, KV_W)
        arrays = [pk, pk, k_seq, pv, pv, v_seq]
        chunks = [lambda c: 0, lambda c: 1, lambda c: 0] * 2
    o = _attention(q.reshape(nb, s, D_MODEL), arrays, chunks, w["at"]["sinks"], nb, s, past_k is None)
    y = residual_moe(x, o.reshape(n, D_MODEL), w["at"], mod1, 1)

    return y.reshape(nb, s, D_MODEL), last_x.reshape(1, nb, D_MODEL), _pairs_to_state(st)[None], k_seq, v_seq


def kernel(x_prompt, x_sample, state_shift, state_wkv, cache_k, cache_v, c_prompt, c_sample, ada_w, ada_b, norm_g, rw_mu, rw_w_rkv, rw_w0, rw_w1, rw_w2, rw_a0, rw_a1, rw_a2, rw_g1, rw_g2, rw_k_k, rw_k_a, rw_r_k, rw_lnx_w, rw_lnx_b, rw_w_o, kv_ada_w, kv_ada_b, kv_norm_g, w_kv, b_kv, at_w_q, at_b_q, at_sinks, at_w_o, at_b_o, moe_w_group, moe_w_expert, moe_w_gu, moe_w_down, final_norm_g):
    bp, sp, _ = x_prompt.shape
    bs, ss, _ = x_sample.shape
    row = lambda t: t.reshape(1, -1).astype(F32)

    c_all = jnp.concatenate([c_prompt, c_sample], axis=0)
    mods = _cond_linear(c_all, ada_w, ada_b[:, None, :])
    kvmods = _cond_linear(c_all, kv_ada_w[None], kv_ada_b[None, None, :])[0]

    lora_pad, gate_pad = LANES, 2 * LANES
    w = {
        "norm_g": norm_g,
        "final_g": row(final_norm_g),
        "rw": {
            "norm_g": norm_g[0, 0:1], "mu": rw_mu[0], "w_rkv": rw_w_rkv[0].astype(BF16),
            "w0": row(rw_w0[0]), "w1": _pad_cols(rw_w1[0], lora_pad).astype(BF16),
            "w2": _pad_rows(rw_w2[0], lora_pad).astype(BF16),
            "a0": row(rw_a0[0]), "a1": _pad_cols(rw_a1[0], lora_pad).astype(BF16),
            "a2": _pad_rows(rw_a2[0], lora_pad).astype(BF16),
            "g1": _pad_cols(rw_g1[0], gate_pad).astype(BF16), "g2": _pad_rows(rw_g2[0], gate_pad).astype(BF16),
            "k_k": row(rw_k_k[0]), "k_a": row(rw_k_a[0]), "r_k": row(rw_r_k[0]),
            "lnx_w": row(rw_lnx_w[0]), "lnx_b": row(rw_lnx_b[0]),
            "w_o": rw_w_o[0].astype(BF16), "b_o": jnp.zeros((1, D_MODEL), F32),
        },
        "at": {
            "norm_gq": norm_g[1, 0:1], "norm_gkv": row(kv_norm_g),
            "w_q": at_w_q[0].astype(BF16), "b_q": row(at_b_q[0]),
            "w_kv": w_kv.astype(BF16), "b_kv": row(b_kv),
            "sinks": at_sinks[0].astype(F32),
            "w_o": at_w_o[0].astype(BF16), "b_o": row(at_b_o[0]),
        },
        "moe_route": jnp.pad(jnp.concatenate([moe_w_group, moe_w_expert], axis=-1),
                             ((0, 0), (0, 0), (0, LANES - N_GROUPS - N_EXPERTS))),
        "moe_gu": moe_w_gu.astype(BF16),
        "moe_down": moe_w_down.astype(BF16),
    }

    pos_p = jnp.arange(sp, dtype=F32)
    pos_s = PAST_LEN + jnp.arange(ss, dtype=F32)
    zero_x = jnp.zeros((bp, D_MODEL), x_prompt.dtype)
    zero_wkv = jnp.zeros((bp,) + state_wkv.shape[2:], state_wkv.dtype)
    y_p, p_shift, p_wkv, p_k, p_v = _trunk(x_prompt, mods[:, :bp], kvmods[:bp], pos_p, zero_x, zero_wkv,
                                           None, None, w, bp, sp)
    y_s, s_shift, s_wkv, s_k, s_v = _trunk(x_sample, mods[:, bp:], kvmods[bp:], pos_s, state_shift[0],
                                           state_wkv[0], cache_k, cache_v, w, bs, ss)
    keep = min(2 * CHUNK, sp)
    heads = lambda t: t.reshape(t.shape[0], t.shape[1], N_KV, HEAD)
    return (y_p, y_s, p_shift, p_wkv.astype(state_wkv.dtype), heads(p_k[:, sp - keep:]), heads(p_v[:, sp - keep:]),
            s_shift, s_wkv.astype(state_wkv.dtype), heads(s_k), heads(s_v))
```

```python
import functools
import math

import jax
import jax.numpy as jnp
from jax import lax
from jax.experimental import pallas as pl
from jax.experimental.pallas import tpu as pltpu

F32 = jnp.float32
BF16 = jnp.bfloat16

D_MODEL = 1024
LANES = 128
HEAD = 64
PAIRS = D_MODEL // LANES
CHUNK = 64
WKV_SEQS = 4
ATTN_SEQS = 2
PAST_LEN = 4096
N_KV = 4
KV_W = N_KV * HEAD
ROPE_DIM = 16
ROPE_THETA = 500000.0
ATT_SCALE = HEAD ** -0.5
N_GROUPS = 4
EXP_PER_GROUP = 8
N_EXPERTS = N_GROUPS * EXP_PER_GROUP
D_EXPERT = D_MODEL // 4
ROUTE_LANE0 = N_GROUPS
RMS_EPS = 1e-6
GN_EPS = 64e-5
NEG_INF = -1e30
VMEM_LIMIT = 56 * 1024 * 1024


def _params(sem):
    return pltpu.CompilerParams(dimension_semantics=sem, vmem_limit_bytes=VMEM_LIMIT)


def _dot(a, b):
    return jnp.dot(a, b, preferred_element_type=F32)


def _dot_nt(a, b):
    return lax.dot_general(a, b, (((1,), (1,)), ((), ())), preferred_element_type=F32)


def _dot_tn(a, b):
    return lax.dot_general(a, b, (((0,), (0,)), ((), ())), preferred_element_type=F32)


def _split2(x):
    hi = x.astype(BF16)
    lo = (x - hi.astype(F32)).astype(BF16)
    return hi, lo


def _dot_x3(a, b, dot=_dot):
    ah, al = _split2(a)
    bh, bl = _split2(b)
    return dot(ah, bh) + dot(ah, bl) + dot(al, bh)


def _sigmoid(x):
    return 1.0 / (1.0 + jnp.exp(-x))


def _rms(x):
    return x * lax.rsqrt(jnp.mean(x * x, axis=-1, keepdims=True) + RMS_EPS)


def _modulate(n, gain, shift, scale, nb):
    rows = n.shape[0]
    h = (n * gain).reshape(nb, rows // nb, D_MODEL)
    return (h * (1.0 + scale) + shift).reshape(rows, D_MODEL)


def _const_spec(shape):
    nd = len(shape)
    return pl.BlockSpec(shape, lambda *_: (0,) * nd)


def _cond_kernel(c_ref, w_ref, b_ref, o_ref):
    c = c_ref[...]
    cs = (c * _sigmoid(c)).astype(BF16)
    o_ref[...] = _dot(cs, w_ref[...].astype(BF16)) + b_ref[...]


def _cond_linear(c, w, b, tn=512):
    nl, _, n = w.shape
    m = c.shape[0]
    return pl.pallas_call(
        _cond_kernel,
        grid=(nl, n // tn),
        in_specs=[
            pl.BlockSpec((m, D_MODEL), lambda l, j: (0, 0)),
            pl.BlockSpec((None, D_MODEL, tn), lambda l, j: (l, 0, j)),
            pl.BlockSpec((None, 1, tn), lambda l, j: (l, 0, j)),
        ],
        out_specs=pl.BlockSpec((None, m, tn), lambda l, j: (l, 0, j)),
        out_shape=jax.ShapeDtypeStruct((nl, m, n), F32),
        compiler_params=_params(("parallel", "parallel")),
        name="cond_linear",
    )(c, w, b)


def _rwkv_proj_kernel(x_ref, mod_ref, prev_ref, ng_ref, mu_ref, wrkv_ref, w0_ref, w1_ref, w2_ref,
                      a0_ref, a1_ref, a2_ref, g1_ref, g2_ref, kk_ref, ka_ref,
                      r_ref, lw_ref, k_ref, v_ref, a_ref, b_ref, g_ref, last_ref, carry_ref):
    tm = x_ref.shape[0]

    @pl.when(pl.program_id(1) == 0)
    def _():
        carry_ref[...] = prev_ref[...]

    shift = mod_ref[:, 0:D_MODEL].reshape(1, 1, D_MODEL)
    scale = mod_ref[:, D_MODEL:2 * D_MODEL].reshape(1, 1, D_MODEL)
    h = _modulate(_rms(x_ref[...]), ng_ref[...], shift, scale, 1)
    row = lax.broadcasted_iota(jnp.int32, (tm, 1), 0)
    shifted = jnp.where(row == 0, carry_ref[...], pltpu.roll(h, 1, axis=0))
    carry_ref[...] = h[tm - 1:tm, :]
    last_ref[...] = h[tm - 1:tm, :]
    dx = shifted - h

    def mix(n):
        return (h + dx * mu_ref[n:n + 1, :]).astype(BF16)

    r = _dot(mix(0), wrkv_ref[0])
    k = _dot(mix(1), wrkv_ref[1])
    v = _dot(mix(2), wrkv_ref[2])
    ww = _dot(jnp.tanh(_dot(mix(3), w1_ref[...])).astype(BF16), w2_ref[...])
    z = -(w0_ref[...] + ww)
    softplus = jnp.maximum(z, 0.0) + jnp.log(1.0 + jnp.exp(-jnp.abs(z)))
    lw = -jnp.exp(-softplus - 0.5)
    asig = _sigmoid(a0_ref[...] + _dot(_dot(mix(4), a1_ref[...]).astype(BF16), a2_ref[...]))
    g = _dot(_sigmoid(_dot(mix(5), g1_ref[...])).astype(BF16), g2_ref[...])

    kk = k * kk_ref[...]
    kk2 = kk * kk
    lo_cols = lax.broadcasted_iota(jnp.int32, (tm, LANES), 1) < HEAD

    def head_sums(t):
        lo_sum = jnp.sum(jnp.where(lo_cols, t, 0.0), axis=-1, keepdims=True)
        hi_sum = jnp.sum(jnp.where(lo_cols, 0.0, t), axis=-1, keepdims=True)
        return jnp.where(lo_cols, lo_sum, hi_sum)

    ss = jnp.concatenate([head_sums(kk2[:, p * LANES:(p + 1) * LANES]) for p in range(PAIRS)], axis=1)
    kkn = kk * lax.rsqrt(jnp.maximum(ss, 1e-24))

    r_ref[...] = r.astype(r_ref.dtype)
    lw_ref[...] = lw
    k_ref[...] = (k * (1.0 + (asig - 1.0) * ka_ref[...])).astype(k_ref.dtype)
    v_ref[...] = v.astype(v_ref.dtype)
    a_ref[...] = (-kkn).astype(a_ref.dtype)
    b_ref[...] = (kkn * asig).astype(b_ref.dtype)
    g_ref[...] = g.astype(g_ref.dtype)


def _rwkv_proj(x, mod, prev_x, w, nb, s, tm):
    n = nb * s
    nt = s // tm
    row_spec = pl.BlockSpec((tm, D_MODEL), lambda b, j: (b * nt + j, 0))
    vec = _const_spec((1, D_MODEL))
    lora_w, lora_g = w["w1"].shape[1], w["g1"].shape[1]
    in_specs = [
        row_spec,
        pl.BlockSpec((None, 1, 6 * D_MODEL), lambda b, j: (b, 0, 0)),
        pl.BlockSpec((None, 1, D_MODEL), lambda b, j: (b, 0, 0)),
        vec, _const_spec((6, D_MODEL)), _const_spec((3, D_MODEL, D_MODEL)),
        vec, _const_spec((D_MODEL, lora_w)), _const_spec((lora_w, D_MODEL)),
        vec, _const_spec((D_MODEL, lora_w)), _const_spec((lora_w, D_MODEL)),
        _const_spec((D_MODEL, lora_g)), _const_spec((lora_g, D_MODEL)),
        vec, vec,
    ]
    act = lambda dt: jax.ShapeDtypeStruct((n, D_MODEL), dt)
    out_types = [act(BF16), act(F32)] + [act(BF16)] * 5
    outs = pl.pallas_call(
        _rwkv_proj_kernel,
        grid=(nb, nt),
        in_specs=in_specs,
        out_specs=[row_spec] * 7 + [pl.BlockSpec((None, 1, D_MODEL), lambda b, j: (b, 0, 0))],
        out_shape=out_types + [jax.ShapeDtypeStruct((nb, 1, D_MODEL), F32)],
        scratch_shapes=[pltpu.VMEM((1, D_MODEL), F32)],
        compiler_params=_params(("parallel", "arbitrary")),
        name="rwkv_proj",
    )(x, mod, prev_x, w["norm_g"], w["mu"], w["w_rkv"], w["w0"], w["w1"], w["w2"],
      w["a0"], w["a1"], w["a2"], w["g1"], w["g2"], w["k_k"], w["k_a"])
    return outs


def _split3(x):
    hi = x.astype(BF16)
    rem = x - hi.astype(F32)
    mid = rem.astype(BF16)
    return hi, mid, (rem - mid.astype(F32)).astype(BF16)


def _wkv_kernel(r_ref, lw_ref, k_ref, v_ref, a_ref, b_ref, g_ref, s0_ref, rk_ref, lnw_ref, lnb_ref,
                z_ref, st_ref):
    @pl.when(pl.program_id(1) == 0)
    def _():
        st_ref[...] = s0_ref[...]

    nseq = r_ref.shape[0]
    wide = lambda ref: jnp.concatenate([ref[i] for i in range(nseq)], axis=1)
    per_seq = lambda ref: jnp.concatenate([ref[...]] * nseq, axis=1)
    b16 = lambda t: t.astype(BF16)
    pairs = range(nseq * PAIRS)
    blk = lambda t, p: t[:, p * LANES:(p + 1) * LANES]
    rows = lambda ts: jnp.concatenate(ts, axis=0)
    lo_cols = lax.broadcasted_iota(jnp.int32, (CHUNK, LANES), 1) < HEAD
    rr = lax.broadcasted_iota(jnp.int32, (LANES, LANES), 0)
    cc = lax.broadcasted_iota(jnp.int32, (LANES, LANES), 1)
    head0 = lambda t: jnp.where(lo_cols, t, 0.0)
    head1 = lambda t: jnp.where(lo_cols, 0.0, t)
    blockdiag = lambda q: rows([head0(q), head1(q)])

    r, lw, k, v, a, b = (wide(t).astype(F32) for t in (r_ref, lw_ref, k_ref, v_ref, a_ref, b_ref))
    ti = lax.broadcasted_iota(jnp.int32, (CHUNK, CHUNK), 0)
    tj = lax.broadcasted_iota(jnp.int32, (CHUNK, CHUNK), 1)
    tri = (tj <= ti).astype(BF16)
    cum = sum(_dot(tri, t) for t in _split3(lw))
    cl = cum[CHUNK - 1:CHUNK, :]
    e_neg = jnp.exp(-cum)
    e_end = jnp.exp(cl - cum)
    e_cl = jnp.exp(cl)
    at = b16(a * jnp.exp(cum - lw))
    rt = b16(r * jnp.exp(cum))
    bt = b * e_neg
    kt = k * e_neg
    bkh = rows([b16(b * e_end), b16(k * e_end)])
    v16 = b16(v)

    t_idx = rr & (CHUNK - 1)
    j_idx = cc & (CHUNK - 1)
    keep = (j_idx < t_idx) | ((rr >= CHUNK) & (j_idx == t_idx))
    g0, g1 = [], []
    for p in pairs:
        btp, ktp = blk(bt, p), blk(kt, p)
        w = b16(rows([head0(btp), head0(ktp), head1(ktp), head1(btp)]))
        g = _dot_nt(rows([blk(at, p), blk(rt, p)]), w)
        g0.append(jnp.where(keep, g[:, :LANES], 0.0))
        g1.append(jnp.where(keep, g[:, LANES:], 0.0))

    eye2 = jnp.where((lax.broadcasted_iota(jnp.int32, (CHUNK, LANES), 1) & (CHUNK - 1))
                     == lax.broadcasted_iota(jnp.int32, (CHUNK, LANES), 0), 1.0, 0.0)
    pc = [jnp.where(lo_cols, g0[p][:CHUNK], g1[p][:CHUNK]) for p in pairs]
    tc = [eye2 + pc[p] for p in pairs]
    q = [_dot(b16(pc[p]), b16(blockdiag(pc[p]))) for p in pairs]
    for _ in range(4):
        res = [_dot(b16(rows([tc[p], q[p]])), b16(blockdiag(q[p]))) for p in pairs]
        tc = [tc[p] + res[p][:CHUNK] for p in pairs]
        q = [res[p][CHUNK:] for p in pairs]
    tc = [b16(tc[p] + _dot(b16(tc[p]), b16(blockdiag(q[p])))) for p in pairs]

    akv = []
    for p in pairs:
        w_ak = b16(jnp.where(lo_cols, g1[p][:CHUNK], g0[p][:CHUNK]))
        vp = blk(v16, p)
        akv.append(_dot(w_ak, rows([head1(vp), head0(vp)])))

    s = [st_ref[p // PAIRS, p % PAIRS] for p in pairs]
    s16 = [b16(s[p]) for p in pairs]
    rhs = [_dot_nt(blk(at, p), s16[p]) + akv[p] for p in pairs]
    u = [_dot(tc[p], b16(rows([head0(rhs[p]), head1(rhs[p])]))) for p in pairs]
    u16 = [b16(u[p]) for p in pairs]

    ys = []
    bd_mask = (rr < HEAD) == (cc < HEAD)
    for p in pairs:
        vp = blk(v16, p)
        rbk = b16(jnp.concatenate([g0[p][CHUNK:], g1[p][CHUNK:]], axis=1))
        uv = rows([head0(u16[p]), head0(vp), head1(vp), head1(u16[p])])
        ys.append(_dot_nt(blk(rt, p), s16[p]) + _dot(rbk, uv))
        fresh = _dot_tn(rows([u16[p], vp]), blk(bkh, p))
        st_ref[p // PAIRS, p % PAIRS] = s[p] * blk(e_cl, p) + jnp.where(bd_mask, fresh, 0.0)

    def head_sums(t):
        lo_sum = jnp.sum(head0(t), axis=-1, keepdims=True)
        hi_sum = jnp.sum(head1(t), axis=-1, keepdims=True)
        return jnp.where(lo_cols, lo_sum, hi_sum)

    rkk = r * k * per_seq(rk_ref)
    for p in pairs:
        seq, cols = p // PAIRS, slice((p % PAIRS) * LANES, (p % PAIRS + 1) * LANES)
        yc = ys[p] - head_sums(ys[p]) * (1.0 / HEAD)
        yn = yc * lax.rsqrt(head_sums(yc * yc) * (1.0 / HEAD) + GN_EPS)
        zp = (yn * lnw_ref[:, cols] + lnb_ref[:, cols] + head_sums(blk(rkk, p)) * blk(v, p)) * g_ref[seq, :, cols].astype(F32)
        z_ref[seq, :, cols] = zp.astype(z_ref.dtype)


def _wkv(r, lw, k, v, a, b, g, st0, w, nb, s):
    nseq = math.gcd(nb, WKV_SEQS)
    blk = pl.BlockSpec((nseq, CHUNK, D_MODEL), lambda bi, c: (bi, c, 0))
    vec = _const_spec((1, D_MODEL))
    st_spec = pl.BlockSpec((nseq, PAIRS, LANES, LANES), lambda bi, c: (bi, 0, 0, 0))
    return pl.pallas_call(
        _wkv_kernel,
        grid=(nb // nseq, s // CHUNK),
        in_specs=[blk] * 7 + [st_spec, vec, vec, vec],
        out_specs=[blk, st_spec],
        out_shape=[jax.ShapeDtypeStruct((nb, s, D_MODEL), BF16),
                   jax.ShapeDtypeStruct((nb, PAIRS, LANES, LANES), F32)],
        compiler_params=_params(("parallel", "arbitrary")),
        name="wkv",
    )(r, lw, k, v, a, b, g, st0, w["r_k"], w["lnx_w"], w["lnx_b"])


def _out_proj_kernel(x_ref, z_ref, gate_ref, w_ref, bias_ref, o_ref, *, nb):
    rows = x_ref.shape[0]
    y = _dot(z_ref[...], w_ref[...]) + bias_ref[...]
    y = (y.reshape(nb, rows // nb, D_MODEL) * gate_ref[...]).reshape(rows, D_MODEL)
    o_ref[...] = x_ref[...] + y


def _out_proj(x, z, gate, w, bias, nb, s, tm):
    n = nb * s
    tb = max(tm // s, 1)
    per_seq = max(s // tm, 1)
    row_spec = pl.BlockSpec((tm, D_MODEL), lambda i: (i, 0))
    return pl.pallas_call(
        functools.partial(_out_proj_kernel, nb=tb),
        grid=(n // tm,),
        in_specs=[row_spec, row_spec,
                  pl.BlockSpec((tb, 1, D_MODEL), lambda i: (i // per_seq, 0, 0)),
                  _const_spec((D_MODEL, D_MODEL)), _const_spec((1, D_MODEL))],
        out_specs=row_spec,
        out_shape=jax.ShapeDtypeStruct((n, D_MODEL), F32),
        compiler_params=_params(("parallel",)),
        name="out_proj",
    )(x, z, gate, w, bias)


def _top2(logits):
    lane = lax.broadcasted_iota(jnp.int32, logits.shape, 1).astype(F32)
    first_of = lambda hit: jnp.min(jnp.where(hit, lane, float(LANES)), axis=-1, keepdims=True)
    lg = jnp.where(lane < N_GROUPS, logits, -jnp.inf)
    gmax = jnp.max(lg, axis=-1, keepdims=True)
    gi = first_of(lg == gmax)
    gp = 1.0 / jnp.sum(jnp.exp(lg - gmax), axis=-1, keepdims=True)
    first = ROUTE_LANE0 + gi * EXP_PER_GROUP
    le = jnp.where((lane >= first) & (lane < first + EXP_PER_GROUP), logits, -jnp.inf)
    top1 = jnp.max(le, axis=-1, keepdims=True)
    i1 = first_of(le == top1)
    le2 = jnp.where(lane == i1, -jnp.inf, le)
    top2 = jnp.max(le2, axis=-1, keepdims=True)
    i2 = first_of(le2 == top2)
    e2 = jnp.exp(top2 - top1)
    w1 = gp / (1.0 + e2)
    return i1, i2, w1, w1 * e2


def _route(logits):
    lane = lax.broadcasted_iota(jnp.int32, logits.shape, 1).astype(F32)
    i1, i2, w1, w2 = _top2(logits)
    return jnp.where(lane == i1, w1, 0.0) + jnp.where(lane == i2, w2, 0.0)


def _moe_kernel(x_ref, mod_ref, ng_ref, wr_ref, wgu_ref, wdn_ref, fg_ref, o_ref, h_scr, cw_scr, acc_scr,
                *, nb, final_norm):
    e = pl.program_id(1)
    rows = x_ref.shape[0]

    @pl.when(e == 0)
    def _():
        shift = mod_ref[:, :, 3 * D_MODEL:4 * D_MODEL]
        scale = mod_ref[:, :, 4 * D_MODEL:5 * D_MODEL]
        h = _modulate(_rms(x_ref[...]), ng_ref[...], shift, scale, nb)
        h_scr[...] = h.astype(BF16)
        cw_scr[...] = _route(_dot_x3(h, wr_ref[...]))
        acc_scr[...] = jnp.zeros_like(acc_scr)

    lane = lax.broadcasted_iota(jnp.int32, (rows, LANES), 1)
    cw = jnp.sum(jnp.where(lane == ROUTE_LANE0 + e, cw_scr[...], 0.0), axis=-1, keepdims=True)
    gu = _dot(h_scr[...], wgu_ref[...].astype(BF16))
    gate = gu[:, :D_EXPERT]
    hid = gate * _sigmoid(gate) * gu[:, D_EXPERT:] * cw
    acc_scr[...] += _dot(hid.astype(BF16), wdn_ref[...].astype(BF16))

    @pl.when(e == N_EXPERTS - 1)
    def _():
        g2 = mod_ref[:, :, 5 * D_MODEL:6 * D_MODEL]
        y = (acc_scr[...].reshape(nb, rows // nb, D_MODEL) * g2).reshape(rows, D_MODEL)
        out = x_ref[...] + y
        if final_norm:
            out = _rms(out) * fg_ref[...]
        o_ref[...] = out


def _moe(x, mod, norm_g, w_route, w_gu, w_down, final_g, nb, s, layer, tm, final_norm):
    n = nb * s
    tb = max(tm // s, 1)
    per_seq = max(s // tm, 1)
    row_spec = pl.BlockSpec((tm, D_MODEL), lambda i, e: (i, 0))
    return pl.pallas_call(
        functools.partial(_moe_kernel, nb=tb, final_norm=final_norm),
        grid=(n // tm, N_EXPERTS),
        in_specs=[row_spec,
                  pl.BlockSpec((tb, 1, 6 * D_MODEL), lambda i, e: (i // per_seq, 0, 0)),
                  _const_spec((1, D_MODEL)), _const_spec((D_MODEL, LANES)),
                  pl.BlockSpec((None, None, D_MODEL, 2 * D_EXPERT), lambda i, e: (layer, e, 0, 0)),
                  pl.BlockSpec((None, None, D_EXPERT, D_MODEL), lambda i, e: (layer, e, 0, 0)),
                  _const_spec((1, D_MODEL))],
        out_specs=row_spec,
        out_shape=jax.ShapeDtypeStruct((n, D_MODEL), F32),
        scratch_shapes=[pltpu.VMEM((tm, D_MODEL), BF16), pltpu.VMEM((tm, LANES), F32),
                        pltpu.VMEM((tm, D_MODEL), F32)],
        compiler_params=_params(("parallel", "arbitrary")),
        name="moe",
    )(x, mod, norm_g, w_route, w_gu, w_down, final_g)


PAIRS_PER_GROUP = EXP_PER_GROUP * (EXP_PER_GROUP - 1) // 2
N_BUCKETS = N_GROUPS * PAIRS_PER_GROUP
BUCKET_TILE = 256
ROW_W = D_MODEL + LANES
META_BUCKET, META_RANK, META_W_LO, META_W_HI = 0, 1, 2, 3
ROUTER_TILE = 512
MOVE_TILE = 512
DMA_UNROLL = 8
SPARSE_MIN_TOKENS = 4096


def _moe_norm(x, mod_ref, ng_ref, nb):
    shift = mod_ref[:, :, 3 * D_MODEL:4 * D_MODEL]
    scale = mod_ref[:, :, 4 * D_MODEL:5 * D_MODEL]
    return _modulate(_rms(x), ng_ref[...], shift, scale, nb)


def _router_kernel(x_ref, mod_ref, ng_ref, wr_ref, earlier_ref, meta_ref, cnt_ref, *, nb):
    @pl.when(pl.program_id(0) == 0)
    def _():
        cnt_ref[...] = jnp.zeros_like(cnt_ref)

    rows = x_ref.shape[0]
    h = _moe_norm(x_ref[...], mod_ref, ng_ref, nb)
    i1, i2, w1, w2 = _top2(_dot_x3(h, wr_ref[...]))
    lo = (jnp.minimum(i1, i2) - ROUTE_LANE0).astype(jnp.int32)
    hi = (jnp.maximum(i1, i2) - ROUTE_LANE0).astype(jnp.int32)
    first_is_lo = i1 < i2
    a = lo & (EXP_PER_GROUP - 1)
    b = hi & (EXP_PER_GROUP - 1)
    group = lo >> (EXP_PER_GROUP.bit_length() - 1)
    bucket = group * PAIRS_PER_GROUP + ((a * (2 * EXP_PER_GROUP - 1 - a)) >> 1) + (b - a - 1)

    lane = lax.broadcasted_iota(jnp.int32, (rows, LANES), 1)
    mine = lane == bucket
    onehot = jnp.where(mine, 1.0, 0.0)
    seen = cnt_ref[...]
    before = _dot(earlier_ref[...], onehot.astype(BF16)) + seen
    rank = jnp.sum(jnp.where(mine, before, 0.0), axis=-1, keepdims=True)
    cnt_ref[...] = seen + jnp.sum(onehot, axis=0, keepdims=True)

    meta = jnp.where(lane == META_BUCKET, bucket.astype(F32), 0.0)
    meta = jnp.where(lane == META_RANK, rank, meta)
    meta = jnp.where(lane == META_W_LO, jnp.where(first_is_lo, w1, w2), meta)
    meta_ref[...] = jnp.where(lane == META_W_HI, jnp.where(first_is_lo, w2, w1), meta)


def _positions_kernel(meta_ref, start_ref, pos_ref):
    meta = meta_ref[...]
    lane = lax.broadcasted_iota(jnp.int32, meta.shape, 1)
    column = lambda c: jnp.sum(jnp.where(lane == c, meta, 0.0), axis=-1, keepdims=True)
    value = jnp.where(lane == column(META_BUCKET).astype(jnp.int32), start_ref[...] + column(META_RANK), 0.0)
    ones = jnp.ones((8, LANES), BF16)
    pos = sum(_dot_nt(ones, piece) for piece in _split3(value))
    pos_ref[...] = pos[0:1, :].astype(jnp.int32)


def _dispatch_kernel(pos_ref, x_ref, mod_ref, ng_ref, meta_ref, xs_ref, buf, sem, *, nb):
    i = pl.program_id(0)
    rows = x_ref.shape[0]
    slot = i % 2
    buf[slot, :, 0:D_MODEL] = _moe_norm(x_ref[...], mod_ref, ng_ref, nb)
    buf[slot, :, D_MODEL:ROW_W] = meta_ref[...]

    def send(r, carry):
        pltpu.make_async_copy(buf.at[slot, pl.ds(r, 1), :],
                              xs_ref.at[pl.ds(pos_ref[i * rows + r], 1), :], sem.at[slot]).start()
        return carry

    lax.fori_loop(0, rows, send, 0, unroll=DMA_UNROLL)

    def wait_all(sl):
        pltpu.make_async_copy(buf.at[sl], xs_ref.at[pl.ds(0, rows), :], sem.at[sl]).wait()

    @pl.when(i >= 1)
    def _():
        wait_all(1 - slot)

    @pl.when(i == pl.num_programs(0) - 1)
    def _():
        wait_all(slot)


def _experts_kernel(tile_ref, bucket_ref, lo_ref, hi_ref, used_ref, xs_ref, gu_lo_ref, gu_hi_ref,
                    dn_lo_ref, dn_hi_ref, ys_ref):
    k = pl.program_id(0)

    @pl.when(k < used_ref[0])
    def _():
        rows = xs_ref.shape[0]
        x16 = xs_ref[:, 0:D_MODEL].astype(BF16)
        meta = xs_ref[:, D_MODEL:ROW_W]
        lane = lax.broadcasted_iota(jnp.int32, (rows, LANES), 1)
        column = lambda c: jnp.sum(jnp.where(lane == c, meta, 0.0), axis=-1, keepdims=True)
        member = column(META_BUCKET) == bucket_ref[k].astype(F32)

        def expert(gu_ref, dn_ref, col):
            gu = _dot(x16, gu_ref[...].astype(BF16))
            gate = gu[:, :D_EXPERT]
            hid = gate * _sigmoid(gate) * gu[:, D_EXPERT:] * jnp.where(member, column(col), 0.0)
            return _dot(hid.astype(BF16), dn_ref[...].astype(BF16))

        y = expert(gu_lo_ref, dn_lo_ref, META_W_LO) + expert(gu_hi_ref, dn_hi_ref, META_W_HI)
        new_tile = (k == 0) | (tile_ref[k] != tile_ref[jnp.maximum(k - 1, 0)])

        @pl.when(new_tile)
        def _():
            ys_ref[...] = y

        @pl.when(jnp.logical_not(new_tile))
        def _():
            ys_ref[...] += y


def _combine_kernel(pos_ref, x_ref, mod_ref, fg_ref, ys_ref, o_ref, buf, sem, *, nb, final_norm):
    i = pl.program_id(0)
    rows = x_ref.shape[0]
    slot = i % 2

    def fetch(step, sl):
        def one(r, carry):
            pltpu.make_async_copy(ys_ref.at[pl.ds(pos_ref[step * rows + r], 1), :],
                                  buf.at[sl, pl.ds(r, 1), :], sem.at[sl]).start()
            return carry
        lax.fori_loop(0, rows, one, 0, unroll=DMA_UNROLL)

    @pl.when(i == 0)
    def _():
        fetch(0, 0)

    pltpu.make_async_copy(ys_ref.at[pl.ds(0, rows), :], buf.at[slot], sem.at[slot]).wait()

    @pl.when(i + 1 < pl.num_programs(0))
    def _():
        fetch(i + 1, 1 - slot)

    g2 = mod_ref[:, :, 5 * D_MODEL:6 * D_MODEL]
    y = (buf[slot].reshape(nb, rows // nb, D_MODEL) * g2).reshape(rows, D_MODEL)
    out = x_ref[...] + y
    if final_norm:
        out = _rms(out) * fg_ref[...]
    o_ref[...] = out


def _bucket_experts():
    lo, hi = [], []
    for g in range(N_GROUPS):
        for a in range(EXP_PER_GROUP):
            for b in range(a + 1, EXP_PER_GROUP):
                lo.append(g * EXP_PER_GROUP + a)
                hi.append(g * EXP_PER_GROUP + b)
    return jnp.asarray(lo, jnp.int32), jnp.asarray(hi, jnp.int32)


def _moe_sparse(x, mod, norm_g, w_route, w_gu, w_down, final_g, nb, s, layer, final_norm):
    n = nb * s
    move_params = pltpu.CompilerParams(dimension_semantics=("arbitrary",), vmem_limit_bytes=VMEM_LIMIT,
                                       disable_bounds_checks=True)

    def token_specs(tm):
        tb, per_seq = max(tm // s, 1), max(s // tm, 1)
        row = pl.BlockSpec((tm, D_MODEL), lambda i, *_: (i, 0))
        modspec = pl.BlockSpec((tb, 1, 6 * D_MODEL), lambda i, *_: (i // per_seq, 0, 0))
        meta = pl.BlockSpec((tm, LANES), lambda i, *_: (i, 0))
        return tb, row, modspec, meta

    vec = pl.BlockSpec((1, D_MODEL), lambda *_: (0, 0))
    tb, row, modspec, metaspec = token_specs(ROUTER_TILE)
    meta, counts = pl.pallas_call(
        functools.partial(_router_kernel, nb=tb),
        grid=(n // ROUTER_TILE,),
        in_specs=[row, modspec, vec, pl.BlockSpec((D_MODEL, LANES), lambda i: (0, 0)),
                  pl.BlockSpec((ROUTER_TILE, ROUTER_TILE), lambda i: (0, 0))],
        out_specs=[metaspec, pl.BlockSpec((1, LANES), lambda i: (0, 0))],
        out_shape=[jax.ShapeDtypeStruct((n, LANES), F32), jax.ShapeDtypeStruct((1, LANES), F32)],
        compiler_params=_params(("arbitrary",)),
        name="moe_router",
    )(x, mod, norm_g, w_route, jnp.tril(jnp.ones((ROUTER_TILE, ROUTER_TILE), BF16), -1))

    counts = counts[0, :N_BUCKETS].astype(jnp.int32)
    end = jnp.cumsum(counts)
    start = end - counts
    first_tile = start // BUCKET_TILE
    per_bucket = jnp.where(counts > 0, (end - 1) // BUCKET_TILE - first_tile + 1, 0)
    item_end = jnp.cumsum(per_bucket)
    used = item_end[-1:]
    n_items = n // BUCKET_TILE + N_BUCKETS
    k = jnp.minimum(jnp.arange(n_items, dtype=jnp.int32), used - 1)
    item_bucket = jnp.sum(item_end[None, :] <= k[:, None], axis=1, dtype=jnp.int32)
    item_tile = first_tile[item_bucket] + k - (item_end - per_bucket)[item_bucket]
    bucket_lo, bucket_hi = _bucket_experts()
    item_lo, item_hi = bucket_lo[item_bucket], bucket_hi[item_bucket]
    start_row = jnp.pad(start, (0, LANES - N_BUCKETS)).astype(F32)[None, :]
    pos = pl.pallas_call(
        _positions_kernel,
        grid=(n // ROUTER_TILE,),
        in_specs=[metaspec, pl.BlockSpec((1, LANES), lambda i: (0, 0))],
        out_specs=pl.BlockSpec((1, ROUTER_TILE), lambda i: (0, i)),
        out_shape=jax.ShapeDtypeStruct((1, n), jnp.int32),
        compiler_params=_params(("parallel",)),
        name="moe_positions",
    )(meta, start_row).reshape(n)

    tb, row, modspec, metaspec = token_specs(MOVE_TILE)
    any_spec = pl.BlockSpec(memory_space=pl.ANY)
    xs = pl.pallas_call(
        functools.partial(_dispatch_kernel, nb=tb),
        grid_spec=pltpu.PrefetchScalarGridSpec(
            num_scalar_prefetch=1, grid=(n // MOVE_TILE,),
            in_specs=[row, modspec, vec, metaspec], out_specs=any_spec,
            scratch_shapes=[pltpu.VMEM((2, MOVE_TILE, ROW_W), F32), pltpu.SemaphoreType.DMA((2,))]),
        out_shape=jax.ShapeDtypeStruct((n, ROW_W), F32),
        compiler_params=move_params,
        name="moe_dispatch",
    )(pos, x, mod, norm_g, meta)

    tile_of = lambda k, tile, *_: (tile[k], 0)
    ys = pl.pallas_call(
        _experts_kernel,
        grid_spec=pltpu.PrefetchScalarGridSpec(
            num_scalar_prefetch=5, grid=(n_items,),
            in_specs=[pl.BlockSpec((BUCKET_TILE, ROW_W), tile_of),
                      pl.BlockSpec((None, None, D_MODEL, 2 * D_EXPERT), lambda k, t, b, lo, hi, nu: (layer, lo[k], 0, 0)),
                      pl.BlockSpec((None, None, D_MODEL, 2 * D_EXPERT), lambda k, t, b, lo, hi, nu: (layer, hi[k], 0, 0)),
                      pl.BlockSpec((None, None, D_EXPERT, D_MODEL), lambda k, t, b, lo, hi, nu: (layer, lo[k], 0, 0)),
                      pl.BlockSpec((None, None, D_EXPERT, D_MODEL), lambda k, t, b, lo, hi, nu: (layer, hi[k], 0, 0))],
            out_specs=pl.BlockSpec((BUCKET_TILE, D_MODEL), tile_of)),
        out_shape=jax.ShapeDtypeStruct((n, D_MODEL), F32),
        compiler_params=_params(("arbitrary",)),
        name="moe_experts",
    )(item_tile, item_bucket, item_lo, item_hi, used, xs, w_gu, w_gu, w_down, w_down)

    return pl.pallas_call(
        functools.partial(_combine_kernel, nb=tb, final_norm=final_norm),
        grid_spec=pltpu.PrefetchScalarGridSpec(
            num_scalar_prefetch=1, grid=(n // MOVE_TILE,),
            in_specs=[row, modspec, vec, any_spec], out_specs=row,
            scratch_shapes=[pltpu.VMEM((2, MOVE_TILE, D_MODEL), F32), pltpu.SemaphoreType.DMA((2,))]),
        out_shape=jax.ShapeDtypeStruct((n, D_MODEL), F32),
        compiler_params=move_params,
        name="moe_combine",
    )(pos, x, mod, final_g, ys)


def _rope(x, cos, sin_lo, sin_hi):
    return x * cos + pltpu.roll(x, LANES - ROPE_DIM // 2, axis=1) * sin_lo + pltpu.roll(x, ROPE_DIM // 2, axis=1) * sin_hi


def _qkv_kernel(x_ref, mod_ref, kvmod_ref, ngq_ref, ngkv_ref, wq_ref, bq_ref, wkv_ref, bkv_ref,
                cos_ref, slo_ref, shi_ref, q_ref, k_ref, v_ref):
    n = _rms(x_ref[...])
    shift = mod_ref[:, 0:D_MODEL].reshape(1, 1, D_MODEL)
    scale = mod_ref[:, D_MODEL:2 * D_MODEL].reshape(1, 1, D_MODEL)
    hq = _modulate(n, ngq_ref[...], shift, scale, 1)
    kshift = kvmod_ref[:, 0:D_MODEL].reshape(1, 1, D_MODEL)
    kscale = kvmod_ref[:, D_MODEL:2 * D_MODEL].reshape(1, 1, D_MODEL)
    hkv = _modulate(n, ngkv_ref[...], kshift, kscale, 1)
    cos, slo, shi = cos_ref[...], slo_ref[...], shi_ref[...]
    q = _dot(hq.astype(BF16), wq_ref[...]) + bq_ref[...]
    for p in range(PAIRS):
        cols = slice(p * LANES, (p + 1) * LANES)
        q_ref[:, cols] = (_rope(q[:, cols], cos, slo, shi) * ATT_SCALE).astype(q_ref.dtype)
    kv = _dot(hkv.astype(BF16), wkv_ref[...]) + bkv_ref[...]
    for p in range(KV_W // LANES):
        cols = slice(p * LANES, (p + 1) * LANES)
        k_ref[:, cols] = _rope(kv[:, cols], cos, slo, shi)
    v_ref[...] = kv[:, KV_W:]


def _qkv_proj(x, mod, kvmod, w, tables, nb, s, tm):
    n = nb * s
    nt = s // tm
    row_spec = pl.BlockSpec((tm, D_MODEL), lambda b, j: (b * nt + j, 0))
    kv_spec = pl.BlockSpec((tm, KV_W), lambda b, j: (b * nt + j, 0))
    tab_spec = pl.BlockSpec((tm, LANES), lambda b, j: (j, 0))
    vec = _const_spec((1, D_MODEL))
    return pl.pallas_call(
        _qkv_kernel,
        grid=(nb, nt),
        in_specs=[row_spec,
                  pl.BlockSpec((None, 1, 6 * D_MODEL), lambda b, j: (b, 0, 0)),
                  pl.BlockSpec((None, 1, 2 * D_MODEL), lambda b, j: (b, 0, 0)),
                  vec, vec, _const_spec((D_MODEL, D_MODEL)), vec,
                  _const_spec((D_MODEL, 2 * KV_W)), _const_spec((1, 2 * KV_W)),
                  tab_spec, tab_spec, tab_spec],
        out_specs=[row_spec, kv_spec, kv_spec],
        out_shape=[jax.ShapeDtypeStruct((n, D_MODEL), BF16),
                   jax.ShapeDtypeStruct((n, KV_W), F32), jax.ShapeDtypeStruct((n, KV_W), F32)],
        compiler_params=_params(("parallel", "parallel")),
        name="qkv_proj",
    )(x, mod, kvmod, w["norm_gq"], w["norm_gkv"], w["w_q"], w["b_q"], w["w_kv"], w["b_kv"], *tables)


def _attn_kernel(sink_ref, q_ref, k0_ref, k1_ref, k2_ref, v0_ref, v1_ref, v2_ref, o_ref, *, banded):
    n = pl.program_id(1)
    seqs = range(q_ref.shape[0])
    k_all = [jnp.concatenate([k0_ref[i], k1_ref[i], k2_ref[i]], axis=0) for i in seqs]
    v_all = [jnp.concatenate([v0_ref[i], v1_ref[i], v2_ref[i]], axis=0) for i in seqs]
    lane = lax.broadcasted_iota(jnp.int32, (1, LANES), 1)
    lo = lane < HEAD
    col = lax.broadcasted_iota(jnp.int32, (1, 3 * CHUNK), 1)
    if banded:
        visible = (col >= 2 * CHUNK) | ((col >= CHUNK) & (n >= 1)) | (n >= 2)

    def halves(t, kv_head):
        blk = t[:, (kv_head // 2) * LANES:(kv_head // 2 + 1) * LANES]
        if kv_head % 2 == 0:
            t_lo = jnp.where(lo, blk, 0.0)
            t_hi = pltpu.roll(t_lo, HEAD, axis=1)
        else:
            t_hi = jnp.where(lo, 0.0, blk)
            t_lo = pltpu.roll(t_hi, HEAD, axis=1)
        return t_lo.astype(BF16), t_hi.astype(BF16)

    first_pair = lax.broadcasted_iota(jnp.int32, (2 * CHUNK, 1), 0) < CHUNK

    def probs(s, head_a, head_b):
        if banded:
            s = jnp.where(visible, s, NEG_INF)
        sink = jnp.where(first_pair, sink_ref[head_a], sink_ref[head_b])
        m = jnp.maximum(jnp.max(s, axis=-1, keepdims=True), sink)
        e = jnp.exp(s - m)
        return (e / (jnp.sum(e, axis=-1, keepdims=True) + jnp.exp(sink - m))).astype(BF16)

    units = [(i, g) for i in seqs for g in range(N_KV)]
    scores, values = [], []
    for i, g in units:
        k_lo, k_hi = halves(k_all[i], g)
        values.append(halves(v_all[i], g))
        q2 = jnp.concatenate([q_ref[i, :, (2 * g + j) * LANES:(2 * g + j + 1) * LANES] for j in range(2)], axis=0)
        scores.append((_dot_nt(q2, k_lo), _dot_nt(q2, k_hi)))
    for (i, g), (s_lo, s_hi), (v_lo, v_hi) in zip(units, scores, values):
        o = _dot(probs(s_lo, 4 * g, 4 * g + 2), v_lo) + _dot(probs(s_hi, 4 * g + 1, 4 * g + 3), v_hi)
        for j in range(2):
            pair = 2 * g + j
            o_ref[i, :, pair * LANES:(pair + 1) * LANES] = o[j * CHUNK:(j + 1) * CHUNK].astype(o_ref.dtype)


def _attention(q, kv_arrays, kv_chunks, sinks, nb, s, banded):
    nseq = math.gcd(nb, ATTN_SEQS)
    q_spec = pl.BlockSpec((nseq, CHUNK, D_MODEL), lambda b, c: (b, c, 0))
    kv_specs = [pl.BlockSpec((nseq, CHUNK, KV_W), lambda b, c, f=f: (b, f(c), 0)) for f in kv_chunks]
    return pl.pallas_call(
        functools.partial(_attn_kernel, banded=banded),
        grid=(nb // nseq, s // CHUNK),
        in_specs=[pl.BlockSpec(memory_space=pltpu.SMEM), q_spec] + kv_specs,
        out_specs=q_spec,
        out_shape=jax.ShapeDtypeStruct((nb, s, D_MODEL), BF16),
        compiler_params=_params(("parallel", "parallel")),
        name="attention",
    )(sinks, q, *kv_arrays)


def _pad_cols(w, n):
    return jnp.pad(w, ((0, 0), (0, n - w.shape[1])))


def _pad_rows(w, n):
    return jnp.pad(w, ((0, n - w.shape[0]), (0, 0)))


def _rope_tables(pos):
    half = ROPE_DIM // 2
    inv = jnp.power(jnp.float32(ROPE_THETA), -jnp.arange(half, dtype=F32) * (2.0 / ROPE_DIM))
    ang = pos[:, None] * inv[None, :]
    cos, sin = jnp.cos(ang), jnp.sin(ang)
    rest = HEAD - ROPE_DIM
    ones = jnp.ones((pos.shape[0], rest), F32)
    zeros = jnp.zeros((pos.shape[0], rest), F32)
    z8 = jnp.zeros_like(sin)
    per_head = (jnp.concatenate([cos, cos, ones], axis=1),
                jnp.concatenate([-sin, z8, zeros], axis=1),
                jnp.concatenate([z8, sin, zeros], axis=1))
    return tuple(jnp.tile(t, (1, LANES // HEAD)) for t in per_head)


def _state_to_pairs(state):
    nb = state.shape[0]
    st = state.astype(F32).reshape(nb, PAIRS, 2, HEAD, HEAD)
    z = jnp.zeros_like(st[:, :, 0])
    top = jnp.concatenate([st[:, :, 0], z], axis=-1)
    bot = jnp.concatenate([z, st[:, :, 1]], axis=-1)
    return jnp.concatenate([top, bot], axis=-2)


def _pairs_to_state(st):
    nb = st.shape[0]
    s0 = st[:, :, :HEAD, :HEAD]
    s1 = st[:, :, HEAD:, HEAD:]
    return jnp.stack([s0, s1], axis=2).reshape(nb, 2 * PAIRS, HEAD, HEAD)


def _trunk(x, mods, kvmod, pos, prev_x, prev_wkv, past_k, past_v, w, nb, s):
    n = nb * s
    x = x.reshape(n, D_MODEL)
    mod0, mod1 = mods[0][:, None, :], mods[1][:, None, :]
    kvmod = kvmod[:, None, :]

    tm = min(s, 256)
    r, lw, k, v, a, b, g, last_x = _rwkv_proj(x, mod0, prev_x[:, None, :], w["rw"], nb, s, tm)
    seqs = lambda t: t.reshape(nb, s, D_MODEL)
    z, st = _wkv(*(seqs(t) for t in (r, lw, k, v, a, b, g)), _state_to_pairs(prev_wkv), w["rw"], nb, s)
    z = z.reshape(n, D_MODEL)

    def residual_moe(x, z, mixer, mod, l):
        args = (mod, w["norm_g"][l][1:2], w["moe_route"][l], w["moe_gu"], w["moe_down"], w["final_g"], nb, s, l)
        x = _out_proj(x, z, mod[:, :, 2 * D_MODEL:3 * D_MODEL], mixer["w_o"], mixer["b_o"], nb, s, min(n, 512))
        if n >= SPARSE_MIN_TOKENS:
            return _moe_sparse(x, *args, l == 1)
        return _moe(x, *args, min(n, 1024), l == 1)

    x = residual_moe(x, z, w["rw"], mod0, 0)

    tm = min(s, 256)
    q, k_new, v_new = _qkv_proj(x, mod1, kvmod, w["at"], _rope_tables(pos), nb, s, tm)
    k_seq, v_seq = k_new.reshape(nb, s, KV_W), v_new.reshape(nb, s, KV_W)
    if past_k is None:
        arrays = [k_seq] * 3 + [v_seq] * 3
        chunks = [lambda c, d=d: jnp.maximum(c - d, 0) for d in (2, 1, 0)] * 2
    else:
        pk = past_k.astype(F32).reshape(nb, 2 * CHUNK, KV_W)
        pv = past_v.astype(F32).reshape(nb, 2 * CHUNK, KV_W)
        arrays = [pk, pk, k_seq, pv, pv, v_seq]
        chunks = [lambda c: 0, lambda c: 1, lambda c: 0] * 2
    o = _attention(q.reshape(nb, s, D_MODEL), arrays, chunks, w["at"]["sinks"], nb, s, past_k is None)
    y = residual_moe(x, o.reshape(n, D_MODEL), w["at"], mod1, 1)

    return y.reshape(nb, s, D_MODEL), last_x.reshape(1, nb, D_MODEL), _pairs_to_state(st)[None], k_seq, v_seq


def kernel(x_prompt, x_sample, state_shift, state_wkv, cache_k, cache_v, c_prompt, c_sample, ada_w, ada_b, norm_g, rw_mu, rw_w_rkv, rw_w0, rw_w1, rw_w2, rw_a0, rw_a1, rw_a2, rw_g1, rw_g2, rw_k_k, rw_k_a, rw_r_k, rw_lnx_w, rw_lnx_b, rw_w_o, kv_ada_w, kv_ada_b, kv_norm_g, w_kv, b_kv, at_w_q, at_b_q, at_sinks, at_w_o, at_b_o, moe_w_group, moe_w_expert, moe_w_gu, moe_w_down, final_norm_g):
    bp, sp, _ = x_prompt.shape
    bs, ss, _ = x_sample.shape
    row = lambda t: t.reshape(1, -1).astype(F32)

    c_all = jnp.concatenate([c_prompt, c_sample], axis=0)
    mods = _cond_linear(c_all, ada_w, ada_b[:, None, :])
    kvmods = _cond_linear(c_all, kv_ada_w[None], kv_ada_b[None, None, :])[0]

    lora_pad, gate_pad = LANES, 2 * LANES
    w = {
        "norm_g": norm_g,
        "final_g": row(final_norm_g),
        "rw": {
            "norm_g": norm_g[0, 0:1], "mu": rw_mu[0], "w_rkv": rw_w_rkv[0].astype(BF16),
            "w0": row(rw_w0[0]), "w1": _pad_cols(rw_w1[0], lora_pad).astype(BF16),
            "w2": _pad_rows(rw_w2[0], lora_pad).astype(BF16),
            "a0": row(rw_a0[0]), "a1": _pad_cols(rw_a1[0], lora_pad).astype(BF16),
            "a2": _pad_rows(rw_a2[0], lora_pad).astype(BF16),
            "g1": _pad_cols(rw_g1[0], gate_pad).astype(BF16), "g2": _pad_rows(rw_g2[0], gate_pad).astype(BF16),
            "k_k": row(rw_k_k[0]), "k_a": row(rw_k_a[0]), "r_k": row(rw_r_k[0]),
            "lnx_w": row(rw_lnx_w[0]), "lnx_b": row(rw_lnx_b[0]),
            "w_o": rw_w_o[0].astype(BF16), "b_o": jnp.zeros((1, D_MODEL), F32),
        },
        "at": {
            "norm_gq": norm_g[1, 0:1], "norm_gkv": row(kv_norm_g),
            "w_q": at_w_q[0].astype(BF16), "b_q": row(at_b_q[0]),
            "w_kv": w_kv.astype(BF16), "b_kv": row(b_kv),
            "sinks": at_sinks[0].astype(F32),
            "w_o": at_w_o[0].astype(BF16), "b_o": row(at_b_o[0]),
        },
        "moe_route": jnp.pad(jnp.concatenate([moe_w_group, moe_w_expert], axis=-1),
                             ((0, 0), (0, 0), (0, LANES - N_GROUPS - N_EXPERTS))),
        "moe_gu": moe_w_gu,
        "moe_down": moe_w_down,
    }

    pos_p = jnp.arange(sp, dtype=F32)
    pos_s = PAST_LEN + jnp.arange(ss, dtype=F32)
    zero_x = jnp.zeros((bp, D_MODEL), x_prompt.dtype)
    zero_wkv = jnp.zeros((bp,) + state_wkv.shape[2:], state_wkv.dtype)
    y_p, p_shift, p_wkv, p_k, p_v = _trunk(x_prompt, mods[:, :bp], kvmods[:bp], pos_p, zero_x, zero_wkv,
                                           None, None, w, bp, sp)
    y_s, s_shift, s_wkv, s_k, s_v = _trunk(x_sample, mods[:, bp:], kvmods[bp:], pos_s, state_shift[0],
                                           state_wkv[0], cache_k, cache_v, w, bs, ss)
    keep = min(2 * CHUNK, sp)
    heads = lambda t: t.reshape(t.shape[0], t.shape[1], N_KV, HEAD)
    return (y_p, y_s, p_shift, p_wkv.astype(state_wkv.dtype), heads(p_k[:, sp - keep:]), heads(p_v[:, sp - keep:]),
            s_shift, s_wkv.astype(state_wkv.dtype), heads(s_k), heads(s_v))
```

```python
import functools
import math

import jax
import jax.numpy as jnp
from jax import lax
from jax.experimental import pallas as pl
from jax.experimental.pallas import tpu as pltpu

F32 = jnp.float32
BF16 = jnp.bfloat16

D_MODEL = 1024
LANES = 128
HEAD = 64
PAIRS = D_MODEL // LANES
CHUNK = 64
WKV_SEQS = 4
ATTN_SEQS = 2
PAST_LEN = 4096
N_KV = 4
KV_W = N_KV * HEAD
ROPE_DIM = 16
ROPE_THETA = 500000.0
ATT_SCALE = HEAD ** -0.5
N_GROUPS = 4
EXP_PER_GROUP = 8
N_EXPERTS = N_GROUPS * EXP_PER_GROUP
D_EXPERT = D_MODEL // 4
ROUTE_LANE0 = N_GROUPS
RMS_EPS = 1e-6
GN_EPS = 64e-5
NEG_INF = -1e30
VMEM_LIMIT = 56 * 1024 * 1024


def _params(sem):
    return pltpu.CompilerParams(dimension_semantics=sem, vmem_limit_bytes=VMEM_LIMIT)


def _dot(a, b):
    return jnp.dot(a, b, preferred_element_type=F32)


def _dot_nt(a, b):
    return lax.dot_general(a, b, (((1,), (1,)), ((), ())), preferred_element_type=F32)


def _dot_tn(a, b):
    return lax.dot_general(a, b, (((0,), (0,)), ((), ())), preferred_element_type=F32)


def _split2(x):
    hi = x.astype(BF16)
    lo = (x - hi.astype(F32)).astype(BF16)
    return hi, lo


def _dot_x3(a, b, dot=_dot):
    ah, al = _split2(a)
    bh, bl = _split2(b)
    return dot(ah, bh) + dot(ah, bl) + dot(al, bh)


def _sigmoid(x):
    return 1.0 / (1.0 + jnp.exp(-x))


def _rms(x):
    return x * lax.rsqrt(jnp.mean(x * x, axis=-1, keepdims=True) + RMS_EPS)


def _modulate(n, gain, shift, scale, nb):
    rows = n.shape[0]
    h = (n * gain).reshape(nb, rows // nb, D_MODEL)
    return (h * (1.0 + scale) + shift).reshape(rows, D_MODEL)


def _const_spec(shape):
    nd = len(shape)
    return pl.BlockSpec(shape, lambda *_: (0,) * nd)


def _cond_kernel(c_ref, w_ref, b_ref, o_ref):
    c = c_ref[...]
    cs = (c * _sigmoid(c)).astype(BF16)
    o_ref[...] = _dot(cs, w_ref[...].astype(BF16)) + b_ref[...]


def _cond_linear(c, w, b, tn=512):
    nl, _, n = w.shape
    m = c.shape[0]
    return pl.pallas_call(
        _cond_kernel,
        grid=(nl, n // tn),
        in_specs=[
            pl.BlockSpec((m, D_MODEL), lambda l, j: (0, 0)),
            pl.BlockSpec((None, D_MODEL, tn), lambda l, j: (l, 0, j)),
            pl.BlockSpec((None, 1, tn), lambda l, j: (l, 0, j)),
        ],
        out_specs=pl.BlockSpec((None, m, tn), lambda l, j: (l, 0, j)),
        out_shape=jax.ShapeDtypeStruct((nl, m, n), F32),
        compiler_params=_params(("parallel", "parallel")),
        name="cond_linear",
    )(c, w, b)


def _rwkv_proj_kernel(x_ref, mod_ref, prev_ref, ng_ref, mu_ref, wrkv_ref, w0_ref, w1_ref, w2_ref,
                      a0_ref, a1_ref, a2_ref, g1_ref, g2_ref, kk_ref, ka_ref,
                      r_ref, lw_ref, k_ref, v_ref, a_ref, b_ref, g_ref, last_ref, carry_ref):
    tm = x_ref.shape[0]

    @pl.when(pl.program_id(1) == 0)
    def _():
        carry_ref[...] = prev_ref[...]

    shift = mod_ref[:, 0:D_MODEL].reshape(1, 1, D_MODEL)
    scale = mod_ref[:, D_MODEL:2 * D_MODEL].reshape(1, 1, D_MODEL)
    h = _modulate(_rms(x_ref[...]), ng_ref[...], shift, scale, 1)
    row = lax.broadcasted_iota(jnp.int32, (tm, 1), 0)
    shifted = jnp.where(row == 0, carry_ref[...], pltpu.roll(h, 1, axis=0))
    carry_ref[...] = h[tm - 1:tm, :]
    last_ref[...] = h[tm - 1:tm, :]
    dx = shifted - h

    def mix(n):
        return (h + dx * mu_ref[n:n + 1, :]).astype(BF16)

    r = _dot(mix(0), wrkv_ref[0])
    k = _dot(mix(1), wrkv_ref[1])
    v = _dot(mix(2), wrkv_ref[2])
    ww = _dot(jnp.tanh(_dot(mix(3), w1_ref[...])).astype(BF16), w2_ref[...])
    z = -(w0_ref[...] + ww)
    softplus = jnp.maximum(z, 0.0) + jnp.log(1.0 + jnp.exp(-jnp.abs(z)))
    lw = -jnp.exp(-softplus - 0.5)
    asig = _sigmoid(a0_ref[...] + _dot(_dot(mix(4), a1_ref[...]).astype(BF16), a2_ref[...]))
    g = _dot(_sigmoid(_dot(mix(5), g1_ref[...])).astype(BF16), g2_ref[...])

    kk = k * kk_ref[...]
    kk2 = kk * kk
    lo_cols = lax.broadcasted_iota(jnp.int32, (tm, LANES), 1) < HEAD

    def head_sums(t):
        lo_sum = jnp.sum(jnp.where(lo_cols, t, 0.0), axis=-1, keepdims=True)
        hi_sum = jnp.sum(jnp.where(lo_cols, 0.0, t), axis=-1, keepdims=True)
        return jnp.where(lo_cols, lo_sum, hi_sum)

    ss = jnp.concatenate([head_sums(kk2[:, p * LANES:(p + 1) * LANES]) for p in range(PAIRS)], axis=1)
    kkn = kk * lax.rsqrt(jnp.maximum(ss, 1e-24))

    r_ref[...] = r.astype(r_ref.dtype)
    lw_ref[...] = lw
    k_ref[...] = (k * (1.0 + (asig - 1.0) * ka_ref[...])).astype(k_ref.dtype)
    v_ref[...] = v.astype(v_ref.dtype)
    a_ref[...] = (-kkn).astype(a_ref.dtype)
    b_ref[...] = (kkn * asig).astype(b_ref.dtype)
    g_ref[...] = g.astype(g_ref.dtype)


def _rwkv_proj(x, mod, prev_x, w, nb, s, tm):
    n = nb * s
    nt = s // tm
    row_spec = pl.BlockSpec((tm, D_MODEL), lambda b, j: (b * nt + j, 0))
    vec = _const_spec((1, D_MODEL))
    lora_w, lora_g = w["w1"].shape[1], w["g1"].shape[1]
    in_specs = [
        row_spec,
        pl.BlockSpec((None, 1, 6 * D_MODEL), lambda b, j: (b, 0, 0)),
        pl.BlockSpec((None, 1, D_MODEL), lambda b, j: (b, 0, 0)),
        vec, _const_spec((6, D_MODEL)), _const_spec((3, D_MODEL, D_MODEL)),
        vec, _const_spec((D_MODEL, lora_w)), _const_spec((lora_w, D_MODEL)),
        vec, _const_spec((D_MODEL, lora_w)), _const_spec((lora_w, D_MODEL)),
        _const_spec((D_MODEL, lora_g)), _const_spec((lora_g, D_MODEL)),
        vec, vec,
    ]
    act = lambda dt: jax.ShapeDtypeStruct((n, D_MODEL), dt)
    out_types = [act(BF16), act(F32)] + [act(BF16)] * 5
    outs = pl.pallas_call(
        _rwkv_proj_kernel,
        grid=(nb, nt),
        in_specs=in_specs,
        out_specs=[row_spec] * 7 + [pl.BlockSpec((None, 1, D_MODEL), lambda b, j: (b, 0, 0))],
        out_shape=out_types + [jax.ShapeDtypeStruct((nb, 1, D_MODEL), F32)],
        scratch_shapes=[pltpu.VMEM((1, D_MODEL), F32)],
        compiler_params=_params(("parallel", "arbitrary")),
        name="rwkv_proj",
    )(x, mod, prev_x, w["norm_g"], w["mu"], w["w_rkv"], w["w0"], w["w1"], w["w2"],
      w["a0"], w["a1"], w["a2"], w["g1"], w["g2"], w["k_k"], w["k_a"])
    return outs


def _split3(x):
    hi = x.astype(BF16)
    rem = x - hi.astype(F32)
    mid = rem.astype(BF16)
    return hi, mid, (rem - mid.astype(F32)).astype(BF16)


def _wkv_kernel(r_ref, lw_ref, k_ref, v_ref, a_ref, b_ref, g_ref, s0_ref, rk_ref, lnw_ref, lnb_ref,
                z_ref, st_ref):
    @pl.when(pl.program_id(1) == 0)
    def _():
        st_ref[...] = s0_ref[...]

    nseq = r_ref.shape[0]
    wide = lambda ref: jnp.concatenate([ref[i] for i in range(nseq)], axis=1)
    per_seq = lambda ref: jnp.concatenate([ref[...]] * nseq, axis=1)
    b16 = lambda t: t.astype(BF16)
    pairs = range(nseq * PAIRS)
    blk = lambda t, p: t[:, p * LANES:(p + 1) * LANES]
    rows = lambda ts: jnp.concatenate(ts, axis=0)
    lo_cols = lax.broadcasted_iota(jnp.int32, (CHUNK, LANES), 1) < HEAD
    rr = lax.broadcasted_iota(jnp.int32, (LANES, LANES), 0)
    cc = lax.broadcasted_iota(jnp.int32, (LANES, LANES), 1)
    head0 = lambda t: jnp.where(lo_cols, t, 0.0)
    head1 = lambda t: jnp.where(lo_cols, 0.0, t)
    blockdiag = lambda q: rows([head0(q), head1(q)])

    r, lw, k, v, a, b = (wide(t).astype(F32) for t in (r_ref, lw_ref, k_ref, v_ref, a_ref, b_ref))
    ti = lax.broadcasted_iota(jnp.int32, (CHUNK, CHUNK), 0)
    tj = lax.broadcasted_iota(jnp.int32, (CHUNK, CHUNK), 1)
    tri = (tj <= ti).astype(BF16)
    cum = sum(_dot(tri, t) for t in _split3(lw))
    cl = cum[CHUNK - 1:CHUNK, :]
    e_neg = jnp.exp(-cum)
    e_end = jnp.exp(cl - cum)
    e_cl = jnp.exp(cl)
    at = b16(a * jnp.exp(cum - lw))
    rt = b16(r * jnp.exp(cum))
    bt = b * e_neg
    kt = k * e_neg
    bkh = rows([b16(b * e_end), b16(k * e_end)])
    v16 = b16(v)

    t_idx = rr & (CHUNK - 1)
    j_idx = cc & (CHUNK - 1)
    keep = (j_idx < t_idx) | ((rr >= CHUNK) & (j_idx == t_idx))
    g0, g1 = [], []
    for p in pairs:
        btp, ktp = blk(bt, p), blk(kt, p)
        w = b16(rows([head0(btp), head0(ktp), head1(ktp), head1(btp)]))
        g = _dot_nt(rows([blk(at, p), blk(rt, p)]), w)
        g0.append(jnp.where(keep, g[:, :LANES], 0.0))
        g1.append(jnp.where(keep, g[:, LANES:], 0.0))

    eye2 = jnp.where((lax.broadcasted_iota(jnp.int32, (CHUNK, LANES), 1) & (CHUNK - 1))
                     == lax.broadcasted_iota(jnp.int32, (CHUNK, LANES), 0), 1.0, 0.0)
    pc = [jnp.where(lo_cols, g0[p][:CHUNK], g1[p][:CHUNK]) for p in pairs]
    tc = [eye2 + pc[p] for p in pairs]
    q = [_dot(b16(pc[p]), b16(blockdiag(pc[p]))) for p in pairs]
    for _ in range(4):
        res = [_dot(b16(rows([tc[p], q[p]])), b16(blockdiag(q[p]))) for p in pairs]
        tc = [tc[p] + res[p][:CHUNK] for p in pairs]
        q = [res[p][CHUNK:] for p in pairs]
    tc = [b16(tc[p] + _dot(b16(tc[p]), b16(blockdiag(q[p])))) for p in pairs]

    akv = []
    for p in pairs:
        w_ak = b16(jnp.where(lo_cols, g1[p][:CHUNK], g0[p][:CHUNK]))
        vp = blk(v16, p)
        akv.append(_dot(w_ak, rows([head1(vp), head0(vp)])))

    s = [st_ref[p // PAIRS, p % PAIRS] for p in pairs]
    s16 = [b16(s[p]) for p in pairs]
    rhs = [_dot_nt(blk(at, p), s16[p]) + akv[p] for p in pairs]
    u = [_dot(tc[p], b16(rows([head0(rhs[p]), head1(rhs[p])]))) for p in pairs]
    u16 = [b16(u[p]) for p in pairs]

    ys = []
    bd_mask = (rr < HEAD) == (cc < HEAD)
    for p in pairs:
        vp = blk(v16, p)
        rbk = b16(jnp.concatenate([g0[p][CHUNK:], g1[p][CHUNK:]], axis=1))
        uv = rows([head0(u16[p]), head0(vp), head1(vp), head1(u16[p])])
        ys.append(_dot_nt(blk(rt, p), s16[p]) + _dot(rbk, uv))
        fresh = _dot_tn(rows([u16[p], vp]), blk(bkh, p))
        st_ref[p // PAIRS, p % PAIRS] = s[p] * blk(e_cl, p) + jnp.where(bd_mask, fresh, 0.0)

    def head_sums(t):
        lo_sum = jnp.sum(head0(t), axis=-1, keepdims=True)
        hi_sum = jnp.sum(head1(t), axis=-1, keepdims=True)
        return jnp.where(lo_cols, lo_sum, hi_sum)

    rkk = r * k * per_seq(rk_ref)
    for p in pairs:
        seq, cols = p // PAIRS, slice((p % PAIRS) * LANES, (p % PAIRS + 1) * LANES)
        yc = ys[p] - head_sums(ys[p]) * (1.0 / HEAD)
        yn = yc * lax.rsqrt(head_sums(yc * yc) * (1.0 / HEAD) + GN_EPS)
        zp = (yn * lnw_ref[:, cols] + lnb_ref[:, cols] + head_sums(blk(rkk, p)) * blk(v, p)) * g_ref[seq, :, cols].astype(F32)
        z_ref[seq, :, cols] = zp.astype(z_ref.dtype)


def _wkv(r, lw, k, v, a, b, g, st0, w, nb, s):
    nseq = math.gcd(nb, WKV_SEQS)
    blk = pl.BlockSpec((nseq, CHUNK, D_MODEL), lambda bi, c: (bi, c, 0))
    vec = _const_spec((1, D_MODEL))
    st_spec = pl.BlockSpec((nseq, PAIRS, LANES, LANES), lambda bi, c: (bi, 0, 0, 0))
    return pl.pallas_call(
        _wkv_kernel,
        grid=(nb // nseq, s // CHUNK),
        in_specs=[blk] * 7 + [st_spec, vec, vec, vec],
        out_specs=[blk, st_spec],
        out_shape=[jax.ShapeDtypeStruct((nb, s, D_MODEL), BF16),
                   jax.ShapeDtypeStruct((nb, PAIRS, LANES, LANES), F32)],
        compiler_params=_params(("parallel", "arbitrary")),
        name="wkv",
    )(r, lw, k, v, a, b, g, st0, w["r_k"], w["lnx_w"], w["lnx_b"])


def _out_proj_kernel(x_ref, z_ref, gate_ref, w_ref, bias_ref, o_ref, *, nb):
    rows = x_ref.shape[0]
    y = _dot(z_ref[...], w_ref[...]) + bias_ref[...]
    y = (y.reshape(nb, rows // nb, D_MODEL) * gate_ref[...]).reshape(rows, D_MODEL)
    o_ref[...] = x_ref[...] + y


def _out_proj(x, z, gate, w, bias, nb, s, tm):
    n = nb * s
    tb = max(tm // s, 1)
    per_seq = max(s // tm, 1)
    row_spec = pl.BlockSpec((tm, D_MODEL), lambda i: (i, 0))
    return pl.pallas_call(
        functools.partial(_out_proj_kernel, nb=tb),
        grid=(n // tm,),
        in_specs=[row_spec, row_spec,
                  pl.BlockSpec((tb, 1, D_MODEL), lambda i: (i // per_seq, 0, 0)),
                  _const_spec((D_MODEL, D_MODEL)), _const_spec((1, D_MODEL))],
        out_specs=row_spec,
        out_shape=jax.ShapeDtypeStruct((n, D_MODEL), F32),
        compiler_params=_params(("parallel",)),
        name="out_proj",
    )(x, z, gate, w, bias)


def _top2(logits):
    lane = lax.broadcasted_iota(jnp.int32, logits.shape, 1).astype(F32)
    first_of = lambda hit: jnp.min(jnp.where(hit, lane, float(LANES)), axis=-1, keepdims=True)
    lg = jnp.where(lane < N_GROUPS, logits, -jnp.inf)
    gmax = jnp.max(lg, axis=-1, keepdims=True)
    gi = first_of(lg == gmax)
    gp = 1.0 / jnp.sum(jnp.exp(lg - gmax), axis=-1, keepdims=True)
    first = ROUTE_LANE0 + gi * EXP_PER_GROUP
    le = jnp.where((lane >= first) & (lane < first + EXP_PER_GROUP), logits, -jnp.inf)
    top1 = jnp.max(le, axis=-1, keepdims=True)
    i1 = first_of(le == top1)
    le2 = jnp.where(lane == i1, -jnp.inf, le)
    top2 = jnp.max(le2, axis=-1, keepdims=True)
    i2 = first_of(le2 == top2)
    e2 = jnp.exp(top2 - top1)
    w1 = gp / (1.0 + e2)
    return i1, i2, w1, w1 * e2


def _route(logits):
    lane = lax.broadcasted_iota(jnp.int32, logits.shape, 1).astype(F32)
    i1, i2, w1, w2 = _top2(logits)
    return jnp.where(lane == i1, w1, 0.0) + jnp.where(lane == i2, w2, 0.0)


def _moe_kernel(x_ref, mod_ref, ng_ref, wr_ref, wgu_ref, wdn_ref, fg_ref, o_ref, h_scr, cw_scr, acc_scr,
                *, nb, final_norm):
    e = pl.program_id(1)
    rows = x_ref.shape[0]

    @pl.when(e == 0)
    def _():
        shift = mod_ref[:, :, 3 * D_MODEL:4 * D_MODEL]
        scale = mod_ref[:, :, 4 * D_MODEL:5 * D_MODEL]
        h = _modulate(_rms(x_ref[...]), ng_ref[...], shift, scale, nb)
        h_scr[...] = h.astype(BF16)
        cw_scr[...] = _route(_dot_x3(h, wr_ref[...]))
        acc_scr[...] = jnp.zeros_like(acc_scr)

    lane = lax.broadcasted_iota(jnp.int32, (rows, LANES), 1)
    cw = jnp.sum(jnp.where(lane == ROUTE_LANE0 + e, cw_scr[...], 0.0), axis=-1, keepdims=True)
    gu = _dot(h_scr[...], wgu_ref[...].astype(BF16))
    gate = gu[:, :D_EXPERT]
    hid = gate * _sigmoid(gate) * gu[:, D_EXPERT:] * cw
    acc_scr[...] += _dot(hid.astype(BF16), wdn_ref[...].astype(BF16))

    @pl.when(e == N_EXPERTS - 1)
    def _():
        g2 = mod_ref[:, :, 5 * D_MODEL:6 * D_MODEL]
        y = (acc_scr[...].reshape(nb, rows // nb, D_MODEL) * g2).reshape(rows, D_MODEL)
        out = x_ref[...] + y
        if final_norm:
            out = _rms(out) * fg_ref[...]
        o_ref[...] = out


def _moe(x, mod, norm_g, w_route, w_gu, w_down, final_g, nb, s, layer, tm, final_norm):
    n = nb * s
    tb = max(tm // s, 1)
    per_seq = max(s // tm, 1)
    row_spec = pl.BlockSpec((tm, D_MODEL), lambda i, e: (i, 0))
    return pl.pallas_call(
        functools.partial(_moe_kernel, nb=tb, final_norm=final_norm),
        grid=(n // tm, N_EXPERTS),
        in_specs=[row_spec,
                  pl.BlockSpec((tb, 1, 6 * D_MODEL), lambda i, e: (i // per_seq, 0, 0)),
                  _const_spec((1, D_MODEL)), _const_spec((D_MODEL, LANES)),
                  pl.BlockSpec((None, None, D_MODEL, 2 * D_EXPERT), lambda i, e: (layer, e, 0, 0)),
                  pl.BlockSpec((None, None, D_EXPERT, D_MODEL), lambda i, e: (layer, e, 0, 0)),
                  _const_spec((1, D_MODEL))],
        out_specs=row_spec,
        out_shape=jax.ShapeDtypeStruct((n, D_MODEL), F32),
        scratch_shapes=[pltpu.VMEM((tm, D_MODEL), BF16), pltpu.VMEM((tm, LANES), F32),
                        pltpu.VMEM((tm, D_MODEL), F32)],
        compiler_params=_params(("parallel", "arbitrary")),
        name="moe",
    )(x, mod, norm_g, w_route, w_gu, w_down, final_g)


PAIRS_PER_GROUP = EXP_PER_GROUP * (EXP_PER_GROUP - 1) // 2
N_BUCKETS = N_GROUPS * PAIRS_PER_GROUP
BUCKET_TILE = 256
ROW_W = D_MODEL + LANES
META_BUCKET, META_RANK, META_W_LO, META_W_HI = 0, 1, 2, 3
ROUTER_TILE = 1024
MOVE_TILE = 1024
DMA_UNROLL = 8
SPARSE_MIN_TOKENS = 4096


def _moe_norm(x, mod_ref, ng_ref, nb):
    shift = mod_ref[:, :, 3 * D_MODEL:4 * D_MODEL]
    scale = mod_ref[:, :, 4 * D_MODEL:5 * D_MODEL]
    return _modulate(_rms(x), ng_ref[...], shift, scale, nb)


def _router_kernel(x_ref, mod_ref, ng_ref, wr_ref, earlier_ref, meta_ref, cnt_ref, *, nb):
    @pl.when(pl.program_id(0) == 0)
    def _():
        cnt_ref[...] = jnp.zeros_like(cnt_ref)

    rows = x_ref.shape[0]
    h = _moe_norm(x_ref[...], mod_ref, ng_ref, nb)
    i1, i2, w1, w2 = _top2(_dot_x3(h, wr_ref[...]))
    lo = (jnp.minimum(i1, i2) - ROUTE_LANE0).astype(jnp.int32)
    hi = (jnp.maximum(i1, i2) - ROUTE_LANE0).astype(jnp.int32)
    first_is_lo = i1 < i2
    a = lo & (EXP_PER_GROUP - 1)
    b = hi & (EXP_PER_GROUP - 1)
    group = lo >> (EXP_PER_GROUP.bit_length() - 1)
    bucket = group * PAIRS_PER_GROUP + ((a * (2 * EXP_PER_GROUP - 1 - a)) >> 1) + (b - a - 1)

    lane = lax.broadcasted_iota(jnp.int32, (rows, LANES), 1)
    mine = lane == bucket
    onehot = jnp.where(mine, 1.0, 0.0)
    seen = cnt_ref[...]
    before = _dot(earlier_ref[...], onehot.astype(BF16)) + seen
    rank = jnp.sum(jnp.where(mine, before, 0.0), axis=-1, keepdims=True)
    cnt_ref[...] = seen + jnp.sum(onehot, axis=0, keepdims=True)

    meta = jnp.where(lane == META_BUCKET, bucket.astype(F32), 0.0)
    meta = jnp.where(lane == META_RANK, rank, meta)
    meta = jnp.where(lane == META_W_LO, jnp.where(first_is_lo, w1, w2), meta)
    meta_ref[...] = jnp.where(lane == META_W_HI, jnp.where(first_is_lo, w2, w1), meta)


def _positions_kernel(meta_ref, start_ref, pos_ref):
    meta = meta_ref[...]
    lane = lax.broadcasted_iota(jnp.int32, meta.shape, 1)
    column = lambda c: jnp.sum(jnp.where(lane == c, meta, 0.0), axis=-1, keepdims=True)
    value = jnp.where(lane == column(META_BUCKET).astype(jnp.int32), start_ref[...] + column(META_RANK), 0.0)
    ones = jnp.ones((8, LANES), BF16)
    pos = sum(_dot_nt(ones, piece) for piece in _split3(value))
    pos_ref[...] = pos[0:1, :].astype(jnp.int32)


def _dispatch_kernel(pos_ref, x_ref, mod_ref, ng_ref, meta_ref, xs_ref, buf, sem, *, nb):
    i = pl.program_id(0)
    rows = x_ref.shape[0]
    slot = i % 2
    buf[slot, :, 0:D_MODEL] = _moe_norm(x_ref[...], mod_ref, ng_ref, nb)
    buf[slot, :, D_MODEL:ROW_W] = meta_ref[...]

    def send(r, carry):
        pltpu.make_async_copy(buf.at[slot, pl.ds(r, 1), :],
                              xs_ref.at[pl.ds(pos_ref[i * rows + r], 1), :], sem.at[slot]).start()
        return carry

    lax.fori_loop(0, rows, send, 0, unroll=DMA_UNROLL)

    def wait_all(sl):
        pltpu.make_async_copy(buf.at[sl], xs_ref.at[pl.ds(0, rows), :], sem.at[sl]).wait()

    @pl.when(i >= 1)
    def _():
        wait_all(1 - slot)

    @pl.when(i == pl.num_programs(0) - 1)
    def _():
        wait_all(slot)


def _experts_kernel(tile_ref, bucket_ref, lo_ref, hi_ref, used_ref, xs_ref, gu_lo_ref, gu_hi_ref,
                    dn_lo_ref, dn_hi_ref, ys_ref):
    k = pl.program_id(0)

    @pl.when(k < used_ref[0])
    def _():
        rows = xs_ref.shape[0]
        x16 = xs_ref[:, 0:D_MODEL].astype(BF16)
        meta = xs_ref[:, D_MODEL:ROW_W]
        lane = lax.broadcasted_iota(jnp.int32, (rows, LANES), 1)
        column = lambda c: jnp.sum(jnp.where(lane == c, meta, 0.0), axis=-1, keepdims=True)
        member = column(META_BUCKET) == bucket_ref[k].astype(F32)

        def expert(gu_ref, dn_ref, col):
            gu = _dot(x16, gu_ref[...].astype(BF16))
            gate = gu[:, :D_EXPERT]
            hid = gate * _sigmoid(gate) * gu[:, D_EXPERT:] * jnp.where(member, column(col), 0.0)
            return _dot(hid.astype(BF16), dn_ref[...].astype(BF16))

        y = expert(gu_lo_ref, dn_lo_ref, META_W_LO) + expert(gu_hi_ref, dn_hi_ref, META_W_HI)
        new_tile = (k == 0) | (tile_ref[k] != tile_ref[jnp.maximum(k - 1, 0)])

        @pl.when(new_tile)
        def _():
            ys_ref[...] = y

        @pl.when(jnp.logical_not(new_tile))
        def _():
            ys_ref[...] += y


def _combine_kernel(pos_ref, x_ref, mod_ref, fg_ref, ys_ref, o_ref, buf, sem, *, nb, final_norm):
    i = pl.program_id(0)
    rows = x_ref.shape[0]
    slot = i % 2

    def fetch(step, sl):
        def one(r, carry):
            pltpu.make_async_copy(ys_ref.at[pl.ds(pos_ref[step * rows + r], 1), :],
                                  buf.at[sl, pl.ds(r, 1), :], sem.at[sl]).start()
            return carry
        lax.fori_loop(0, rows, one, 0, unroll=DMA_UNROLL)

    @pl.when(i == 0)
    def _():
        fetch(0, 0)

    pltpu.make_async_copy(ys_ref.at[pl.ds(0, rows), :], buf.at[slot], sem.at[slot]).wait()

    @pl.when(i + 1 < pl.num_programs(0))
    def _():
        fetch(i + 1, 1 - slot)

    g2 = mod_ref[:, :, 5 * D_MODEL:6 * D_MODEL]
    y = (buf[slot].reshape(nb, rows // nb, D_MODEL) * g2).reshape(rows, D_MODEL)
    out = x_ref[...] + y
    if final_norm:
        out = _rms(out) * fg_ref[...]
    o_ref[...] = out


def _bucket_experts():
    lo, hi = [], []
    for g in range(N_GROUPS):
        for a in range(EXP_PER_GROUP):
            for b in range(a + 1, EXP_PER_GROUP):
                lo.append(g * EXP_PER_GROUP + a)
                hi.append(g * EXP_PER_GROUP + b)
    return jnp.asarray(lo, jnp.int32), jnp.asarray(hi, jnp.int32)


def _moe_sparse(x, mod, norm_g, w_route, w_gu, w_down, final_g, nb, s, layer, final_norm):
    n = nb * s
    move_params = pltpu.CompilerParams(dimension_semantics=("arbitrary",), vmem_limit_bytes=VMEM_LIMIT,
                                       disable_bounds_checks=True)

    def token_specs(tm):
        tb, per_seq = max(tm // s, 1), max(s // tm, 1)
        row = pl.BlockSpec((tm, D_MODEL), lambda i, *_: (i, 0))
        modspec = pl.BlockSpec((tb, 1, 6 * D_MODEL), lambda i, *_: (i // per_seq, 0, 0))
        meta = pl.BlockSpec((tm, LANES), lambda i, *_: (i, 0))
        return tb, row, modspec, meta

    vec = pl.BlockSpec((1, D_MODEL), lambda *_: (0, 0))
    tb, row, modspec, metaspec = token_specs(ROUTER_TILE)
    meta, counts = pl.pallas_call(
        functools.partial(_router_kernel, nb=tb),
        grid=(n // ROUTER_TILE,),
        in_specs=[row, modspec, vec, pl.BlockSpec((D_MODEL, LANES), lambda i: (0, 0)),
                  pl.BlockSpec((ROUTER_TILE, ROUTER_TILE), lambda i: (0, 0))],
        out_specs=[metaspec, pl.BlockSpec((1, LANES), lambda i: (0, 0))],
        out_shape=[jax.ShapeDtypeStruct((n, LANES), F32), jax.ShapeDtypeStruct((1, LANES), F32)],
        compiler_params=_params(("arbitrary",)),
        name="moe_router",
    )(x, mod, norm_g, w_route, jnp.tril(jnp.ones((ROUTER_TILE, ROUTER_TILE), BF16), -1))

    counts = counts[0, :N_BUCKETS].astype(jnp.int32)
    end = jnp.cumsum(counts)
    start = end - counts
    first_tile = start // BUCKET_TILE
    per_bucket = jnp.where(counts > 0, (end - 1) // BUCKET_TILE - first_tile + 1, 0)
    item_end = jnp.cumsum(per_bucket)
    used = item_end[-1:]
    n_items = n // BUCKET_TILE + N_BUCKETS
    k = jnp.minimum(jnp.arange(n_items, dtype=jnp.int32), used - 1)
    item_bucket = jnp.sum(item_end[None, :] <= k[:, None], axis=1, dtype=jnp.int32)
    item_tile = first_tile[item_bucket] + k - (item_end - per_bucket)[item_bucket]
    bucket_lo, bucket_hi = _bucket_experts()
    item_lo, item_hi = bucket_lo[item_bucket], bucket_hi[item_bucket]
    start_row = jnp.pad(start, (0, LANES - N_BUCKETS)).astype(F32)[None, :]
    pos = pl.pallas_call(
        _positions_kernel,
        grid=(n // ROUTER_TILE,),
        in_specs=[metaspec, pl.BlockSpec((1, LANES), lambda i: (0, 0))],
        out_specs=pl.BlockSpec((1, ROUTER_TILE), lambda i: (0, i)),
        out_shape=jax.ShapeDtypeStruct((1, n), jnp.int32),
        compiler_params=_params(("parallel",)),
        name="moe_positions",
    )(meta, start_row).reshape(n)

    tb, row, modspec, metaspec = token_specs(MOVE_TILE)
    any_spec = pl.BlockSpec(memory_space=pl.ANY)
    xs = pl.pallas_call(
        functools.partial(_dispatch_kernel, nb=tb),
        grid_spec=pltpu.PrefetchScalarGridSpec(
            num_scalar_prefetch=1, grid=(n // MOVE_TILE,),
            in_specs=[row, modspec, vec, metaspec], out_specs=any_spec,
            scratch_shapes=[pltpu.VMEM((2, MOVE_TILE, ROW_W), F32), pltpu.SemaphoreType.DMA((2,))]),
        out_shape=jax.ShapeDtypeStruct((n, ROW_W), F32),
        compiler_params=move_params,
        name="moe_dispatch",
    )(pos, x, mod, norm_g, meta)

    tile_of = lambda k, tile, *_: (tile[k], 0)
    ys = pl.pallas_call(
        _experts_kernel,
        grid_spec=pltpu.PrefetchScalarGridSpec(
            num_scalar_prefetch=5, grid=(n_items,),
            in_specs=[pl.BlockSpec((BUCKET_TILE, ROW_W), tile_of),
                      pl.BlockSpec((None, None, D_MODEL, 2 * D_EXPERT), lambda k, t, b, lo, hi, nu: (layer, lo[k], 0, 0)),
                      pl.BlockSpec((None, None, D_MODEL, 2 * D_EXPERT), lambda k, t, b, lo, hi, nu: (layer, hi[k], 0, 0)),
                      pl.BlockSpec((None, None, D_EXPERT, D_MODEL), lambda k, t, b, lo, hi, nu: (layer, lo[k], 0, 0)),
                      pl.BlockSpec((None, None, D_EXPERT, D_MODEL), lambda k, t, b, lo, hi, nu: (layer, hi[k], 0, 0))],
            out_specs=pl.BlockSpec((BUCKET_TILE, D_MODEL), tile_of)),
        out_shape=jax.ShapeDtypeStruct((n, D_MODEL), F32),
        compiler_params=_params(("arbitrary",)),
        name="moe_experts",
    )(item_tile, item_bucket, item_lo, item_hi, used, xs, w_gu, w_gu, w_down, w_down)

    return pl.pallas_call(
        functools.partial(_combine_kernel, nb=tb, final_norm=final_norm),
        grid_spec=pltpu.PrefetchScalarGridSpec(
            num_scalar_prefetch=1, grid=(n // MOVE_TILE,),
            in_specs=[row, modspec, vec, any_spec], out_specs=row,
            scratch_shapes=[pltpu.VMEM((2, MOVE_TILE, D_MODEL), F32), pltpu.SemaphoreType.DMA((2,))]),
        out_shape=jax.ShapeDtypeStruct((n, D_MODEL), F32),
        compiler_params=move_params,
        name="moe_combine",
    )(pos, x, mod, final_g, ys)


def _rope(x, cos, sin_lo, sin_hi):
    return x * cos + pltpu.roll(x, LANES - ROPE_DIM // 2, axis=1) * sin_lo + pltpu.roll(x, ROPE_DIM // 2, axis=1) * sin_hi


def _qkv_kernel(x_ref, mod_ref, kvmod_ref, ngq_ref, ngkv_ref, wq_ref, bq_ref, wkv_ref, bkv_ref,
                cos_ref, slo_ref, shi_ref, q_ref, k_ref, v_ref):
    n = _rms(x_ref[...])
    shift = mod_ref[:, 0:D_MODEL].reshape(1, 1, D_MODEL)
    scale = mod_ref[:, D_MODEL:2 * D_MODEL].reshape(1, 1, D_MODEL)
    hq = _modulate(n, ngq_ref[...], shift, scale, 1)
    kshift = kvmod_ref[:, 0:D_MODEL].reshape(1, 1, D_MODEL)
    kscale = kvmod_ref[:, D_MODEL:2 * D_MODEL].reshape(1, 1, D_MODEL)
    hkv = _modulate(n, ngkv_ref[...], kshift, kscale, 1)
    cos, slo, shi = cos_ref[...], slo_ref[...], shi_ref[...]
    q = _dot(hq.astype(BF16), wq_ref[...]) + bq_ref[...]
    for p in range(PAIRS):
        cols = slice(p * LANES, (p + 1) * LANES)
        q_ref[:, cols] = (_rope(q[:, cols], cos, slo, shi) * ATT_SCALE).astype(q_ref.dtype)
    kv = _dot(hkv.astype(BF16), wkv_ref[...]) + bkv_ref[...]
    for p in range(KV_W // LANES):
        cols = slice(p * LANES, (p + 1) * LANES)
        k_ref[:, cols] = _rope(kv[:, cols], cos, slo, shi)
    v_ref[...] = kv[:, KV_W:]


def _qkv_proj(x, mod, kvmod, w, tables, nb, s, tm):
    n = nb * s
    nt = s // tm
    row_spec = pl.BlockSpec((tm, D_MODEL), lambda b, j: (b * nt + j, 0))
    kv_spec = pl.BlockSpec((tm, KV_W), lambda b, j: (b * nt + j, 0))
    tab_spec = pl.BlockSpec((tm, LANES), lambda b, j: (j, 0))
    vec = _const_spec((1, D_MODEL))
    return pl.pallas_call(
        _qkv_kernel,
        grid=(nb, nt),
        in_specs=[row_spec,
                  pl.BlockSpec((None, 1, 6 * D_MODEL), lambda b, j: (b, 0, 0)),
                  pl.BlockSpec((None, 1, 2 * D_MODEL), lambda b, j: (b, 0, 0)),
                  vec, vec, _const_spec((D_MODEL, D_MODEL)), vec,
                  _const_spec((D_MODEL, 2 * KV_W)), _const_spec((1, 2 * KV_W)),
                  tab_spec, tab_spec, tab_spec],
        out_specs=[row_spec, kv_spec, kv_spec],
        out_shape=[jax.ShapeDtypeStruct((n, D_MODEL), BF16),
                   jax.ShapeDtypeStruct((n, KV_W), F32), jax.ShapeDtypeStruct((n, KV_W), F32)],
        compiler_params=_params(("parallel", "parallel")),
        name="qkv_proj",
    )(x, mod, kvmod, w["norm_gq"], w["norm_gkv"], w["w_q"], w["b_q"], w["w_kv"], w["b_kv"], *tables)


def _attn_kernel(sink_ref, q_ref, k0_ref, k1_ref, k2_ref, v0_ref, v1_ref, v2_ref, o_ref, *, banded):
    n = pl.program_id(1)
    seqs = range(q_ref.shape[0])
    k_all = [jnp.concatenate([k0_ref[i], k1_ref[i], k2_ref[i]], axis=0) for i in seqs]
    v_all = [jnp.concatenate([v0_ref[i], v1_ref[i], v2_ref[i]], axis=0) for i in seqs]
    lane = lax.broadcasted_iota(jnp.int32, (1, LANES), 1)
    lo = lane < HEAD
    col = lax.broadcasted_iota(jnp.int32, (1, 3 * CHUNK), 1)
    if banded:
        visible = (col >= 2 * CHUNK) | ((col >= CHUNK) & (n >= 1)) | (n >= 2)

    def halves(t, kv_head):
        blk = t[:, (kv_head // 2) * LANES:(kv_head // 2 + 1) * LANES]
        if kv_head % 2 == 0:
            t_lo = jnp.where(lo, blk, 0.0)
            t_hi = pltpu.roll(t_lo, HEAD, axis=1)
        else:
            t_hi = jnp.where(lo, 0.0, blk)
            t_lo = pltpu.roll(t_hi, HEAD, axis=1)
        return t_lo.astype(BF16), t_hi.astype(BF16)

    first_pair = lax.broadcasted_iota(jnp.int32, (2 * CHUNK, 1), 0) < CHUNK

    def probs(s, head_a, head_b):
        if banded:
            s = jnp.where(visible, s, NEG_INF)
        sink = jnp.where(first_pair, sink_ref[head_a], sink_ref[head_b])
        m = jnp.maximum(jnp.max(s, axis=-1, keepdims=True), sink)
        e = jnp.exp(s - m)
        return (e / (jnp.sum(e, axis=-1, keepdims=True) + jnp.exp(sink - m))).astype(BF16)

    units = [(i, g) for i in seqs for g in range(N_KV)]
    scores, values = [], []
    for i, g in units:
        k_lo, k_hi = halves(k_all[i], g)
        values.append(halves(v_all[i], g))
        q2 = jnp.concatenate([q_ref[i, :, (2 * g + j) * LANES:(2 * g + j + 1) * LANES] for j in range(2)], axis=0)
        scores.append((_dot_nt(q2, k_lo), _dot_nt(q2, k_hi)))
    for (i, g), (s_lo, s_hi), (v_lo, v_hi) in zip(units, scores, values):
        o = _dot(probs(s_lo, 4 * g, 4 * g + 2), v_lo) + _dot(probs(s_hi, 4 * g + 1, 4 * g + 3), v_hi)
        for j in range(2):
            pair = 2 * g + j
            o_ref[i, :, pair * LANES:(pair + 1) * LANES] = o[j * CHUNK:(j + 1) * CHUNK].astype(o_ref.dtype)


def _attention(q, kv_arrays, kv_chunks, sinks, nb, s, banded):
    nseq = math.gcd(nb, ATTN_SEQS)
    q_spec = pl.BlockSpec((nseq, CHUNK, D_MODEL), lambda b, c: (b, c, 0))
    kv_specs = [pl.BlockSpec((nseq, CHUNK, KV_W), lambda b, c, f=f: (b, f(c), 0)) for f in kv_chunks]
    return pl.pallas_call(
        functools.partial(_attn_kernel, banded=banded),
        grid=(nb // nseq, s // CHUNK),
        in_specs=[pl.BlockSpec(memory_space=pltpu.SMEM), q_spec] + kv_specs,
        out_specs=q_spec,
        out_shape=jax.ShapeDtypeStruct((nb, s, D_MODEL), BF16),
        compiler_params=_params(("parallel", "parallel")),
        name="attention",
    )(sinks, q, *kv_arrays)


def _pad_cols(w, n):
    return jnp.pad(w, ((0, 0), (0, n - w.shape[1])))


def _pad_rows(w, n):
    return jnp.pad(w, ((0, n - w.shape[0]), (0, 0)))


def _rope_tables(pos):
    half = ROPE_DIM // 2
    inv = jnp.power(jnp.float32(ROPE_THETA), -jnp.arange(half, dtype=F32) * (2.0 / ROPE_DIM))
    ang = pos[:, None] * inv[None, :]
    cos, sin = jnp.cos(ang), jnp.sin(ang)
    rest = HEAD - ROPE_DIM
    ones = jnp.ones((pos.shape[0], rest), F32)
    zeros = jnp.zeros((pos.shape[0], rest), F32)
    z8 = jnp.zeros_like(sin)
    per_head = (jnp.concatenate([cos, cos, ones], axis=1),
                jnp.concatenate([-sin, z8, zeros], axis=1),
                jnp.concatenate([z8, sin, zeros], axis=1))
    return tuple(jnp.tile(t, (1, LANES // HEAD)) for t in per_head)


def _state_to_pairs(state):
    nb = state.shape[0]
    st = state.astype(F32).reshape(nb, PAIRS, 2, HEAD, HEAD)
    z = jnp.zeros_like(st[:, :, 0])
    top = jnp.concatenate([st[:, :, 0], z], axis=-1)
    bot = jnp.concatenate([z, st[:, :, 1]], axis=-1)
    return jnp.concatenate([top, bot], axis=-2)


def _pairs_to_state(st):
    nb = st.shape[0]
    s0 = st[:, :, :HEAD, :HEAD]
    s1 = st[:, :, HEAD:, HEAD:]
    return jnp.stack([s0, s1], axis=2).reshape(nb, 2 * PAIRS, HEAD, HEAD)


def _trunk(x, mods, kvmod, pos, prev_x, prev_wkv, past_k, past_v, w, nb, s):
    n = nb * s
    x = x.reshape(n, D_MODEL)
    mod0, mod1 = mods[0][:, None, :], mods[1][:, None, :]
    kvmod = kvmod[:, None, :]

    tm = min(s, 512)
    r, lw, k, v, a, b, g, last_x = _rwkv_proj(x, mod0, prev_x[:, None, :], w["rw"], nb, s, tm)
    seqs = lambda t: t.reshape(nb, s, D_MODEL)
    z, st = _wkv(*(seqs(t) for t in (r, lw, k, v, a, b, g)), _state_to_pairs(prev_wkv), w["rw"], nb, s)
    z = z.reshape(n, D_MODEL)

    def residual_moe(x, z, mixer, mod, l):
        args = (mod, w["norm_g"][l][1:2], w["moe_route"][l], w["moe_gu"], w["moe_down"], w["final_g"], nb, s, l)
        x = _out_proj(x, z, mod[:, :, 2 * D_MODEL:3 * D_MODEL], mixer["w_o"], mixer["b_o"], nb, s, min(n, 1024))
        if n >= SPARSE_MIN_TOKENS:
            return _moe_sparse(x, *args, l == 1)
        return _moe(x, *args, min(n, 1024), l == 1)

    x = residual_moe(x, z, w["rw"], mod0, 0)

    tm = min(s, 512)
    q, k_new, v_new = _qkv_proj(x, mod1, kvmod, w["at"], _rope_tables(pos), nb, s, tm)
    k_seq, v_seq = k_new.reshape(nb, s, KV_W), v_new.reshape(nb, s, KV_W)
    if past_k is None:
        arrays = [k_seq] * 3 + [v_seq] * 3
        chunks = [lambda c, d=d: jnp.maximum(c - d, 0) for d in (2, 1, 0)] * 2
    else:
        pk = past_k.astype(F32).reshape(nb, 2 * CHUNK, KV_W)
        pv = past_v.astype(F32).reshape(nb, 2 * CHUNK, KV_W)
        arrays = [pk, pk, k_seq, pv, pv, v_seq]
        chunks = [lambda c: 0, lambda c: 1, lambda c: 0] * 2
    o = _attention(q.reshape(nb, s, D_MODEL), arrays, chunks, w["at"]["sinks"], nb, s, past_k is None)
    y = residual_moe(x, o.reshape(n, D_MODEL), w["at"], mod1, 1)

    return y.reshape(nb, s, D_MODEL), last_x.reshape(1, nb, D_MODEL), _pairs_to_state(st)[None], k_seq, v_seq


def kernel(x_prompt, x_sample, state_shift, state_wkv, cache_k, cache_v, c_prompt, c_sample, ada_w, ada_b, norm_g, rw_mu, rw_w_rkv, rw_w0, rw_w1, rw_w2, rw_a0, rw_a1, rw_a2, rw_g1, rw_g2, rw_k_k, rw_k_a, rw_r_k, rw_lnx_w, rw_lnx_b, rw_w_o, kv_ada_w, kv_ada_b, kv_norm_g, w_kv, b_kv, at_w_q, at_b_q, at_sinks, at_w_o, at_b_o, moe_w_group, moe_w_expert, moe_w_gu, moe_w_down, final_norm_g):
    bp, sp, _ = x_prompt.shape
    bs, ss, _ = x_sample.shape
    row = lambda t: t.reshape(1, -1).astype(F32)

    c_all = jnp.concatenate([c_prompt, c_sample], axis=0)
    mods = _cond_linear(c_all, ada_w, ada_b[:, None, :])
    kvmods = _cond_linear(c_all, kv_ada_w[None], kv_ada_b[None, None, :])[0]

    lora_pad, gate_pad = LANES, 2 * LANES
    w = {
        "norm_g": norm_g,
        "final_g": row(final_norm_g),
        "rw": {
            "norm_g": norm_g[0, 0:1], "mu": rw_mu[0], "w_rkv": rw_w_rkv[0].astype(BF16),
            "w0": row(rw_w0[0]), "w1": _pad_cols(rw_w1[0], lora_pad).astype(BF16),
            "w2": _pad_rows(rw_w2[0], lora_pad).astype(BF16),
            "a0": row(rw_a0[0]), "a1": _pad_cols(rw_a1[0], lora_pad).astype(BF16),
            "a2": _pad_rows(rw_a2[0], lora_pad).astype(BF16),
            "g1": _pad_cols(rw_g1[0], gate_pad).astype(BF16), "g2": _pad_rows(rw_g2[0], gate_pad).astype(BF16),
            "k_k": row(rw_k_k[0]), "k_a": row(rw_k_a[0]), "r_k": row(rw_r_k[0]),
            "lnx_w": row(rw_lnx_w[0]), "lnx_b": row(rw_lnx_b[0]),
            "w_o": rw_w_o[0].astype(BF16), "b_o": jnp.zeros((1, D_MODEL), F32),
        },
        "at": {
            "norm_gq": norm_g[1, 0:1], "norm_gkv": row(kv_norm_g),
            "w_q": at_w_q[0].astype(BF16), "b_q": row(at_b_q[0]),
            "w_kv": w_kv.astype(BF16), "b_kv": row(b_kv),
            "sinks": at_sinks[0].astype(F32),
            "w_o": at_w_o[0].astype(BF16), "b_o": row(at_b_o[0]),
        },
        "moe_route": jnp.pad(jnp.concatenate([moe_w_group, moe_w_expert], axis=-1),
                             ((0, 0), (0, 0), (0, LANES - N_GROUPS - N_EXPERTS))),
        "moe_gu": moe_w_gu,
        "moe_down": moe_w_down,
    }

    pos_p = jnp.arange(sp, dtype=F32)
    pos_s = PAST_LEN + jnp.arange(ss, dtype=F32)
    zero_x = jnp.zeros((bp, D_MODEL), x_prompt.dtype)
    zero_wkv = jnp.zeros((bp,) + state_wkv.shape[2:], state_wkv.dtype)
    y_p, p_shift, p_wkv, p_k, p_v = _trunk(x_prompt, mods[:, :bp], kvmods[:bp], pos_p, zero_x, zero_wkv,
                                           None, None, w, bp, sp)
    y_s, s_shift, s_wkv, s_k, s_v = _trunk(x_sample, mods[:, bp:], kvmods[bp:], pos_s, state_shift[0],
                                           state_wkv[0], cache_k, cache_v, w, bs, ss)
    keep = min(2 * CHUNK, sp)
    heads = lambda t: t.reshape(t.shape[0], t.shape[1], N_KV, HEAD)
    return (y_p, y_s, p_shift, p_wkv.astype(state_wkv.dtype), heads(p_k[:, sp - keep:]), heads(p_v[:, sp - keep:]),
            s_shift, s_wkv.astype(state_wkv.dtype), heads(s_k), heads(s_v))
```

```python
import functools
import math

import jax
import jax.numpy as jnp
from jax import lax
from jax.experimental import pallas as pl
from jax.experimental.pallas import tpu as pltpu

F32 = jnp.float32
BF16 = jnp.bfloat16

D_MODEL = 1024
LANES = 128
HEAD = 64
PAIRS = D_MODEL // LANES
CHUNK = 64
WKV_SEQS = 4
ATTN_SEQS = 2
PAST_LEN = 4096
N_KV = 4
KV_W = N_KV * HEAD
ROPE_DIM = 16
ROPE_THETA = 500000.0
ATT_SCALE = HEAD ** -0.5
N_GROUPS = 4
EXP_PER_GROUP = 8
N_EXPERTS = N_GROUPS * EXP_PER_GROUP
D_EXPERT = D_MODEL // 4
ROUTE_LANE0 = N_GROUPS
RMS_EPS = 1e-6
GN_EPS = 64e-5
NEG_INF = -1e30
VMEM_LIMIT = 56 * 1024 * 1024


def _params(sem):
    return pltpu.CompilerParams(dimension_semantics=sem, vmem_limit_bytes=VMEM_LIMIT)


def _dot(a, b):
    return jnp.dot(a, b, preferred_element_type=F32)


def _dot_nt(a, b):
    return lax.dot_general(a, b, (((1,), (1,)), ((), ())), preferred_element_type=F32)


def _dot_tn(a, b):
    return lax.dot_general(a, b, (((0,), (0,)), ((), ())), preferred_element_type=F32)


def _split2(x):
    hi = x.astype(BF16)
    lo = (x - hi.astype(F32)).astype(BF16)
    return hi, lo


def _dot_x3(a, b, dot=_dot):
    ah, al = _split2(a)
    bh, bl = _split2(b)
    return dot(ah, bh) + dot(ah, bl) + dot(al, bh)


def _sigmoid(x):
    return 1.0 / (1.0 + jnp.exp(-x))


def _rms(x):
    return x * lax.rsqrt(jnp.mean(x * x, axis=-1, keepdims=True) + RMS_EPS)


def _modulate(n, gain, shift, scale, nb):
    rows = n.shape[0]
    h = (n * gain).reshape(nb, rows // nb, D_MODEL)
    return (h * (1.0 + scale) + shift).reshape(rows, D_MODEL)


def _const_spec(shape):
    nd = len(shape)
    return pl.BlockSpec(shape, lambda *_: (0,) * nd)


def _cond_kernel(c_ref, w_ref, b_ref, o_ref):
    c = c_ref[...]
    cs = (c * _sigmoid(c)).astype(BF16)
    o_ref[...] = _dot(cs, w_ref[...].astype(BF16)) + b_ref[...]


def _cond_linear(c, w, b, tn=512):
    nl, _, n = w.shape
    m = c.shape[0]
    return pl.pallas_call(
        _cond_kernel,
        grid=(nl, n // tn),
        in_specs=[
            pl.BlockSpec((m, D_MODEL), lambda l, j: (0, 0)),
            pl.BlockSpec((None, D_MODEL, tn), lambda l, j: (l, 0, j)),
            pl.BlockSpec((None, 1, tn), lambda l, j: (l, 0, j)),
        ],
        out_specs=pl.BlockSpec((None, m, tn), lambda l, j: (l, 0, j)),
        out_shape=jax.ShapeDtypeStruct((nl, m, n), F32),
        compiler_params=_params(("parallel", "parallel")),
        name="cond_linear",
    )(c, w, b)


def _rwkv_proj_kernel(x_ref, mod_ref, prev_ref, ng_ref, mu_ref, wrkv_ref, w0_ref, w1_ref, w2_ref,
                      a0_ref, a1_ref, a2_ref, g1_ref, g2_ref, kk_ref, ka_ref,
                      r_ref, lw_ref, k_ref, v_ref, a_ref, b_ref, g_ref, last_ref, carry_ref):
    tm = x_ref.shape[0]

    @pl.when(pl.program_id(1) == 0)
    def _():
        carry_ref[...] = prev_ref[...]

    shift = mod_ref[:, 0:D_MODEL].reshape(1, 1, D_MODEL)
    scale = mod_ref[:, D_MODEL:2 * D_MODEL].reshape(1, 1, D_MODEL)
    h = _modulate(_rms(x_ref[...]), ng_ref[...], shift, scale, 1)
    row = lax.broadcasted_iota(jnp.int32, (tm, 1), 0)
    shifted = jnp.where(row == 0, carry_ref[...], pltpu.roll(h, 1, axis=0))
    carry_ref[...] = h[tm - 1:tm, :]
    last_ref[...] = h[tm - 1:tm, :]
    dx = shifted - h

    def mix(n):
        return (h + dx * mu_ref[n:n + 1, :]).astype(BF16)

    r = _dot(mix(0), wrkv_ref[0])
    k = _dot(mix(1), wrkv_ref[1])
    v = _dot(mix(2), wrkv_ref[2])
    ww = _dot(jnp.tanh(_dot(mix(3), w1_ref[...])).astype(BF16), w2_ref[...])
    z = -(w0_ref[...] + ww)
    softplus = jnp.maximum(z, 0.0) + jnp.log(1.0 + jnp.exp(-jnp.abs(z)))
    lw = -jnp.exp(-softplus - 0.5)
    asig = _sigmoid(a0_ref[...] + _dot(_dot(mix(4), a1_ref[...]).astype(BF16), a2_ref[...]))
    g = _dot(_sigmoid(_dot(mix(5), g1_ref[...])).astype(BF16), g2_ref[...])

    kk = k * kk_ref[...]
    kk2 = kk * kk
    lo_cols = lax.broadcasted_iota(jnp.int32, (tm, LANES), 1) < HEAD

    def head_sums(t):
        lo_sum = jnp.sum(jnp.where(lo_cols, t, 0.0), axis=-1, keepdims=True)
        hi_sum = jnp.sum(jnp.where(lo_cols, 0.0, t), axis=-1, keepdims=True)
        return jnp.where(lo_cols, lo_sum, hi_sum)

    ss = jnp.concatenate([head_sums(kk2[:, p * LANES:(p + 1) * LANES]) for p in range(PAIRS)], axis=1)
    kkn = kk * lax.rsqrt(jnp.maximum(ss, 1e-24))

    r_ref[...] = r.astype(r_ref.dtype)
    lw_ref[...] = lw
    k_ref[...] = (k * (1.0 + (asig - 1.0) * ka_ref[...])).astype(k_ref.dtype)
    v_ref[...] = v.astype(v_ref.dtype)
    a_ref[...] = (-kkn).astype(a_ref.dtype)
    b_ref[...] = (kkn * asig).astype(b_ref.dtype)
    g_ref[...] = g.astype(g_ref.dtype)


def _rwkv_proj(x, mod, prev_x, w, nb, s, tm):
    n = nb * s
    nt = s // tm
    row_spec = pl.BlockSpec((tm, D_MODEL), lambda b, j: (b * nt + j, 0))
    vec = _const_spec((1, D_MODEL))
    lora_w, lora_g = w["w1"].shape[1], w["g1"].shape[1]
    in_specs = [
        row_spec,
        pl.BlockSpec((None, 1, 6 * D_MODEL), lambda b, j: (b, 0, 0)),
        pl.BlockSpec((None, 1, D_MODEL), lambda b, j: (b, 0, 0)),
        vec, _const_spec((6, D_MODEL)), _const_spec((3, D_MODEL, D_MODEL)),
        vec, _const_spec((D_MODEL, lora_w)), _const_spec((lora_w, D_MODEL)),
        vec, _const_spec((D_MODEL, lora_w)), _const_spec((lora_w, D_MODEL)),
        _const_spec((D_MODEL, lora_g)), _const_spec((lora_g, D_MODEL)),
        vec, vec,
    ]
    act = lambda dt: jax.ShapeDtypeStruct((n, D_MODEL), dt)
    out_types = [act(BF16), act(F32)] + [act(BF16)] * 5
    outs = pl.pallas_call(
        _rwkv_proj_kernel,
        grid=(nb, nt),
        in_specs=in_specs,
        out_specs=[row_spec] * 7 + [pl.BlockSpec((None, 1, D_MODEL), lambda b, j: (b, 0, 0))],
        out_shape=out_types + [jax.ShapeDtypeStruct((nb, 1, D_MODEL), F32)],
        scratch_shapes=[pltpu.VMEM((1, D_MODEL), F32)],
        compiler_params=_params(("parallel", "arbitrary")),
        name="rwkv_proj",
    )(x, mod, prev_x, w["norm_g"], w["mu"], w["w_rkv"], w["w0"], w["w1"], w["w2"],
      w["a0"], w["a1"], w["a2"], w["g1"], w["g2"], w["k_k"], w["k_a"])
    return outs


def _split3(x):
    hi = x.astype(BF16)
    rem = x - hi.astype(F32)
    mid = rem.astype(BF16)
    return hi, mid, (rem - mid.astype(F32)).astype(BF16)


def _wkv_kernel(r_ref, lw_ref, k_ref, v_ref, a_ref, b_ref, g_ref, s0_ref, rk_ref, lnw_ref, lnb_ref,
                z_ref, st_ref):
    @pl.when(pl.program_id(1) == 0)
    def _():
        st_ref[...] = s0_ref[...]

    nseq = r_ref.shape[0]
    wide = lambda ref: jnp.concatenate([ref[i] for i in range(nseq)], axis=1)
    per_seq = lambda ref: jnp.concatenate([ref[...]] * nseq, axis=1)
    b16 = lambda t: t.astype(BF16)
    pairs = range(nseq * PAIRS)
    blk = lambda t, p: t[:, p * LANES:(p + 1) * LANES]
    rows = lambda ts: jnp.concatenate(ts, axis=0)
    lo_cols = lax.broadcasted_iota(jnp.int32, (CHUNK, LANES), 1) < HEAD
    rr = lax.broadcasted_iota(jnp.int32, (LANES, LANES), 0)
    cc = lax.broadcasted_iota(jnp.int32, (LANES, LANES), 1)
    head0 = lambda t: jnp.where(lo_cols, t, 0.0)
    head1 = lambda t: jnp.where(lo_cols, 0.0, t)
    blockdiag = lambda q: rows([head0(q), head1(q)])

    r, lw, k, v, a, b = (wide(t).astype(F32) for t in (r_ref, lw_ref, k_ref, v_ref, a_ref, b_ref))
    ti = lax.broadcasted_iota(jnp.int32, (CHUNK, CHUNK), 0)
    tj = lax.broadcasted_iota(jnp.int32, (CHUNK, CHUNK), 1)
    tri = (tj <= ti).astype(BF16)
    cum = sum(_dot(tri, t) for t in _split3(lw))
    cl = cum[CHUNK - 1:CHUNK, :]
    e_neg = jnp.exp(-cum)
    e_end = jnp.exp(cl - cum)
    e_cl = jnp.exp(cl)
    at = b16(a * jnp.exp(cum - lw))
    rt = b16(r * jnp.exp(cum))
    bt = b * e_neg
    kt = k * e_neg
    bkh = rows([b16(b * e_end), b16(k * e_end)])
    v16 = b16(v)

    t_idx = rr & (CHUNK - 1)
    j_idx = cc & (CHUNK - 1)
    keep = (j_idx < t_idx) | ((rr >= CHUNK) & (j_idx == t_idx))
    g0, g1 = [], []
    for p in pairs:
        btp, ktp = blk(bt, p), blk(kt, p)
        w = b16(rows([head0(btp), head0(ktp), head1(ktp), head1(btp)]))
        g = _dot_nt(rows([blk(at, p), blk(rt, p)]), w)
        g0.append(jnp.where(keep, g[:, :LANES], 0.0))
        g1.append(jnp.where(keep, g[:, LANES:], 0.0))

    eye2 = jnp.where((lax.broadcasted_iota(jnp.int32, (CHUNK, LANES), 1) & (CHUNK - 1))
                     == lax.broadcasted_iota(jnp.int32, (CHUNK, LANES), 0), 1.0, 0.0)
    pc = [jnp.where(lo_cols, g0[p][:CHUNK], g1[p][:CHUNK]) for p in pairs]
    tc = [eye2 + pc[p] for p in pairs]
    q = [_dot(b16(pc[p]), b16(blockdiag(pc[p]))) for p in pairs]
    for _ in range(4):
        res = [_dot(b16(rows([tc[p], q[p]])), b16(blockdiag(q[p]))) for p in pairs]
        tc = [tc[p] + res[p][:CHUNK] for p in pairs]
        q = [res[p][CHUNK:] for p in pairs]
    tc = [b16(tc[p] + _dot(b16(tc[p]), b16(blockdiag(q[p])))) for p in pairs]

    akv = []
    for p in pairs:
        w_ak = b16(jnp.where(lo_cols, g1[p][:CHUNK], g0[p][:CHUNK]))
        vp = blk(v16, p)
        akv.append(_dot(w_ak, rows([head1(vp), head0(vp)])))

    s = [st_ref[p // PAIRS, p % PAIRS] for p in pairs]
    s16 = [b16(s[p]) for p in pairs]
    rhs = [_dot_nt(blk(at, p), s16[p]) + akv[p] for p in pairs]
    u = [_dot(tc[p], b16(rows([head0(rhs[p]), head1(rhs[p])]))) for p in pairs]
    u16 = [b16(u[p]) for p in pairs]

    ys = []
    bd_mask = (rr < HEAD) == (cc < HEAD)
    for p in pairs:
        vp = blk(v16, p)
        rbk = b16(jnp.concatenate([g0[p][CHUNK:], g1[p][CHUNK:]], axis=1))
        uv = rows([head0(u16[p]), head0(vp), head1(vp), head1(u16[p])])
        ys.append(_dot_nt(blk(rt, p), s16[p]) + _dot(rbk, uv))
        fresh = _dot_tn(rows([u16[p], vp]), blk(bkh, p))
        st_ref[p // PAIRS, p % PAIRS] = s[p] * blk(e_cl, p) + jnp.where(bd_mask, fresh, 0.0)

    def head_sums(t):
        lo_sum = jnp.sum(head0(t), axis=-1, keepdims=True)
        hi_sum = jnp.sum(head1(t), axis=-1, keepdims=True)
        return jnp.where(lo_cols, lo_sum, hi_sum)

    rkk = r * k * per_seq(rk_ref)
    for p in pairs:
        seq, cols = p // PAIRS, slice((p % PAIRS) * LANES, (p % PAIRS + 1) * LANES)
        yc = ys[p] - head_sums(ys[p]) * (1.0 / HEAD)
        yn = yc * lax.rsqrt(head_sums(yc * yc) * (1.0 / HEAD) + GN_EPS)
        zp = (yn * lnw_ref[:, cols] + lnb_ref[:, cols] + head_sums(blk(rkk, p)) * blk(v, p)) * g_ref[seq, :, cols].astype(F32)
        z_ref[seq, :, cols] = zp.astype(z_ref.dtype)


def _wkv(r, lw, k, v, a, b, g, st0, w, nb, s):
    nseq = math.gcd(nb, WKV_SEQS)
    blk = pl.BlockSpec((nseq, CHUNK, D_MODEL), lambda bi, c: (bi, c, 0))
    vec = _const_spec((1, D_MODEL))
    st_spec = pl.BlockSpec((nseq, PAIRS, LANES, LANES), lambda bi, c: (bi, 0, 0, 0))
    return pl.pallas_call(
        _wkv_kernel,
        grid=(nb // nseq, s // CHUNK),
        in_specs=[blk] * 7 + [st_spec, vec, vec, vec],
        out_specs=[blk, st_spec],
        out_shape=[jax.ShapeDtypeStruct((nb, s, D_MODEL), BF16),
                   jax.ShapeDtypeStruct((nb, PAIRS, LANES, LANES), F32)],
        compiler_params=_params(("parallel", "arbitrary")),
        name="wkv",
    )(r, lw, k, v, a, b, g, st0, w["r_k"], w["lnx_w"], w["lnx_b"])


def _out_proj_kernel(x_ref, z_ref, gate_ref, w_ref, bias_ref, o_ref, *, nb):
    rows = x_ref.shape[0]
    y = _dot(z_ref[...], w_ref[...]) + bias_ref[...]
    y = (y.reshape(nb, rows // nb, D_MODEL) * gate_ref[...]).reshape(rows, D_MODEL)
    o_ref[...] = x_ref[...] + y


def _out_proj(x, z, gate, w, bias, nb, s, tm):
    n = nb * s
    tb = max(tm // s, 1)
    per_seq = max(s // tm, 1)
    row_spec = pl.BlockSpec((tm, D_MODEL), lambda i: (i, 0))
    return pl.pallas_call(
        functools.partial(_out_proj_kernel, nb=tb),
        grid=(n // tm,),
        in_specs=[row_spec, row_spec,
                  pl.BlockSpec((tb, 1, D_MODEL), lambda i: (i // per_seq, 0, 0)),
                  _const_spec((D_MODEL, D_MODEL)), _const_spec((1, D_MODEL))],
        out_specs=row_spec,
        out_shape=jax.ShapeDtypeStruct((n, D_MODEL), F32),
        compiler_params=_params(("parallel",)),
        name="out_proj",
    )(x, z, gate, w, bias)


def _top2(logits):
    lane = lax.broadcasted_iota(jnp.int32, logits.shape, 1).astype(F32)
    first_of = lambda hit: jnp.min(jnp.where(hit, lane, float(LANES)), axis=-1, keepdims=True)
    lg = jnp.where(lane < N_GROUPS, logits, -jnp.inf)
    gmax = jnp.max(lg, axis=-1, keepdims=True)
    gi = first_of(lg == gmax)
    gp = 1.0 / jnp.sum(jnp.exp(lg - gmax), axis=-1, keepdims=True)
    first = ROUTE_LANE0 + gi * EXP_PER_GROUP
    le = jnp.where((lane >= first) & (lane < first + EXP_PER_GROUP), logits, -jnp.inf)
    top1 = jnp.max(le, axis=-1, keepdims=True)
    i1 = first_of(le == top1)
    le2 = jnp.where(lane == i1, -jnp.inf, le)
    top2 = jnp.max(le2, axis=-1, keepdims=True)
    i2 = first_of(le2 == top2)
    e2 = jnp.exp(top2 - top1)
    w1 = gp / (1.0 + e2)
    return i1, i2, w1, w1 * e2


def _route(logits):
    lane = lax.broadcasted_iota(jnp.int32, logits.shape, 1).astype(F32)
    i1, i2, w1, w2 = _top2(logits)
    return jnp.where(lane == i1, w1, 0.0) + jnp.where(lane == i2, w2, 0.0)


def _moe_kernel(x_ref, mod_ref, ng_ref, wr_ref, wgu_ref, wdn_ref, fg_ref, o_ref, gu16_ref, dn16_ref,
                h_scr, cw_scr, acc_scr, *, nb, final_norm):
    e = pl.program_id(1)
    rows = x_ref.shape[0]

    @pl.when(e == 0)
    def _():
        shift = mod_ref[:, :, 3 * D_MODEL:4 * D_MODEL]
        scale = mod_ref[:, :, 4 * D_MODEL:5 * D_MODEL]
        h = _modulate(_rms(x_ref[...]), ng_ref[...], shift, scale, nb)
        h_scr[...] = h.astype(BF16)
        cw_scr[...] = _route(_dot_x3(h, wr_ref[...]))
        acc_scr[...] = jnp.zeros_like(acc_scr)

    lane = lax.broadcasted_iota(jnp.int32, (rows, LANES), 1)
    cw = jnp.sum(jnp.where(lane == ROUTE_LANE0 + e, cw_scr[...], 0.0), axis=-1, keepdims=True)
    gu16_ref[...] = wgu_ref[...].astype(BF16)
    dn16_ref[...] = wdn_ref[...].astype(BF16)
    gu = _dot(h_scr[...], gu16_ref[...])
    gate = gu[:, :D_EXPERT]
    hid = gate * _sigmoid(gate) * gu[:, D_EXPERT:] * cw
    acc_scr[...] += _dot(hid.astype(BF16), dn16_ref[...])

    @pl.when(e == N_EXPERTS - 1)
    def _():
        g2 = mod_ref[:, :, 5 * D_MODEL:6 * D_MODEL]
        y = (acc_scr[...].reshape(nb, rows // nb, D_MODEL) * g2).reshape(rows, D_MODEL)
        out = x_ref[...] + y
        if final_norm:
            out = _rms(out) * fg_ref[...]
        o_ref[...] = out


def _moe(x, mod, norm_g, w_route, w_gu, w_down, final_g, nb, s, layer, tm, final_norm):
    n = nb * s
    tb = max(tm // s, 1)
    per_seq = max(s // tm, 1)
    row_spec = pl.BlockSpec((tm, D_MODEL), lambda i, e: (i, 0))
    return pl.pallas_call(
        functools.partial(_moe_kernel, nb=tb, final_norm=final_norm),
        grid=(n // tm, N_EXPERTS),
        in_specs=[row_spec,
                  pl.BlockSpec((tb, 1, 6 * D_MODEL), lambda i, e: (i // per_seq, 0, 0)),
                  _const_spec((1, D_MODEL)), _const_spec((D_MODEL, LANES)),
                  pl.BlockSpec((None, None, D_MODEL, 2 * D_EXPERT), lambda i, e: (layer, e, 0, 0)),
                  pl.BlockSpec((None, None, D_EXPERT, D_MODEL), lambda i, e: (layer, e, 0, 0)),
                  _const_spec((1, D_MODEL))],
        out_specs=[row_spec,
                   pl.BlockSpec((None, D_MODEL, 2 * D_EXPERT), lambda i, e: (e, 0, 0)),
                   pl.BlockSpec((None, D_EXPERT, D_MODEL), lambda i, e: (e, 0, 0))],
        out_shape=[jax.ShapeDtypeStruct((n, D_MODEL), F32),
                   jax.ShapeDtypeStruct((N_EXPERTS, D_MODEL, 2 * D_EXPERT), BF16),
                   jax.ShapeDtypeStruct((N_EXPERTS, D_EXPERT, D_MODEL), BF16)],
        scratch_shapes=[pltpu.VMEM((tm, D_MODEL), BF16), pltpu.VMEM((tm, LANES), F32),
                        pltpu.VMEM((tm, D_MODEL), F32)],
        compiler_params=_params(("arbitrary", "arbitrary")),
        name="moe",
    )(x, mod, norm_g, w_route, w_gu, w_down, final_g)


PAIRS_PER_GROUP = EXP_PER_GROUP * (EXP_PER_GROUP - 1) // 2
N_BUCKETS = N_GROUPS * PAIRS_PER_GROUP
BUCKET_TILE = 256
ROW_W = D_MODEL + LANES
META_BUCKET, META_RANK, META_W_LO, META_W_HI = 0, 1, 2, 3
ROUTER_TILE = 1024
MOVE_TILE = 1024
DMA_UNROLL = 8
SPARSE_MIN_TOKENS = 4096


def _moe_norm(x, mod_ref, ng_ref, nb):
    shift = mod_ref[:, :, 3 * D_MODEL:4 * D_MODEL]
    scale = mod_ref[:, :, 4 * D_MODEL:5 * D_MODEL]
    return _modulate(_rms(x), ng_ref[...], shift, scale, nb)


def _router_kernel(x_ref, mod_ref, ng_ref, wr_ref, earlier_ref, meta_ref, cnt_ref, *, nb):
    @pl.when(pl.program_id(0) == 0)
    def _():
        cnt_ref[...] = jnp.zeros_like(cnt_ref)

    rows = x_ref.shape[0]
    h = _moe_norm(x_ref[...], mod_ref, ng_ref, nb)
    i1, i2, w1, w2 = _top2(_dot_x3(h, wr_ref[...]))
    lo = (jnp.minimum(i1, i2) - ROUTE_LANE0).astype(jnp.int32)
    hi = (jnp.maximum(i1, i2) - ROUTE_LANE0).astype(jnp.int32)
    first_is_lo = i1 < i2
    a = lo & (EXP_PER_GROUP - 1)
    b = hi & (EXP_PER_GROUP - 1)
    group = lo >> (EXP_PER_GROUP.bit_length() - 1)
    bucket = group * PAIRS_PER_GROUP + ((a * (2 * EXP_PER_GROUP - 1 - a)) >> 1) + (b - a - 1)

    lane = lax.broadcasted_iota(jnp.int32, (rows, LANES), 1)
    mine = lane == bucket
    onehot = jnp.where(mine, 1.0, 0.0)
    seen = cnt_ref[...]
    before = _dot(earlier_ref[...], onehot.astype(BF16)) + seen
    rank = jnp.sum(jnp.where(mine, before, 0.0), axis=-1, keepdims=True)
    cnt_ref[...] = seen + jnp.sum(onehot, axis=0, keepdims=True)

    meta = jnp.where(lane == META_BUCKET, bucket.astype(F32), 0.0)
    meta = jnp.where(lane == META_RANK, rank, meta)
    meta = jnp.where(lane == META_W_LO, jnp.where(first_is_lo, w1, w2), meta)
    meta_ref[...] = jnp.where(lane == META_W_HI, jnp.where(first_is_lo, w2, w1), meta)


def _positions_kernel(meta_ref, start_ref, pos_ref):
    meta = meta_ref[...]
    lane = lax.broadcasted_iota(jnp.int32, meta.shape, 1)
    column = lambda c: jnp.sum(jnp.where(lane == c, meta, 0.0), axis=-1, keepdims=True)
    value = jnp.where(lane == column(META_BUCKET).astype(jnp.int32), start_ref[...] + column(META_RANK), 0.0)
    ones = jnp.ones((8, LANES), BF16)
    pos = sum(_dot_nt(ones, piece) for piece in _split3(value))
    pos_ref[...] = pos[0:1, :].astype(jnp.int32)


def _dispatch_kernel(pos_ref, x_ref, mod_ref, ng_ref, meta_ref, xs_ref, buf, sem, *, nb):
    i = pl.program_id(0)
    rows = x_ref.shape[0]
    slot = i % 2
    buf[slot, :, 0:D_MODEL] = _moe_norm(x_ref[...], mod_ref, ng_ref, nb)
    buf[slot, :, D_MODEL:ROW_W] = meta_ref[...]

    def send(r, carry):
        pltpu.make_async_copy(buf.at[slot, pl.ds(r, 1), :],
                              xs_ref.at[pl.ds(pos_ref[i * rows + r], 1), :], sem.at[slot]).start()
        return carry

    lax.fori_loop(0, rows, send, 0, unroll=DMA_UNROLL)

    def wait_all(sl):
        pltpu.make_async_copy(buf.at[sl], xs_ref.at[pl.ds(0, rows), :], sem.at[sl]).wait()

    @pl.when(i >= 1)
    def _():
        wait_all(1 - slot)

    @pl.when(i == pl.num_programs(0) - 1)
    def _():
        wait_all(slot)


def _experts_kernel(tile_ref, bucket_ref, lo_ref, hi_ref, used_ref, xs_ref, gu_lo_ref, gu_hi_ref,
                    dn_lo_ref, dn_hi_ref, ys_ref):
    k = pl.program_id(0)

    @pl.when(k < used_ref[0])
    def _():
        rows = xs_ref.shape[0]
        x16 = xs_ref[:, 0:D_MODEL].astype(BF16)
        meta = xs_ref[:, D_MODEL:ROW_W]
        lane = lax.broadcasted_iota(jnp.int32, (rows, LANES), 1)
        column = lambda c: jnp.sum(jnp.where(lane == c, meta, 0.0), axis=-1, keepdims=True)
        member = column(META_BUCKET) == bucket_ref[k].astype(F32)

        def expert(gu_ref, dn_ref, col):
            gu = _dot(x16, gu_ref[...])
            gate = gu[:, :D_EXPERT]
            hid = gate * _sigmoid(gate) * gu[:, D_EXPERT:] * jnp.where(member, column(col), 0.0)
            return _dot(hid.astype(BF16), dn_ref[...])

        y = expert(gu_lo_ref, dn_lo_ref, META_W_LO) + expert(gu_hi_ref, dn_hi_ref, META_W_HI)
        new_tile = (k == 0) | (tile_ref[k] != tile_ref[jnp.maximum(k - 1, 0)])

        @pl.when(new_tile)
        def _():
            ys_ref[...] = y

        @pl.when(jnp.logical_not(new_tile))
        def _():
            ys_ref[...] += y


def _combine_kernel(pos_ref, x_ref, mod_ref, fg_ref, ys_ref, o_ref, buf, sem, *, nb, final_norm):
    i = pl.program_id(0)
    rows = x_ref.shape[0]
    slot = i % 2

    def fetch(step, sl):
        def one(r, carry):
            pltpu.make_async_copy(ys_ref.at[pl.ds(pos_ref[step * rows + r], 1), :],
                                  buf.at[sl, pl.ds(r, 1), :], sem.at[sl]).start()
            return carry
        lax.fori_loop(0, rows, one, 0, unroll=DMA_UNROLL)

    @pl.when(i == 0)
    def _():
        fetch(0, 0)

    pltpu.make_async_copy(ys_ref.at[pl.ds(0, rows), :], buf.at[slot], sem.at[slot]).wait()

    @pl.when(i + 1 < pl.num_programs(0))
    def _():
        fetch(i + 1, 1 - slot)

    g2 = mod_ref[:, :, 5 * D_MODEL:6 * D_MODEL]
    y = (buf[slot].reshape(nb, rows // nb, D_MODEL) * g2).reshape(rows, D_MODEL)
    out = x_ref[...] + y
    if final_norm:
        out = _rms(out) * fg_ref[...]
    o_ref[...] = out


def _bucket_experts():
    lo, hi = [], []
    for g in range(N_GROUPS):
        for a in range(EXP_PER_GROUP):
            for b in range(a + 1, EXP_PER_GROUP):
                lo.append(g * EXP_PER_GROUP + a)
                hi.append(g * EXP_PER_GROUP + b)
    return jnp.asarray(lo, jnp.int32), jnp.asarray(hi, jnp.int32)


def _moe_sparse(x, mod, norm_g, w_route, w_gu, w_down, final_g, nb, s, final_norm):
    n = nb * s
    move_params = pltpu.CompilerParams(dimension_semantics=("arbitrary",), vmem_limit_bytes=VMEM_LIMIT,
                                       disable_bounds_checks=True)

    def token_specs(tm):
        tb, per_seq = max(tm // s, 1), max(s // tm, 1)
        row = pl.BlockSpec((tm, D_MODEL), lambda i, *_: (i, 0))
        modspec = pl.BlockSpec((tb, 1, 6 * D_MODEL), lambda i, *_: (i // per_seq, 0, 0))
        meta = pl.BlockSpec((tm, LANES), lambda i, *_: (i, 0))
        return tb, row, modspec, meta

    vec = pl.BlockSpec((1, D_MODEL), lambda *_: (0, 0))
    tb, row, modspec, metaspec = token_specs(ROUTER_TILE)
    meta, counts = pl.pallas_call(
        functools.partial(_router_kernel, nb=tb),
        grid=(n // ROUTER_TILE,),
        in_specs=[row, modspec, vec, pl.BlockSpec((D_MODEL, LANES), lambda i: (0, 0)),
                  pl.BlockSpec((ROUTER_TILE, ROUTER_TILE), lambda i: (0, 0))],
        out_specs=[metaspec, pl.BlockSpec((1, LANES), lambda i: (0, 0))],
        out_shape=[jax.ShapeDtypeStruct((n, LANES), F32), jax.ShapeDtypeStruct((1, LANES), F32)],
        compiler_params=_params(("arbitrary",)),
        name="moe_router",
    )(x, mod, norm_g, w_route, jnp.tril(jnp.ones((ROUTER_TILE, ROUTER_TILE), BF16), -1))

    counts = counts[0, :N_BUCKETS].astype(jnp.int32)
    end = jnp.cumsum(counts)
    start = end - counts
    first_tile = start // BUCKET_TILE
    per_bucket = jnp.where(counts > 0, (end - 1) // BUCKET_TILE - first_tile + 1, 0)
    item_end = jnp.cumsum(per_bucket)
    used = item_end[-1:]
    n_items = n // BUCKET_TILE + N_BUCKETS
    k = jnp.minimum(jnp.arange(n_items, dtype=jnp.int32), used - 1)
    item_bucket = jnp.sum(item_end[None, :] <= k[:, None], axis=1, dtype=jnp.int32)
    item_tile = first_tile[item_bucket] + k - (item_end - per_bucket)[item_bucket]
    bucket_lo, bucket_hi = _bucket_experts()
    item_lo, item_hi = bucket_lo[item_bucket], bucket_hi[item_bucket]
    start_row = jnp.pad(start, (0, LANES - N_BUCKETS)).astype(F32)[None, :]
    pos = pl.pallas_call(
        _positions_kernel,
        grid=(n // ROUTER_TILE,),
        in_specs=[metaspec, pl.BlockSpec((1, LANES), lambda i: (0, 0))],
        out_specs=pl.BlockSpec((1, ROUTER_TILE), lambda i: (0, i)),
        out_shape=jax.ShapeDtypeStruct((1, n), jnp.int32),
        compiler_params=_params(("parallel",)),
        name="moe_positions",
    )(meta, start_row).reshape(n)

    tb, row, modspec, metaspec = token_specs(MOVE_TILE)
    any_spec = pl.BlockSpec(memory_space=pl.ANY)
    xs = pl.pallas_call(
        functools.partial(_dispatch_kernel, nb=tb),
        grid_spec=pltpu.PrefetchScalarGridSpec(
            num_scalar_prefetch=1, grid=(n // MOVE_TILE,),
            in_specs=[row, modspec, vec, metaspec], out_specs=any_spec,
            scratch_shapes=[pltpu.VMEM((2, MOVE_TILE, ROW_W), F32), pltpu.SemaphoreType.DMA((2,))]),
        out_shape=jax.ShapeDtypeStruct((n, ROW_W), F32),
        compiler_params=move_params,
        name="moe_dispatch",
    )(pos, x, mod, norm_g, meta)

    tile_of = lambda k, tile, *_: (tile[k], 0)
    ys = pl.pallas_call(
        _experts_kernel,
        grid_spec=pltpu.PrefetchScalarGridSpec(
            num_scalar_prefetch=5, grid=(n_items,),
            in_specs=[pl.BlockSpec((BUCKET_TILE, ROW_W), tile_of),
                      pl.BlockSpec((None, D_MODEL, 2 * D_EXPERT), lambda k, t, b, lo, hi, nu: (lo[k], 0, 0)),
                      pl.BlockSpec((None, D_MODEL, 2 * D_EXPERT), lambda k, t, b, lo, hi, nu: (hi[k], 0, 0)),
                      pl.BlockSpec((None, D_EXPERT, D_MODEL), lambda k, t, b, lo, hi, nu: (lo[k], 0, 0)),
                      pl.BlockSpec((None, D_EXPERT, D_MODEL), lambda k, t, b, lo, hi, nu: (hi[k], 0, 0))],
            out_specs=pl.BlockSpec((BUCKET_TILE, D_MODEL), tile_of)),
        out_shape=jax.ShapeDtypeStruct((n, D_MODEL), F32),
        compiler_params=_params(("arbitrary",)),
        name="moe_experts",
    )(item_tile, item_bucket, item_lo, item_hi, used, xs, w_gu, w_gu, w_down, w_down)

    return pl.pallas_call(
        functools.partial(_combine_kernel, nb=tb, final_norm=final_norm),
        grid_spec=pltpu.PrefetchScalarGridSpec(
            num_scalar_prefetch=1, grid=(n // MOVE_TILE,),
            in_specs=[row, modspec, vec, any_spec], out_specs=row,
            scratch_shapes=[pltpu.VMEM((2, MOVE_TILE, D_MODEL), F32), pltpu.SemaphoreType.DMA((2,))]),
        out_shape=jax.ShapeDtypeStruct((n, D_MODEL), F32),
        compiler_params=move_params,
        name="moe_combine",
    )(pos, x, mod, final_g, ys)


def _rope(x, cos, sin_lo, sin_hi):
    return x * cos + pltpu.roll(x, LANES - ROPE_DIM // 2, axis=1) * sin_lo + pltpu.roll(x, ROPE_DIM // 2, axis=1) * sin_hi


def _qkv_kernel(x_ref, mod_ref, kvmod_ref, ngq_ref, ngkv_ref, wq_ref, bq_ref, wkv_ref, bkv_ref,
                cos_ref, slo_ref, shi_ref, q_ref, k_ref, v_ref):
    n = _rms(x_ref[...])
    shift = mod_ref[:, 0:D_MODEL].reshape(1, 1, D_MODEL)
    scale = mod_ref[:, D_MODEL:2 * D_MODEL].reshape(1, 1, D_MODEL)
    hq = _modulate(n, ngq_ref[...], shift, scale, 1)
    kshift = kvmod_ref[:, 0:D_MODEL].reshape(1, 1, D_MODEL)
    kscale = kvmod_ref[:, D_MODEL:2 * D_MODEL].reshape(1, 1, D_MODEL)
    hkv = _modulate(n, ngkv_ref[...], kshift, kscale, 1)
    cos, slo, shi = cos_ref[...], slo_ref[...], shi_ref[...]
    q = _dot(hq.astype(BF16), wq_ref[...]) + bq_ref[...]
    for p in range(PAIRS):
        cols = slice(p * LANES, (p + 1) * LANES)
        q_ref[:, cols] = (_rope(q[:, cols], cos, slo, shi) * ATT_SCALE).astype(q_ref.dtype)
    kv = _dot(hkv.astype(BF16), wkv_ref[...]) + bkv_ref[...]
    for p in range(KV_W // LANES):
        cols = slice(p * LANES, (p + 1) * LANES)
        k_ref[:, cols] = _rope(kv[:, cols], cos, slo, shi)
    v_ref[...] = kv[:, KV_W:]


def _qkv_proj(x, mod, kvmod, w, tables, nb, s, tm):
    n = nb * s
    nt = s // tm
    row_spec = pl.BlockSpec((tm, D_MODEL), lambda b, j: (b * nt + j, 0))
    kv_spec = pl.BlockSpec((tm, KV_W), lambda b, j: (b * nt + j, 0))
    tab_spec = pl.BlockSpec((tm, LANES), lambda b, j: (j, 0))
    vec = _const_spec((1, D_MODEL))
    return pl.pallas_call(
        _qkv_kernel,
        grid=(nb, nt),
        in_specs=[row_spec,
                  pl.BlockSpec((None, 1, 6 * D_MODEL), lambda b, j: (b, 0, 0)),
                  pl.BlockSpec((None, 1, 2 * D_MODEL), lambda b, j: (b, 0, 0)),
                  vec, vec, _const_spec((D_MODEL, D_MODEL)), vec,
                  _const_spec((D_MODEL, 2 * KV_W)), _const_spec((1, 2 * KV_W)),
                  tab_spec, tab_spec, tab_spec],
        out_specs=[row_spec, kv_spec, kv_spec],
        out_shape=[jax.ShapeDtypeStruct((n, D_MODEL), BF16),
                   jax.ShapeDtypeStruct((n, KV_W), F32), jax.ShapeDtypeStruct((n, KV_W), F32)],
        compiler_params=_params(("parallel", "parallel")),
        name="qkv_proj",
    )(x, mod, kvmod, w["norm_gq"], w["norm_gkv"], w["w_q"], w["b_q"], w["w_kv"], w["b_kv"], *tables)


def _attn_kernel(sink_ref, q_ref, k0_ref, k1_ref, k2_ref, v0_ref, v1_ref, v2_ref, o_ref, *, banded):
    n = pl.program_id(1)
    seqs = range(q_ref.shape[0])
    k_all = [jnp.concatenate([k0_ref[i], k1_ref[i], k2_ref[i]], axis=0) for i in seqs]
    v_all = [jnp.concatenate([v0_ref[i], v1_ref[i], v2_ref[i]], axis=0) for i in seqs]
    lane = lax.broadcasted_iota(jnp.int32, (1, LANES), 1)
    lo = lane < HEAD
    col = lax.broadcasted_iota(jnp.int32, (1, 3 * CHUNK), 1)
    if banded:
        visible = (col >= 2 * CHUNK) | ((col >= CHUNK) & (n >= 1)) | (n >= 2)

    def halves(t, kv_head):
        blk = t[:, (kv_head // 2) * LANES:(kv_head // 2 + 1) * LANES]
        if kv_head % 2 == 0:
            t_lo = jnp.where(lo, blk, 0.0)
            t_hi = pltpu.roll(t_lo, HEAD, axis=1)
        else:
            t_hi = jnp.where(lo, 0.0, blk)
            t_lo = pltpu.roll(t_hi, HEAD, axis=1)
        return t_lo.astype(BF16), t_hi.astype(BF16)

    first_pair = lax.broadcasted_iota(jnp.int32, (2 * CHUNK, 1), 0) < CHUNK

    def probs(s, head_a, head_b):
        if banded:
            s = jnp.where(visible, s, NEG_INF)
        sink = jnp.where(first_pair, sink_ref[head_a], sink_ref[head_b])
        m = jnp.maximum(jnp.max(s, axis=-1, keepdims=True), sink)
        e = jnp.exp(s - m)
        return (e / (jnp.sum(e, axis=-1, keepdims=True) + jnp.exp(sink - m))).astype(BF16)

    units = [(i, g) for i in seqs for g in range(N_KV)]
    scores, values = [], []
    for i, g in units:
        k_lo, k_hi = halves(k_all[i], g)
        values.append(halves(v_all[i], g))
        q2 = jnp.concatenate([q_ref[i, :, (2 * g + j) * LANES:(2 * g + j + 1) * LANES] for j in range(2)], axis=0)
        scores.append((_dot_nt(q2, k_lo), _dot_nt(q2, k_hi)))
    for (i, g), (s_lo, s_hi), (v_lo, v_hi) in zip(units, scores, values):
        o = _dot(probs(s_lo, 4 * g, 4 * g + 2), v_lo) + _dot(probs(s_hi, 4 * g + 1, 4 * g + 3), v_hi)
        for j in range(2):
            pair = 2 * g + j
            o_ref[i, :, pair * LANES:(pair + 1) * LANES] = o[j * CHUNK:(j + 1) * CHUNK].astype(o_ref.dtype)


def _attention(q, kv_arrays, kv_chunks, sinks, nb, s, banded):
    nseq = math.gcd(nb, ATTN_SEQS)
    q_spec = pl.BlockSpec((nseq, CHUNK, D_MODEL), lambda b, c: (b, c, 0))
    kv_specs = [pl.BlockSpec((nseq, CHUNK, KV_W), lambda b, c, f=f: (b, f(c), 0)) for f in kv_chunks]
    return pl.pallas_call(
        functools.partial(_attn_kernel, banded=banded),
        grid=(nb // nseq, s // CHUNK),
        in_specs=[pl.BlockSpec(memory_space=pltpu.SMEM), q_spec] + kv_specs,
        out_specs=q_spec,
        out_shape=jax.ShapeDtypeStruct((nb, s, D_MODEL), BF16),
        compiler_params=_params(("parallel", "parallel")),
        name="attention",
    )(sinks, q, *kv_arrays)


def _pad_cols(w, n):
    return jnp.pad(w, ((0, 0), (0, n - w.shape[1])))


def _pad_rows(w, n):
    return jnp.pad(w, ((0, n - w.shape[0]), (0, 0)))


def _rope_tables(pos):
    half = ROPE_DIM // 2
    inv = jnp.power(jnp.float32(ROPE_THETA), -jnp.arange(half, dtype=F32) * (2.0 / ROPE_DIM))
    ang = pos[:, None] * inv[None, :]
    cos, sin = jnp.cos(ang), jnp.sin(ang)
    rest = HEAD - ROPE_DIM
    ones = jnp.ones((pos.shape[0], rest), F32)
    zeros = jnp.zeros((pos.shape[0], rest), F32)
    z8 = jnp.zeros_like(sin)
    per_head = (jnp.concatenate([cos, cos, ones], axis=1),
                jnp.concatenate([-sin, z8, zeros], axis=1),
                jnp.concatenate([z8, sin, zeros], axis=1))
    return tuple(jnp.tile(t, (1, LANES // HEAD)) for t in per_head)


def _state_to_pairs(state):
    nb = state.shape[0]
    st = state.astype(F32).reshape(nb, PAIRS, 2, HEAD, HEAD)
    z = jnp.zeros_like(st[:, :, 0])
    top = jnp.concatenate([st[:, :, 0], z], axis=-1)
    bot = jnp.concatenate([z, st[:, :, 1]], axis=-1)
    return jnp.concatenate([top, bot], axis=-2)


def _pairs_to_state(st):
    nb = st.shape[0]
    s0 = st[:, :, :HEAD, :HEAD]
    s1 = st[:, :, HEAD:, HEAD:]
    return jnp.stack([s0, s1], axis=2).reshape(nb, 2 * PAIRS, HEAD, HEAD)


def _trunk(x, mods, kvmod, pos, prev_x, prev_wkv, past_k, past_v, w, nb, s, expert_w16=None):
    n = nb * s
    made_w16 = {}
    x = x.reshape(n, D_MODEL)
    mod0, mod1 = mods[0][:, None, :], mods[1][:, None, :]
    kvmod = kvmod[:, None, :]

    tm = min(s, 512)
    r, lw, k, v, a, b, g, last_x = _rwkv_proj(x, mod0, prev_x[:, None, :], w["rw"], nb, s, tm)
    seqs = lambda t: t.reshape(nb, s, D_MODEL)
    z, st = _wkv(*(seqs(t) for t in (r, lw, k, v, a, b, g)), _state_to_pairs(prev_wkv), w["rw"], nb, s)
    z = z.reshape(n, D_MODEL)

    def residual_moe(x, z, mixer, mod, l):
        x = _out_proj(x, z, mod[:, :, 2 * D_MODEL:3 * D_MODEL], mixer["w_o"], mixer["b_o"], nb, s, min(n, 1024))
        route = (mod, w["norm_g"][l][1:2], w["moe_route"][l])
        if n >= SPARSE_MIN_TOKENS:
            if expert_w16 is not None and l in expert_w16:
                gu16, dn16 = expert_w16[l]
            else:
                gu16, dn16 = w["moe_gu"][l].astype(BF16), w["moe_down"][l].astype(BF16)
            return _moe_sparse(x, *route, gu16, dn16, w["final_g"], nb, s, l == 1)
        y, *made_w16[l] = _moe(x, *route, w["moe_gu"], w["moe_down"], w["final_g"], nb, s, l, min(n, 1024), l == 1)
        return y

    x = residual_moe(x, z, w["rw"], mod0, 0)

    tm = min(s, 512)
    q, k_new, v_new = _qkv_proj(x, mod1, kvmod, w["at"], _rope_tables(pos), nb, s, tm)
    k_seq, v_seq = k_new.reshape(nb, s, KV_W), v_new.reshape(nb, s, KV_W)
    if past_k is None:
        arrays = [k_seq] * 3 + [v_seq] * 3
        chunks = [lambda c, d=d: jnp.maximum(c - d, 0) for d in (2, 1, 0)] * 2
    else:
        pk = past_k.astype(F32).reshape(nb, 2 * CHUNK, KV_W)
        pv = past_v.astype(F32).reshape(nb, 2 * CHUNK, KV_W)
        arrays = [pk, pk, k_seq, pv, pv, v_seq]
        chunks = [lambda c: 0, lambda c: 1, lambda c: 0] * 2
    o = _attention(q.reshape(nb, s, D_MODEL), arrays, chunks, w["at"]["sinks"], nb, s, past_k is None)
    y = residual_moe(x, o.reshape(n, D_MODEL), w["at"], mod1, 1)

    return (y.reshape(nb, s, D_MODEL), last_x.reshape(1, nb, D_MODEL), _pairs_to_state(st)[None], k_seq, v_seq,
            made_w16)


def kernel(x_prompt, x_sample, state_shift, state_wkv, cache_k, cache_v, c_prompt, c_sample, ada_w, ada_b, norm_g, rw_mu, rw_w_rkv, rw_w0, rw_w1, rw_w2, rw_a0, rw_a1, rw_a2, rw_g1, rw_g2, rw_k_k, rw_k_a, rw_r_k, rw_lnx_w, rw_lnx_b, rw_w_o, kv_ada_w, kv_ada_b, kv_norm_g, w_kv, b_kv, at_w_q, at_b_q, at_sinks, at_w_o, at_b_o, moe_w_group, moe_w_expert, moe_w_gu, moe_w_down, final_norm_g):
    bp, sp, _ = x_prompt.shape
    bs, ss, _ = x_sample.shape
    row = lambda t: t.reshape(1, -1).astype(F32)

    c_all = jnp.concatenate([c_prompt, c_sample], axis=0)
    mods = _cond_linear(c_all, ada_w, ada_b[:, None, :])
    kvmods = _cond_linear(c_all, kv_ada_w[None], kv_ada_b[None, None, :])[0]

    lora_pad, gate_pad = LANES, 2 * LANES
    w = {
        "norm_g": norm_g,
        "final_g": row(final_norm_g),
        "rw": {
            "norm_g": norm_g[0, 0:1], "mu": rw_mu[0], "w_rkv": rw_w_rkv[0].astype(BF16),
            "w0": row(rw_w0[0]), "w1": _pad_cols(rw_w1[0], lora_pad).astype(BF16),
            "w2": _pad_rows(rw_w2[0], lora_pad).astype(BF16),
            "a0": row(rw_a0[0]), "a1": _pad_cols(rw_a1[0], lora_pad).astype(BF16),
            "a2": _pad_rows(rw_a2[0], lora_pad).astype(BF16),
            "g1": _pad_cols(rw_g1[0], gate_pad).astype(BF16), "g2": _pad_rows(rw_g2[0], gate_pad).astype(BF16),
            "k_k": row(rw_k_k[0]), "k_a": row(rw_k_a[0]), "r_k": row(rw_r_k[0]),
            "lnx_w": row(rw_lnx_w[0]), "lnx_b": row(rw_lnx_b[0]),
            "w_o": rw_w_o[0].astype(BF16), "b_o": jnp.zeros((1, D_MODEL), F32),
        },
        "at": {
            "norm_gq": norm_g[1, 0:1], "norm_gkv": row(kv_norm_g),
            "w_q": at_w_q[0].astype(BF16), "b_q": row(at_b_q[0]),
            "w_kv": w_kv.astype(BF16), "b_kv": row(b_kv),
            "sinks": at_sinks[0].astype(F32),
            "w_o": at_w_o[0].astype(BF16), "b_o": row(at_b_o[0]),
        },
        "moe_route": jnp.pad(jnp.concatenate([moe_w_group, moe_w_expert], axis=-1),
                             ((0, 0), (0, 0), (0, LANES - N_GROUPS - N_EXPERTS))),
        "moe_gu": moe_w_gu,
        "moe_down": moe_w_down,
    }

    pos_p = jnp.arange(sp, dtype=F32)
    pos_s = PAST_LEN + jnp.arange(ss, dtype=F32)
    zero_x = jnp.zeros((bp, D_MODEL), x_prompt.dtype)
    zero_wkv = jnp.zeros((bp,) + state_wkv.shape[2:], state_wkv.dtype)
    y_s, s_shift, s_wkv, s_k, s_v, w16 = _trunk(x_sample, mods[:, bp:], kvmods[bp:], pos_s, state_shift[0],
                                                state_wkv[0], cache_k, cache_v, w, bs, ss)
    y_p, p_shift, p_wkv, p_k, p_v, _ = _trunk(x_prompt, mods[:, :bp], kvmods[:bp], pos_p, zero_x, zero_wkv,
                                              None, None, w, bp, sp, w16)
    keep = min(2 * CHUNK, sp)
    heads = lambda t: t.reshape(t.shape[0], t.shape[1], N_KV, HEAD)
    return (y_p, y_s, p_shift, p_wkv.astype(state_wkv.dtype), heads(p_k[:, sp - keep:]), heads(p_v[:, sp - keep:]),
            s_shift, s_wkv.astype(state_wkv.dtype), heads(s_k), heads(s_v))
```

```python
import functools
import math

import jax
import jax.numpy as jnp
from jax import lax
from jax.experimental import pallas as pl
from jax.experimental.pallas import tpu as pltpu

F32 = jnp.float32
BF16 = jnp.bfloat16

D_MODEL = 1024
LANES = 128
HEAD = 64
PAIRS = D_MODEL // LANES
CHUNK = 64
WKV_SEQS = 4
PROJ_TILE = 512
ATTN_SEQS = 2
PAST_LEN = 4096
N_KV = 4
KV_W = N_KV * HEAD
ROPE_DIM = 16
ROPE_THETA = 500000.0
ATT_SCALE = HEAD ** -0.5
N_GROUPS = 4
EXP_PER_GROUP = 8
N_EXPERTS = N_GROUPS * EXP_PER_GROUP
D_EXPERT = D_MODEL // 4
DENSE_GROUP = 2
ROUTE_LANE0 = N_GROUPS
RMS_EPS = 1e-6
GN_EPS = 64e-5
NEG_INF = -1e30
VMEM_LIMIT = 56 * 1024 * 1024


def _params(sem):
    return pltpu.CompilerParams(dimension_semantics=sem, vmem_limit_bytes=VMEM_LIMIT)


def _dot(a, b):
    return jnp.dot(a, b, preferred_element_type=F32)


def _dot_nt(a, b):
    return lax.dot_general(a, b, (((1,), (1,)), ((), ())), preferred_element_type=F32)


def _dot_tn(a, b):
    return lax.dot_general(a, b, (((0,), (0,)), ((), ())), preferred_element_type=F32)


def _split2(x):
    hi = x.astype(BF16)
    lo = (x - hi.astype(F32)).astype(BF16)
    return hi, lo


def _dot_x3(a, b, dot=_dot):
    ah, al = _split2(a)
    bh, bl = _split2(b)
    return dot(ah, bh) + dot(ah, bl) + dot(al, bh)


def _sigmoid(x):
    return 1.0 / (1.0 + jnp.exp(-x))


def _rms(x):
    return x * lax.rsqrt(jnp.mean(x * x, axis=-1, keepdims=True) + RMS_EPS)


def _modulate(n, gain, shift, scale, nb):
    rows = n.shape[0]
    h = (n * gain).reshape(nb, rows // nb, D_MODEL)
    return (h * (1.0 + scale) + shift).reshape(rows, D_MODEL)


def _const_spec(shape):
    nd = len(shape)
    return pl.BlockSpec(shape, lambda *_: (0,) * nd)


def _cond_kernel(c_ref, w_ref, b_ref, o_ref):
    c = c_ref[...]
    cs = (c * _sigmoid(c)).astype(BF16)
    o_ref[...] = _dot(cs, w_ref[...].astype(BF16)) + b_ref[...]


def _cond_linear(c, w, b, tn=512):
    nl, _, n = w.shape
    m = c.shape[0]
    return pl.pallas_call(
        _cond_kernel,
        grid=(nl, n // tn),
        in_specs=[
            pl.BlockSpec((m, D_MODEL), lambda l, j: (0, 0)),
            pl.BlockSpec((None, D_MODEL, tn), lambda l, j: (l, 0, j)),
            pl.BlockSpec((None, 1, tn), lambda l, j: (l, 0, j)),
        ],
        out_specs=pl.BlockSpec((None, m, tn), lambda l, j: (l, 0, j)),
        out_shape=jax.ShapeDtypeStruct((nl, m, n), F32),
        compiler_params=_params(("parallel", "parallel")),
        name="cond_linear",
    )(c, w, b)


def _rwkv_proj_kernel(x_ref, mod_ref, prev_ref, ng_ref, mu_ref, wrkv_ref, w0_ref, w1_ref, w2_ref,
                      a0_ref, a1_ref, a2_ref, g1_ref, g2_ref, kk_ref, ka_ref,
                      r_ref, lw_ref, k_ref, v_ref, a_ref, b_ref, g_ref, last_ref, carry_ref):
    tm = x_ref.shape[0]
    tb = mod_ref.shape[0]
    rps = tm // tb

    @pl.when(pl.program_id(1) == 0)
    def _():
        carry_ref[...] = prev_ref[...]

    h = _modulate(_rms(x_ref[...]), ng_ref[...], mod_ref[:, :, 0:D_MODEL], mod_ref[:, :, D_MODEL:2 * D_MODEL], tb)
    seqs = lambda t: t.reshape(tb, rps, D_MODEL)
    first = lax.broadcasted_iota(jnp.int32, (tb, rps, 1), 1) == 0
    shifted = jnp.where(first, carry_ref[...], seqs(pltpu.roll(h, 1, axis=0))).reshape(tm, D_MODEL)
    carry_ref[...] = seqs(h)[:, rps - 1:rps, :]
    last_ref[...] = seqs(h)[:, rps - 1:rps, :]
    dx = shifted - h

    def mix(n):
        return (h + dx * mu_ref[n:n + 1, :]).astype(BF16)

    r = _dot(mix(0), wrkv_ref[0])
    k = _dot(mix(1), wrkv_ref[1])
    v = _dot(mix(2), wrkv_ref[2])
    ww = _dot(jnp.tanh(_dot(mix(3), w1_ref[...])).astype(BF16), w2_ref[...])
    z = -(w0_ref[...] + ww)
    softplus = jnp.maximum(z, 0.0) + jnp.log(1.0 + jnp.exp(-jnp.abs(z)))
    lw = -jnp.exp(-softplus - 0.5)
    asig = _sigmoid(a0_ref[...] + _dot(_dot(mix(4), a1_ref[...]).astype(BF16), a2_ref[...]))
    g = _dot(_sigmoid(_dot(mix(5), g1_ref[...])).astype(BF16), g2_ref[...])

    kk = k * kk_ref[...]
    kk2 = kk * kk
    lo_cols = lax.broadcasted_iota(jnp.int32, (tm, LANES), 1) < HEAD

    def head_sums(t):
        lo_sum = jnp.sum(jnp.where(lo_cols, t, 0.0), axis=-1, keepdims=True)
        hi_sum = jnp.sum(jnp.where(lo_cols, 0.0, t), axis=-1, keepdims=True)
        return jnp.where(lo_cols, lo_sum, hi_sum)

    ss = jnp.concatenate([head_sums(kk2[:, p * LANES:(p + 1) * LANES]) for p in range(PAIRS)], axis=1)
    kkn = kk * lax.rsqrt(jnp.maximum(ss, 1e-24))

    r_ref[...] = r.astype(r_ref.dtype)
    lw_ref[...] = lw
    k_ref[...] = (k * (1.0 + (asig - 1.0) * ka_ref[...])).astype(k_ref.dtype)
    v_ref[...] = v.astype(v_ref.dtype)
    a_ref[...] = (-kkn).astype(a_ref.dtype)
    b_ref[...] = (kkn * asig).astype(b_ref.dtype)
    g_ref[...] = g.astype(g_ref.dtype)


def _rwkv_proj(x, mod, prev_x, w, nb, s, tm):
    n = nb * s
    tb, nt = max(tm // s, 1), max(s // tm, 1)
    row_spec = pl.BlockSpec((tm, D_MODEL), lambda b, j: (b * nt + j, 0))
    per_seq = lambda width: pl.BlockSpec((tb, 1, width), lambda b, j: (b, 0, 0))
    vec = _const_spec((1, D_MODEL))
    lora_w, lora_g = w["w1"].shape[1], w["g1"].shape[1]
    in_specs = [
        row_spec,
        per_seq(6 * D_MODEL),
        per_seq(D_MODEL),
        vec, _const_spec((6, D_MODEL)), _const_spec((3, D_MODEL, D_MODEL)),
        vec, _const_spec((D_MODEL, lora_w)), _const_spec((lora_w, D_MODEL)),
        vec, _const_spec((D_MODEL, lora_w)), _const_spec((lora_w, D_MODEL)),
        _const_spec((D_MODEL, lora_g)), _const_spec((lora_g, D_MODEL)),
        vec, vec,
    ]
    act = lambda dt: jax.ShapeDtypeStruct((n, D_MODEL), dt)
    out_types = [act(BF16), act(F32)] + [act(BF16)] * 5
    outs = pl.pallas_call(
        _rwkv_proj_kernel,
        grid=(nb // tb, nt),
        in_specs=in_specs,
        out_specs=[row_spec] * 7 + [per_seq(D_MODEL)],
        out_shape=out_types + [jax.ShapeDtypeStruct((nb, 1, D_MODEL), F32)],
        scratch_shapes=[pltpu.VMEM((tb, 1, D_MODEL), F32)],
        compiler_params=_params(("parallel", "arbitrary")),
        name="rwkv_proj",
    )(x, mod, prev_x, w["norm_g"], w["mu"], w["w_rkv"], w["w0"], w["w1"], w["w2"],
      w["a0"], w["a1"], w["a2"], w["g1"], w["g2"], w["k_k"], w["k_a"])
    return outs


def _split3(x):
    hi = x.astype(BF16)
    rem = x - hi.astype(F32)
    mid = rem.astype(BF16)
    return hi, mid, (rem - mid.astype(F32)).astype(BF16)


def _wkv_kernel(r_ref, lw_ref, k_ref, v_ref, a_ref, b_ref, g_ref, s0_ref, rk_ref, lnw_ref, lnb_ref,
                z_ref, st_ref):
    @pl.when(pl.program_id(1) == 0)
    def _():
        st_ref[...] = s0_ref[...]

    nseq = r_ref.shape[0]
    wide = lambda ref: jnp.concatenate([ref[i] for i in range(nseq)], axis=1)
    per_seq = lambda ref: jnp.concatenate([ref[...]] * nseq, axis=1)
    b16 = lambda t: t.astype(BF16)
    pairs = range(nseq * PAIRS)
    blk = lambda t, p: t[:, p * LANES:(p + 1) * LANES]
    rows = lambda ts: jnp.concatenate(ts, axis=0)
    lo_cols = lax.broadcasted_iota(jnp.int32, (CHUNK, LANES), 1) < HEAD
    rr = lax.broadcasted_iota(jnp.int32, (LANES, LANES), 0)
    cc = lax.broadcasted_iota(jnp.int32, (LANES, LANES), 1)
    head0 = lambda t: jnp.where(lo_cols, t, 0.0)
    head1 = lambda t: jnp.where(lo_cols, 0.0, t)
    blockdiag = lambda q: rows([head0(q), head1(q)])

    r, lw, k, v, a, b = (wide(t).astype(F32) for t in (r_ref, lw_ref, k_ref, v_ref, a_ref, b_ref))
    ti = lax.broadcasted_iota(jnp.int32, (CHUNK, CHUNK), 0)
    tj = lax.broadcasted_iota(jnp.int32, (CHUNK, CHUNK), 1)
    tri = (tj <= ti).astype(BF16)
    cum = sum(_dot(tri, t) for t in _split3(lw))
    cl = cum[CHUNK - 1:CHUNK, :]
    e_neg = jnp.exp(-cum)
    e_end = jnp.exp(cl - cum)
    e_cl = jnp.exp(cl)
    at = b16(a * jnp.exp(cum - lw))
    rt = b16(r * jnp.exp(cum))
    bt = b * e_neg
    kt = k * e_neg
    bkh = rows([b16(b * e_end), b16(k * e_end)])
    v16 = b16(v)

    t_idx = rr & (CHUNK - 1)
    j_idx = cc & (CHUNK - 1)
    keep = (j_idx < t_idx) | ((rr >= CHUNK) & (j_idx == t_idx))
    g0, g1 = [], []
    for p in pairs:
        btp, ktp = blk(bt, p), blk(kt, p)
        w = b16(rows([head0(btp), head0(ktp), head1(ktp), head1(btp)]))
        g = _dot_nt(rows([blk(at, p), blk(rt, p)]), w)
        g0.append(jnp.where(keep, g[:, :LANES], 0.0))
        g1.append(jnp.where(keep, g[:, LANES:], 0.0))

    eye2 = jnp.where((lax.broadcasted_iota(jnp.int32, (CHUNK, LANES), 1) & (CHUNK - 1))
                     == lax.broadcasted_iota(jnp.int32, (CHUNK, LANES), 0), 1.0, 0.0)
    pc = [jnp.where(lo_cols, g0[p][:CHUNK], g1[p][:CHUNK]) for p in pairs]
    tc = [eye2 + pc[p] for p in pairs]
    q = [_dot(b16(pc[p]), b16(blockdiag(pc[p]))) for p in pairs]
    for _ in range(4):
        res = [_dot(b16(rows([tc[p], q[p]])), b16(blockdiag(q[p]))) for p in pairs]
        tc = [tc[p] + res[p][:CHUNK] for p in pairs]
        q = [res[p][CHUNK:] for p in pairs]
    tc = [b16(tc[p] + _dot(b16(tc[p]), b16(blockdiag(q[p])))) for p in pairs]

    akv = []
    for p in pairs:
        w_ak = b16(jnp.where(lo_cols, g1[p][:CHUNK], g0[p][:CHUNK]))
        vp = blk(v16, p)
        akv.append(_dot(w_ak, rows([head1(vp), head0(vp)])))

    s = [st_ref[p // PAIRS, p % PAIRS] for p in pairs]
    s16 = [b16(s[p]) for p in pairs]
    rhs = [_dot_nt(blk(at, p), s16[p]) + akv[p] for p in pairs]
    u = [_dot(tc[p], b16(rows([head0(rhs[p]), head1(rhs[p])]))) for p in pairs]
    u16 = [b16(u[p]) for p in pairs]

    ys = []
    bd_mask = (rr < HEAD) == (cc < HEAD)
    for p in pairs:
        vp = blk(v16, p)
        rbk = b16(jnp.concatenate([g0[p][CHUNK:], g1[p][CHUNK:]], axis=1))
        uv = rows([head0(u16[p]), head0(vp), head1(vp), head1(u16[p])])
        ys.append(_dot_nt(blk(rt, p), s16[p]) + _dot(rbk, uv))
        fresh = _dot_tn(rows([u16[p], vp]), blk(bkh, p))
        st_ref[p // PAIRS, p % PAIRS] = s[p] * blk(e_cl, p) + jnp.where(bd_mask, fresh, 0.0)

    def head_sums(t):
        lo_sum = jnp.sum(head0(t), axis=-1, keepdims=True)
        hi_sum = jnp.sum(head1(t), axis=-1, keepdims=True)
        return jnp.where(lo_cols, lo_sum, hi_sum)

    rkk = r * k * per_seq(rk_ref)
    for p in pairs:
        seq, cols = p // PAIRS, slice((p % PAIRS) * LANES, (p % PAIRS + 1) * LANES)
        yc = ys[p] - head_sums(ys[p]) * (1.0 / HEAD)
        yn = yc * lax.rsqrt(head_sums(yc * yc) * (1.0 / HEAD) + GN_EPS)
        zp = (yn * lnw_ref[:, cols] + lnb_ref[:, cols] + head_sums(blk(rkk, p)) * blk(v, p)) * g_ref[seq, :, cols].astype(F32)
        z_ref[seq, :, cols] = zp.astype(z_ref.dtype)


def _wkv(r, lw, k, v, a, b, g, st0, w, nb, s):
    nseq = math.gcd(nb, WKV_SEQS)
    blk = pl.BlockSpec((nseq, CHUNK, D_MODEL), lambda bi, c: (bi, c, 0))
    vec = _const_spec((1, D_MODEL))
    st_spec = pl.BlockSpec((nseq, PAIRS, LANES, LANES), lambda bi, c: (bi, 0, 0, 0))
    return pl.pallas_call(
        _wkv_kernel,
        grid=(nb // nseq, s // CHUNK),
        in_specs=[blk] * 7 + [st_spec, vec, vec, vec],
        out_specs=[blk, st_spec],
        out_shape=[jax.ShapeDtypeStruct((nb, s, D_MODEL), BF16),
                   jax.ShapeDtypeStruct((nb, PAIRS, LANES, LANES), F32)],
        compiler_params=_params(("parallel", "arbitrary")),
        name="wkv",
    )(r, lw, k, v, a, b, g, st0, w["r_k"], w["lnx_w"], w["lnx_b"])


def _out_proj_kernel(x_ref, z_ref, gate_ref, w_ref, bias_ref, o_ref, *, nb):
    rows = x_ref.shape[0]
    y = _dot(z_ref[...], w_ref[...]) + bias_ref[...]
    y = (y.reshape(nb, rows // nb, D_MODEL) * gate_ref[...]).reshape(rows, D_MODEL)
    o_ref[...] = x_ref[...] + y


def _out_proj(x, z, gate, w, bias, nb, s, tm):
    n = nb * s
    tb = max(tm // s, 1)
    per_seq = max(s // tm, 1)
    row_spec = pl.BlockSpec((tm, D_MODEL), lambda i: (i, 0))
    return pl.pallas_call(
        functools.partial(_out_proj_kernel, nb=tb),
        grid=(n // tm,),
        in_specs=[row_spec, row_spec,
                  pl.BlockSpec((tb, 1, D_MODEL), lambda i: (i // per_seq, 0, 0)),
                  _const_spec((D_MODEL, D_MODEL)), _const_spec((1, D_MODEL))],
        out_specs=row_spec,
        out_shape=jax.ShapeDtypeStruct((n, D_MODEL), F32),
        compiler_params=_params(("parallel",)),
        name="out_proj",
    )(x, z, gate, w, bias)


def _top2(logits):
    lane = lax.broadcasted_iota(jnp.int32, logits.shape, 1).astype(F32)
    first_of = lambda hit: jnp.min(jnp.where(hit, lane, float(LANES)), axis=-1, keepdims=True)
    lg = jnp.where(lane < N_GROUPS, logits, -jnp.inf)
    gmax = jnp.max(lg, axis=-1, keepdims=True)
    gi = first_of(lg == gmax)
    gp = 1.0 / jnp.sum(jnp.exp(lg - gmax), axis=-1, keepdims=True)
    first = ROUTE_LANE0 + gi * EXP_PER_GROUP
    le = jnp.where((lane >= first) & (lane < first + EXP_PER_GROUP), logits, -jnp.inf)
    top1 = jnp.max(le, axis=-1, keepdims=True)
    i1 = first_of(le == top1)
    le2 = jnp.where(lane == i1, -jnp.inf, le)
    top2 = jnp.max(le2, axis=-1, keepdims=True)
    i2 = first_of(le2 == top2)
    e2 = jnp.exp(top2 - top1)
    w1 = gp / (1.0 + e2)
    return i1, i2, w1, w1 * e2


def _route(logits):
    lane = lax.broadcasted_iota(jnp.int32, logits.shape, 1).astype(F32)
    i1, i2, w1, w2 = _top2(logits)
    return jnp.where(lane == i1, w1, 0.0) + jnp.where(lane == i2, w2, 0.0)


def _moe_kernel(x_ref, mod_ref, ng_ref, wr_ref, wgu_ref, wdn_ref, fg_ref, o_ref, gu16_ref, dn16_ref,
                h_scr, cw_scr, acc_scr, *, nb, final_norm):
    e = pl.program_id(1)
    rows = x_ref.shape[0]

    @pl.when(e == 0)
    def _():
        shift = mod_ref[:, :, 3 * D_MODEL:4 * D_MODEL]
        scale = mod_ref[:, :, 4 * D_MODEL:5 * D_MODEL]
        h = _modulate(_rms(x_ref[...]), ng_ref[...], shift, scale, nb)
        h_scr[...] = h.astype(BF16)
        cw_scr[...] = _route(_dot_x3(h, wr_ref[...]))
        acc_scr[...] = jnp.zeros_like(acc_scr)

    lane = lax.broadcasted_iota(jnp.int32, (rows, LANES), 1)
    gu16_ref[...] = wgu_ref[...].astype(BF16)
    dn16_ref[...] = wdn_ref[...].astype(BF16)
    hid = []
    for j in range(DENSE_GROUP):
        cw = jnp.sum(jnp.where(lane == ROUTE_LANE0 + e * DENSE_GROUP + j, cw_scr[...], 0.0), axis=-1, keepdims=True)
        gu = _dot(h_scr[...], gu16_ref[j])
        gate = gu[:, :D_EXPERT]
        hid.append((gate * _sigmoid(gate) * gu[:, D_EXPERT:] * cw).astype(BF16))
    acc_scr[...] += _dot(jnp.concatenate(hid, axis=1), dn16_ref[...].reshape(DENSE_GROUP * D_EXPERT, D_MODEL))

    @pl.when(e == pl.num_programs(1) - 1)
    def _():
        g2 = mod_ref[:, :, 5 * D_MODEL:6 * D_MODEL]
        y = (acc_scr[...].reshape(nb, rows // nb, D_MODEL) * g2).reshape(rows, D_MODEL)
        out = x_ref[...] + y
        if final_norm:
            out = _rms(out) * fg_ref[...]
        o_ref[...] = out


def _moe(x, mod, norm_g, w_route, w_gu, w_down, final_g, nb, s, layer, tm, final_norm):
    n = nb * s
    tb = max(tm // s, 1)
    per_seq = max(s // tm, 1)
    row_spec = pl.BlockSpec((tm, D_MODEL), lambda i, e: (i, 0))
    return pl.pallas_call(
        functools.partial(_moe_kernel, nb=tb, final_norm=final_norm),
        grid=(n // tm, N_EXPERTS // DENSE_GROUP),
        in_specs=[row_spec,
                  pl.BlockSpec((tb, 1, 6 * D_MODEL), lambda i, e: (i // per_seq, 0, 0)),
                  _const_spec((1, D_MODEL)), _const_spec((D_MODEL, LANES)),
                  pl.BlockSpec((None, DENSE_GROUP, D_MODEL, 2 * D_EXPERT), lambda i, e: (layer, e, 0, 0)),
                  pl.BlockSpec((None, DENSE_GROUP, D_EXPERT, D_MODEL), lambda i, e: (layer, e, 0, 0)),
                  _const_spec((1, D_MODEL))],
        out_specs=[row_spec,
                   pl.BlockSpec((DENSE_GROUP, D_MODEL, 2 * D_EXPERT), lambda i, e: (e, 0, 0)),
                   pl.BlockSpec((DENSE_GROUP, D_EXPERT, D_MODEL), lambda i, e: (e, 0, 0))],
        out_shape=[jax.ShapeDtypeStruct((n, D_MODEL), F32),
                   jax.ShapeDtypeStruct((N_EXPERTS, D_MODEL, 2 * D_EXPERT), BF16),
                   jax.ShapeDtypeStruct((N_EXPERTS, D_EXPERT, D_MODEL), BF16)],
        scratch_shapes=[pltpu.VMEM((tm, D_MODEL), BF16), pltpu.VMEM((tm, LANES), F32),
                        pltpu.VMEM((tm, D_MODEL), F32)],
        compiler_params=_params(("arbitrary", "arbitrary")),
        name="moe",
    )(x, mod, norm_g, w_route, w_gu, w_down, final_g)


PAIRS_PER_GROUP = EXP_PER_GROUP * (EXP_PER_GROUP - 1) // 2
N_BUCKETS = N_GROUPS * PAIRS_PER_GROUP
BUCKET_TILE = 256
ROW_W = D_MODEL + LANES
META_BUCKET, META_RANK, META_W_LO, META_W_HI = 0, 1, 2, 3
ROUTER_TILE = 1024
MOVE_TILE = 1024
DMA_UNROLL = 8
SPARSE_MIN_TOKENS = 4096


def _moe_norm(x, mod_ref, ng_ref, nb):
    shift = mod_ref[:, :, 3 * D_MODEL:4 * D_MODEL]
    scale = mod_ref[:, :, 4 * D_MODEL:5 * D_MODEL]
    return _modulate(_rms(x), ng_ref[...], shift, scale, nb)


def _router_kernel(x_ref, mod_ref, ng_ref, wr_ref, earlier_ref, meta_ref, cnt_ref, *, nb):
    @pl.when(pl.program_id(0) == 0)
    def _():
        cnt_ref[...] = jnp.zeros_like(cnt_ref)

    rows = x_ref.shape[0]
    h = _moe_norm(x_ref[...], mod_ref, ng_ref, nb)
    i1, i2, w1, w2 = _top2(_dot_x3(h, wr_ref[...]))
    lo = (jnp.minimum(i1, i2) - ROUTE_LANE0).astype(jnp.int32)
    hi = (jnp.maximum(i1, i2) - ROUTE_LANE0).astype(jnp.int32)
    first_is_lo = i1 < i2
    a = lo & (EXP_PER_GROUP - 1)
    b = hi & (EXP_PER_GROUP - 1)
    group = lo >> (EXP_PER_GROUP.bit_length() - 1)
    bucket = group * PAIRS_PER_GROUP + ((a * (2 * EXP_PER_GROUP - 1 - a)) >> 1) + (b - a - 1)

    lane = lax.broadcasted_iota(jnp.int32, (rows, LANES), 1)
    mine = lane == bucket
    onehot = jnp.where(mine, 1.0, 0.0)
    seen = cnt_ref[...]
    before = _dot(earlier_ref[...], onehot.astype(BF16)) + seen
    rank = jnp.sum(jnp.where(mine, before, 0.0), axis=-1, keepdims=True)
    cnt_ref[...] = seen + jnp.sum(onehot, axis=0, keepdims=True)

    meta = jnp.where(lane == META_BUCKET, bucket.astype(F32), 0.0)
    meta = jnp.where(lane == META_RANK, rank, meta)
    meta = jnp.where(lane == META_W_LO, jnp.where(first_is_lo, w1, w2), meta)
    meta_ref[...] = jnp.where(lane == META_W_HI, jnp.where(first_is_lo, w2, w1), meta)


def _positions_kernel(meta_ref, start_ref, pos_ref):
    meta = meta_ref[...]
    lane = lax.broadcasted_iota(jnp.int32, meta.shape, 1)
    column = lambda c: jnp.sum(jnp.where(lane == c, meta, 0.0), axis=-1, keepdims=True)
    value = jnp.where(lane == column(META_BUCKET).astype(jnp.int32), start_ref[...] + column(META_RANK), 0.0)
    ones = jnp.ones((8, LANES), BF16)
    pos = sum(_dot_nt(ones, piece) for piece in _split3(value))
    pos_ref[...] = pos[0:1, :].astype(jnp.int32)


def _dispatch_kernel(pos_ref, x_ref, mod_ref, ng_ref, meta_ref, xs_ref, buf, sem, *, nb):
    i = pl.program_id(0)
    rows = x_ref.shape[0]
    slot = i % 2
    buf[slot, :, 0:D_MODEL] = _moe_norm(x_ref[...], mod_ref, ng_ref, nb)
    buf[slot, :, D_MODEL:ROW_W] = meta_ref[...]

    def send(r, carry):
        pltpu.make_async_copy(buf.at[slot, pl.ds(r, 1), :],
                              xs_ref.at[pl.ds(pos_ref[i * rows + r], 1), :], sem.at[slot]).start()
        return carry

    lax.fori_loop(0, rows, send, 0, unroll=DMA_UNROLL)

    def wait_all(sl):
        pltpu.make_async_copy(buf.at[sl], xs_ref.at[pl.ds(0, rows), :], sem.at[sl]).wait()

    @pl.when(i >= 1)
    def _():
        wait_all(1 - slot)

    @pl.when(i == pl.num_programs(0) - 1)
    def _():
        wait_all(slot)


def _experts_kernel(tile_ref, bucket_ref, lo_ref, hi_ref, used_ref, xs_ref, gu_lo_ref, gu_hi_ref,
                    dn_lo_ref, dn_hi_ref, ys_ref):
    k = pl.program_id(0)

    @pl.when(k < used_ref[0])
    def _():
        rows = xs_ref.shape[0]
        x16 = xs_ref[:, 0:D_MODEL].astype(BF16)
        meta = xs_ref[:, D_MODEL:ROW_W]
        lane = lax.broadcasted_iota(jnp.int32, (rows, LANES), 1)
        column = lambda c: jnp.sum(jnp.where(lane == c, meta, 0.0), axis=-1, keepdims=True)
        member = column(META_BUCKET) == bucket_ref[k].astype(F32)

        def expert(gu_ref, dn_ref, col):
            gu = _dot(x16, gu_ref[...])
            gate = gu[:, :D_EXPERT]
            hid = gate * _sigmoid(gate) * gu[:, D_EXPERT:] * jnp.where(member, column(col), 0.0)
            return _dot(hid.astype(BF16), dn_ref[...])

        y = expert(gu_lo_ref, dn_lo_ref, META_W_LO) + expert(gu_hi_ref, dn_hi_ref, META_W_HI)
        new_tile = (k == 0) | (tile_ref[k] != tile_ref[jnp.maximum(k - 1, 0)])

        @pl.when(new_tile)
        def _():
            ys_ref[...] = y

        @pl.when(jnp.logical_not(new_tile))
        def _():
            ys_ref[...] += y


def _combine_kernel(pos_ref, x_ref, mod_ref, fg_ref, ys_ref, o_ref, buf, sem, *, nb, final_norm):
    i = pl.program_id(0)
    rows = x_ref.shape[0]
    slot = i % 2

    def fetch(step, sl):
        def one(r, carry):
            pltpu.make_async_copy(ys_ref.at[pl.ds(pos_ref[step * rows + r], 1), :],
                                  buf.at[sl, pl.ds(r, 1), :], sem.at[sl]).start()
            return carry
        lax.fori_loop(0, rows, one, 0, unroll=DMA_UNROLL)

    @pl.when(i == 0)
    def _():
        fetch(0, 0)

    pltpu.make_async_copy(ys_ref.at[pl.ds(0, rows), :], buf.at[slot], sem.at[slot]).wait()

    @pl.when(i + 1 < pl.num_programs(0))
    def _():
        fetch(i + 1, 1 - slot)

    g2 = mod_ref[:, :, 5 * D_MODEL:6 * D_MODEL]
    y = (buf[slot].reshape(nb, rows // nb, D_MODEL) * g2).reshape(rows, D_MODEL)
    out = x_ref[...] + y
    if final_norm:
        out = _rms(out) * fg_ref[...]
    o_ref[...] = out


def _bucket_experts():
    lo, hi = [], []
    for g in range(N_GROUPS):
        for a in range(EXP_PER_GROUP):
            for b in range(a + 1, EXP_PER_GROUP):
                lo.append(g * EXP_PER_GROUP + a)
                hi.append(g * EXP_PER_GROUP + b)
    return jnp.asarray(lo, jnp.int32), jnp.asarray(hi, jnp.int32)


def _moe_sparse(x, mod, norm_g, w_route, w_gu, w_down, final_g, nb, s, final_norm):
    n = nb * s
    move_params = pltpu.CompilerParams(dimension_semantics=("arbitrary",), vmem_limit_bytes=VMEM_LIMIT,
                                       disable_bounds_checks=True)

    def token_specs(tm):
        tb, per_seq = max(tm // s, 1), max(s // tm, 1)
        row = pl.BlockSpec((tm, D_MODEL), lambda i, *_: (i, 0))
        modspec = pl.BlockSpec((tb, 1, 6 * D_MODEL), lambda i, *_: (i // per_seq, 0, 0))
        meta = pl.BlockSpec((tm, LANES), lambda i, *_: (i, 0))
        return tb, row, modspec, meta

    vec = pl.BlockSpec((1, D_MODEL), lambda *_: (0, 0))
    tb, row, modspec, metaspec = token_specs(ROUTER_TILE)
    meta, counts = pl.pallas_call(
        functools.partial(_router_kernel, nb=tb),
        grid=(n // ROUTER_TILE,),
        in_specs=[row, modspec, vec, pl.BlockSpec((D_MODEL, LANES), lambda i: (0, 0)),
                  pl.BlockSpec((ROUTER_TILE, ROUTER_TILE), lambda i: (0, 0))],
        out_specs=[metaspec, pl.BlockSpec((1, LANES), lambda i: (0, 0))],
        out_shape=[jax.ShapeDtypeStruct((n, LANES), F32), jax.ShapeDtypeStruct((1, LANES), F32)],
        compiler_params=_params(("arbitrary",)),
        name="moe_router",
    )(x, mod, norm_g, w_route, jnp.tril(jnp.ones((ROUTER_TILE, ROUTER_TILE), BF16), -1))

    counts = counts[0, :N_BUCKETS].astype(jnp.int32)
    end = jnp.cumsum(counts)
    start = end - counts
    first_tile = start // BUCKET_TILE
    per_bucket = jnp.where(counts > 0, (end - 1) // BUCKET_TILE - first_tile + 1, 0)
    item_end = jnp.cumsum(per_bucket)
    used = item_end[-1:]
    n_items = n // BUCKET_TILE + N_BUCKETS
    k = jnp.minimum(jnp.arange(n_items, dtype=jnp.int32), used - 1)
    item_bucket = jnp.sum(item_end[None, :] <= k[:, None], axis=1, dtype=jnp.int32)
    item_tile = first_tile[item_bucket] + k - (item_end - per_bucket)[item_bucket]
    bucket_lo, bucket_hi = _bucket_experts()
    item_lo, item_hi = bucket_lo[item_bucket], bucket_hi[item_bucket]
    start_row = jnp.pad(start, (0, LANES - N_BUCKETS)).astype(F32)[None, :]
    pos = pl.pallas_call(
        _positions_kernel,
        grid=(n // ROUTER_TILE,),
        in_specs=[metaspec, pl.BlockSpec((1, LANES), lambda i: (0, 0))],
        out_specs=pl.BlockSpec((1, ROUTER_TILE), lambda i: (0, i)),
        out_shape=jax.ShapeDtypeStruct((1, n), jnp.int32),
        compiler_params=_params(("parallel",)),
        name="moe_positions",
    )(meta, start_row).reshape(n)

    tb, row, modspec, metaspec = token_specs(MOVE_TILE)
    any_spec = pl.BlockSpec(memory_space=pl.ANY)
    xs = pl.pallas_call(
        functools.partial(_dispatch_kernel, nb=tb),
        grid_spec=pltpu.PrefetchScalarGridSpec(
            num_scalar_prefetch=1, grid=(n // MOVE_TILE,),
            in_specs=[row, modspec, vec, metaspec], out_specs=any_spec,
            scratch_shapes=[pltpu.VMEM((2, MOVE_TILE, ROW_W), F32), pltpu.SemaphoreType.DMA((2,))]),
        out_shape=jax.ShapeDtypeStruct((n, ROW_W), F32),
        compiler_params=move_params,
        name="moe_dispatch",
    )(pos, x, mod, norm_g, meta)

    tile_of = lambda k, tile, *_: (tile[k], 0)
    ys = pl.pallas_call(
        _experts_kernel,
        grid_spec=pltpu.PrefetchScalarGridSpec(
            num_scalar_prefetch=5, grid=(n_items,),
            in_specs=[pl.BlockSpec((BUCKET_TILE, ROW_W), tile_of),
                      pl.BlockSpec((None, D_MODEL, 2 * D_EXPERT), lambda k, t, b, lo, hi, nu: (lo[k], 0, 0)),
                      pl.BlockSpec((None, D_MODEL, 2 * D_EXPERT), lambda k, t, b, lo, hi, nu: (hi[k], 0, 0)),
                      pl.BlockSpec((None, D_EXPERT, D_MODEL), lambda k, t, b, lo, hi, nu: (lo[k], 0, 0)),
                      pl.BlockSpec((None, D_EXPERT, D_MODEL), lambda k, t, b, lo, hi, nu: (hi[k], 0, 0))],
            out_specs=pl.BlockSpec((BUCKET_TILE, D_MODEL), tile_of)),
        out_shape=jax.ShapeDtypeStruct((n, D_MODEL), F32),
        compiler_params=_params(("arbitrary",)),
        name="moe_experts",
    )(item_tile, item_bucket, item_lo, item_hi, used, xs, w_gu, w_gu, w_down, w_down)

    return pl.pallas_call(
        functools.partial(_combine_kernel, nb=tb, final_norm=final_norm),
        grid_spec=pltpu.PrefetchScalarGridSpec(
            num_scalar_prefetch=1, grid=(n // MOVE_TILE,),
            in_specs=[row, modspec, vec, any_spec], out_specs=row,
            scratch_shapes=[pltpu.VMEM((2, MOVE_TILE, D_MODEL), F32), pltpu.SemaphoreType.DMA((2,))]),
        out_shape=jax.ShapeDtypeStruct((n, D_MODEL), F32),
        compiler_params=move_params,
        name="moe_combine",
    )(pos, x, mod, final_g, ys)


def _rope(x, cos, sin_lo, sin_hi):
    return x * cos + pltpu.roll(x, LANES - ROPE_DIM // 2, axis=1) * sin_lo + pltpu.roll(x, ROPE_DIM // 2, axis=1) * sin_hi


def _qkv_kernel(x_ref, mod_ref, kvmod_ref, ngq_ref, ngkv_ref, wq_ref, bq_ref, wkv_ref, bkv_ref,
                cos_ref, slo_ref, shi_ref, q_ref, k_ref, v_ref):
    n = _rms(x_ref[...])
    tb = mod_ref.shape[0]
    hq = _modulate(n, ngq_ref[...], mod_ref[:, :, 0:D_MODEL], mod_ref[:, :, D_MODEL:2 * D_MODEL], tb)
    hkv = _modulate(n, ngkv_ref[...], kvmod_ref[:, :, 0:D_MODEL], kvmod_ref[:, :, D_MODEL:2 * D_MODEL], tb)
    cos, slo, shi = (jnp.concatenate([t[...]] * tb, axis=0) for t in (cos_ref, slo_ref, shi_ref))
    q = _dot(hq.astype(BF16), wq_ref[...]) + bq_ref[...]
    for p in range(PAIRS):
        cols = slice(p * LANES, (p + 1) * LANES)
        q_ref[:, cols] = (_rope(q[:, cols], cos, slo, shi) * ATT_SCALE).astype(q_ref.dtype)
    kv = _dot(hkv.astype(BF16), wkv_ref[...]) + bkv_ref[...]
    for p in range(KV_W // LANES):
        cols = slice(p * LANES, (p + 1) * LANES)
        k_ref[:, cols] = _rope(kv[:, cols], cos, slo, shi)
    v_ref[...] = kv[:, KV_W:]


def _qkv_proj(x, mod, kvmod, w, tables, nb, s, tm):
    n = nb * s
    tb, nt = max(tm // s, 1), max(s // tm, 1)
    row_spec = pl.BlockSpec((tm, D_MODEL), lambda b, j: (b * nt + j, 0))
    kv_spec = pl.BlockSpec((tm, KV_W), lambda b, j: (b * nt + j, 0))
    tab_spec = pl.BlockSpec((tm // tb, LANES), lambda b, j: (j, 0))
    vec = _const_spec((1, D_MODEL))
    return pl.pallas_call(
        _qkv_kernel,
        grid=(nb // tb, nt),
        in_specs=[row_spec,
                  pl.BlockSpec((tb, 1, 6 * D_MODEL), lambda b, j: (b, 0, 0)),
                  pl.BlockSpec((tb, 1, 2 * D_MODEL), lambda b, j: (b, 0, 0)),
                  vec, vec, _const_spec((D_MODEL, D_MODEL)), vec,
                  _const_spec((D_MODEL, 2 * KV_W)), _const_spec((1, 2 * KV_W)),
                  tab_spec, tab_spec, tab_spec],
        out_specs=[row_spec, kv_spec, kv_spec],
        out_shape=[jax.ShapeDtypeStruct((n, D_MODEL), BF16),
                   jax.ShapeDtypeStruct((n, KV_W), F32), jax.ShapeDtypeStruct((n, KV_W), F32)],
        compiler_params=_params(("parallel", "parallel")),
        name="qkv_proj",
    )(x, mod, kvmod, w["norm_gq"], w["norm_gkv"], w["w_q"], w["b_q"], w["w_kv"], w["b_kv"], *tables)


def _attn_kernel(sink_ref, q_ref, k0_ref, k1_ref, k2_ref, v0_ref, v1_ref, v2_ref, o_ref, *, banded):
    n = pl.program_id(1)
    seqs = range(q_ref.shape[0])
    k_all = [jnp.concatenate([k0_ref[i], k1_ref[i], k2_ref[i]], axis=0) for i in seqs]
    v_all = [jnp.concatenate([v0_ref[i], v1_ref[i], v2_ref[i]], axis=0) for i in seqs]
    lane = lax.broadcasted_iota(jnp.int32, (1, LANES), 1)
    lo = lane < HEAD
    col = lax.broadcasted_iota(jnp.int32, (1, 3 * CHUNK), 1)
    if banded:
        visible = (col >= 2 * CHUNK) | ((col >= CHUNK) & (n >= 1)) | (n >= 2)

    def halves(t, kv_head):
        blk = t[:, (kv_head // 2) * LANES:(kv_head // 2 + 1) * LANES]
        if kv_head % 2 == 0:
            t_lo = jnp.where(lo, blk, 0.0)
            t_hi = pltpu.roll(t_lo, HEAD, axis=1)
        else:
            t_hi = jnp.where(lo, 0.0, blk)
            t_lo = pltpu.roll(t_hi, HEAD, axis=1)
        return t_lo.astype(BF16), t_hi.astype(BF16)

    first_pair = lax.broadcasted_iota(jnp.int32, (2 * CHUNK, 1), 0) < CHUNK

    def probs(s, head_a, head_b):
        if banded:
            s = jnp.where(visible, s, NEG_INF)
        sink = jnp.where(first_pair, sink_ref[head_a], sink_ref[head_b])
        m = jnp.maximum(jnp.max(s, axis=-1, keepdims=True), sink)
        e = jnp.exp(s - m)
        return (e / (jnp.sum(e, axis=-1, keepdims=True) + jnp.exp(sink - m))).astype(BF16)

    units = [(i, g) for i in seqs for g in range(N_KV)]
    scores, values = [], []
    for i, g in units:
        k_lo, k_hi = halves(k_all[i], g)
        values.append(halves(v_all[i], g))
        q2 = jnp.concatenate([q_ref[i, :, (2 * g + j) * LANES:(2 * g + j + 1) * LANES] for j in range(2)], axis=0)
        scores.append((_dot_nt(q2, k_lo), _dot_nt(q2, k_hi)))
    for (i, g), (s_lo, s_hi), (v_lo, v_hi) in zip(units, scores, values):
        o = _dot(probs(s_lo, 4 * g, 4 * g + 2), v_lo) + _dot(probs(s_hi, 4 * g + 1, 4 * g + 3), v_hi)
        for j in range(2):
            pair = 2 * g + j
            o_ref[i, :, pair * LANES:(pair + 1) * LANES] = o[j * CHUNK:(j + 1) * CHUNK].astype(o_ref.dtype)


def _attention(q, kv_arrays, kv_chunks, sinks, nb, s, banded):
    nseq = math.gcd(nb, ATTN_SEQS)
    q_spec = pl.BlockSpec((nseq, CHUNK, D_MODEL), lambda b, c: (b, c, 0))
    kv_specs = [pl.BlockSpec((nseq, CHUNK, KV_W), lambda b, c, f=f: (b, f(c), 0)) for f in kv_chunks]
    return pl.pallas_call(
        functools.partial(_attn_kernel, banded=banded),
        grid=(nb // nseq, s // CHUNK),
        in_specs=[pl.BlockSpec(memory_space=pltpu.SMEM), q_spec] + kv_specs,
        out_specs=q_spec,
        out_shape=jax.ShapeDtypeStruct((nb, s, D_MODEL), BF16),
        compiler_params=_params(("parallel", "parallel")),
        name="attention",
    )(sinks, q, *kv_arrays)


def _pad_cols(w, n):
    return jnp.pad(w, ((0, 0), (0, n - w.shape[1])))


def _pad_rows(w, n):
    return jnp.pad(w, ((0, n - w.shape[0]), (0, 0)))


def _rope_tables(pos):
    half = ROPE_DIM // 2
    inv = jnp.power(jnp.float32(ROPE_THETA), -jnp.arange(half, dtype=F32) * (2.0 / ROPE_DIM))
    ang = pos[:, None] * inv[None, :]
    cos, sin = jnp.cos(ang), jnp.sin(ang)
    rest = HEAD - ROPE_DIM
    ones = jnp.ones((pos.shape[0], rest), F32)
    zeros = jnp.zeros((pos.shape[0], rest), F32)
    z8 = jnp.zeros_like(sin)
    per_head = (jnp.concatenate([cos, cos, ones], axis=1),
                jnp.concatenate([-sin, z8, zeros], axis=1),
                jnp.concatenate([z8, sin, zeros], axis=1))
    return tuple(jnp.tile(t, (1, LANES // HEAD)) for t in per_head)


def _state_to_pairs(state):
    nb = state.shape[0]
    st = state.astype(F32).reshape(nb, PAIRS, 2, HEAD, HEAD)
    z = jnp.zeros_like(st[:, :, 0])
    top = jnp.concatenate([st[:, :, 0], z], axis=-1)
    bot = jnp.concatenate([z, st[:, :, 1]], axis=-1)
    return jnp.concatenate([top, bot], axis=-2)


def _pairs_to_state(st):
    nb = st.shape[0]
    s0 = st[:, :, :HEAD, :HEAD]
    s1 = st[:, :, HEAD:, HEAD:]
    return jnp.stack([s0, s1], axis=2).reshape(nb, 2 * PAIRS, HEAD, HEAD)


def _trunk(x, mods, kvmod, pos, prev_x, prev_wkv, past_k, past_v, w, nb, s, expert_w16=None):
    n = nb * s
    made_w16 = {}
    x = x.reshape(n, D_MODEL)
    mod0, mod1 = mods[0][:, None, :], mods[1][:, None, :]
    kvmod = kvmod[:, None, :]

    tm = min(n, PROJ_TILE)
    r, lw, k, v, a, b, g, last_x = _rwkv_proj(x, mod0, prev_x[:, None, :], w["rw"], nb, s, tm)
    seqs = lambda t: t.reshape(nb, s, D_MODEL)
    z, st = _wkv(*(seqs(t) for t in (r, lw, k, v, a, b, g)), _state_to_pairs(prev_wkv), w["rw"], nb, s)
    z = z.reshape(n, D_MODEL)

    def residual_moe(x, z, mixer, mod, l):
        x = _out_proj(x, z, mod[:, :, 2 * D_MODEL:3 * D_MODEL], mixer["w_o"], mixer["b_o"], nb, s, min(n, 1024))
        route = (mod, w["norm_g"][l][1:2], w["moe_route"][l])
        if n >= SPARSE_MIN_TOKENS:
            if expert_w16 is not None and l in expert_w16:
                gu16, dn16 = expert_w16[l]
            else:
                gu16, dn16 = w["moe_gu"][l].astype(BF16), w["moe_down"][l].astype(BF16)
            return _moe_sparse(x, *route, gu16, dn16, w["final_g"], nb, s, l == 1)
        y, *made_w16[l] = _moe(x, *route, w["moe_gu"], w["moe_down"], w["final_g"], nb, s, l, min(n, 1024), l == 1)
        return y

    x = residual_moe(x, z, w["rw"], mod0, 0)

    q, k_new, v_new = _qkv_proj(x, mod1, kvmod, w["at"], _rope_tables(pos), nb, s, tm)
    k_seq, v_seq = k_new.reshape(nb, s, KV_W), v_new.reshape(nb, s, KV_W)
    if past_k is None:
        arrays = [k_seq] * 3 + [v_seq] * 3
        chunks = [lambda c, d=d: jnp.maximum(c - d, 0) for d in (2, 1, 0)] * 2
    else:
        pk = past_k.astype(F32).reshape(nb, 2 * CHUNK, KV_W)
        pv = past_v.astype(F32).reshape(nb, 2 * CHUNK, KV_W)
        arrays = [pk, pk, k_seq, pv, pv, v_seq]
        chunks = [lambda c: 0, lambda c: 1, lambda c: 0] * 2
    o = _attention(q.reshape(nb, s, D_MODEL), arrays, chunks, w["at"]["sinks"], nb, s, past_k is None)
    y = residual_moe(x, o.reshape(n, D_MODEL), w["at"], mod1, 1)

    return (y.reshape(nb, s, D_MODEL), last_x.reshape(1, nb, D_MODEL), _pairs_to_state(st)[None], k_seq, v_seq,
            made_w16)


def kernel(x_prompt, x_sample, state_shift, state_wkv, cache_k, cache_v, c_prompt, c_sample, ada_w, ada_b, norm_g, rw_mu, rw_w_rkv, rw_w0, rw_w1, rw_w2, rw_a0, rw_a1, rw_a2, rw_g1, rw_g2, rw_k_k, rw_k_a, rw_r_k, rw_lnx_w, rw_lnx_b, rw_w_o, kv_ada_w, kv_ada_b, kv_norm_g, w_kv, b_kv, at_w_q, at_b_q, at_sinks, at_w_o, at_b_o, moe_w_group, moe_w_expert, moe_w_gu, moe_w_down, final_norm_g):
    bp, sp, _ = x_prompt.shape
    bs, ss, _ = x_sample.shape
    row = lambda t: t.reshape(1, -1).astype(F32)

    c_all = jnp.concatenate([c_prompt, c_sample], axis=0)
    mods = _cond_linear(c_all, ada_w, ada_b[:, None, :])
    kvmods = _cond_linear(c_all, kv_ada_w[None], kv_ada_b[None, None, :])[0]

    lora_pad, gate_pad = LANES, 2 * LANES
    w = {
        "norm_g": norm_g,
        "final_g": row(final_norm_g),
        "rw": {
            "norm_g": norm_g[0, 0:1], "mu": rw_mu[0], "w_rkv": rw_w_rkv[0].astype(BF16),
            "w0": row(rw_w0[0]), "w1": _pad_cols(rw_w1[0], lora_pad).astype(BF16),
            "w2": _pad_rows(rw_w2[0], lora_pad).astype(BF16),
            "a0": row(rw_a0[0]), "a1": _pad_cols(rw_a1[0], lora_pad).astype(BF16),
            "a2": _pad_rows(rw_a2[0], lora_pad).astype(BF16),
            "g1": _pad_cols(rw_g1[0], gate_pad).astype(BF16), "g2": _pad_rows(rw_g2[0], gate_pad).astype(BF16),
            "k_k": row(rw_k_k[0]), "k_a": row(rw_k_a[0]), "r_k": row(rw_r_k[0]),
            "lnx_w": row(rw_lnx_w[0]), "lnx_b": row(rw_lnx_b[0]),
            "w_o": rw_w_o[0].astype(BF16), "b_o": jnp.zeros((1, D_MODEL), F32),
        },
        "at": {
            "norm_gq": norm_g[1, 0:1], "norm_gkv": row(kv_norm_g),
            "w_q": at_w_q[0].astype(BF16), "b_q": row(at_b_q[0]),
            "w_kv": w_kv.astype(BF16), "b_kv": row(b_kv),
            "sinks": at_sinks[0].astype(F32),
            "w_o": at_w_o[0].astype(BF16), "b_o": row(at_b_o[0]),
        },
        "moe_route": jnp.pad(jnp.concatenate([moe_w_group, moe_w_expert], axis=-1),
                             ((0, 0), (0, 0), (0, LANES - N_GROUPS - N_EXPERTS))),
        "moe_gu": moe_w_gu,
        "moe_down": moe_w_down,
    }

    pos_p = jnp.arange(sp, dtype=F32)
    pos_s = PAST_LEN + jnp.arange(ss, dtype=F32)
    zero_x = jnp.zeros((bp, D_MODEL), x_prompt.dtype)
    zero_wkv = jnp.zeros((bp,) + state_wkv.shape[2:], state_wkv.dtype)
    y_s, s_shift, s_wkv, s_k, s_v, w16 = _trunk(x_sample, mods[:, bp:], kvmods[bp:], pos_s, state_shift[0],
                                                state_wkv[0], cache_k, cache_v, w, bs, ss)
    y_p, p_shift, p_wkv, p_k, p_v, _ = _trunk(x_prompt, mods[:, :bp], kvmods[:bp], pos_p, zero_x, zero_wkv,
                                              None, None, w, bp, sp, w16)
    keep = min(2 * CHUNK, sp)
    heads = lambda t: t.reshape(t.shape[0], t.shape[1], N_KV, HEAD)
    return (y_p, y_s, p_shift, p_wkv.astype(state_wkv.dtype), heads(p_k[:, sp - keep:]), heads(p_v[:, sp - keep:]),
            s_shift, s_wkv.astype(state_wkv.dtype), heads(s_k), heads(s_v))
```

```python
import functools
import math

import jax
import jax.numpy as jnp
from jax import lax
from jax.experimental import pallas as pl
from jax.experimental.pallas import tpu as pltpu

F32 = jnp.float32
BF16 = jnp.bfloat16

D_MODEL = 1024
LANES = 128
HEAD = 64
PAIRS = D_MODEL // LANES
CHUNK = 64
WKV_SEQS = 4
PROJ_TILE = 512
ATTN_SEQS = 2
PAST_LEN = 4096
N_KV = 4
KV_W = N_KV * HEAD
ROPE_DIM = 16
ROPE_THETA = 500000.0
ATT_SCALE = HEAD ** -0.5
N_GROUPS = 4
EXP_PER_GROUP = 8
N_EXPERTS = N_GROUPS * EXP_PER_GROUP
D_EXPERT = D_MODEL // 4
DENSE_GROUP = 2
ROUTE_LANE0 = N_GROUPS
RMS_EPS = 1e-6
GN_EPS = 64e-5
NEG_INF = -1e30
VMEM_LIMIT = 56 * 1024 * 1024


def _params(sem):
    return pltpu.CompilerParams(dimension_semantics=sem, vmem_limit_bytes=VMEM_LIMIT)


def _dot(a, b):
    return jnp.dot(a, b, preferred_element_type=F32)


def _dot_nt(a, b):
    return lax.dot_general(a, b, (((1,), (1,)), ((), ())), preferred_element_type=F32)


def _dot_tn(a, b):
    return lax.dot_general(a, b, (((0,), (0,)), ((), ())), preferred_element_type=F32)


def _split2(x):
    hi = x.astype(BF16)
    lo = (x - hi.astype(F32)).astype(BF16)
    return hi, lo


def _dot_x3(a, b, dot=_dot):
    ah, al = _split2(a)
    bh, bl = _split2(b)
    return dot(ah, bh) + dot(ah, bl) + dot(al, bh)


def _sigmoid(x):
    return 1.0 / (1.0 + jnp.exp(-x))


def _rms(x):
    return x * lax.rsqrt(jnp.mean(x * x, axis=-1, keepdims=True) + RMS_EPS)


def _modulate(n, gain, shift, scale, nb):
    rows = n.shape[0]
    h = (n * gain).reshape(nb, rows // nb, D_MODEL)
    return (h * (1.0 + scale) + shift).reshape(rows, D_MODEL)


def _const_spec(shape):
    nd = len(shape)
    return pl.BlockSpec(shape, lambda *_: (0,) * nd)


def _cond_kernel(c_ref, w_ref, b_ref, o_ref):
    c = c_ref[...]
    cs = (c * _sigmoid(c)).astype(BF16)
    o_ref[...] = _dot(cs, w_ref[...].astype(BF16)) + b_ref[...]


def _cond_linear(c, w, b, tn=512):
    nl, _, n = w.shape
    m = c.shape[0]
    return pl.pallas_call(
        _cond_kernel,
        grid=(nl, n // tn),
        in_specs=[
            pl.BlockSpec((m, D_MODEL), lambda l, j: (0, 0)),
            pl.BlockSpec((None, D_MODEL, tn), lambda l, j: (l, 0, j)),
            pl.BlockSpec((None, 1, tn), lambda l, j: (l, 0, j)),
        ],
        out_specs=pl.BlockSpec((None, m, tn), lambda l, j: (l, 0, j)),
        out_shape=jax.ShapeDtypeStruct((nl, m, n), F32),
        compiler_params=_params(("parallel", "parallel")),
        name="cond_linear",
    )(c, w, b)


def _rwkv_proj_kernel(x_ref, mod_ref, prev_ref, ng_ref, mu_ref, wrkv_ref, w0_ref, w1_ref, w2_ref,
                      a0_ref, a1_ref, a2_ref, g1_ref, g2_ref, kk_ref, ka_ref,
                      r_ref, lw_ref, k_ref, v_ref, a_ref, b_ref, g_ref, last_ref, carry_ref):
    tm = x_ref.shape[0]
    tb = mod_ref.shape[0]
    rps = tm // tb

    @pl.when(pl.program_id(1) == 0)
    def _():
        carry_ref[...] = prev_ref[...]

    h = _modulate(_rms(x_ref[...]), ng_ref[...], mod_ref[:, :, 0:D_MODEL], mod_ref[:, :, D_MODEL:2 * D_MODEL], tb)
    seqs = lambda t: t.reshape(tb, rps, D_MODEL)
    first = lax.broadcasted_iota(jnp.int32, (tb, rps, 1), 1) == 0
    shifted = jnp.where(first, carry_ref[...], seqs(pltpu.roll(h, 1, axis=0))).reshape(tm, D_MODEL)
    carry_ref[...] = seqs(h)[:, rps - 1:rps, :]
    last_ref[...] = seqs(h)[:, rps - 1:rps, :]
    dx = shifted - h

    def mix(n):
        return (h + dx * mu_ref[n:n + 1, :]).astype(BF16)

    r = _dot(mix(0), wrkv_ref[0])
    k = _dot(mix(1), wrkv_ref[1])
    v = _dot(mix(2), wrkv_ref[2])
    ww = _dot(jnp.tanh(_dot(mix(3), w1_ref[...])).astype(BF16), w2_ref[...])
    z = -(w0_ref[...] + ww)
    softplus = jnp.maximum(z, 0.0) + jnp.log(1.0 + jnp.exp(-jnp.abs(z)))
    lw = -jnp.exp(-softplus - 0.5)
    asig = _sigmoid(a0_ref[...] + _dot(_dot(mix(4), a1_ref[...]).astype(BF16), a2_ref[...]))
    g = _dot(_sigmoid(_dot(mix(5), g1_ref[...])).astype(BF16), g2_ref[...])

    kk = k * kk_ref[...]
    kk2 = kk * kk
    lo_cols = lax.broadcasted_iota(jnp.int32, (tm, LANES), 1) < HEAD

    def head_sums(t):
        lo_sum = jnp.sum(jnp.where(lo_cols, t, 0.0), axis=-1, keepdims=True)
        hi_sum = jnp.sum(jnp.where(lo_cols, 0.0, t), axis=-1, keepdims=True)
        return jnp.where(lo_cols, lo_sum, hi_sum)

    ss = jnp.concatenate([head_sums(kk2[:, p * LANES:(p + 1) * LANES]) for p in range(PAIRS)], axis=1)
    kkn = kk * lax.rsqrt(jnp.maximum(ss, 1e-24))

    r_ref[...] = r.astype(r_ref.dtype)
    lw_ref[...] = lw
    k_ref[...] = (k * (1.0 + (asig - 1.0) * ka_ref[...])).astype(k_ref.dtype)
    v_ref[...] = v.astype(v_ref.dtype)
    a_ref[...] = (-kkn).astype(a_ref.dtype)
    b_ref[...] = (kkn * asig).astype(b_ref.dtype)
    g_ref[...] = g.astype(g_ref.dtype)


def _rwkv_proj(x, mod, prev_x, w, nb, s, tm):
    n = nb * s
    tb, nt = max(tm // s, 1), max(s // tm, 1)
    row_spec = pl.BlockSpec((tm, D_MODEL), lambda b, j: (b * nt + j, 0))
    per_seq = lambda width: pl.BlockSpec((tb, 1, width), lambda b, j: (b, 0, 0))
    vec = _const_spec((1, D_MODEL))
    lora_w, lora_g = w["w1"].shape[1], w["g1"].shape[1]
    in_specs = [
        row_spec,
        per_seq(6 * D_MODEL),
        per_seq(D_MODEL),
        vec, _const_spec((6, D_MODEL)), _const_spec((3, D_MODEL, D_MODEL)),
        vec, _const_spec((D_MODEL, lora_w)), _const_spec((lora_w, D_MODEL)),
        vec, _const_spec((D_MODEL, lora_w)), _const_spec((lora_w, D_MODEL)),
        _const_spec((D_MODEL, lora_g)), _const_spec((lora_g, D_MODEL)),
        vec, vec,
    ]
    act = lambda dt: jax.ShapeDtypeStruct((n, D_MODEL), dt)
    out_types = [act(BF16), act(F32)] + [act(BF16)] * 5
    outs = pl.pallas_call(
        _rwkv_proj_kernel,
        grid=(nb // tb, nt),
        in_specs=in_specs,
        out_specs=[row_spec] * 7 + [per_seq(D_MODEL)],
        out_shape=out_types + [jax.ShapeDtypeStruct((nb, 1, D_MODEL), F32)],
        scratch_shapes=[pltpu.VMEM((tb, 1, D_MODEL), F32)],
        compiler_params=_params(("parallel", "arbitrary")),
        name="rwkv_proj",
    )(x, mod, prev_x, w["norm_g"], w["mu"], w["w_rkv"], w["w0"], w["w1"], w["w2"],
      w["a0"], w["a1"], w["a2"], w["g1"], w["g2"], w["k_k"], w["k_a"])
    return outs


def _split3(x):
    hi = x.astype(BF16)
    rem = x - hi.astype(F32)
    mid = rem.astype(BF16)
    return hi, mid, (rem - mid.astype(F32)).astype(BF16)


def _wkv_kernel(r_ref, lw_ref, k_ref, v_ref, a_ref, b_ref, g_ref, s0_ref, rk_ref, lnw_ref, lnb_ref,
                z_ref, st_ref):
    @pl.when(pl.program_id(1) == 0)
    def _():
        st_ref[...] = s0_ref[...]

    nseq = r_ref.shape[0]
    wide = lambda ref: jnp.concatenate([ref[i] for i in range(nseq)], axis=1)
    per_seq = lambda ref: jnp.concatenate([ref[...]] * nseq, axis=1)
    b16 = lambda t: t.astype(BF16)
    pairs = range(nseq * PAIRS)
    blk = lambda t, p: t[:, p * LANES:(p + 1) * LANES]
    rows = lambda ts: jnp.concatenate(ts, axis=0)
    lo_cols = lax.broadcasted_iota(jnp.int32, (CHUNK, LANES), 1) < HEAD
    rr = lax.broadcasted_iota(jnp.int32, (LANES, LANES), 0)
    cc = lax.broadcasted_iota(jnp.int32, (LANES, LANES), 1)
    head0 = lambda t: jnp.where(lo_cols, t, 0.0)
    head1 = lambda t: jnp.where(lo_cols, 0.0, t)
    blockdiag = lambda q: rows([head0(q), head1(q)])

    r, lw, k, v, a, b = (wide(t).astype(F32) for t in (r_ref, lw_ref, k_ref, v_ref, a_ref, b_ref))
    ti = lax.broadcasted_iota(jnp.int32, (CHUNK, CHUNK), 0)
    tj = lax.broadcasted_iota(jnp.int32, (CHUNK, CHUNK), 1)
    tri = (tj <= ti).astype(BF16)
    cum = sum(_dot(tri, t) for t in _split3(lw))
    cl = cum[CHUNK - 1:CHUNK, :]
    e_neg = jnp.exp(-cum)
    e_end = jnp.exp(cl - cum)
    e_cl = jnp.exp(cl)
    at = b16(a * jnp.exp(cum - lw))
    rt = b16(r * jnp.exp(cum))
    bt = b * e_neg
    kt = k * e_neg
    bkh = rows([b16(b * e_end), b16(k * e_end)])
    v16 = b16(v)

    t_idx = rr & (CHUNK - 1)
    j_idx = cc & (CHUNK - 1)
    keep = (j_idx < t_idx) | ((rr >= CHUNK) & (j_idx == t_idx))
    g0, g1 = [], []
    for p in pairs:
        btp, ktp = blk(bt, p), blk(kt, p)
        w = b16(rows([head0(btp), head0(ktp), head1(ktp), head1(btp)]))
        g = _dot_nt(rows([blk(at, p), blk(rt, p)]), w)
        g0.append(jnp.where(keep, g[:, :LANES], 0.0))
        g1.append(jnp.where(keep, g[:, LANES:], 0.0))

    eye2 = jnp.where((lax.broadcasted_iota(jnp.int32, (CHUNK, LANES), 1) & (CHUNK - 1))
                     == lax.broadcasted_iota(jnp.int32, (CHUNK, LANES), 0), 1.0, 0.0)
    pc = [jnp.where(lo_cols, g0[p][:CHUNK], g1[p][:CHUNK]) for p in pairs]
    tc = [eye2 + pc[p] for p in pairs]
    q = [_dot(b16(pc[p]), b16(blockdiag(pc[p]))) for p in pairs]
    for _ in range(4):
        res = [_dot(b16(rows([tc[p], q[p]])), b16(blockdiag(q[p]))) for p in pairs]
        tc = [tc[p] + res[p][:CHUNK] for p in pairs]
        q = [res[p][CHUNK:] for p in pairs]
    tc = [b16(tc[p] + _dot(b16(tc[p]), b16(blockdiag(q[p])))) for p in pairs]

    akv = []
    for p in pairs:
        w_ak = b16(jnp.where(lo_cols, g1[p][:CHUNK], g0[p][:CHUNK]))
        vp = blk(v16, p)
        akv.append(_dot(w_ak, rows([head1(vp), head0(vp)])))

    s = [st_ref[p // PAIRS, p % PAIRS] for p in pairs]
    s16 = [b16(s[p]) for p in pairs]
    rhs = [_dot_nt(blk(at, p), s16[p]) + akv[p] for p in pairs]
    u = [_dot(tc[p], b16(rows([head0(rhs[p]), head1(rhs[p])]))) for p in pairs]
    u16 = [b16(u[p]) for p in pairs]

    ys = []
    bd_mask = (rr < HEAD) == (cc < HEAD)
    for p in pairs:
        vp = blk(v16, p)
        rbk = b16(jnp.concatenate([g0[p][CHUNK:], g1[p][CHUNK:]], axis=1))
        uv = rows([head0(u16[p]), head0(vp), head1(vp), head1(u16[p])])
        ys.append(_dot_nt(blk(rt, p), s16[p]) + _dot(rbk, uv))
        fresh = _dot_tn(rows([u16[p], vp]), blk(bkh, p))
        st_ref[p // PAIRS, p % PAIRS] = s[p] * blk(e_cl, p) + jnp.where(bd_mask, fresh, 0.0)

    def head_sums(t):
        lo_sum = jnp.sum(head0(t), axis=-1, keepdims=True)
        hi_sum = jnp.sum(head1(t), axis=-1, keepdims=True)
        return jnp.where(lo_cols, lo_sum, hi_sum)

    rkk = r * k * per_seq(rk_ref)
    for p in pairs:
        seq, cols = p // PAIRS, slice((p % PAIRS) * LANES, (p % PAIRS + 1) * LANES)
        yc = ys[p] - head_sums(ys[p]) * (1.0 / HEAD)
        yn = yc * lax.rsqrt(head_sums(yc * yc) * (1.0 / HEAD) + GN_EPS)
        zp = (yn * lnw_ref[:, cols] + lnb_ref[:, cols] + head_sums(blk(rkk, p)) * blk(v, p)) * g_ref[seq, :, cols].astype(F32)
        z_ref[seq, :, cols] = zp.astype(z_ref.dtype)


def _wkv(r, lw, k, v, a, b, g, st0, w, nb, s):
    nseq = math.gcd(nb, WKV_SEQS)
    blk = pl.BlockSpec((nseq, CHUNK, D_MODEL), lambda bi, c: (bi, c, 0))
    vec = _const_spec((1, D_MODEL))
    st_spec = pl.BlockSpec((nseq, PAIRS, LANES, LANES), lambda bi, c: (bi, 0, 0, 0))
    return pl.pallas_call(
        _wkv_kernel,
        grid=(nb // nseq, s // CHUNK),
        in_specs=[blk] * 7 + [st_spec, vec, vec, vec],
        out_specs=[blk, st_spec],
        out_shape=[jax.ShapeDtypeStruct((nb, s, D_MODEL), BF16),
                   jax.ShapeDtypeStruct((nb, PAIRS, LANES, LANES), F32)],
        compiler_params=_params(("parallel", "arbitrary")),
        name="wkv",
    )(r, lw, k, v, a, b, g, st0, w["r_k"], w["lnx_w"], w["lnx_b"])


def _out_proj_kernel(x_ref, z_ref, gate_ref, w_ref, bias_ref, o_ref, *, nb):
    rows = x_ref.shape[0]
    y = _dot(z_ref[...], w_ref[...]) + bias_ref[...]
    y = (y.reshape(nb, rows // nb, D_MODEL) * gate_ref[...]).reshape(rows, D_MODEL)
    o_ref[...] = x_ref[...] + y


def _out_proj(x, z, gate, w, bias, nb, s, tm):
    n = nb * s
    tb = max(tm // s, 1)
    per_seq = max(s // tm, 1)
    row_spec = pl.BlockSpec((tm, D_MODEL), lambda i: (i, 0))
    return pl.pallas_call(
        functools.partial(_out_proj_kernel, nb=tb),
        grid=(n // tm,),
        in_specs=[row_spec, row_spec,
                  pl.BlockSpec((tb, 1, D_MODEL), lambda i: (i // per_seq, 0, 0)),
                  _const_spec((D_MODEL, D_MODEL)), _const_spec((1, D_MODEL))],
        out_specs=row_spec,
        out_shape=jax.ShapeDtypeStruct((n, D_MODEL), F32),
        compiler_params=_params(("parallel",)),
        name="out_proj",
    )(x, z, gate, w, bias)


def _top2(logits):
    lane = lax.broadcasted_iota(jnp.int32, logits.shape, 1).astype(F32)
    first_of = lambda hit: jnp.min(jnp.where(hit, lane, float(LANES)), axis=-1, keepdims=True)
    lg = jnp.where(lane < N_GROUPS, logits, -jnp.inf)
    gmax = jnp.max(lg, axis=-1, keepdims=True)
    gi = first_of(lg == gmax)
    gp = 1.0 / jnp.sum(jnp.exp(lg - gmax), axis=-1, keepdims=True)
    first = ROUTE_LANE0 + gi * EXP_PER_GROUP
    le = jnp.where((lane >= first) & (lane < first + EXP_PER_GROUP), logits, -jnp.inf)
    top1 = jnp.max(le, axis=-1, keepdims=True)
    i1 = first_of(le == top1)
    le2 = jnp.where(lane == i1, -jnp.inf, le)
    top2 = jnp.max(le2, axis=-1, keepdims=True)
    i2 = first_of(le2 == top2)
    e2 = jnp.exp(top2 - top1)
    w1 = gp / (1.0 + e2)
    return i1, i2, w1, w1 * e2


def _route(logits):
    lane = lax.broadcasted_iota(jnp.int32, logits.shape, 1).astype(F32)
    i1, i2, w1, w2 = _top2(logits)
    return jnp.where(lane == i1, w1, 0.0) + jnp.where(lane == i2, w2, 0.0)


def _moe_kernel(x_ref, mod_ref, ng_ref, wr_ref, wgu_ref, wdn_ref, fg_ref, o_ref, gu16_ref, dn16_ref,
                h_scr, cw_scr, acc_scr, *, nb, final_norm):
    e = pl.program_id(1)
    rows = x_ref.shape[0]

    @pl.when(e == 0)
    def _():
        shift = mod_ref[:, :, 3 * D_MODEL:4 * D_MODEL]
        scale = mod_ref[:, :, 4 * D_MODEL:5 * D_MODEL]
        h = _modulate(_rms(x_ref[...]), ng_ref[...], shift, scale, nb)
        h_scr[...] = h.astype(BF16)
        cw_scr[...] = _route(_dot_x3(h, wr_ref[...]))
        acc_scr[...] = jnp.zeros_like(acc_scr)

    lane = lax.broadcasted_iota(jnp.int32, (rows, LANES), 1)
    gu16_ref[...] = wgu_ref[...].astype(BF16)
    dn16_ref[...] = wdn_ref[...].astype(BF16)
    hid = []
    for j in range(DENSE_GROUP):
        cw = jnp.sum(jnp.where(lane == ROUTE_LANE0 + e * DENSE_GROUP + j, cw_scr[...], 0.0), axis=-1, keepdims=True)
        gu = _dot(h_scr[...], gu16_ref[j])
        gate = gu[:, :D_EXPERT]
        hid.append((gate * _sigmoid(gate) * gu[:, D_EXPERT:] * cw).astype(BF16))
    acc_scr[...] += _dot(jnp.concatenate(hid, axis=1), dn16_ref[...].reshape(DENSE_GROUP * D_EXPERT, D_MODEL))

    @pl.when(e == pl.num_programs(1) - 1)
    def _():
        g2 = mod_ref[:, :, 5 * D_MODEL:6 * D_MODEL]
        y = (acc_scr[...].reshape(nb, rows // nb, D_MODEL) * g2).reshape(rows, D_MODEL)
        out = x_ref[...] + y
        if final_norm:
            out = _rms(out) * fg_ref[...]
        o_ref[...] = out


def _moe(x, mod, norm_g, w_route, w_gu, w_down, final_g, nb, s, layer, tm, final_norm):
    n = nb * s
    tb = max(tm // s, 1)
    per_seq = max(s // tm, 1)
    row_spec = pl.BlockSpec((tm, D_MODEL), lambda i, e: (i, 0))
    return pl.pallas_call(
        functools.partial(_moe_kernel, nb=tb, final_norm=final_norm),
        grid=(n // tm, N_EXPERTS // DENSE_GROUP),
        in_specs=[row_spec,
                  pl.BlockSpec((tb, 1, 6 * D_MODEL), lambda i, e: (i // per_seq, 0, 0)),
                  _const_spec((1, D_MODEL)), _const_spec((D_MODEL, LANES)),
                  pl.BlockSpec((None, DENSE_GROUP, D_MODEL, 2 * D_EXPERT), lambda i, e: (layer, e, 0, 0)),
                  pl.BlockSpec((None, DENSE_GROUP, D_EXPERT, D_MODEL), lambda i, e: (layer, e, 0, 0)),
                  _const_spec((1, D_MODEL))],
        out_specs=[row_spec,
                   pl.BlockSpec((DENSE_GROUP, D_MODEL, 2 * D_EXPERT), lambda i, e: (e, 0, 0)),
                   pl.BlockSpec((DENSE_GROUP, D_EXPERT, D_MODEL), lambda i, e: (e, 0, 0))],
        out_shape=[jax.ShapeDtypeStruct((n, D_MODEL), F32),
                   jax.ShapeDtypeStruct((N_EXPERTS, D_MODEL, 2 * D_EXPERT), BF16),
                   jax.ShapeDtypeStruct((N_EXPERTS, D_EXPERT, D_MODEL), BF16)],
        scratch_shapes=[pltpu.VMEM((tm, D_MODEL), BF16), pltpu.VMEM((tm, LANES), F32),
                        pltpu.VMEM((tm, D_MODEL), F32)],
        compiler_params=_params(("arbitrary", "arbitrary")),
        name="moe",
    )(x, mod, norm_g, w_route, w_gu, w_down, final_g)


PAIRS_PER_GROUP = EXP_PER_GROUP * (EXP_PER_GROUP - 1) // 2
N_BUCKETS = N_GROUPS * PAIRS_PER_GROUP
BUCKET_TILE = 256
ROW_W = D_MODEL + LANES
META_BUCKET, META_RANK, META_W_LO, META_W_HI = 0, 1, 2, 3
ROUTER_TILE = 1024
MOVE_TILE = 1024
DMA_UNROLL = 8
SPARSE_MIN_TOKENS = 4096


def _moe_norm(x, mod_ref, ng_ref, nb):
    shift = mod_ref[:, :, 3 * D_MODEL:4 * D_MODEL]
    scale = mod_ref[:, :, 4 * D_MODEL:5 * D_MODEL]
    return _modulate(_rms(x), ng_ref[...], shift, scale, nb)


def _router_kernel(x_ref, mod_ref, ng_ref, wr_ref, earlier_ref, meta_ref, cnt_ref, *, nb):
    @pl.when(pl.program_id(0) == 0)
    def _():
        cnt_ref[...] = jnp.zeros_like(cnt_ref)

    rows = x_ref.shape[0]
    h = _moe_norm(x_ref[...], mod_ref, ng_ref, nb)
    i1, i2, w1, w2 = _top2(_dot_x3(h, wr_ref[...]))
    lo = (jnp.minimum(i1, i2) - ROUTE_LANE0).astype(jnp.int32)
    hi = (jnp.maximum(i1, i2) - ROUTE_LANE0).astype(jnp.int32)
    first_is_lo = i1 < i2
    a = lo & (EXP_PER_GROUP - 1)
    b = hi & (EXP_PER_GROUP - 1)
    group = lo >> (EXP_PER_GROUP.bit_length() - 1)
    bucket = group * PAIRS_PER_GROUP + ((a * (2 * EXP_PER_GROUP - 1 - a)) >> 1) + (b - a - 1)

    lane = lax.broadcasted_iota(jnp.int32, (rows, LANES), 1)
    mine = lane == bucket
    onehot = jnp.where(mine, 1.0, 0.0)
    seen = cnt_ref[...]
    before = _dot(earlier_ref[...], onehot.astype(BF16)) + seen
    rank = jnp.sum(jnp.where(mine, before, 0.0), axis=-1, keepdims=True)
    cnt_ref[...] = seen + jnp.sum(onehot, axis=0, keepdims=True)

    meta = jnp.where(lane == META_BUCKET, bucket.astype(F32), 0.0)
    meta = jnp.where(lane == META_RANK, rank, meta)
    meta = jnp.where(lane == META_W_LO, jnp.where(first_is_lo, w1, w2), meta)
    meta_ref[...] = jnp.where(lane == META_W_HI, jnp.where(first_is_lo, w2, w1), meta)


def _positions_kernel(meta_ref, start_ref, pos_ref):
    meta = meta_ref[...]
    lane = lax.broadcasted_iota(jnp.int32, meta.shape, 1)
    column = lambda c: jnp.sum(jnp.where(lane == c, meta, 0.0), axis=-1, keepdims=True)
    value = jnp.where(lane == column(META_BUCKET).astype(jnp.int32), start_ref[...] + column(META_RANK), 0.0)
    ones = jnp.ones((8, LANES), BF16)
    pos = sum(_dot_nt(ones, piece) for piece in _split3(value))
    pos_ref[...] = pos[0:1, :].astype(jnp.int32)


def _dispatch_kernel(pos_ref, x_ref, mod_ref, ng_ref, meta_ref, xs_ref, buf, sem, *, nb):
    i = pl.program_id(0)
    rows = x_ref.shape[0]
    slot = i % 2
    buf[slot, :, 0:D_MODEL] = _moe_norm(x_ref[...], mod_ref, ng_ref, nb)
    buf[slot, :, D_MODEL:ROW_W] = meta_ref[...]

    def send(r, carry):
        pltpu.make_async_copy(buf.at[slot, pl.ds(r, 1), :],
                              xs_ref.at[pl.ds(pos_ref[i * rows + r], 1), :], sem.at[slot]).start()
        return carry

    lax.fori_loop(0, rows, send, 0, unroll=DMA_UNROLL)

    def wait_all(sl):
        pltpu.make_async_copy(buf.at[sl], xs_ref.at[pl.ds(0, rows), :], sem.at[sl]).wait()

    @pl.when(i >= 1)
    def _():
        wait_all(1 - slot)

    @pl.when(i == pl.num_programs(0) - 1)
    def _():
        wait_all(slot)


def _experts_kernel(tile_ref, bucket_ref, lo_ref, hi_ref, first_ref, last_ref, used_ref, xs_ref,
                    gu_lo_ref, gu_hi_ref, dn_lo_ref, dn_hi_ref, ys_ref):
    k = pl.program_id(0)
    half = BUCKET_TILE // 2

    @pl.when(k < used_ref[0])
    def _():
        @pl.when((k == 0) | (tile_ref[k] != tile_ref[jnp.maximum(k - 1, 0)]))
        def _():
            ys_ref[...] = jnp.zeros_like(ys_ref)

        def run(base, rows):
            window = pl.ds(base, rows)
            x16 = xs_ref[window, 0:D_MODEL].astype(BF16)
            meta = xs_ref[window, D_MODEL:ROW_W]
            lane = lax.broadcasted_iota(jnp.int32, (rows, LANES), 1)
            column = lambda c: jnp.sum(jnp.where(lane == c, meta, 0.0), axis=-1, keepdims=True)
            member = column(META_BUCKET) == bucket_ref[k].astype(F32)

            def expert(gu_ref, dn_ref, col):
                gu = _dot(x16, gu_ref[...])
                gate = gu[:, :D_EXPERT]
                hid = gate * _sigmoid(gate) * gu[:, D_EXPERT:] * jnp.where(member, column(col), 0.0)
                return _dot(hid.astype(BF16), dn_ref[...])

            ys_ref[window, :] += expert(gu_lo_ref, dn_lo_ref, META_W_LO) + expert(gu_hi_ref, dn_hi_ref, META_W_HI)

        sublanes = 8
        base = jnp.minimum(first_ref[k] // sublanes * sublanes, BUCKET_TILE - half)
        fits = last_ref[k] - base <= half

        @pl.when(fits)
        def _():
            run(pl.multiple_of(base, sublanes), half)

        @pl.when(jnp.logical_not(fits))
        def _():
            run(0, BUCKET_TILE)


def _combine_kernel(pos_ref, x_ref, mod_ref, fg_ref, ys_ref, o_ref, buf, sem, *, nb, final_norm):
    i = pl.program_id(0)
    rows = x_ref.shape[0]
    slot = i % 2

    def fetch(step, sl):
        def one(r, carry):
            pltpu.make_async_copy(ys_ref.at[pl.ds(pos_ref[step * rows + r], 1), :],
                                  buf.at[sl, pl.ds(r, 1), :], sem.at[sl]).start()
            return carry
        lax.fori_loop(0, rows, one, 0, unroll=DMA_UNROLL)

    @pl.when(i == 0)
    def _():
        fetch(0, 0)

    pltpu.make_async_copy(ys_ref.at[pl.ds(0, rows), :], buf.at[slot], sem.at[slot]).wait()

    @pl.when(i + 1 < pl.num_programs(0))
    def _():
        fetch(i + 1, 1 - slot)

    g2 = mod_ref[:, :, 5 * D_MODEL:6 * D_MODEL]
    y = (buf[slot].reshape(nb, rows // nb, D_MODEL) * g2).reshape(rows, D_MODEL)
    out = x_ref[...] + y
    if final_norm:
        out = _rms(out) * fg_ref[...]
    o_ref[...] = out


def _bucket_experts():
    lo, hi = [], []
    for g in range(N_GROUPS):
        for a in range(EXP_PER_GROUP):
            for b in range(a + 1, EXP_PER_GROUP):
                lo.append(g * EXP_PER_GROUP + a)
                hi.append(g * EXP_PER_GROUP + b)
    return jnp.asarray(lo, jnp.int32), jnp.asarray(hi, jnp.int32)


def _moe_sparse(x, mod, norm_g, w_route, w_gu, w_down, final_g, nb, s, final_norm):
    n = nb * s
    move_params = pltpu.CompilerParams(dimension_semantics=("arbitrary",), vmem_limit_bytes=VMEM_LIMIT,
                                       disable_bounds_checks=True)

    def token_specs(tm):
        tb, per_seq = max(tm // s, 1), max(s // tm, 1)
        row = pl.BlockSpec((tm, D_MODEL), lambda i, *_: (i, 0))
        modspec = pl.BlockSpec((tb, 1, 6 * D_MODEL), lambda i, *_: (i // per_seq, 0, 0))
        meta = pl.BlockSpec((tm, LANES), lambda i, *_: (i, 0))
        return tb, row, modspec, meta

    vec = pl.BlockSpec((1, D_MODEL), lambda *_: (0, 0))
    tb, row, modspec, metaspec = token_specs(ROUTER_TILE)
    meta, counts = pl.pallas_call(
        functools.partial(_router_kernel, nb=tb),
        grid=(n // ROUTER_TILE,),
        in_specs=[row, modspec, vec, pl.BlockSpec((D_MODEL, LANES), lambda i: (0, 0)),
                  pl.BlockSpec((ROUTER_TILE, ROUTER_TILE), lambda i: (0, 0))],
        out_specs=[metaspec, pl.BlockSpec((1, LANES), lambda i: (0, 0))],
        out_shape=[jax.ShapeDtypeStruct((n, LANES), F32), jax.ShapeDtypeStruct((1, LANES), F32)],
        compiler_params=_params(("arbitrary",)),
        name="moe_router",
    )(x, mod, norm_g, w_route, jnp.tril(jnp.ones((ROUTER_TILE, ROUTER_TILE), BF16), -1))

    counts = counts[0, :N_BUCKETS].astype(jnp.int32)
    end = jnp.cumsum(counts)
    start = end - counts
    first_tile = start // BUCKET_TILE
    per_bucket = jnp.where(counts > 0, (end - 1) // BUCKET_TILE - first_tile + 1, 0)
    item_end = jnp.cumsum(per_bucket)
    used = item_end[-1:]
    n_items = n // BUCKET_TILE + N_BUCKETS
    k = jnp.minimum(jnp.arange(n_items, dtype=jnp.int32), used - 1)
    item_bucket = jnp.sum(item_end[None, :] <= k[:, None], axis=1, dtype=jnp.int32)
    item_tile = first_tile[item_bucket] + k - (item_end - per_bucket)[item_bucket]
    bucket_lo, bucket_hi = _bucket_experts()
    item_lo, item_hi = bucket_lo[item_bucket], bucket_hi[item_bucket]
    tile_row0 = item_tile * BUCKET_TILE
    item_first = jnp.clip(start[item_bucket] - tile_row0, 0, BUCKET_TILE)
    item_last = jnp.clip(end[item_bucket] - tile_row0, 0, BUCKET_TILE)
    start_row = jnp.pad(start, (0, LANES - N_BUCKETS)).astype(F32)[None, :]
    pos = pl.pallas_call(
        _positions_kernel,
        grid=(n // ROUTER_TILE,),
        in_specs=[metaspec, pl.BlockSpec((1, LANES), lambda i: (0, 0))],
        out_specs=pl.BlockSpec((1, ROUTER_TILE), lambda i: (0, i)),
        out_shape=jax.ShapeDtypeStruct((1, n), jnp.int32),
        compiler_params=_params(("parallel",)),
        name="moe_positions",
    )(meta, start_row).reshape(n)

    tb, row, modspec, metaspec = token_specs(MOVE_TILE)
    any_spec = pl.BlockSpec(memory_space=pl.ANY)
    xs = pl.pallas_call(
        functools.partial(_dispatch_kernel, nb=tb),
        grid_spec=pltpu.PrefetchScalarGridSpec(
            num_scalar_prefetch=1, grid=(n // MOVE_TILE,),
            in_specs=[row, modspec, vec, metaspec], out_specs=any_spec,
            scratch_shapes=[pltpu.VMEM((2, MOVE_TILE, ROW_W), F32), pltpu.SemaphoreType.DMA((2,))]),
        out_shape=jax.ShapeDtypeStruct((n, ROW_W), F32),
        compiler_params=move_params,
        name="moe_dispatch",
    )(pos, x, mod, norm_g, meta)

    tile_of = lambda k, tile, *_: (tile[k], 0)
    ys = pl.pallas_call(
        _experts_kernel,
        grid_spec=pltpu.PrefetchScalarGridSpec(
            num_scalar_prefetch=7, grid=(n_items,),
            in_specs=[pl.BlockSpec((BUCKET_TILE, ROW_W), tile_of),
                      pl.BlockSpec((None, D_MODEL, 2 * D_EXPERT), lambda k, t, b, lo, hi, *_: (lo[k], 0, 0)),
                      pl.BlockSpec((None, D_MODEL, 2 * D_EXPERT), lambda k, t, b, lo, hi, *_: (hi[k], 0, 0)),
                      pl.BlockSpec((None, D_EXPERT, D_MODEL), lambda k, t, b, lo, hi, *_: (lo[k], 0, 0)),
                      pl.BlockSpec((None, D_EXPERT, D_MODEL), lambda k, t, b, lo, hi, *_: (hi[k], 0, 0))],
            out_specs=pl.BlockSpec((BUCKET_TILE, D_MODEL), tile_of)),
        out_shape=jax.ShapeDtypeStruct((n, D_MODEL), F32),
        compiler_params=_params(("arbitrary",)),
        name="moe_experts",
    )(item_tile, item_bucket, item_lo, item_hi, item_first, item_last, used, xs, w_gu, w_gu, w_down, w_down)

    return pl.pallas_call(
        functools.partial(_combine_kernel, nb=tb, final_norm=final_norm),
        grid_spec=pltpu.PrefetchScalarGridSpec(
            num_scalar_prefetch=1, grid=(n // MOVE_TILE,),
            in_specs=[row, modspec, vec, any_spec], out_specs=row,
            scratch_shapes=[pltpu.VMEM((2, MOVE_TILE, D_MODEL), F32), pltpu.SemaphoreType.DMA((2,))]),
        out_shape=jax.ShapeDtypeStruct((n, D_MODEL), F32),
        compiler_params=move_params,
        name="moe_combine",
    )(pos, x, mod, final_g, ys)


def _rope(x, cos, sin_lo, sin_hi):
    return x * cos + pltpu.roll(x, LANES - ROPE_DIM // 2, axis=1) * sin_lo + pltpu.roll(x, ROPE_DIM // 2, axis=1) * sin_hi


def _qkv_kernel(x_ref, mod_ref, kvmod_ref, ngq_ref, ngkv_ref, wq_ref, bq_ref, wkv_ref, bkv_ref,
                cos_ref, slo_ref, shi_ref, q_ref, k_ref, v_ref):
    n = _rms(x_ref[...])
    tb = mod_ref.shape[0]
    hq = _modulate(n, ngq_ref[...], mod_ref[:, :, 0:D_MODEL], mod_ref[:, :, D_MODEL:2 * D_MODEL], tb)
    hkv = _modulate(n, ngkv_ref[...], kvmod_ref[:, :, 0:D_MODEL], kvmod_ref[:, :, D_MODEL:2 * D_MODEL], tb)
    cos, slo, shi = (jnp.concatenate([t[...]] * tb, axis=0) for t in (cos_ref, slo_ref, shi_ref))
    q = _dot(hq.astype(BF16), wq_ref[...]) + bq_ref[...]
    for p in range(PAIRS):
        cols = slice(p * LANES, (p + 1) * LANES)
        q_ref[:, cols] = (_rope(q[:, cols], cos, slo, shi) * ATT_SCALE).astype(q_ref.dtype)
    kv = _dot(hkv.astype(BF16), wkv_ref[...]) + bkv_ref[...]
    for p in range(KV_W // LANES):
        cols = slice(p * LANES, (p + 1) * LANES)
        k_ref[:, cols] = _rope(kv[:, cols], cos, slo, shi)
    v_ref[...] = kv[:, KV_W:]


def _qkv_proj(x, mod, kvmod, w, tables, nb, s, tm):
    n = nb * s
    tb, nt = max(tm // s, 1), max(s // tm, 1)
    row_spec = pl.BlockSpec((tm, D_MODEL), lambda b, j: (b * nt + j, 0))
    kv_spec = pl.BlockSpec((tm, KV_W), lambda b, j: (b * nt + j, 0))
    tab_spec = pl.BlockSpec((tm // tb, LANES), lambda b, j: (j, 0))
    vec = _const_spec((1, D_MODEL))
    return pl.pallas_call(
        _qkv_kernel,
        grid=(nb // tb, nt),
        in_specs=[row_spec,
                  pl.BlockSpec((tb, 1, 6 * D_MODEL), lambda b, j: (b, 0, 0)),
                  pl.BlockSpec((tb, 1, 2 * D_MODEL), lambda b, j: (b, 0, 0)),
                  vec, vec, _const_spec((D_MODEL, D_MODEL)), vec,
                  _const_spec((D_MODEL, 2 * KV_W)), _const_spec((1, 2 * KV_W)),
                  tab_spec, tab_spec, tab_spec],
        out_specs=[row_spec, kv_spec, kv_spec],
        out_shape=[jax.ShapeDtypeStruct((n, D_MODEL), BF16),
                   jax.ShapeDtypeStruct((n, KV_W), F32), jax.ShapeDtypeStruct((n, KV_W), F32)],
        compiler_params=_params(("parallel", "parallel")),
        name="qkv_proj",
    )(x, mod, kvmod, w["norm_gq"], w["norm_gkv"], w["w_q"], w["b_q"], w["w_kv"], w["b_kv"], *tables)


def _attn_kernel(sink_ref, q_ref, k0_ref, k1_ref, k2_ref, v0_ref, v1_ref, v2_ref, o_ref, *, banded):
    n = pl.program_id(1)
    seqs = range(q_ref.shape[0])
    k_all = [jnp.concatenate([k0_ref[i], k1_ref[i], k2_ref[i]], axis=0) for i in seqs]
    v_all = [jnp.concatenate([v0_ref[i], v1_ref[i], v2_ref[i]], axis=0) for i in seqs]
    lane = lax.broadcasted_iota(jnp.int32, (1, LANES), 1)
    lo = lane < HEAD
    col = lax.broadcasted_iota(jnp.int32, (1, 3 * CHUNK), 1)
    if banded:
        visible = (col >= 2 * CHUNK) | ((col >= CHUNK) & (n >= 1)) | (n >= 2)

    def halves(t, kv_head):
        blk = t[:, (kv_head // 2) * LANES:(kv_head // 2 + 1) * LANES]
        if kv_head % 2 == 0:
            t_lo = jnp.where(lo, blk, 0.0)
            t_hi = pltpu.roll(t_lo, HEAD, axis=1)
        else:
            t_hi = jnp.where(lo, 0.0, blk)
            t_lo = pltpu.roll(t_hi, HEAD, axis=1)
        return t_lo.astype(BF16), t_hi.astype(BF16)

    first_pair = lax.broadcasted_iota(jnp.int32, (2 * CHUNK, 1), 0) < CHUNK

    def probs(s, head_a, head_b):
        if banded:
            s = jnp.where(visible, s, NEG_INF)
        sink = jnp.where(first_pair, sink_ref[head_a], sink_ref[head_b])
        m = jnp.maximum(jnp.max(s, axis=-1, keepdims=True), sink)
        e = jnp.exp(s - m)
        return (e / (jnp.sum(e, axis=-1, keepdims=True) + jnp.exp(sink - m))).astype(BF16)

    units = [(i, g) for i in seqs for g in range(N_KV)]
    scores, values = [], []
    for i, g in units:
        k_lo, k_hi = halves(k_all[i], g)
        values.append(halves(v_all[i], g))
        q2 = jnp.concatenate([q_ref[i, :, (2 * g + j) * LANES:(2 * g + j + 1) * LANES] for j in range(2)], axis=0)
        scores.append((_dot_nt(q2, k_lo), _dot_nt(q2, k_hi)))
    for (i, g), (s_lo, s_hi), (v_lo, v_hi) in zip(units, scores, values):
        o = _dot(probs(s_lo, 4 * g, 4 * g + 2), v_lo) + _dot(probs(s_hi, 4 * g + 1, 4 * g + 3), v_hi)
        for j in range(2):
            pair = 2 * g + j
            o_ref[i, :, pair * LANES:(pair + 1) * LANES] = o[j * CHUNK:(j + 1) * CHUNK].astype(o_ref.dtype)


def _attention(q, kv_arrays, kv_chunks, sinks, nb, s, banded):
    nseq = math.gcd(nb, ATTN_SEQS)
    q_spec = pl.BlockSpec((nseq, CHUNK, D_MODEL), lambda b, c: (b, c, 0))
    kv_specs = [pl.BlockSpec((nseq, CHUNK, KV_W), lambda b, c, f=f: (b, f(c), 0)) for f in kv_chunks]
    return pl.pallas_call(
        functools.partial(_attn_kernel, banded=banded),
        grid=(nb // nseq, s // CHUNK),
        in_specs=[pl.BlockSpec(memory_space=pltpu.SMEM), q_spec] + kv_specs,
        out_specs=q_spec,
        out_shape=jax.ShapeDtypeStruct((nb, s, D_MODEL), BF16),
        compiler_params=_params(("parallel", "parallel")),
        name="attention",
    )(sinks, q, *kv_arrays)


def _pad_cols(w, n):
    return jnp.pad(w, ((0, 0), (0, n - w.shape[1])))


def _pad_rows(w, n):
    return jnp.pad(w, ((0, n - w.shape[0]), (0, 0)))


def _rope_tables(pos):
    half = ROPE_DIM // 2
    inv = jnp.power(jnp.float32(ROPE_THETA), -jnp.arange(half, dtype=F32) * (2.0 / ROPE_DIM))
    ang = pos[:, None] * inv[None, :]
    cos, sin = jnp.cos(ang), jnp.sin(ang)
    rest = HEAD - ROPE_DIM
    ones = jnp.ones((pos.shape[0], rest), F32)
    zeros = jnp.zeros((pos.shape[0], rest), F32)
    z8 = jnp.zeros_like(sin)
    per_head = (jnp.concatenate([cos, cos, ones], axis=1),
                jnp.concatenate([-sin, z8, zeros], axis=1),
                jnp.concatenate([z8, sin, zeros], axis=1))
    return tuple(jnp.tile(t, (1, LANES // HEAD)) for t in per_head)


def _state_to_pairs(state):
    nb = state.shape[0]
    st = state.astype(F32).reshape(nb, PAIRS, 2, HEAD, HEAD)
    z = jnp.zeros_like(st[:, :, 0])
    top = jnp.concatenate([st[:, :, 0], z], axis=-1)
    bot = jnp.concatenate([z, st[:, :, 1]], axis=-1)
    return jnp.concatenate([top, bot], axis=-2)


def _pairs_to_state(st):
    nb = st.shape[0]
    s0 = st[:, :, :HEAD, :HEAD]
    s1 = st[:, :, HEAD:, HEAD:]
    return jnp.stack([s0, s1], axis=2).reshape(nb, 2 * PAIRS, HEAD, HEAD)


def _trunk(x, mods, kvmod, pos, prev_x, prev_wkv, past_k, past_v, w, nb, s, expert_w16=None):
    n = nb * s
    made_w16 = {}
    x = x.reshape(n, D_MODEL)
    mod0, mod1 = mods[0][:, None, :], mods[1][:, None, :]
    kvmod = kvmod[:, None, :]

    tm = min(n, PROJ_TILE)
    r, lw, k, v, a, b, g, last_x = _rwkv_proj(x, mod0, prev_x[:, None, :], w["rw"], nb, s, tm)
    seqs = lambda t: t.reshape(nb, s, D_MODEL)
    z, st = _wkv(*(seqs(t) for t in (r, lw, k, v, a, b, g)), _state_to_pairs(prev_wkv), w["rw"], nb, s)
    z = z.reshape(n, D_MODEL)

    def residual_moe(x, z, mixer, mod, l):
        x = _out_proj(x, z, mod[:, :, 2 * D_MODEL:3 * D_MODEL], mixer["w_o"], mixer["b_o"], nb, s, min(n, 1024))
        route = (mod, w["norm_g"][l][1:2], w["moe_route"][l])
        if n >= SPARSE_MIN_TOKENS:
            if expert_w16 is not None and l in expert_w16:
                gu16, dn16 = expert_w16[l]
            else:
                gu16, dn16 = w["moe_gu"][l].astype(BF16), w["moe_down"][l].astype(BF16)
            return _moe_sparse(x, *route, gu16, dn16, w["final_g"], nb, s, l == 1)
        y, *made_w16[l] = _moe(x, *route, w["moe_gu"], w["moe_down"], w["final_g"], nb, s, l, min(n, 1024), l == 1)
        return y

    x = residual_moe(x, z, w["rw"], mod0, 0)

    q, k_new, v_new = _qkv_proj(x, mod1, kvmod, w["at"], _rope_tables(pos), nb, s, tm)
    k_seq, v_seq = k_new.reshape(nb, s, KV_W), v_new.reshape(nb, s, KV_W)
    if past_k is None:
        arrays = [k_seq] * 3 + [v_seq] * 3
        chunks = [lambda c, d=d: jnp.maximum(c - d, 0) for d in (2, 1, 0)] * 2
    else:
        pk = past_k.astype(F32).reshape(nb, 2 * CHUNK, KV_W)
        pv = past_v.astype(F32).reshape(nb, 2 * CHUNK, KV_W)
        arrays = [pk, pk, k_seq, pv, pv, v_seq]
        chunks = [lambda c: 0, lambda c: 1, lambda c: 0] * 2
    o = _attention(q.reshape(nb, s, D_MODEL), arrays, chunks, w["at"]["sinks"], nb, s, past_k is None)
    y = residual_moe(x, o.reshape(n, D_MODEL), w["at"], mod1, 1)

    return (y.reshape(nb, s, D_MODEL), last_x.reshape(1, nb, D_MODEL), _pairs_to_state(st)[None], k_seq, v_seq,
            made_w16)


def kernel(x_prompt, x_sample, state_shift, state_wkv, cache_k, cache_v, c_prompt, c_sample, ada_w, ada_b, norm_g, rw_mu, rw_w_rkv, rw_w0, rw_w1, rw_w2, rw_a0, rw_a1, rw_a2, rw_g1, rw_g2, rw_k_k, rw_k_a, rw_r_k, rw_lnx_w, rw_lnx_b, rw_w_o, kv_ada_w, kv_ada_b, kv_norm_g, w_kv, b_kv, at_w_q, at_b_q, at_sinks, at_w_o, at_b_o, moe_w_group, moe_w_expert, moe_w_gu, moe_w_down, final_norm_g):
    bp, sp, _ = x_prompt.shape
    bs, ss, _ = x_sample.shape
    row = lambda t: t.reshape(1, -1).astype(F32)

    c_all = jnp.concatenate([c_prompt, c_sample], axis=0)
    mods = _cond_linear(c_all, ada_w, ada_b[:, None, :])
    kvmods = _cond_linear(c_all, kv_ada_w[None], kv_ada_b[None, None, :])[0]

    lora_pad, gate_pad = LANES, 2 * LANES
    w = {
        "norm_g": norm_g,
        "final_g": row(final_norm_g),
        "rw": {
            "norm_g": norm_g[0, 0:1], "mu": rw_mu[0], "w_rkv": rw_w_rkv[0].astype(BF16),
            "w0": row(rw_w0[0]), "w1": _pad_cols(rw_w1[0], lora_pad).astype(BF16),
            "w2": _pad_rows(rw_w2[0], lora_pad).astype(BF16),
            "a0": row(rw_a0[0]), "a1": _pad_cols(rw_a1[0], lora_pad).astype(BF16),
            "a2": _pad_rows(rw_a2[0], lora_pad).astype(BF16),
            "g1": _pad_cols(rw_g1[0], gate_pad).astype(BF16), "g2": _pad_rows(rw_g2[0], gate_pad).astype(BF16),
            "k_k": row(rw_k_k[0]), "k_a": row(rw_k_a[0]), "r_k": row(rw_r_k[0]),
            "lnx_w": row(rw_lnx_w[0]), "lnx_b": row(rw_lnx_b[0]),
            "w_o": rw_w_o[0].astype(BF16), "b_o": jnp.zeros((1, D_MODEL), F32),
        },
        "at": {
            "norm_gq": norm_g[1, 0:1], "norm_gkv": row(kv_norm_g),
            "w_q": at_w_q[0].astype(BF16), "b_q": row(at_b_q[0]),
            "w_kv": w_kv.astype(BF16), "b_kv": row(b_kv),
            "sinks": at_sinks[0].astype(F32),
            "w_o": at_w_o[0].astype(BF16), "b_o": row(at_b_o[0]),
        },
        "moe_route": jnp.pad(jnp.concatenate([moe_w_group, moe_w_expert], axis=-1),
                             ((0, 0), (0, 0), (0, LANES - N_GROUPS - N_EXPERTS))),
        "moe_gu": moe_w_gu,
        "moe_down": moe_w_down,
    }

    pos_p = jnp.arange(sp, dtype=F32)
    pos_s = PAST_LEN + jnp.arange(ss, dtype=F32)
    zero_x = jnp.zeros((bp, D_MODEL), x_prompt.dtype)
    zero_wkv = jnp.zeros((bp,) + state_wkv.shape[2:], state_wkv.dtype)
    y_s, s_shift, s_wkv, s_k, s_v, w16 = _trunk(x_sample, mods[:, bp:], kvmods[bp:], pos_s, state_shift[0],
                                                state_wkv[0], cache_k, cache_v, w, bs, ss)
    y_p, p_shift, p_wkv, p_k, p_v, _ = _trunk(x_prompt, mods[:, :bp], kvmods[:bp], pos_p, zero_x, zero_wkv,
                                              None, None, w, bp, sp, w16)
    keep = min(2 * CHUNK, sp)
    heads = lambda t: t.reshape(t.shape[0], t.shape[1], N_KV, HEAD)
    return (y_p, y_s, p_shift, p_wkv.astype(state_wkv.dtype), heads(p_k[:, sp - keep:]), heads(p_v[:, sp - keep:]),
            s_shift, s_wkv.astype(state_wkv.dtype), heads(s_k), heads(s_v))
```

```python
import functools
import math

import jax
import jax.numpy as jnp
from jax import lax
from jax.experimental import pallas as pl
from jax.experimental.pallas import tpu as pltpu

F32 = jnp.float32
BF16 = jnp.bfloat16

D_MODEL = 1024
LANES = 128
HEAD = 64
PAIRS = D_MODEL // LANES
CHUNK = 64
WKV_SEQS = 4
PROJ_TILE = 512
ATTN_SEQS = 2
PAST_LEN = 4096
N_KV = 4
KV_W = N_KV * HEAD
ROPE_DIM = 16
ROPE_THETA = 500000.0
ATT_SCALE = HEAD ** -0.5
N_GROUPS = 4
EXP_PER_GROUP = 8
N_EXPERTS = N_GROUPS * EXP_PER_GROUP
D_EXPERT = D_MODEL // 4
DENSE_GROUP = 2
ROUTE_LANE0 = N_GROUPS
RMS_EPS = 1e-6
GN_EPS = 64e-5
NEG_INF = -1e30
VMEM_LIMIT = 56 * 1024 * 1024


def _params(sem):
    return pltpu.CompilerParams(dimension_semantics=sem, vmem_limit_bytes=VMEM_LIMIT)


def _dot(a, b):
    return jnp.dot(a, b, preferred_element_type=F32)


def _dot_nt(a, b):
    return lax.dot_general(a, b, (((1,), (1,)), ((), ())), preferred_element_type=F32)


def _dot_tn(a, b):
    return lax.dot_general(a, b, (((0,), (0,)), ((), ())), preferred_element_type=F32)


def _split2(x):
    hi = x.astype(BF16)
    lo = (x - hi.astype(F32)).astype(BF16)
    return hi, lo


def _dot_x3(a, b, dot=_dot):
    ah, al = _split2(a)
    bh, bl = _split2(b)
    return dot(ah, bh) + dot(ah, bl) + dot(al, bh)


def _sigmoid(x):
    return 1.0 / (1.0 + jnp.exp(-x))


def _rms(x):
    return x * lax.rsqrt(jnp.mean(x * x, axis=-1, keepdims=True) + RMS_EPS)


def _modulate(n, gain, shift, scale, nb):
    rows = n.shape[0]
    h = (n * gain).reshape(nb, rows // nb, D_MODEL)
    return (h * (1.0 + scale) + shift).reshape(rows, D_MODEL)


def _const_spec(shape):
    nd = len(shape)
    return pl.BlockSpec(shape, lambda *_: (0,) * nd)


def _cond_kernel(c_ref, w_ref, b_ref, o_ref):
    c = c_ref[...]
    cs = (c * _sigmoid(c)).astype(BF16)
    o_ref[...] = _dot(cs, w_ref[...].astype(BF16)) + b_ref[...]


def _cond_linear(c, w, b, tn=512):
    nl, _, n = w.shape
    m = c.shape[0]
    return pl.pallas_call(
        _cond_kernel,
        grid=(nl, n // tn),
        in_specs=[
            pl.BlockSpec((m, D_MODEL), lambda l, j: (0, 0)),
            pl.BlockSpec((None, D_MODEL, tn), lambda l, j: (l, 0, j)),
            pl.BlockSpec((None, 1, tn), lambda l, j: (l, 0, j)),
        ],
        out_specs=pl.BlockSpec((None, m, tn), lambda l, j: (l, 0, j)),
        out_shape=jax.ShapeDtypeStruct((nl, m, n), F32),
        compiler_params=_params(("parallel", "parallel")),
        name="cond_linear",
    )(c, w, b)


def _rwkv_proj_kernel(x_ref, mod_ref, prev_ref, ng_ref, mu_ref, wrkv_ref, w0_ref, w1_ref, w2_ref,
                      a0_ref, a1_ref, a2_ref, g1_ref, g2_ref, kk_ref, ka_ref,
                      r_ref, lw_ref, k_ref, v_ref, a_ref, b_ref, g_ref, last_ref, carry_ref):
    tm = x_ref.shape[0]
    tb = mod_ref.shape[0]
    rps = tm // tb

    @pl.when(pl.program_id(1) == 0)
    def _():
        carry_ref[...] = prev_ref[...]

    h = _modulate(_rms(x_ref[...]), ng_ref[...], mod_ref[:, :, 0:D_MODEL], mod_ref[:, :, D_MODEL:2 * D_MODEL], tb)
    seqs = lambda t: t.reshape(tb, rps, D_MODEL)
    first = lax.broadcasted_iota(jnp.int32, (tb, rps, 1), 1) == 0
    shifted = jnp.where(first, carry_ref[...], seqs(pltpu.roll(h, 1, axis=0))).reshape(tm, D_MODEL)
    carry_ref[...] = seqs(h)[:, rps - 1:rps, :]
    last_ref[...] = seqs(h)[:, rps - 1:rps, :]
    dx = shifted - h

    def mix(n):
        return (h + dx * mu_ref[n:n + 1, :]).astype(BF16)

    r = _dot(mix(0), wrkv_ref[0])
    k = _dot(mix(1), wrkv_ref[1])
    v = _dot(mix(2), wrkv_ref[2])
    ww = _dot(jnp.tanh(_dot(mix(3), w1_ref[...])).astype(BF16), w2_ref[...])
    z = -(w0_ref[...] + ww)
    softplus = jnp.maximum(z, 0.0) + jnp.log(1.0 + jnp.exp(-jnp.abs(z)))
    lw = -jnp.exp(-softplus - 0.5)
    asig = _sigmoid(a0_ref[...] + _dot(_dot(mix(4), a1_ref[...]).astype(BF16), a2_ref[...]))
    g = _dot(_sigmoid(_dot(mix(5), g1_ref[...])).astype(BF16), g2_ref[...])

    kk = k * kk_ref[...]
    kk2 = kk * kk
    lo_cols = lax.broadcasted_iota(jnp.int32, (tm, LANES), 1) < HEAD

    def head_sums(t):
        lo_sum = jnp.sum(jnp.where(lo_cols, t, 0.0), axis=-1, keepdims=True)
        hi_sum = jnp.sum(jnp.where(lo_cols, 0.0, t), axis=-1, keepdims=True)
        return jnp.where(lo_cols, lo_sum, hi_sum)

    ss = jnp.concatenate([head_sums(kk2[:, p * LANES:(p + 1) * LANES]) for p in range(PAIRS)], axis=1)
    kkn = kk * lax.rsqrt(jnp.maximum(ss, 1e-24))

    r_ref[...] = r.astype(r_ref.dtype)
    lw_ref[...] = lw
    k_ref[...] = (k * (1.0 + (asig - 1.0) * ka_ref[...])).astype(k_ref.dtype)
    v_ref[...] = v.astype(v_ref.dtype)
    a_ref[...] = (-kkn).astype(a_ref.dtype)
    b_ref[...] = (kkn * asig).astype(b_ref.dtype)
    g_ref[...] = g.astype(g_ref.dtype)


def _rwkv_proj(x, mod, prev_x, w, nb, s, tm):
    n = nb * s
    tb, nt = max(tm // s, 1), max(s // tm, 1)
    row_spec = pl.BlockSpec((tm, D_MODEL), lambda b, j: (b * nt + j, 0))
    per_seq = lambda width: pl.BlockSpec((tb, 1, width), lambda b, j: (b, 0, 0))
    vec = _const_spec((1, D_MODEL))
    lora_w, lora_g = w["w1"].shape[1], w["g1"].shape[1]
    in_specs = [
        row_spec,
        per_seq(6 * D_MODEL),
        per_seq(D_MODEL),
        vec, _const_spec((6, D_MODEL)), _const_spec((3, D_MODEL, D_MODEL)),
        vec, _const_spec((D_MODEL, lora_w)), _const_spec((lora_w, D_MODEL)),
        vec, _const_spec((D_MODEL, lora_w)), _const_spec((lora_w, D_MODEL)),
        _const_spec((D_MODEL, lora_g)), _const_spec((lora_g, D_MODEL)),
        vec, vec,
    ]
    act = lambda dt: jax.ShapeDtypeStruct((n, D_MODEL), dt)
    out_types = [act(BF16), act(F32)] + [act(BF16)] * 5
    outs = pl.pallas_call(
        _rwkv_proj_kernel,
        grid=(nb // tb, nt),
        in_specs=in_specs,
        out_specs=[row_spec] * 7 + [per_seq(D_MODEL)],
        out_shape=out_types + [jax.ShapeDtypeStruct((nb, 1, D_MODEL), F32)],
        scratch_shapes=[pltpu.VMEM((tb, 1, D_MODEL), F32)],
        compiler_params=_params(("parallel", "arbitrary")),
        name="rwkv_proj",
    )(x, mod, prev_x, w["norm_g"], w["mu"], w["w_rkv"], w["w0"], w["w1"], w["w2"],
      w["a0"], w["a1"], w["a2"], w["g1"], w["g2"], w["k_k"], w["k_a"])
    return outs


def _split3(x):
    hi = x.astype(BF16)
    rem = x - hi.astype(F32)
    mid = rem.astype(BF16)
    return hi, mid, (rem - mid.astype(F32)).astype(BF16)


def _wkv_kernel(r_ref, lw_ref, k_ref, v_ref, a_ref, b_ref, g_ref, s0_ref, rk_ref, lnw_ref, lnb_ref,
                z_ref, st_ref):
    @pl.when(pl.program_id(1) == 0)
    def _():
        st_ref[...] = s0_ref[...]

    nseq = r_ref.shape[0]
    wide = lambda ref: jnp.concatenate([ref[i] for i in range(nseq)], axis=1)
    per_seq = lambda ref: jnp.concatenate([ref[...]] * nseq, axis=1)
    b16 = lambda t: t.astype(BF16)
    pairs = range(nseq * PAIRS)
    blk = lambda t, p: t[:, p * LANES:(p + 1) * LANES]
    rows = lambda ts: jnp.concatenate(ts, axis=0)
    lo_cols = lax.broadcasted_iota(jnp.int32, (CHUNK, LANES), 1) < HEAD
    rr = lax.broadcasted_iota(jnp.int32, (LANES, LANES), 0)
    cc = lax.broadcasted_iota(jnp.int32, (LANES, LANES), 1)
    head0 = lambda t: jnp.where(lo_cols, t, 0.0)
    head1 = lambda t: jnp.where(lo_cols, 0.0, t)
    blockdiag = lambda q: rows([head0(q), head1(q)])

    r, lw, k, v, a, b = (wide(t).astype(F32) for t in (r_ref, lw_ref, k_ref, v_ref, a_ref, b_ref))
    ti = lax.broadcasted_iota(jnp.int32, (CHUNK, CHUNK), 0)
    tj = lax.broadcasted_iota(jnp.int32, (CHUNK, CHUNK), 1)
    tri = (tj <= ti).astype(BF16)
    cum = sum(_dot(tri, t) for t in _split3(lw))
    cl = cum[CHUNK - 1:CHUNK, :]
    e_neg = jnp.exp(-cum)
    e_end = jnp.exp(cl - cum)
    e_cl = jnp.exp(cl)
    at = b16(a * jnp.exp(cum - lw))
    rt = b16(r * jnp.exp(cum))
    bt = b * e_neg
    kt = k * e_neg
    bkh = rows([b16(b * e_end), b16(k * e_end)])
    v16 = b16(v)

    t_idx = rr & (CHUNK - 1)
    j_idx = cc & (CHUNK - 1)
    keep = (j_idx < t_idx) | ((rr >= CHUNK) & (j_idx == t_idx))
    g0, g1 = [], []
    for p in pairs:
        btp, ktp = blk(bt, p), blk(kt, p)
        w = b16(rows([head0(btp), head0(ktp), head1(ktp), head1(btp)]))
        g = _dot_nt(rows([blk(at, p), blk(rt, p)]), w)
        g0.append(jnp.where(keep, g[:, :LANES], 0.0))
        g1.append(jnp.where(keep, g[:, LANES:], 0.0))

    eye2 = jnp.where((lax.broadcasted_iota(jnp.int32, (CHUNK, LANES), 1) & (CHUNK - 1))
                     == lax.broadcasted_iota(jnp.int32, (CHUNK, LANES), 0), 1.0, 0.0)
    pc = [jnp.where(lo_cols, g0[p][:CHUNK], g1[p][:CHUNK]) for p in pairs]
    tc = [eye2 + pc[p] for p in pairs]
    q = [_dot(b16(pc[p]), b16(blockdiag(pc[p]))) for p in pairs]
    for _ in range(4):
        res = [_dot(b16(rows([tc[p], q[p]])), b16(blockdiag(q[p]))) for p in pairs]
        tc = [tc[p] + res[p][:CHUNK] for p in pairs]
        q = [res[p][CHUNK:] for p in pairs]
    tc = [b16(tc[p] + _dot(b16(tc[p]), b16(blockdiag(q[p])))) for p in pairs]

    akv = []
    for p in pairs:
        w_ak = b16(jnp.where(lo_cols, g1[p][:CHUNK], g0[p][:CHUNK]))
        vp = blk(v16, p)
        akv.append(_dot(w_ak, rows([head1(vp), head0(vp)])))

    s = [st_ref[p // PAIRS, p % PAIRS] for p in pairs]
    s16 = [b16(s[p]) for p in pairs]
    rhs = [_dot_nt(blk(at, p), s16[p]) + akv[p] for p in pairs]
    u = [_dot(tc[p], b16(rows([head0(rhs[p]), head1(rhs[p])]))) for p in pairs]
    u16 = [b16(u[p]) for p in pairs]

    ys = []
    bd_mask = (rr < HEAD) == (cc < HEAD)
    for p in pairs:
        vp = blk(v16, p)
        rbk = b16(jnp.concatenate([g0[p][CHUNK:], g1[p][CHUNK:]], axis=1))
        uv = rows([head0(u16[p]), head0(vp), head1(vp), head1(u16[p])])
        ys.append(_dot_nt(blk(rt, p), s16[p]) + _dot(rbk, uv))
        fresh = _dot_tn(rows([u16[p], vp]), blk(bkh, p))
        st_ref[p // PAIRS, p % PAIRS] = s[p] * blk(e_cl, p) + jnp.where(bd_mask, fresh, 0.0)

    def head_sums(t):
        lo_sum = jnp.sum(head0(t), axis=-1, keepdims=True)
        hi_sum = jnp.sum(head1(t), axis=-1, keepdims=True)
        return jnp.where(lo_cols, lo_sum, hi_sum)

    rkk = r * k * per_seq(rk_ref)
    for p in pairs:
        seq, cols = p // PAIRS, slice((p % PAIRS) * LANES, (p % PAIRS + 1) * LANES)
        yc = ys[p] - head_sums(ys[p]) * (1.0 / HEAD)
        yn = yc * lax.rsqrt(head_sums(yc * yc) * (1.0 / HEAD) + GN_EPS)
        zp = (yn * lnw_ref[:, cols] + lnb_ref[:, cols] + head_sums(blk(rkk, p)) * blk(v, p)) * g_ref[seq, :, cols].astype(F32)
        z_ref[seq, :, cols] = zp.astype(z_ref.dtype)


def _wkv(r, lw, k, v, a, b, g, st0, w, nb, s):
    nseq = math.gcd(nb, WKV_SEQS)
    blk = pl.BlockSpec((nseq, CHUNK, D_MODEL), lambda bi, c: (bi, c, 0))
    vec = _const_spec((1, D_MODEL))
    st_spec = pl.BlockSpec((nseq, PAIRS, LANES, LANES), lambda bi, c: (bi, 0, 0, 0))
    return pl.pallas_call(
        _wkv_kernel,
        grid=(nb // nseq, s // CHUNK),
        in_specs=[blk] * 7 + [st_spec, vec, vec, vec],
        out_specs=[blk, st_spec],
        out_shape=[jax.ShapeDtypeStruct((nb, s, D_MODEL), BF16),
                   jax.ShapeDtypeStruct((nb, PAIRS, LANES, LANES), F32)],
        compiler_params=_params(("parallel", "arbitrary")),
        name="wkv",
    )(r, lw, k, v, a, b, g, st0, w["r_k"], w["lnx_w"], w["lnx_b"])


def _out_proj_kernel(x_ref, z_ref, gate_ref, w_ref, bias_ref, o_ref, *, nb):
    rows = x_ref.shape[0]
    y = _dot(z_ref[...], w_ref[...]) + bias_ref[...]
    y = (y.reshape(nb, rows // nb, D_MODEL) * gate_ref[...]).reshape(rows, D_MODEL)
    o_ref[...] = x_ref[...] + y


def _out_proj(x, z, gate, w, bias, nb, s, tm):
    n = nb * s
    tb = max(tm // s, 1)
    per_seq = max(s // tm, 1)
    row_spec = pl.BlockSpec((tm, D_MODEL), lambda i: (i, 0))
    return pl.pallas_call(
        functools.partial(_out_proj_kernel, nb=tb),
        grid=(n // tm,),
        in_specs=[row_spec, row_spec,
                  pl.BlockSpec((tb, 1, D_MODEL), lambda i: (i // per_seq, 0, 0)),
                  _const_spec((D_MODEL, D_MODEL)), _const_spec((1, D_MODEL))],
        out_specs=row_spec,
        out_shape=jax.ShapeDtypeStruct((n, D_MODEL), F32),
        compiler_params=_params(("parallel",)),
        name="out_proj",
    )(x, z, gate, w, bias)


def _top2(logits):
    lane = lax.broadcasted_iota(jnp.int32, logits.shape, 1).astype(F32)
    first_of = lambda hit: jnp.min(jnp.where(hit, lane, float(LANES)), axis=-1, keepdims=True)
    lg = jnp.where(lane < N_GROUPS, logits, -jnp.inf)
    gmax = jnp.max(lg, axis=-1, keepdims=True)
    gi = first_of(lg == gmax)
    gp = 1.0 / jnp.sum(jnp.exp(lg - gmax), axis=-1, keepdims=True)
    first = ROUTE_LANE0 + gi * EXP_PER_GROUP
    le = jnp.where((lane >= first) & (lane < first + EXP_PER_GROUP), logits, -jnp.inf)
    top1 = jnp.max(le, axis=-1, keepdims=True)
    i1 = first_of(le == top1)
    le2 = jnp.where(lane == i1, -jnp.inf, le)
    top2 = jnp.max(le2, axis=-1, keepdims=True)
    i2 = first_of(le2 == top2)
    e2 = jnp.exp(top2 - top1)
    w1 = gp / (1.0 + e2)
    return i1, i2, w1, w1 * e2


def _route(logits):
    lane = lax.broadcasted_iota(jnp.int32, logits.shape, 1).astype(F32)
    i1, i2, w1, w2 = _top2(logits)
    return jnp.where(lane == i1, w1, 0.0) + jnp.where(lane == i2, w2, 0.0)


def _moe_kernel(x_ref, mod_ref, ng_ref, wr_ref, wgu_ref, wdn_ref, fg_ref, o_ref, gu16_ref, dn16_ref,
                h_scr, cw_scr, acc_scr, *, nb, final_norm):
    e = pl.program_id(1)
    rows = x_ref.shape[0]

    @pl.when(e == 0)
    def _():
        shift = mod_ref[:, :, 3 * D_MODEL:4 * D_MODEL]
        scale = mod_ref[:, :, 4 * D_MODEL:5 * D_MODEL]
        h = _modulate(_rms(x_ref[...]), ng_ref[...], shift, scale, nb)
        h_scr[...] = h.astype(BF16)
        cw_scr[...] = _route(_dot_x3(h, wr_ref[...]))
        acc_scr[...] = jnp.zeros_like(acc_scr)

    lane = lax.broadcasted_iota(jnp.int32, (rows, LANES), 1)
    gu16_ref[...] = wgu_ref[...].astype(BF16)
    dn16_ref[...] = wdn_ref[...].astype(BF16)
    hid = []
    for j in range(DENSE_GROUP):
        cw = jnp.sum(jnp.where(lane == ROUTE_LANE0 + e * DENSE_GROUP + j, cw_scr[...], 0.0), axis=-1, keepdims=True)
        gu = _dot(h_scr[...], gu16_ref[j])
        gate = gu[:, :D_EXPERT]
        hid.append((gate * _sigmoid(gate) * gu[:, D_EXPERT:] * cw).astype(BF16))
    acc_scr[...] += _dot(jnp.concatenate(hid, axis=1), dn16_ref[...].reshape(DENSE_GROUP * D_EXPERT, D_MODEL))

    @pl.when(e == pl.num_programs(1) - 1)
    def _():
        g2 = mod_ref[:, :, 5 * D_MODEL:6 * D_MODEL]
        y = (acc_scr[...].reshape(nb, rows // nb, D_MODEL) * g2).reshape(rows, D_MODEL)
        out = x_ref[...] + y
        if final_norm:
            out = _rms(out) * fg_ref[...]
        o_ref[...] = out


def _moe(x, mod, norm_g, w_route, w_gu, w_down, final_g, nb, s, layer, tm, final_norm):
    n = nb * s
    tb = max(tm // s, 1)
    per_seq = max(s // tm, 1)
    row_spec = pl.BlockSpec((tm, D_MODEL), lambda i, e: (i, 0))
    return pl.pallas_call(
        functools.partial(_moe_kernel, nb=tb, final_norm=final_norm),
        grid=(n // tm, N_EXPERTS // DENSE_GROUP),
        in_specs=[row_spec,
                  pl.BlockSpec((tb, 1, 6 * D_MODEL), lambda i, e: (i // per_seq, 0, 0)),
                  _const_spec((1, D_MODEL)), _const_spec((D_MODEL, LANES)),
                  pl.BlockSpec((None, DENSE_GROUP, D_MODEL, 2 * D_EXPERT), lambda i, e: (layer, e, 0, 0)),
                  pl.BlockSpec((None, DENSE_GROUP, D_EXPERT, D_MODEL), lambda i, e: (layer, e, 0, 0)),
                  _const_spec((1, D_MODEL))],
        out_specs=[row_spec,
                   pl.BlockSpec((DENSE_GROUP, D_MODEL, 2 * D_EXPERT), lambda i, e: (e, 0, 0)),
                   pl.BlockSpec((DENSE_GROUP, D_EXPERT, D_MODEL), lambda i, e: (e, 0, 0))],
        out_shape=[jax.ShapeDtypeStruct((n, D_MODEL), F32),
                   jax.ShapeDtypeStruct((N_EXPERTS, D_MODEL, 2 * D_EXPERT), BF16),
                   jax.ShapeDtypeStruct((N_EXPERTS, D_EXPERT, D_MODEL), BF16)],
        scratch_shapes=[pltpu.VMEM((tm, D_MODEL), BF16), pltpu.VMEM((tm, LANES), F32),
                        pltpu.VMEM((tm, D_MODEL), F32)],
        compiler_params=_params(("arbitrary", "arbitrary")),
        name="moe",
    )(x, mod, norm_g, w_route, w_gu, w_down, final_g)


PAIRS_PER_GROUP = EXP_PER_GROUP * (EXP_PER_GROUP - 1) // 2
N_BUCKETS = N_GROUPS * PAIRS_PER_GROUP
BUCKET_TILE = 256
ROW_W = D_MODEL + LANES
META_BUCKET, META_RANK, META_W_LO, META_W_HI = 0, 1, 2, 3
ROUTER_TILE = 1024
MOVE_TILE = 1024
DMA_UNROLL = 8
SPARSE_MIN_TOKENS = 4096


def _moe_norm(x, mod_ref, ng_ref, nb):
    shift = mod_ref[:, :, 3 * D_MODEL:4 * D_MODEL]
    scale = mod_ref[:, :, 4 * D_MODEL:5 * D_MODEL]
    return _modulate(_rms(x), ng_ref[...], shift, scale, nb)


def _router_kernel(x_ref, mod_ref, ng_ref, wr_ref, earlier_ref, meta_ref, cnt_ref, *, nb):
    @pl.when(pl.program_id(0) == 0)
    def _():
        cnt_ref[...] = jnp.zeros_like(cnt_ref)

    rows = x_ref.shape[0]
    h = _moe_norm(x_ref[...], mod_ref, ng_ref, nb)
    i1, i2, w1, w2 = _top2(_dot_x3(h, wr_ref[...]))
    lo = (jnp.minimum(i1, i2) - ROUTE_LANE0).astype(jnp.int32)
    hi = (jnp.maximum(i1, i2) - ROUTE_LANE0).astype(jnp.int32)
    first_is_lo = i1 < i2
    a = lo & (EXP_PER_GROUP - 1)
    b = hi & (EXP_PER_GROUP - 1)
    group = lo >> (EXP_PER_GROUP.bit_length() - 1)
    bucket = group * PAIRS_PER_GROUP + ((a * (2 * EXP_PER_GROUP - 1 - a)) >> 1) + (b - a - 1)

    lane = lax.broadcasted_iota(jnp.int32, (rows, LANES), 1)
    mine = lane == bucket
    onehot = jnp.where(mine, 1.0, 0.0)
    seen = cnt_ref[...]
    before = _dot(earlier_ref[...], onehot.astype(BF16)) + seen
    rank = jnp.sum(jnp.where(mine, before, 0.0), axis=-1, keepdims=True)
    cnt_ref[...] = seen + jnp.sum(onehot, axis=0, keepdims=True)

    meta = jnp.where(lane == META_BUCKET, bucket.astype(F32), 0.0)
    meta = jnp.where(lane == META_RANK, rank, meta)
    meta = jnp.where(lane == META_W_LO, jnp.where(first_is_lo, w1, w2), meta)
    meta_ref[...] = jnp.where(lane == META_W_HI, jnp.where(first_is_lo, w2, w1), meta)


def _positions_kernel(meta_ref, start_ref, pos_ref):
    meta = meta_ref[...]
    lane = lax.broadcasted_iota(jnp.int32, meta.shape, 1)
    column = lambda c: jnp.sum(jnp.where(lane == c, meta, 0.0), axis=-1, keepdims=True)
    value = jnp.where(lane == column(META_BUCKET).astype(jnp.int32), start_ref[...] + column(META_RANK), 0.0)
    ones = jnp.ones((8, LANES), BF16)
    pos = sum(_dot_nt(ones, piece) for piece in _split3(value))
    pos_ref[...] = pos[0:1, :].astype(jnp.int32)


def _dispatch_kernel(pos_ref, x_ref, mod_ref, ng_ref, meta_ref, xs_ref, buf, sem, *, nb):
    i = pl.program_id(0)
    rows = x_ref.shape[0]
    slot = i % 2
    buf[slot, :, 0:D_MODEL] = _moe_norm(x_ref[...], mod_ref, ng_ref, nb)
    buf[slot, :, D_MODEL:ROW_W] = meta_ref[...]

    def send(r, carry):
        pltpu.make_async_copy(buf.at[slot, pl.ds(r, 1), :],
                              xs_ref.at[pl.ds(pos_ref[i * rows + r], 1), :], sem.at[slot]).start()
        return carry

    lax.fori_loop(0, rows, send, 0, unroll=DMA_UNROLL)

    def wait_all(sl):
        pltpu.make_async_copy(buf.at[sl], xs_ref.at[pl.ds(0, rows), :], sem.at[sl]).wait()

    @pl.when(i >= 1)
    def _():
        wait_all(1 - slot)

    @pl.when(i == pl.num_programs(0) - 1)
    def _():
        wait_all(slot)


def _experts_kernel(tile_ref, bucket_ref, lo_ref, hi_ref, first_ref, last_ref, used_ref, xs_ref,
                    gu_lo_ref, gu_hi_ref, dn_lo_ref, dn_hi_ref, ys_ref):
    k = pl.program_id(0)

    @pl.when(k < used_ref[0])
    def _():
        @pl.when((k == 0) | (tile_ref[k] != tile_ref[jnp.maximum(k - 1, 0)]))
        def _():
            ys_ref[...] = jnp.zeros_like(ys_ref)

        def run(base, rows):
            window = pl.ds(base, rows)
            x16 = xs_ref[window, 0:D_MODEL].astype(BF16)
            meta = xs_ref[window, D_MODEL:ROW_W]
            lane = lax.broadcasted_iota(jnp.int32, (rows, LANES), 1)
            column = lambda c: jnp.sum(jnp.where(lane == c, meta, 0.0), axis=-1, keepdims=True)
            member = column(META_BUCKET) == bucket_ref[k].astype(F32)

            def expert(gu_ref, dn_ref, col):
                gu = _dot(x16, gu_ref[...])
                gate = gu[:, :D_EXPERT]
                hid = gate * _sigmoid(gate) * gu[:, D_EXPERT:] * jnp.where(member, column(col), 0.0)
                return _dot(hid.astype(BF16), dn_ref[...])

            ys_ref[window, :] += expert(gu_lo_ref, dn_lo_ref, META_W_LO) + expert(gu_hi_ref, dn_hi_ref, META_W_HI)

        sublanes = 8
        aligned = first_ref[k] // sublanes * sublanes
        done = False
        for rows in (BUCKET_TILE // 4, BUCKET_TILE // 2):
            base = jnp.minimum(aligned, BUCKET_TILE - rows)
            fits = jnp.logical_and(last_ref[k] - base <= rows, jnp.logical_not(done))

            @pl.when(fits)
            def _(base=base, rows=rows):
                run(pl.multiple_of(base, sublanes), rows)

            done = jnp.logical_or(done, fits)

        @pl.when(jnp.logical_not(done))
        def _():
            run(0, BUCKET_TILE)


def _combine_kernel(pos_ref, x_ref, mod_ref, fg_ref, ys_ref, o_ref, buf, sem, *, nb, final_norm):
    i = pl.program_id(0)
    rows = x_ref.shape[0]
    slot = i % 2

    def fetch(step, sl):
        def one(r, carry):
            pltpu.make_async_copy(ys_ref.at[pl.ds(pos_ref[step * rows + r], 1), :],
                                  buf.at[sl, pl.ds(r, 1), :], sem.at[sl]).start()
            return carry
        lax.fori_loop(0, rows, one, 0, unroll=DMA_UNROLL)

    @pl.when(i == 0)
    def _():
        fetch(0, 0)

    pltpu.make_async_copy(ys_ref.at[pl.ds(0, rows), :], buf.at[slot], sem.at[slot]).wait()

    @pl.when(i + 1 < pl.num_programs(0))
    def _():
        fetch(i + 1, 1 - slot)

    g2 = mod_ref[:, :, 5 * D_MODEL:6 * D_MODEL]
    y = (buf[slot].reshape(nb, rows // nb, D_MODEL) * g2).reshape(rows, D_MODEL)
    out = x_ref[...] + y
    if final_norm:
        out = _rms(out) * fg_ref[...]
    o_ref[...] = out


def _bucket_experts():
    lo, hi = [], []
    for g in range(N_GROUPS):
        for a in range(EXP_PER_GROUP):
            for b in range(a + 1, EXP_PER_GROUP):
                lo.append(g * EXP_PER_GROUP + a)
                hi.append(g * EXP_PER_GROUP + b)
    return jnp.asarray(lo, jnp.int32), jnp.asarray(hi, jnp.int32)


def _moe_sparse(x, mod, norm_g, w_route, w_gu, w_down, final_g, nb, s, final_norm):
    n = nb * s
    move_params = pltpu.CompilerParams(dimension_semantics=("arbitrary",), vmem_limit_bytes=VMEM_LIMIT,
                                       disable_bounds_checks=True)

    def token_specs(tm):
        tb, per_seq = max(tm // s, 1), max(s // tm, 1)
        row = pl.BlockSpec((tm, D_MODEL), lambda i, *_: (i, 0))
        modspec = pl.BlockSpec((tb, 1, 6 * D_MODEL), lambda i, *_: (i // per_seq, 0, 0))
        meta = pl.BlockSpec((tm, LANES), lambda i, *_: (i, 0))
        return tb, row, modspec, meta

    vec = pl.BlockSpec((1, D_MODEL), lambda *_: (0, 0))
    tb, row, modspec, metaspec = token_specs(ROUTER_TILE)
    meta, counts = pl.pallas_call(
        functools.partial(_router_kernel, nb=tb),
        grid=(n // ROUTER_TILE,),
        in_specs=[row, modspec, vec, pl.BlockSpec((D_MODEL, LANES), lambda i: (0, 0)),
                  pl.BlockSpec((ROUTER_TILE, ROUTER_TILE), lambda i: (0, 0))],
        out_specs=[metaspec, pl.BlockSpec((1, LANES), lambda i: (0, 0))],
        out_shape=[jax.ShapeDtypeStruct((n, LANES), F32), jax.ShapeDtypeStruct((1, LANES), F32)],
        compiler_params=_params(("arbitrary",)),
        name="moe_router",
    )(x, mod, norm_g, w_route, jnp.tril(jnp.ones((ROUTER_TILE, ROUTER_TILE), BF16), -1))

    counts = counts[0, :N_BUCKETS].astype(jnp.int32)
    end = jnp.cumsum(counts)
    start = end - counts
    first_tile = start // BUCKET_TILE
    per_bucket = jnp.where(counts > 0, (end - 1) // BUCKET_TILE - first_tile + 1, 0)
    item_end = jnp.cumsum(per_bucket)
    used = item_end[-1:]
    n_items = n // BUCKET_TILE + N_BUCKETS
    k = jnp.minimum(jnp.arange(n_items, dtype=jnp.int32), used - 1)
    item_bucket = jnp.sum(item_end[None, :] <= k[:, None], axis=1, dtype=jnp.int32)
    item_tile = first_tile[item_bucket] + k - (item_end - per_bucket)[item_bucket]
    bucket_lo, bucket_hi = _bucket_experts()
    item_lo, item_hi = bucket_lo[item_bucket], bucket_hi[item_bucket]
    tile_row0 = item_tile * BUCKET_TILE
    item_first = jnp.clip(start[item_bucket] - tile_row0, 0, BUCKET_TILE)
    item_last = jnp.clip(end[item_bucket] - tile_row0, 0, BUCKET_TILE)
    start_row = jnp.pad(start, (0, LANES - N_BUCKETS)).astype(F32)[None, :]
    pos = pl.pallas_call(
        _positions_kernel,
        grid=(n // ROUTER_TILE,),
        in_specs=[metaspec, pl.BlockSpec((1, LANES), lambda i: (0, 0))],
        out_specs=pl.BlockSpec((1, ROUTER_TILE), lambda i: (0, i)),
        out_shape=jax.ShapeDtypeStruct((1, n), jnp.int32),
        compiler_params=_params(("parallel",)),
        name="moe_positions",
    )(meta, start_row).reshape(n)

    tb, row, modspec, metaspec = token_specs(MOVE_TILE)
    any_spec = pl.BlockSpec(memory_space=pl.ANY)
    xs = pl.pallas_call(
        functools.partial(_dispatch_kernel, nb=tb),
        grid_spec=pltpu.PrefetchScalarGridSpec(
            num_scalar_prefetch=1, grid=(n // MOVE_TILE,),
            in_specs=[row, modspec, vec, metaspec], out_specs=any_spec,
            scratch_shapes=[pltpu.VMEM((2, MOVE_TILE, ROW_W), F32), pltpu.SemaphoreType.DMA((2,))]),
        out_shape=jax.ShapeDtypeStruct((n, ROW_W), F32),
        compiler_params=move_params,
        name="moe_dispatch",
    )(pos, x, mod, norm_g, meta)

    tile_of = lambda k, tile, *_: (tile[k], 0)
    ys = pl.pallas_call(
        _experts_kernel,
        grid_spec=pltpu.PrefetchScalarGridSpec(
            num_scalar_prefetch=7, grid=(n_items,),
            in_specs=[pl.BlockSpec((BUCKET_TILE, ROW_W), tile_of),
                      pl.BlockSpec((None, D_MODEL, 2 * D_EXPERT), lambda k, t, b, lo, hi, *_: (lo[k], 0, 0)),
                      pl.BlockSpec((None, D_MODEL, 2 * D_EXPERT), lambda k, t, b, lo, hi, *_: (hi[k], 0, 0)),
                      pl.BlockSpec((None, D_EXPERT, D_MODEL), lambda k, t, b, lo, hi, *_: (lo[k], 0, 0)),
                      pl.BlockSpec((None, D_EXPERT, D_MODEL), lambda k, t, b, lo, hi, *_: (hi[k], 0, 0))],
            out_specs=pl.BlockSpec((BUCKET_TILE, D_MODEL), tile_of)),
        out_shape=jax.ShapeDtypeStruct((n, D_MODEL), F32),
        compiler_params=_params(("arbitrary",)),
        name="moe_experts",
    )(item_tile, item_bucket, item_lo, item_hi, item_first, item_last, used, xs, w_gu, w_gu, w_down, w_down)

    return pl.pallas_call(
        functools.partial(_combine_kernel, nb=tb, final_norm=final_norm),
        grid_spec=pltpu.PrefetchScalarGridSpec(
            num_scalar_prefetch=1, grid=(n // MOVE_TILE,),
            in_specs=[row, modspec, vec, any_spec], out_specs=row,
            scratch_shapes=[pltpu.VMEM((2, MOVE_TILE, D_MODEL), F32), pltpu.SemaphoreType.DMA((2,))]),
        out_shape=jax.ShapeDtypeStruct((n, D_MODEL), F32),
        compiler_params=move_params,
        name="moe_combine",
    )(pos, x, mod, final_g, ys)


def _rope(x, cos, sin_lo, sin_hi):
    return x * cos + pltpu.roll(x, LANES - ROPE_DIM // 2, axis=1) * sin_lo + pltpu.roll(x, ROPE_DIM // 2, axis=1) * sin_hi


def _qkv_kernel(x_ref, mod_ref, kvmod_ref, ngq_ref, ngkv_ref, wq_ref, bq_ref, wkv_ref, bkv_ref,
                cos_ref, slo_ref, shi_ref, q_ref, k_ref, v_ref):
    n = _rms(x_ref[...])
    tb = mod_ref.shape[0]
    hq = _modulate(n, ngq_ref[...], mod_ref[:, :, 0:D_MODEL], mod_ref[:, :, D_MODEL:2 * D_MODEL], tb)
    hkv = _modulate(n, ngkv_ref[...], kvmod_ref[:, :, 0:D_MODEL], kvmod_ref[:, :, D_MODEL:2 * D_MODEL], tb)
    cos, slo, shi = (jnp.concatenate([t[...]] * tb, axis=0) for t in (cos_ref, slo_ref, shi_ref))
    q = _dot(hq.astype(BF16), wq_ref[...]) + bq_ref[...]
    for p in range(PAIRS):
        cols = slice(p * LANES, (p + 1) * LANES)
        q_ref[:, cols] = (_rope(q[:, cols], cos, slo, shi) * ATT_SCALE).astype(q_ref.dtype)
    kv = _dot(hkv.astype(BF16), wkv_ref[...]) + bkv_ref[...]
    for p in range(KV_W // LANES):
        cols = slice(p * LANES, (p + 1) * LANES)
        k_ref[:, cols] = _rope(kv[:, cols], cos, slo, shi)
    v_ref[...] = kv[:, KV_W:]


def _qkv_proj(x, mod, kvmod, w, tables, nb, s, tm):
    n = nb * s
    tb, nt = max(tm // s, 1), max(s // tm, 1)
    row_spec = pl.BlockSpec((tm, D_MODEL), lambda b, j: (b * nt + j, 0))
    kv_spec = pl.BlockSpec((tm, KV_W), lambda b, j: (b * nt + j, 0))
    tab_spec = pl.BlockSpec((tm // tb, LANES), lambda b, j: (j, 0))
    vec = _const_spec((1, D_MODEL))
    return pl.pallas_call(
        _qkv_kernel,
        grid=(nb // tb, nt),
        in_specs=[row_spec,
                  pl.BlockSpec((tb, 1, 6 * D_MODEL), lambda b, j: (b, 0, 0)),
                  pl.BlockSpec((tb, 1, 2 * D_MODEL), lambda b, j: (b, 0, 0)),
                  vec, vec, _const_spec((D_MODEL, D_MODEL)), vec,
                  _const_spec((D_MODEL, 2 * KV_W)), _const_spec((1, 2 * KV_W)),
                  tab_spec, tab_spec, tab_spec],
        out_specs=[row_spec, kv_spec, kv_spec],
        out_shape=[jax.ShapeDtypeStruct((n, D_MODEL), BF16),
                   jax.ShapeDtypeStruct((n, KV_W), F32), jax.ShapeDtypeStruct((n, KV_W), F32)],
        compiler_params=_params(("parallel", "parallel")),
        name="qkv_proj",
    )(x, mod, kvmod, w["norm_gq"], w["norm_gkv"], w["w_q"], w["b_q"], w["w_kv"], w["b_kv"], *tables)


def _attn_kernel(sink_ref, q_ref, k0_ref, k1_ref, k2_ref, v0_ref, v1_ref, v2_ref, o_ref, *, banded):
    n = pl.program_id(1)
    seqs = range(q_ref.shape[0])
    k_all = [jnp.concatenate([k0_ref[i], k1_ref[i], k2_ref[i]], axis=0) for i in seqs]
    v_all = [jnp.concatenate([v0_ref[i], v1_ref[i], v2_ref[i]], axis=0) for i in seqs]
    lane = lax.broadcasted_iota(jnp.int32, (1, LANES), 1)
    lo = lane < HEAD
    col = lax.broadcasted_iota(jnp.int32, (1, 3 * CHUNK), 1)
    if banded:
        visible = (col >= 2 * CHUNK) | ((col >= CHUNK) & (n >= 1)) | (n >= 2)

    def halves(t, kv_head):
        blk = t[:, (kv_head // 2) * LANES:(kv_head // 2 + 1) * LANES]
        if kv_head % 2 == 0:
            t_lo = jnp.where(lo, blk, 0.0)
            t_hi = pltpu.roll(t_lo, HEAD, axis=1)
        else:
            t_hi = jnp.where(lo, 0.0, blk)
            t_lo = pltpu.roll(t_hi, HEAD, axis=1)
        return t_lo.astype(BF16), t_hi.astype(BF16)

    first_pair = lax.broadcasted_iota(jnp.int32, (2 * CHUNK, 1), 0) < CHUNK

    def probs(s, head_a, head_b):
        if banded:
            s = jnp.where(visible, s, NEG_INF)
        sink = jnp.where(first_pair, sink_ref[head_a], sink_ref[head_b])
        m = jnp.maximum(jnp.max(s, axis=-1, keepdims=True), sink)
        e = jnp.exp(s - m)
        return (e / (jnp.sum(e, axis=-1, keepdims=True) + jnp.exp(sink - m))).astype(BF16)

    units = [(i, g) for i in seqs for g in range(N_KV)]
    scores, values = [], []
    for i, g in units:
        k_lo, k_hi = halves(k_all[i], g)
        values.append(halves(v_all[i], g))
        q2 = jnp.concatenate([q_ref[i, :, (2 * g + j) * LANES:(2 * g + j + 1) * LANES] for j in range(2)], axis=0)
        scores.append((_dot_nt(q2, k_lo), _dot_nt(q2, k_hi)))
    for (i, g), (s_lo, s_hi), (v_lo, v_hi) in zip(units, scores, values):
        o = _dot(probs(s_lo, 4 * g, 4 * g + 2), v_lo) + _dot(probs(s_hi, 4 * g + 1, 4 * g + 3), v_hi)
        for j in range(2):
            pair = 2 * g + j
            o_ref[i, :, pair * LANES:(pair + 1) * LANES] = o[j * CHUNK:(j + 1) * CHUNK].astype(o_ref.dtype)


def _attention(q, kv_arrays, kv_chunks, sinks, nb, s, banded):
    nseq = math.gcd(nb, ATTN_SEQS)
    q_spec = pl.BlockSpec((nseq, CHUNK, D_MODEL), lambda b, c: (b, c, 0))
    kv_specs = [pl.BlockSpec((nseq, CHUNK, KV_W), lambda b, c, f=f: (b, f(c), 0)) for f in kv_chunks]
    return pl.pallas_call(
        functools.partial(_attn_kernel, banded=banded),
        grid=(nb // nseq, s // CHUNK),
        in_specs=[pl.BlockSpec(memory_space=pltpu.SMEM), q_spec] + kv_specs,
        out_specs=q_spec,
        out_shape=jax.ShapeDtypeStruct((nb, s, D_MODEL), BF16),
        compiler_params=_params(("parallel", "parallel")),
        name="attention",
    )(sinks, q, *kv_arrays)


def _pad_cols(w, n):
    return jnp.pad(w, ((0, 0), (0, n - w.shape[1])))


def _pad_rows(w, n):
    return jnp.pad(w, ((0, n - w.shape[0]), (0, 0)))


def _rope_tables(pos):
    half = ROPE_DIM // 2
    inv = jnp.power(jnp.float32(ROPE_THETA), -jnp.arange(half, dtype=F32) * (2.0 / ROPE_DIM))
    ang = pos[:, None] * inv[None, :]
    cos, sin = jnp.cos(ang), jnp.sin(ang)
    rest = HEAD - ROPE_DIM
    ones = jnp.ones((pos.shape[0], rest), F32)
    zeros = jnp.zeros((pos.shape[0], rest), F32)
    z8 = jnp.zeros_like(sin)
    per_head = (jnp.concatenate([cos, cos, ones], axis=1),
                jnp.concatenate([-sin, z8, zeros], axis=1),
                jnp.concatenate([z8, sin, zeros], axis=1))
    return tuple(jnp.tile(t, (1, LANES // HEAD)) for t in per_head)


def _state_to_pairs(state):
    nb = state.shape[0]
    st = state.astype(F32).reshape(nb, PAIRS, 2, HEAD, HEAD)
    z = jnp.zeros_like(st[:, :, 0])
    top = jnp.concatenate([st[:, :, 0], z], axis=-1)
    bot = jnp.concatenate([z, st[:, :, 1]], axis=-1)
    return jnp.concatenate([top, bot], axis=-2)


def _pairs_to_state(st):
    nb = st.shape[0]
    s0 = st[:, :, :HEAD, :HEAD]
    s1 = st[:, :, HEAD:, HEAD:]
    return jnp.stack([s0, s1], axis=2).reshape(nb, 2 * PAIRS, HEAD, HEAD)


def _trunk(x, mods, kvmod, pos, prev_x, prev_wkv, past_k, past_v, w, nb, s, expert_w16=None):
    n = nb * s
    made_w16 = {}
    x = x.reshape(n, D_MODEL)
    mod0, mod1 = mods[0][:, None, :], mods[1][:, None, :]
    kvmod = kvmod[:, None, :]

    tm = min(n, PROJ_TILE)
    r, lw, k, v, a, b, g, last_x = _rwkv_proj(x, mod0, prev_x[:, None, :], w["rw"], nb, s, tm)
    seqs = lambda t: t.reshape(nb, s, D_MODEL)
    z, st = _wkv(*(seqs(t) for t in (r, lw, k, v, a, b, g)), _state_to_pairs(prev_wkv), w["rw"], nb, s)
    z = z.reshape(n, D_MODEL)

    def residual_moe(x, z, mixer, mod, l):
        x = _out_proj(x, z, mod[:, :, 2 * D_MODEL:3 * D_MODEL], mixer["w_o"], mixer["b_o"], nb, s, min(n, 1024))
        route = (mod, w["norm_g"][l][1:2], w["moe_route"][l])
        if n >= SPARSE_MIN_TOKENS:
            if expert_w16 is not None and l in expert_w16:
                gu16, dn16 = expert_w16[l]
            else:
                gu16, dn16 = w["moe_gu"][l].astype(BF16), w["moe_down"][l].astype(BF16)
            return _moe_sparse(x, *route, gu16, dn16, w["final_g"], nb, s, l == 1)
        y, *made_w16[l] = _moe(x, *route, w["moe_gu"], w["moe_down"], w["final_g"], nb, s, l, min(n, 1024), l == 1)
        return y

    x = residual_moe(x, z, w["rw"], mod0, 0)

    q, k_new, v_new = _qkv_proj(x, mod1, kvmod, w["at"], _rope_tables(pos), nb, s, tm)
    k_seq, v_seq = k_new.reshape(nb, s, KV_W), v_new.reshape(nb, s, KV_W)
    if past_k is None:
        arrays = [k_seq] * 3 + [v_seq] * 3
        chunks = [lambda c, d=d: jnp.maximum(c - d, 0) for d in (2, 1, 0)] * 2
    else:
        pk = past_k.astype(F32).reshape(nb, 2 * CHUNK, KV_W)
        pv = past_v.astype(F32).reshape(nb, 2 * CHUNK, KV_W)
        arrays = [pk, pk, k_seq, pv, pv, v_seq]
        chunks = [lambda c: 0, lambda c: 1, lambda c: 0] * 2
    o = _attention(q.reshape(nb, s, D_MODEL), arrays, chunks, w["at"]["sinks"], nb, s, past_k is None)
    y = residual_moe(x, o.reshape(n, D_MODEL), w["at"], mod1, 1)

    return (y.reshape(nb, s, D_MODEL), last_x.reshape(1, nb, D_MODEL), _pairs_to_state(st)[None], k_seq, v_seq,
            made_w16)


def kernel(x_prompt, x_sample, state_shift, state_wkv, cache_k, cache_v, c_prompt, c_sample, ada_w, ada_b, norm_g, rw_mu, rw_w_rkv, rw_w0, rw_w1, rw_w2, rw_a0, rw_a1, rw_a2, rw_g1, rw_g2, rw_k_k, rw_k_a, rw_r_k, rw_lnx_w, rw_lnx_b, rw_w_o, kv_ada_w, kv_ada_b, kv_norm_g, w_kv, b_kv, at_w_q, at_b_q, at_sinks, at_w_o, at_b_o, moe_w_group, moe_w_expert, moe_w_gu, moe_w_down, final_norm_g):
    bp, sp, _ = x_prompt.shape
    bs, ss, _ = x_sample.shape
    row = lambda t: t.reshape(1, -1).astype(F32)

    c_all = jnp.concatenate([c_prompt, c_sample], axis=0)
    mods = _cond_linear(c_all, ada_w, ada_b[:, None, :])
    kvmods = _cond_linear(c_all, kv_ada_w[None], kv_ada_b[None, None, :])[0]

    lora_pad, gate_pad = LANES, 2 * LANES
    w = {
        "norm_g": norm_g,
        "final_g": row(final_norm_g),
        "rw": {
            "norm_g": norm_g[0, 0:1], "mu": rw_mu[0], "w_rkv": rw_w_rkv[0].astype(BF16),
            "w0": row(rw_w0[0]), "w1": _pad_cols(rw_w1[0], lora_pad).astype(BF16),
            "w2": _pad_rows(rw_w2[0], lora_pad).astype(BF16),
            "a0": row(rw_a0[0]), "a1": _pad_cols(rw_a1[0], lora_pad).astype(BF16),
            "a2": _pad_rows(rw_a2[0], lora_pad).astype(BF16),
            "g1": _pad_cols(rw_g1[0], gate_pad).astype(BF16), "g2": _pad_rows(rw_g2[0], gate_pad).astype(BF16),
            "k_k": row(rw_k_k[0]), "k_a": row(rw_k_a[0]), "r_k": row(rw_r_k[0]),
            "lnx_w": row(rw_lnx_w[0]), "lnx_b": row(rw_lnx_b[0]),
            "w_o": rw_w_o[0].astype(BF16), "b_o": jnp.zeros((1, D_MODEL), F32),
        },
        "at": {
            "norm_gq": norm_g[1, 0:1], "norm_gkv": row(kv_norm_g),
            "w_q": at_w_q[0].astype(BF16), "b_q": row(at_b_q[0]),
            "w_kv": w_kv.astype(BF16), "b_kv": row(b_kv),
            "sinks": at_sinks[0].astype(F32),
            "w_o": at_w_o[0].astype(BF16), "b_o": row(at_b_o[0]),
        },
        "moe_route": jnp.pad(jnp.concatenate([moe_w_group, moe_w_expert], axis=-1),
                             ((0, 0), (0, 0), (0, LANES - N_GROUPS - N_EXPERTS))),
        "moe_gu": moe_w_gu,
        "moe_down": moe_w_down,
    }

    pos_p = jnp.arange(sp, dtype=F32)
    pos_s = PAST_LEN + jnp.arange(ss, dtype=F32)
    zero_x = jnp.zeros((bp, D_MODEL), x_prompt.dtype)
    zero_wkv = jnp.zeros((bp,) + state_wkv.shape[2:], state_wkv.dtype)
    y_s, s_shift, s_wkv, s_k, s_v, w16 = _trunk(x_sample, mods[:, bp:], kvmods[bp:], pos_s, state_shift[0],
                                                state_wkv[0], cache_k, cache_v, w, bs, ss)
    y_p, p_shift, p_wkv, p_k, p_v, _ = _trunk(x_prompt, mods[:, :bp], kvmods[:bp], pos_p, zero_x, zero_wkv,
                                              None, None, w, bp, sp, w16)
    keep = min(2 * CHUNK, sp)
    heads = lambda t: t.reshape(t.shape[0], t.shape[1], N_KV, HEAD)
    return (y_p, y_s, p_shift, p_wkv.astype(state_wkv.dtype), heads(p_k[:, sp - keep:]), heads(p_v[:, sp - keep:]),
            s_shift, s_wkv.astype(state_wkv.dtype), heads(s_k), heads(s_v))
```

```python
import functools
import math

import jax
import jax.numpy as jnp
from jax import lax
from jax.experimental import pallas as pl
from jax.experimental.pallas import tpu as pltpu

F32 = jnp.float32
BF16 = jnp.bfloat16

D_MODEL = 1024
LANES = 128
HEAD = 64
PAIRS = D_MODEL // LANES
CHUNK = 64
WKV_SEQS = 4
PROJ_TILE = 512
ATTN_SEQS = 2
PAST_LEN = 4096
N_KV = 4
KV_W = N_KV * HEAD
ROPE_DIM = 16
ROPE_THETA = 500000.0
ATT_SCALE = HEAD ** -0.5
N_GROUPS = 4
EXP_PER_GROUP = 8
N_EXPERTS = N_GROUPS * EXP_PER_GROUP
D_EXPERT = D_MODEL // 4
DENSE_GROUP = 2
ROUTE_LANE0 = N_GROUPS
RMS_EPS = 1e-6
GN_EPS = 64e-5
NEG_INF = -1e30
VMEM_LIMIT = 56 * 1024 * 1024


def _params(sem):
    return pltpu.CompilerParams(dimension_semantics=sem, vmem_limit_bytes=VMEM_LIMIT)


def _dot(a, b):
    return jnp.dot(a, b, preferred_element_type=F32)


def _dot_nt(a, b):
    return lax.dot_general(a, b, (((1,), (1,)), ((), ())), preferred_element_type=F32)


def _dot_tn(a, b):
    return lax.dot_general(a, b, (((0,), (0,)), ((), ())), preferred_element_type=F32)


def _split2(x):
    hi = x.astype(BF16)
    lo = (x - hi.astype(F32)).astype(BF16)
    return hi, lo


def _dot_x3(a, b, dot=_dot):
    ah, al = _split2(a)
    bh, bl = _split2(b)
    return dot(ah, bh) + dot(ah, bl) + dot(al, bh)


def _sigmoid(x):
    return 1.0 / (1.0 + jnp.exp(-x))


def _rms(x):
    return x * lax.rsqrt(jnp.mean(x * x, axis=-1, keepdims=True) + RMS_EPS)


def _modulate(n, gain, shift, scale, nb):
    rows = n.shape[0]
    h = (n * gain).reshape(nb, rows // nb, D_MODEL)
    return (h * (1.0 + scale) + shift).reshape(rows, D_MODEL)


def _const_spec(shape):
    nd = len(shape)
    return pl.BlockSpec(shape, lambda *_: (0,) * nd)


def _cond_kernel(c_ref, w_ref, b_ref, o_ref):
    c = c_ref[...]
    cs = (c * _sigmoid(c)).astype(BF16)
    o_ref[...] = _dot(cs, w_ref[...].astype(BF16)) + b_ref[...]


def _cond_linear(c, w, b, tn=512):
    nl, _, n = w.shape
    m = c.shape[0]
    return pl.pallas_call(
        _cond_kernel,
        grid=(nl, n // tn),
        in_specs=[
            pl.BlockSpec((m, D_MODEL), lambda l, j: (0, 0)),
            pl.BlockSpec((None, D_MODEL, tn), lambda l, j: (l, 0, j)),
            pl.BlockSpec((None, 1, tn), lambda l, j: (l, 0, j)),
        ],
        out_specs=pl.BlockSpec((None, m, tn), lambda l, j: (l, 0, j)),
        out_shape=jax.ShapeDtypeStruct((nl, m, n), F32),
        compiler_params=_params(("parallel", "parallel")),
        name="cond_linear",
    )(c, w, b)


def _rwkv_proj_kernel(x_ref, mod_ref, prev_ref, ng_ref, mu_ref, wrkv_ref, w0_ref, w1_ref, w2_ref,
                      a0_ref, a1_ref, a2_ref, g1_ref, g2_ref, kk_ref, ka_ref,
                      r_ref, lw_ref, k_ref, v_ref, a_ref, b_ref, g_ref, last_ref, carry_ref):
    tm = x_ref.shape[0]
    tb = mod_ref.shape[0]
    rps = tm // tb

    @pl.when(pl.program_id(1) == 0)
    def _():
        carry_ref[...] = prev_ref[...]

    h = _modulate(_rms(x_ref[...]), ng_ref[...], mod_ref[:, :, 0:D_MODEL], mod_ref[:, :, D_MODEL:2 * D_MODEL], tb)
    seqs = lambda t: t.reshape(tb, rps, D_MODEL)
    first = lax.broadcasted_iota(jnp.int32, (tb, rps, 1), 1) == 0
    shifted = jnp.where(first, carry_ref[...], seqs(pltpu.roll(h, 1, axis=0))).reshape(tm, D_MODEL)
    carry_ref[...] = seqs(h)[:, rps - 1:rps, :]
    last_ref[...] = seqs(h)[:, rps - 1:rps, :]
    dx = shifted - h

    def mix(n):
        return (h + dx * mu_ref[n:n + 1, :]).astype(BF16)

    r = _dot(mix(0), wrkv_ref[0])
    k = _dot(mix(1), wrkv_ref[1])
    v = _dot(mix(2), wrkv_ref[2])
    ww = _dot(jnp.tanh(_dot(mix(3), w1_ref[...])).astype(BF16), w2_ref[...])
    z = -(w0_ref[...] + ww)
    softplus = jnp.maximum(z, 0.0) + jnp.log(1.0 + jnp.exp(-jnp.abs(z)))
    lw = -jnp.exp(-softplus - 0.5)
    asig = _sigmoid(a0_ref[...] + _dot(_dot(mix(4), a1_ref[...]).astype(BF16), a2_ref[...]))
    g = _dot(_sigmoid(_dot(mix(5), g1_ref[...])).astype(BF16), g2_ref[...])

    kk = k * kk_ref[...]
    kk2 = kk * kk
    lo_cols = lax.broadcasted_iota(jnp.int32, (tm, LANES), 1) < HEAD

    def head_sums(t):
        lo_sum = jnp.sum(jnp.where(lo_cols, t, 0.0), axis=-1, keepdims=True)
        hi_sum = jnp.sum(jnp.where(lo_cols, 0.0, t), axis=-1, keepdims=True)
        return jnp.where(lo_cols, lo_sum, hi_sum)

    ss = jnp.concatenate([head_sums(kk2[:, p * LANES:(p + 1) * LANES]) for p in range(PAIRS)], axis=1)
    kkn = kk * lax.rsqrt(jnp.maximum(ss, 1e-24))

    r_ref[...] = r.astype(r_ref.dtype)
    lw_ref[...] = lw
    k_ref[...] = (k * (1.0 + (asig - 1.0) * ka_ref[...])).astype(k_ref.dtype)
    v_ref[...] = v.astype(v_ref.dtype)
    a_ref[...] = (-kkn).astype(a_ref.dtype)
    b_ref[...] = (kkn * asig).astype(b_ref.dtype)
    g_ref[...] = g.astype(g_ref.dtype)


def _rwkv_proj(x, mod, prev_x, w, nb, s, tm):
    n = nb * s
    tb, nt = max(tm // s, 1), max(s // tm, 1)
    row_spec = pl.BlockSpec((tm, D_MODEL), lambda b, j: (b * nt + j, 0))
    per_seq = lambda width: pl.BlockSpec((tb, 1, width), lambda b, j: (b, 0, 0))
    vec = _const_spec((1, D_MODEL))
    lora_w, lora_g = w["w1"].shape[1], w["g1"].shape[1]
    in_specs = [
        row_spec,
        per_seq(6 * D_MODEL),
        per_seq(D_MODEL),
        vec, _const_spec((6, D_MODEL)), _const_spec((3, D_MODEL, D_MODEL)),
        vec, _const_spec((D_MODEL, lora_w)), _const_spec((lora_w, D_MODEL)),
        vec, _const_spec((D_MODEL, lora_w)), _const_spec((lora_w, D_MODEL)),
        _const_spec((D_MODEL, lora_g)), _const_spec((lora_g, D_MODEL)),
        vec, vec,
    ]
    act = lambda dt: jax.ShapeDtypeStruct((n, D_MODEL), dt)
    out_types = [act(BF16), act(F32)] + [act(BF16)] * 5
    outs = pl.pallas_call(
        _rwkv_proj_kernel,
        grid=(nb // tb, nt),
        in_specs=in_specs,
        out_specs=[row_spec] * 7 + [per_seq(D_MODEL)],
        out_shape=out_types + [jax.ShapeDtypeStruct((nb, 1, D_MODEL), F32)],
        scratch_shapes=[pltpu.VMEM((tb, 1, D_MODEL), F32)],
        compiler_params=_params(("parallel", "arbitrary")),
        name="rwkv_proj",
    )(x, mod, prev_x, w["norm_g"], w["mu"], w["w_rkv"], w["w0"], w["w1"], w["w2"],
      w["a0"], w["a1"], w["a2"], w["g1"], w["g2"], w["k_k"], w["k_a"])
    return outs


def _split3(x):
    hi = x.astype(BF16)
    rem = x - hi.astype(F32)
    mid = rem.astype(BF16)
    return hi, mid, (rem - mid.astype(F32)).astype(BF16)


def _wkv_kernel(r_ref, lw_ref, k_ref, v_ref, a_ref, b_ref, g_ref, s0_ref, rk_ref, lnw_ref, lnb_ref,
                z_ref, st_ref):
    @pl.when(pl.program_id(1) == 0)
    def _():
        st_ref[...] = s0_ref[...]

    nseq = r_ref.shape[0]
    wide = lambda ref: jnp.concatenate([ref[i] for i in range(nseq)], axis=1)
    per_seq = lambda ref: jnp.concatenate([ref[...]] * nseq, axis=1)
    b16 = lambda t: t.astype(BF16)
    pairs = range(nseq * PAIRS)
    blk = lambda t, p: t[:, p * LANES:(p + 1) * LANES]
    rows = lambda ts: jnp.concatenate(ts, axis=0)
    lo_cols = lax.broadcasted_iota(jnp.int32, (CHUNK, LANES), 1) < HEAD
    rr = lax.broadcasted_iota(jnp.int32, (LANES, LANES), 0)
    cc = lax.broadcasted_iota(jnp.int32, (LANES, LANES), 1)
    head0 = lambda t: jnp.where(lo_cols, t, 0.0)
    head1 = lambda t: jnp.where(lo_cols, 0.0, t)
    blockdiag = lambda q: rows([head0(q), head1(q)])

    r, lw, k, v, a, b = (wide(t).astype(F32) for t in (r_ref, lw_ref, k_ref, v_ref, a_ref, b_ref))
    ti = lax.broadcasted_iota(jnp.int32, (CHUNK, CHUNK), 0)
    tj = lax.broadcasted_iota(jnp.int32, (CHUNK, CHUNK), 1)
    tri = (tj <= ti).astype(BF16)
    cum = sum(_dot(tri, t) for t in _split3(lw))
    cl = cum[CHUNK - 1:CHUNK, :]
    e_neg = jnp.exp(-cum)
    e_end = jnp.exp(cl - cum)
    e_cl = jnp.exp(cl)
    at = b16(a * jnp.exp(cum - lw))
    rt = b16(r * jnp.exp(cum))
    bt = b * e_neg
    kt = k * e_neg
    bkh = rows([b16(b * e_end), b16(k * e_end)])
    v16 = b16(v)

    t_idx = rr & (CHUNK - 1)
    j_idx = cc & (CHUNK - 1)
    keep = (j_idx < t_idx) | ((rr >= CHUNK) & (j_idx == t_idx))
    g0, g1 = [], []
    for p in pairs:
        btp, ktp = blk(bt, p), blk(kt, p)
        w = b16(rows([head0(btp), head0(ktp), head1(ktp), head1(btp)]))
        g = _dot_nt(rows([blk(at, p), blk(rt, p)]), w)
        g0.append(jnp.where(keep, g[:, :LANES], 0.0))
        g1.append(jnp.where(keep, g[:, LANES:], 0.0))

    eye2 = jnp.where((lax.broadcasted_iota(jnp.int32, (CHUNK, LANES), 1) & (CHUNK - 1))
                     == lax.broadcasted_iota(jnp.int32, (CHUNK, LANES), 0), 1.0, 0.0)
    pc = [jnp.where(lo_cols, g0[p][:CHUNK], g1[p][:CHUNK]) for p in pairs]
    tc = [eye2 + pc[p] for p in pairs]
    q = [_dot(b16(pc[p]), b16(blockdiag(pc[p]))) for p in pairs]
    for _ in range(4):
        res = [_dot(b16(rows([tc[p], q[p]])), b16(blockdiag(q[p]))) for p in pairs]
        tc = [tc[p] + res[p][:CHUNK] for p in pairs]
        q = [res[p][CHUNK:] for p in pairs]
    tc = [b16(tc[p] + _dot(b16(tc[p]), b16(blockdiag(q[p])))) for p in pairs]

    akv = []
    for p in pairs:
        w_ak = b16(jnp.where(lo_cols, g1[p][:CHUNK], g0[p][:CHUNK]))
        vp = blk(v16, p)
        akv.append(_dot(w_ak, rows([head1(vp), head0(vp)])))

    s = [st_ref[p // PAIRS, p % PAIRS] for p in pairs]
    s16 = [b16(s[p]) for p in pairs]
    rhs = [_dot_nt(blk(at, p), s16[p]) + akv[p] for p in pairs]
    u = [_dot(tc[p], b16(rows([head0(rhs[p]), head1(rhs[p])]))) for p in pairs]
    u16 = [b16(u[p]) for p in pairs]

    ys = []
    bd_mask = (rr < HEAD) == (cc < HEAD)
    for p in pairs:
        vp = blk(v16, p)
        rbk = b16(jnp.concatenate([g0[p][CHUNK:], g1[p][CHUNK:]], axis=1))
        uv = rows([head0(u16[p]), head0(vp), head1(vp), head1(u16[p])])
        ys.append(_dot_nt(blk(rt, p), s16[p]) + _dot(rbk, uv))
        fresh = _dot_tn(rows([u16[p], vp]), blk(bkh, p))
        st_ref[p // PAIRS, p % PAIRS] = s[p] * blk(e_cl, p) + jnp.where(bd_mask, fresh, 0.0)

    def head_sums(t):
        lo_sum = jnp.sum(head0(t), axis=-1, keepdims=True)
        hi_sum = jnp.sum(head1(t), axis=-1, keepdims=True)
        return jnp.where(lo_cols, lo_sum, hi_sum)

    rkk = r * k * per_seq(rk_ref)
    for p in pairs:
        seq, cols = p // PAIRS, slice((p % PAIRS) * LANES, (p % PAIRS + 1) * LANES)
        yc = ys[p] - head_sums(ys[p]) * (1.0 / HEAD)
        yn = yc * lax.rsqrt(head_sums(yc * yc) * (1.0 / HEAD) + GN_EPS)
        zp = (yn * lnw_ref[:, cols] + lnb_ref[:, cols] + head_sums(blk(rkk, p)) * blk(v, p)) * g_ref[seq, :, cols].astype(F32)
        z_ref[seq, :, cols] = zp.astype(z_ref.dtype)


def _wkv(r, lw, k, v, a, b, g, st0, w, nb, s):
    nseq = math.gcd(nb, WKV_SEQS)
    blk = pl.BlockSpec((nseq, CHUNK, D_MODEL), lambda bi, c: (bi, c, 0))
    vec = _const_spec((1, D_MODEL))
    st_spec = pl.BlockSpec((nseq, PAIRS, LANES, LANES), lambda bi, c: (bi, 0, 0, 0))
    return pl.pallas_call(
        _wkv_kernel,
        grid=(nb // nseq, s // CHUNK),
        in_specs=[blk] * 7 + [st_spec, vec, vec, vec],
        out_specs=[blk, st_spec],
        out_shape=[jax.ShapeDtypeStruct((nb, s, D_MODEL), BF16),
                   jax.ShapeDtypeStruct((nb, PAIRS, LANES, LANES), F32)],
        compiler_params=_params(("parallel", "arbitrary")),
        name="wkv",
    )(r, lw, k, v, a, b, g, st0, w["r_k"], w["lnx_w"], w["lnx_b"])


def _out_proj_kernel(x_ref, z_ref, gate_ref, w_ref, bias_ref, o_ref, *, nb):
    rows = x_ref.shape[0]
    y = _dot(z_ref[...], w_ref[...]) + bias_ref[...]
    y = (y.reshape(nb, rows // nb, D_MODEL) * gate_ref[...]).reshape(rows, D_MODEL)
    o_ref[...] = x_ref[...] + y


def _out_proj(x, z, gate, w, bias, nb, s, tm):
    n = nb * s
    tb = max(tm // s, 1)
    per_seq = max(s // tm, 1)
    row_spec = pl.BlockSpec((tm, D_MODEL), lambda i: (i, 0))
    return pl.pallas_call(
        functools.partial(_out_proj_kernel, nb=tb),
        grid=(n // tm,),
        in_specs=[row_spec, row_spec,
                  pl.BlockSpec((tb, 1, D_MODEL), lambda i: (i // per_seq, 0, 0)),
                  _const_spec((D_MODEL, D_MODEL)), _const_spec((1, D_MODEL))],
        out_specs=row_spec,
        out_shape=jax.ShapeDtypeStruct((n, D_MODEL), F32),
        compiler_params=_params(("parallel",)),
        name="out_proj",
    )(x, z, gate, w, bias)


def _top2(logits):
    lane = lax.broadcasted_iota(jnp.int32, logits.shape, 1).astype(F32)
    first_of = lambda hit: jnp.min(jnp.where(hit, lane, float(LANES)), axis=-1, keepdims=True)
    lg = jnp.where(lane < N_GROUPS, logits, -jnp.inf)
    gmax = jnp.max(lg, axis=-1, keepdims=True)
    gi = first_of(lg == gmax)
    gp = 1.0 / jnp.sum(jnp.exp(lg - gmax), axis=-1, keepdims=True)
    first = ROUTE_LANE0 + gi * EXP_PER_GROUP
    le = jnp.where((lane >= first) & (lane < first + EXP_PER_GROUP), logits, -jnp.inf)
    top1 = jnp.max(le, axis=-1, keepdims=True)
    i1 = first_of(le == top1)
    le2 = jnp.where(lane == i1, -jnp.inf, le)
    top2 = jnp.max(le2, axis=-1, keepdims=True)
    i2 = first_of(le2 == top2)
    e2 = jnp.exp(top2 - top1)
    w1 = gp / (1.0 + e2)
    return i1, i2, w1, w1 * e2


def _route(logits):
    lane = lax.broadcasted_iota(jnp.int32, logits.shape, 1).astype(F32)
    i1, i2, w1, w2 = _top2(logits)
    return jnp.where(lane == i1, w1, 0.0) + jnp.where(lane == i2, w2, 0.0)


def _moe_kernel(x_ref, mod_ref, ng_ref, wr_ref, wgu_ref, wdn_ref, fg_ref, o_ref, gu16_ref, dn16_ref,
                h_scr, cw_scr, acc_scr, *, nb, final_norm):
    e = pl.program_id(1)
    rows = x_ref.shape[0]

    @pl.when(e == 0)
    def _():
        shift = mod_ref[:, :, 3 * D_MODEL:4 * D_MODEL]
        scale = mod_ref[:, :, 4 * D_MODEL:5 * D_MODEL]
        h = _modulate(_rms(x_ref[...]), ng_ref[...], shift, scale, nb)
        h_scr[...] = h.astype(BF16)
        cw_scr[...] = _route(_dot_x3(h, wr_ref[...]))
        acc_scr[...] = jnp.zeros_like(acc_scr)

    lane = lax.broadcasted_iota(jnp.int32, (rows, LANES), 1)
    gu16_ref[...] = wgu_ref[...].astype(BF16)
    dn16_ref[...] = wdn_ref[...].astype(BF16)
    hid = []
    for j in range(DENSE_GROUP):
        cw = jnp.sum(jnp.where(lane == ROUTE_LANE0 + e * DENSE_GROUP + j, cw_scr[...], 0.0), axis=-1, keepdims=True)
        gu = _dot(h_scr[...], gu16_ref[j])
        gate = gu[:, :D_EXPERT]
        hid.append((gate * _sigmoid(gate) * gu[:, D_EXPERT:] * cw).astype(BF16))
    acc_scr[...] += _dot(jnp.concatenate(hid, axis=1), dn16_ref[...].reshape(DENSE_GROUP * D_EXPERT, D_MODEL))

    @pl.when(e == pl.num_programs(1) - 1)
    def _():
        g2 = mod_ref[:, :, 5 * D_MODEL:6 * D_MODEL]
        y = (acc_scr[...].reshape(nb, rows // nb, D_MODEL) * g2).reshape(rows, D_MODEL)
        out = x_ref[...] + y
        if final_norm:
            out = _rms(out) * fg_ref[...]
        o_ref[...] = out


def _moe(x, mod, norm_g, w_route, w_gu, w_down, final_g, nb, s, layer, tm, final_norm):
    n = nb * s
    tb = max(tm // s, 1)
    per_seq = max(s // tm, 1)
    row_spec = pl.BlockSpec((tm, D_MODEL), lambda i, e: (i, 0))
    return pl.pallas_call(
        functools.partial(_moe_kernel, nb=tb, final_norm=final_norm),
        grid=(n // tm, N_EXPERTS // DENSE_GROUP),
        in_specs=[row_spec,
                  pl.BlockSpec((tb, 1, 6 * D_MODEL), lambda i, e: (i // per_seq, 0, 0)),
                  _const_spec((1, D_MODEL)), _const_spec((D_MODEL, LANES)),
                  pl.BlockSpec((None, DENSE_GROUP, D_MODEL, 2 * D_EXPERT), lambda i, e: (layer, e, 0, 0)),
                  pl.BlockSpec((None, DENSE_GROUP, D_EXPERT, D_MODEL), lambda i, e: (layer, e, 0, 0)),
                  _const_spec((1, D_MODEL))],
        out_specs=[row_spec,
                   pl.BlockSpec((DENSE_GROUP, D_MODEL, 2 * D_EXPERT), lambda i, e: (e, 0, 0)),
                   pl.BlockSpec((DENSE_GROUP, D_EXPERT, D_MODEL), lambda i, e: (e, 0, 0))],
        out_shape=[jax.ShapeDtypeStruct((n, D_MODEL), F32),
                   jax.ShapeDtypeStruct((N_EXPERTS, D_MODEL, 2 * D_EXPERT), BF16),
                   jax.ShapeDtypeStruct((N_EXPERTS, D_EXPERT, D_MODEL), BF16)],
        scratch_shapes=[pltpu.VMEM((tm, D_MODEL), BF16), pltpu.VMEM((tm, LANES), F32),
                        pltpu.VMEM((tm, D_MODEL), F32)],
        compiler_params=_params(("arbitrary", "arbitrary")),
        name="moe",
    )(x, mod, norm_g, w_route, w_gu, w_down, final_g)


PAIRS_PER_GROUP = EXP_PER_GROUP * (EXP_PER_GROUP - 1) // 2
N_BUCKETS = N_GROUPS * PAIRS_PER_GROUP
BUCKET_TILE = 256
ROW_W = D_MODEL + LANES
META_BUCKET, META_RANK, META_W_LO, META_W_HI = 0, 1, 2, 3
ROUTER_TILE = 1024
MOVE_TILE = 1024
DMA_UNROLL = 8
SPARSE_MIN_TOKENS = 4096


def _moe_norm(x, mod_ref, ng_ref, nb):
    shift = mod_ref[:, :, 3 * D_MODEL:4 * D_MODEL]
    scale = mod_ref[:, :, 4 * D_MODEL:5 * D_MODEL]
    return _modulate(_rms(x), ng_ref[...], shift, scale, nb)


def _router_kernel(x_ref, mod_ref, ng_ref, wr_ref, earlier_ref, meta_ref, cnt_ref, *, nb):
    @pl.when(pl.program_id(0) == 0)
    def _():
        cnt_ref[...] = jnp.zeros_like(cnt_ref)

    rows = x_ref.shape[0]
    h = _moe_norm(x_ref[...], mod_ref, ng_ref, nb)
    i1, i2, w1, w2 = _top2(_dot_x3(h, wr_ref[...]))
    lo = (jnp.minimum(i1, i2) - ROUTE_LANE0).astype(jnp.int32)
    hi = (jnp.maximum(i1, i2) - ROUTE_LANE0).astype(jnp.int32)
    first_is_lo = i1 < i2
    a = lo & (EXP_PER_GROUP - 1)
    b = hi & (EXP_PER_GROUP - 1)
    group = lo >> (EXP_PER_GROUP.bit_length() - 1)
    bucket = group * PAIRS_PER_GROUP + ((a * (2 * EXP_PER_GROUP - 1 - a)) >> 1) + (b - a - 1)

    lane = lax.broadcasted_iota(jnp.int32, (rows, LANES), 1)
    mine = lane == bucket
    onehot = jnp.where(mine, 1.0, 0.0)
    seen = cnt_ref[...]
    before = _dot(earlier_ref[...], onehot.astype(BF16)) + seen
    rank = jnp.sum(jnp.where(mine, before, 0.0), axis=-1, keepdims=True)
    cnt_ref[...] = seen + jnp.sum(onehot, axis=0, keepdims=True)

    meta = jnp.where(lane == META_BUCKET, bucket.astype(F32), 0.0)
    meta = jnp.where(lane == META_RANK, rank, meta)
    meta = jnp.where(lane == META_W_LO, jnp.where(first_is_lo, w1, w2), meta)
    meta_ref[...] = jnp.where(lane == META_W_HI, jnp.where(first_is_lo, w2, w1), meta)


def _positions_kernel(meta_ref, start_ref, pos_ref):
    meta = meta_ref[...]
    lane = lax.broadcasted_iota(jnp.int32, meta.shape, 1)
    column = lambda c: jnp.sum(jnp.where(lane == c, meta, 0.0), axis=-1, keepdims=True)
    value = jnp.where(lane == column(META_BUCKET).astype(jnp.int32), start_ref[...] + column(META_RANK), 0.0)
    ones = jnp.ones((8, LANES), BF16)
    pos = sum(_dot_nt(ones, piece) for piece in _split3(value))
    pos_ref[...] = pos[0:1, :].astype(jnp.int32)


def _dispatch_kernel(pos_ref, x_ref, mod_ref, ng_ref, meta_ref, xs_ref, buf, sem, *, nb):
    i = pl.program_id(0)
    rows = x_ref.shape[0]
    slot = i % 2
    buf[slot, :, 0:D_MODEL] = _moe_norm(x_ref[...], mod_ref, ng_ref, nb)
    buf[slot, :, D_MODEL:ROW_W] = meta_ref[...]

    def send(r, carry):
        pltpu.make_async_copy(buf.at[slot, pl.ds(r, 1), :],
                              xs_ref.at[pl.ds(pos_ref[i * rows + r], 1), :], sem.at[slot]).start()
        return carry

    lax.fori_loop(0, rows, send, 0, unroll=DMA_UNROLL)

    def wait_all(sl):
        pltpu.make_async_copy(buf.at[sl], xs_ref.at[pl.ds(0, rows), :], sem.at[sl]).wait()

    @pl.when(i >= 1)
    def _():
        wait_all(1 - slot)

    @pl.when(i == pl.num_programs(0) - 1)
    def _():
        wait_all(slot)


def _experts_kernel(tile_ref, bucket_ref, lo_ref, hi_ref, first_ref, last_ref, used_ref, xs_ref,
                    gu_lo_ref, gu_hi_ref, dn_lo_ref, dn_hi_ref, ys_ref):
    k = pl.program_id(0)
    half = BUCKET_TILE // 2

    @pl.when(k < used_ref[0])
    def _():
        @pl.when((k == 0) | (tile_ref[k] != tile_ref[jnp.maximum(k - 1, 0)]))
        def _():
            ys_ref[...] = jnp.zeros_like(ys_ref)

        def run(base, rows):
            window = pl.ds(base, rows)
            x16 = xs_ref[window, 0:D_MODEL].astype(BF16)
            meta = xs_ref[window, D_MODEL:ROW_W]
            lane = lax.broadcasted_iota(jnp.int32, (rows, LANES), 1)
            column = lambda c: jnp.sum(jnp.where(lane == c, meta, 0.0), axis=-1, keepdims=True)
            member = column(META_BUCKET) == bucket_ref[k].astype(F32)

            def expert(gu_ref, dn_ref, col):
                gu = _dot(x16, gu_ref[...])
                gate = gu[:, :D_EXPERT]
                hid = gate * _sigmoid(gate) * gu[:, D_EXPERT:] * jnp.where(member, column(col), 0.0)
                return _dot(hid.astype(BF16), dn_ref[...])

            ys_ref[window, :] += expert(gu_lo_ref, dn_lo_ref, META_W_LO) + expert(gu_hi_ref, dn_hi_ref, META_W_HI)

        sublanes = 8
        base = jnp.minimum(first_ref[k] // sublanes * sublanes, BUCKET_TILE - half)
        fits = last_ref[k] - base <= half

        @pl.when(fits)
        def _():
            run(pl.multiple_of(base, sublanes), half)

        @pl.when(jnp.logical_not(fits))
        def _():
            run(0, BUCKET_TILE)


def _combine_kernel(pos_ref, x_ref, mod_ref, fg_ref, ys_ref, o_ref, buf, sem, *, nb, final_norm):
    i = pl.program_id(0)
    rows = x_ref.shape[0]
    slot = i % 2

    def fetch(step, sl):
        def one(r, carry):
            pltpu.make_async_copy(ys_ref.at[pl.ds(pos_ref[step * rows + r], 1), :],
                                  buf.at[sl, pl.ds(r, 1), :], sem.at[sl]).start()
            return carry
        lax.fori_loop(0, rows, one, 0, unroll=DMA_UNROLL)

    @pl.when(i == 0)
    def _():
        fetch(0, 0)

    pltpu.make_async_copy(ys_ref.at[pl.ds(0, rows), :], buf.at[slot], sem.at[slot]).wait()

    @pl.when(i + 1 < pl.num_programs(0))
    def _():
        fetch(i + 1, 1 - slot)

    g2 = mod_ref[:, :, 5 * D_MODEL:6 * D_MODEL]
    y = (buf[slot].reshape(nb, rows // nb, D_MODEL) * g2).reshape(rows, D_MODEL)
    out = x_ref[...] + y
    if final_norm:
        out = _rms(out) * fg_ref[...]
    o_ref[...] = out


def _bucket_experts():
    lo, hi = [], []
    for g in range(N_GROUPS):
        for a in range(EXP_PER_GROUP):
            for b in range(a + 1, EXP_PER_GROUP):
                lo.append(g * EXP_PER_GROUP + a)
                hi.append(g * EXP_PER_GROUP + b)
    return jnp.asarray(lo, jnp.int32), jnp.asarray(hi, jnp.int32)


def _moe_sparse(x, mod, norm_g, w_route, w_gu, w_down, final_g, nb, s, final_norm):
    n = nb * s
    move_params = pltpu.CompilerParams(dimension_semantics=("arbitrary",), vmem_limit_bytes=VMEM_LIMIT,
                                       disable_bounds_checks=True)

    def token_specs(tm):
        tb, per_seq = max(tm // s, 1), max(s // tm, 1)
        row = pl.BlockSpec((tm, D_MODEL), lambda i, *_: (i, 0))
        modspec = pl.BlockSpec((tb, 1, 6 * D_MODEL), lambda i, *_: (i // per_seq, 0, 0))
        meta = pl.BlockSpec((tm, LANES), lambda i, *_: (i, 0))
        return tb, row, modspec, meta

    vec = pl.BlockSpec((1, D_MODEL), lambda *_: (0, 0))
    tb, row, modspec, metaspec = token_specs(ROUTER_TILE)
    meta, counts = pl.pallas_call(
        functools.partial(_router_kernel, nb=tb),
        grid=(n // ROUTER_TILE,),
        in_specs=[row, modspec, vec, pl.BlockSpec((D_MODEL, LANES), lambda i: (0, 0)),
                  pl.BlockSpec((ROUTER_TILE, ROUTER_TILE), lambda i: (0, 0))],
        out_specs=[metaspec, pl.BlockSpec((1, LANES), lambda i: (0, 0))],
        out_shape=[jax.ShapeDtypeStruct((n, LANES), F32), jax.ShapeDtypeStruct((1, LANES), F32)],
        compiler_params=_params(("arbitrary",)),
        name="moe_router",
    )(x, mod, norm_g, w_route, jnp.tril(jnp.ones((ROUTER_TILE, ROUTER_TILE), BF16), -1))

    counts = counts[0, :N_BUCKETS].astype(jnp.int32)
    end = jnp.cumsum(counts)
    start = end - counts
    first_tile = start // BUCKET_TILE
    per_bucket = jnp.where(counts > 0, (end - 1) // BUCKET_TILE - first_tile + 1, 0)
    item_end = jnp.cumsum(per_bucket)
    used = item_end[-1:]
    n_items = n // BUCKET_TILE + N_BUCKETS
    k = jnp.minimum(jnp.arange(n_items, dtype=jnp.int32), used - 1)
    item_bucket = jnp.sum(item_end[None, :] <= k[:, None], axis=1, dtype=jnp.int32)
    item_tile = first_tile[item_bucket] + k - (item_end - per_bucket)[item_bucket]
    bucket_lo, bucket_hi = _bucket_experts()
    item_lo, item_hi = bucket_lo[item_bucket], bucket_hi[item_bucket]
    tile_row0 = item_tile * BUCKET_TILE
    item_first = jnp.clip(start[item_bucket] - tile_row0, 0, BUCKET_TILE)
    item_last = jnp.clip(end[item_bucket] - tile_row0, 0, BUCKET_TILE)
    start_row = jnp.pad(start, (0, LANES - N_BUCKETS)).astype(F32)[None, :]
    pos = pl.pallas_call(
        _positions_kernel,
        grid=(n // ROUTER_TILE,),
        in_specs=[metaspec, pl.BlockSpec((1, LANES), lambda i: (0, 0))],
        out_specs=pl.BlockSpec((1, ROUTER_TILE), lambda i: (0, i)),
        out_shape=jax.ShapeDtypeStruct((1, n), jnp.int32),
        compiler_params=_params(("parallel",)),
        name="moe_positions",
    )(meta, start_row).reshape(n)

    tb, row, modspec, metaspec = token_specs(MOVE_TILE)
    any_spec = pl.BlockSpec(memory_space=pl.ANY)
    xs = pl.pallas_call(
        functools.partial(_dispatch_kernel, nb=tb),
        grid_spec=pltpu.PrefetchScalarGridSpec(
            num_scalar_prefetch=1, grid=(n // MOVE_TILE,),
            in_specs=[row, modspec, vec, metaspec], out_specs=any_spec,
            scratch_shapes=[pltpu.VMEM((2, MOVE_TILE, ROW_W), F32), pltpu.SemaphoreType.DMA((2,))]),
        out_shape=jax.ShapeDtypeStruct((n, ROW_W), F32),
        compiler_params=move_params,
        name="moe_dispatch",
    )(pos, x, mod, norm_g, meta)

    tile_of = lambda k, tile, *_: (tile[k], 0)
    ys = pl.pallas_call(
        _experts_kernel,
        grid_spec=pltpu.PrefetchScalarGridSpec(
            num_scalar_prefetch=7, grid=(n_items,),
            in_specs=[pl.BlockSpec((BUCKET_TILE, ROW_W), tile_of),
                      pl.BlockSpec((None, D_MODEL, 2 * D_EXPERT), lambda k, t, b, lo, hi, *_: (lo[k], 0, 0)),
                      pl.BlockSpec((None, D_MODEL, 2 * D_EXPERT), lambda k, t, b, lo, hi, *_: (hi[k], 0, 0)),
                      pl.BlockSpec((None, D_EXPERT, D_MODEL), lambda k, t, b, lo, hi, *_: (lo[k], 0, 0)),
                      pl.BlockSpec((None, D_EXPERT, D_MODEL), lambda k, t, b, lo, hi, *_: (hi[k], 0, 0))],
            out_specs=pl.BlockSpec((BUCKET_TILE, D_MODEL), tile_of)),
        out_shape=jax.ShapeDtypeStruct((n, D_MODEL), F32),
        compiler_params=_params(("arbitrary",)),
        name="moe_experts",
    )(item_tile, item_bucket, item_lo, item_hi, item_first, item_last, used, xs, w_gu, w_gu, w_down, w_down)

    return pl.pallas_call(
        functools.partial(_combine_kernel, nb=tb, final_norm=final_norm),
        grid_spec=pltpu.PrefetchScalarGridSpec(
            num_scalar_prefetch=1, grid=(n // MOVE_TILE,),
            in_specs=[row, modspec, vec, any_spec], out_specs=row,
            scratch_shapes=[pltpu.VMEM((2, MOVE_TILE, D_MODEL), F32), pltpu.SemaphoreType.DMA((2,))]),
        out_shape=jax.ShapeDtypeStruct((n, D_MODEL), F32),
        compiler_params=move_params,
        name="moe_combine",
    )(pos, x, mod, final_g, ys)


def _rope(x, cos, sin_lo, sin_hi):
    return x * cos + pltpu.roll(x, LANES - ROPE_DIM // 2, axis=1) * sin_lo + pltpu.roll(x, ROPE_DIM // 2, axis=1) * sin_hi


def _qkv_kernel(x_ref, mod_ref, kvmod_ref, ngq_ref, ngkv_ref, wq_ref, bq_ref, wkv_ref, bkv_ref,
                cos_ref, slo_ref, shi_ref, q_ref, k_ref, v_ref):
    n = _rms(x_ref[...])
    tb = mod_ref.shape[0]
    hq = _modulate(n, ngq_ref[...], mod_ref[:, :, 0:D_MODEL], mod_ref[:, :, D_MODEL:2 * D_MODEL], tb)
    hkv = _modulate(n, ngkv_ref[...], kvmod_ref[:, :, 0:D_MODEL], kvmod_ref[:, :, D_MODEL:2 * D_MODEL], tb)
    cos, slo, shi = (jnp.concatenate([t[...]] * tb, axis=0) for t in (cos_ref, slo_ref, shi_ref))
    q = _dot(hq.astype(BF16), wq_ref[...]) + bq_ref[...]
    for p in range(PAIRS):
        cols = slice(p * LANES, (p + 1) * LANES)
        q_ref[:, cols] = (_rope(q[:, cols], cos, slo, shi) * ATT_SCALE).astype(q_ref.dtype)
    kv = _dot(hkv.astype(BF16), wkv_ref[...]) + bkv_ref[...]
    for p in range(KV_W // LANES):
        cols = slice(p * LANES, (p + 1) * LANES)
        k_ref[:, cols] = _rope(kv[:, cols], cos, slo, shi)
    v_ref[...] = kv[:, KV_W:]


def _qkv_proj(x, mod, kvmod, w, tables, nb, s, tm):
    n = nb * s
    tb, nt = max(tm // s, 1), max(s // tm, 1)
    row_spec = pl.BlockSpec((tm, D_MODEL), lambda b, j: (b * nt + j, 0))
    kv_spec = pl.BlockSpec((tm, KV_W), lambda b, j: (b * nt + j, 0))
    tab_spec = pl.BlockSpec((tm // tb, LANES), lambda b, j: (j, 0))
    vec = _const_spec((1, D_MODEL))
    return pl.pallas_call(
        _qkv_kernel,
        grid=(nb // tb, nt),
        in_specs=[row_spec,
                  pl.BlockSpec((tb, 1, 6 * D_MODEL), lambda b, j: (b, 0, 0)),
                  pl.BlockSpec((tb, 1, 2 * D_MODEL), lambda b, j: (b, 0, 0)),
                  vec, vec, _const_spec((D_MODEL, D_MODEL)), vec,
                  _const_spec((D_MODEL, 2 * KV_W)), _const_spec((1, 2 * KV_W)),
                  tab_spec, tab_spec, tab_spec],
        out_specs=[row_spec, kv_spec, kv_spec],
        out_shape=[jax.ShapeDtypeStruct((n, D_MODEL), BF16),
                   jax.ShapeDtypeStruct((n, KV_W), F32), jax.ShapeDtypeStruct((n, KV_W), F32)],
        compiler_params=_params(("parallel", "parallel")),
        name="qkv_proj",
    )(x, mod, kvmod, w["norm_gq"], w["norm_gkv"], w["w_q"], w["b_q"], w["w_kv"], w["b_kv"], *tables)


def _attn_kernel(sink_ref, q_ref, k0_ref, k1_ref, k2_ref, v0_ref, v1_ref, v2_ref, o_ref, *, banded):
    n = pl.program_id(1)
    seqs = range(q_ref.shape[0])
    k_all = [jnp.concatenate([k0_ref[i], k1_ref[i], k2_ref[i]], axis=0) for i in seqs]
    v_all = [jnp.concatenate([v0_ref[i], v1_ref[i], v2_ref[i]], axis=0) for i in seqs]
    lane = lax.broadcasted_iota(jnp.int32, (1, LANES), 1)
    lo = lane < HEAD
    col = lax.broadcasted_iota(jnp.int32, (1, 3 * CHUNK), 1)
    if banded:
        visible = (col >= 2 * CHUNK) | ((col >= CHUNK) & (n >= 1)) | (n >= 2)

    def halves(t, kv_head):
        blk = t[:, (kv_head // 2) * LANES:(kv_head // 2 + 1) * LANES]
        if kv_head % 2 == 0:
            t_lo = jnp.where(lo, blk, 0.0)
            t_hi = pltpu.roll(t_lo, HEAD, axis=1)
        else:
            t_hi = jnp.where(lo, 0.0, blk)
            t_lo = pltpu.roll(t_hi, HEAD, axis=1)
        return t_lo.astype(BF16), t_hi.astype(BF16)

    first_pair = lax.broadcasted_iota(jnp.int32, (2 * CHUNK, 1), 0) < CHUNK

    def probs(s, head_a, head_b):
        if banded:
            s = jnp.where(visible, s, NEG_INF)
        sink = jnp.where(first_pair, sink_ref[head_a], sink_ref[head_b])
        m = jnp.maximum(jnp.max(s, axis=-1, keepdims=True), sink)
        e = jnp.exp(s - m)
        return (e / (jnp.sum(e, axis=-1, keepdims=True) + jnp.exp(sink - m))).astype(BF16)

    units = [(i, g) for i in seqs for g in range(N_KV)]
    scores, values = [], []
    for i, g in units:
        k_lo, k_hi = halves(k_all[i], g)
        values.append(halves(v_all[i], g))
        q2 = jnp.concatenate([q_ref[i, :, (2 * g + j) * LANES:(2 * g + j + 1) * LANES] for j in range(2)], axis=0)
        scores.append((_dot_nt(q2, k_lo), _dot_nt(q2, k_hi)))
    for (i, g), (s_lo, s_hi), (v_lo, v_hi) in zip(units, scores, values):
        o = _dot(probs(s_lo, 4 * g, 4 * g + 2), v_lo) + _dot(probs(s_hi, 4 * g + 1, 4 * g + 3), v_hi)
        for j in range(2):
            pair = 2 * g + j
            o_ref[i, :, pair * LANES:(pair + 1) * LANES] = o[j * CHUNK:(j + 1) * CHUNK].astype(o_ref.dtype)


def _attention(q, kv_arrays, kv_chunks, sinks, nb, s, banded):
    nseq = math.gcd(nb, ATTN_SEQS)
    q_spec = pl.BlockSpec((nseq, CHUNK, D_MODEL), lambda b, c: (b, c, 0))
    kv_specs = [pl.BlockSpec((nseq, CHUNK, KV_W), lambda b, c, f=f: (b, f(c), 0)) for f in kv_chunks]
    return pl.pallas_call(
        functools.partial(_attn_kernel, banded=banded),
        grid=(nb // nseq, s // CHUNK),
        in_specs=[pl.BlockSpec(memory_space=pltpu.SMEM), q_spec] + kv_specs,
        out_specs=q_spec,
        out_shape=jax.ShapeDtypeStruct((nb, s, D_MODEL), BF16),
        compiler_params=_params(("parallel", "parallel")),
        name="attention",
    )(sinks, q, *kv_arrays)


def _pad_cols(w, n):
    return jnp.pad(w, ((0, 0), (0, n - w.shape[1])))


def _pad_rows(w, n):
    return jnp.pad(w, ((0, n - w.shape[0]), (0, 0)))


def _rope_tables(pos):
    half = ROPE_DIM // 2
    inv = jnp.power(jnp.float32(ROPE_THETA), -jnp.arange(half, dtype=F32) * (2.0 / ROPE_DIM))
    ang = pos[:, None] * inv[None, :]
    cos, sin = jnp.cos(ang), jnp.sin(ang)
    rest = HEAD - ROPE_DIM
    ones = jnp.ones((pos.shape[0], rest), F32)
    zeros = jnp.zeros((pos.shape[0], rest), F32)
    z8 = jnp.zeros_like(sin)
    per_head = (jnp.concatenate([cos, cos, ones], axis=1),
                jnp.concatenate([-sin, z8, zeros], axis=1),
                jnp.concatenate([z8, sin, zeros], axis=1))
    return tuple(jnp.tile(t, (1, LANES // HEAD)) for t in per_head)


def _state_to_pairs(state):
    nb = state.shape[0]
    st = state.astype(F32).reshape(nb, PAIRS, 2, HEAD, HEAD)
    z = jnp.zeros_like(st[:, :, 0])
    top = jnp.concatenate([st[:, :, 0], z], axis=-1)
    bot = jnp.concatenate([z, st[:, :, 1]], axis=-1)
    return jnp.concatenate([top, bot], axis=-2)


def _pairs_to_state(st):
    nb = st.shape[0]
    s0 = st[:, :, :HEAD, :HEAD]
    s1 = st[:, :, HEAD:, HEAD:]
    return jnp.stack([s0, s1], axis=2).reshape(nb, 2 * PAIRS, HEAD, HEAD)


def _trunk(x, mods, kvmod, pos, prev_x, prev_wkv, past_k, past_v, w, nb, s, expert_w16=None):
    n = nb * s
    made_w16 = {}
    x = x.reshape(n, D_MODEL)
    mod0, mod1 = mods[0][:, None, :], mods[1][:, None, :]
    kvmod = kvmod[:, None, :]

    tm = min(n, PROJ_TILE)
    r, lw, k, v, a, b, g, last_x = _rwkv_proj(x, mod0, prev_x[:, None, :], w["rw"], nb, s, tm)
    seqs = lambda t: t.reshape(nb, s, D_MODEL)
    z, st = _wkv(*(seqs(t) for t in (r, lw, k, v, a, b, g)), _state_to_pairs(prev_wkv), w["rw"], nb, s)
    z = z.reshape(n, D_MODEL)

    def residual_moe(x, z, mixer, mod, l):
        x = _out_proj(x, z, mod[:, :, 2 * D_MODEL:3 * D_MODEL], mixer["w_o"], mixer["b_o"], nb, s, min(n, 1024))
        route = (mod, w["norm_g"][l][1:2], w["moe_route"][l])
        if n >= SPARSE_MIN_TOKENS:
            if expert_w16 is not None and l in expert_w16:
                gu16, dn16 = expert_w16[l]
            else:
                gu16, dn16 = w["moe_gu"][l].astype(BF16), w["moe_down"][l].astype(BF16)
            return _moe_sparse(x, *route, gu16, dn16, w["final_g"], nb, s, l == 1)
        y, *made_w16[l] = _moe(x, *route, w["moe_gu"], w["moe_down"], w["final_g"], nb, s, l, min(n, 1024), l == 1)
        return y

    x = residual_moe(x, z, w["rw"], mod0, 0)

    q, k_new, v_new = _qkv_proj(x, mod1, kvmod, w["at"], _rope_tables(pos), nb, s, tm)
    k_seq, v_seq = k_new.reshape(nb, s, KV_W), v_new.reshape(nb, s, KV_W)
    if past_k is None:
        arrays = [k_seq] * 3 + [v_seq] * 3
        chunks = [lambda c, d=d: jnp.maximum(c - d, 0) for d in (2, 1, 0)] * 2
    else:
        pk = past_k.astype(F32).reshape(nb, 2 * CHUNK, KV_W)
        pv = past_v.astype(F32).reshape(nb, 2 * CHUNK, KV_W)
        arrays = [pk, pk, k_seq, pv, pv, v_seq]
        chunks = [lambda c: 0, lambda c: 1, lambda c: 0] * 2
    o = _attention(q.reshape(nb, s, D_MODEL), arrays, chunks, w["at"]["sinks"], nb, s, past_k is None)
    y = residual_moe(x, o.reshape(n, D_MODEL), w["at"], mod1, 1)

    return (y.reshape(nb, s, D_MODEL), last_x.reshape(1, nb, D_MODEL), _pairs_to_state(st)[None], k_seq, v_seq,
            made_w16)


def kernel(x_prompt, x_sample, state_shift, state_wkv, cache_k, cache_v, c_prompt, c_sample, ada_w, ada_b, norm_g, rw_mu, rw_w_rkv, rw_w0, rw_w1, rw_w2, rw_a0, rw_a1, rw_a2, rw_g1, rw_g2, rw_k_k, rw_k_a, rw_r_k, rw_lnx_w, rw_lnx_b, rw_w_o, kv_ada_w, kv_ada_b, kv_norm_g, w_kv, b_kv, at_w_q, at_b_q, at_sinks, at_w_o, at_b_o, moe_w_group, moe_w_expert, moe_w_gu, moe_w_down, final_norm_g):
    bp, sp, _ = x_prompt.shape
    bs, ss, _ = x_sample.shape
    row = lambda t: t.reshape(1, -1).astype(F32)

    c_all = jnp.concatenate([c_prompt, c_sample], axis=0)
    mods = _cond_linear(c_all, ada_w, ada_b[:, None, :])
    kvmods = _cond_linear(c_all, kv_ada_w[None], kv_ada_b[None, None, :])[0]

    lora_pad, gate_pad = LANES, 2 * LANES
    w = {
        "norm_g": norm_g,
        "final_g": row(final_norm_g),
        "rw": {
            "norm_g": norm_g[0, 0:1], "mu": rw_mu[0], "w_rkv": rw_w_rkv[0].astype(BF16),
            "w0": row(rw_w0[0]), "w1": _pad_cols(rw_w1[0], lora_pad).astype(BF16),
            "w2": _pad_rows(rw_w2[0], lora_pad).astype(BF16),
            "a0": row(rw_a0[0]), "a1": _pad_cols(rw_a1[0], lora_pad).astype(BF16),
            "a2": _pad_rows(rw_a2[0], lora_pad).astype(BF16),
            "g1": _pad_cols(rw_g1[0], gate_pad).astype(BF16), "g2": _pad_rows(rw_g2[0], gate_pad).astype(BF16),
            "k_k": row(rw_k_k[0]), "k_a": row(rw_k_a[0]), "r_k": row(rw_r_k[0]),
            "lnx_w": row(rw_lnx_w[0]), "lnx_b": row(rw_lnx_b[0]),
            "w_o": rw_w_o[0].astype(BF16), "b_o": jnp.zeros((1, D_MODEL), F32),
        },
        "at": {
            "norm_gq": norm_g[1, 0:1], "norm_gkv": row(kv_norm_g),
            "w_q": at_w_q[0].astype(BF16), "b_q": row(at_b_q[0]),
            "w_kv": w_kv.astype(BF16), "b_kv": row(b_kv),
            "sinks": at_sinks[0].astype(F32),
            "w_o": at_w_o[0].astype(BF16), "b_o": row(at_b_o[0]),
        },
        "moe_route": jnp.pad(jnp.concatenate([moe_w_group, moe_w_expert], axis=-1),
                             ((0, 0), (0, 0), (0, LANES - N_GROUPS - N_EXPERTS))),
        "moe_gu": moe_w_gu,
        "moe_down": moe_w_down,
    }

    pos_p = jnp.arange(sp, dtype=F32)
    pos_s = PAST_LEN + jnp.arange(ss, dtype=F32)
    zero_x = jnp.zeros((bp, D_MODEL), x_prompt.dtype)
    zero_wkv = jnp.zeros((bp,) + state_wkv.shape[2:], state_wkv.dtype)
    y_s, s_shift, s_wkv, s_k, s_v, w16 = _trunk(x_sample, mods[:, bp:], kvmods[bp:], pos_s, state_shift[0],
                                                state_wkv[0], cache_k, cache_v, w, bs, ss)
    y_p, p_shift, p_wkv, p_k, p_v, _ = _trunk(x_prompt, mods[:, :bp], kvmods[:bp], pos_p, zero_x, zero_wkv,
                                              None, None, w, bp, sp, w16)
    keep = min(2 * CHUNK, sp)
    heads = lambda t: t.reshape(t.shape[0], t.shape[1], N_KV, HEAD)
    return (y_p, y_s, p_shift, p_wkv.astype(state_wkv.dtype), heads(p_k[:, sp - keep:]), heads(p_v[:, sp - keep:]),
            s_shift, s_wkv.astype(state_wkv.dtype), heads(s_k), heads(s_v))
```

```python
import functools
import math

import jax
import jax.numpy as jnp
from jax import lax
from jax.experimental import pallas as pl
from jax.experimental.pallas import tpu as pltpu

F32 = jnp.float32
BF16 = jnp.bfloat16

D_MODEL = 1024
LANES = 128
HEAD = 64
PAIRS = D_MODEL // LANES
CHUNK = 64
WKV_SEQS = 4
PROJ_TILE = 512
ATTN_SEQS = 2
PAST_LEN = 4096
N_KV = 4
KV_W = N_KV * HEAD
ROPE_DIM = 16
ROPE_THETA = 500000.0
ATT_SCALE = HEAD ** -0.5
N_GROUPS = 4
EXP_PER_GROUP = 8
N_EXPERTS = N_GROUPS * EXP_PER_GROUP
D_EXPERT = D_MODEL // 4
DENSE_GROUP = 2
ROUTE_LANE0 = N_GROUPS
RMS_EPS = 1e-6
GN_EPS = 64e-5
NEG_INF = -1e30
VMEM_LIMIT = 56 * 1024 * 1024


def _params(sem):
    return pltpu.CompilerParams(dimension_semantics=sem, vmem_limit_bytes=VMEM_LIMIT)


def _dot(a, b):
    return jnp.dot(a, b, preferred_element_type=F32)


def _dot_nt(a, b):
    return lax.dot_general(a, b, (((1,), (1,)), ((), ())), preferred_element_type=F32)


def _dot_tn(a, b):
    return lax.dot_general(a, b, (((0,), (0,)), ((), ())), preferred_element_type=F32)


def _split2(x):
    hi = x.astype(BF16)
    lo = (x - hi.astype(F32)).astype(BF16)
    return hi, lo


def _dot_x3(a, b, dot=_dot):
    ah, al = _split2(a)
    bh, bl = _split2(b)
    return dot(ah, bh) + dot(ah, bl) + dot(al, bh)


def _sigmoid(x):
    return 1.0 / (1.0 + jnp.exp(-x))


def _rms(x):
    return x * lax.rsqrt(jnp.mean(x * x, axis=-1, keepdims=True) + RMS_EPS)


def _modulate(n, gain, shift, scale, nb):
    rows = n.shape[0]
    h = (n * gain).reshape(nb, rows // nb, D_MODEL)
    return (h * (1.0 + scale) + shift).reshape(rows, D_MODEL)


def _const_spec(shape):
    nd = len(shape)
    return pl.BlockSpec(shape, lambda *_: (0,) * nd)


def _cond_kernel(c_ref, w_ref, b_ref, o_ref):
    c = c_ref[...]
    cs = (c * _sigmoid(c)).astype(BF16)
    o_ref[...] = _dot(cs, w_ref[...].astype(BF16)) + b_ref[...]


def _cond_linear(c, w, b, tn=512):
    nl, _, n = w.shape
    m = c.shape[0]
    return pl.pallas_call(
        _cond_kernel,
        grid=(nl, n // tn),
        in_specs=[
            pl.BlockSpec((m, D_MODEL), lambda l, j: (0, 0)),
            pl.BlockSpec((None, D_MODEL, tn), lambda l, j: (l, 0, j)),
            pl.BlockSpec((None, 1, tn), lambda l, j: (l, 0, j)),
        ],
        out_specs=pl.BlockSpec((None, m, tn), lambda l, j: (l, 0, j)),
        out_shape=jax.ShapeDtypeStruct((nl, m, n), F32),
        compiler_params=_params(("parallel", "parallel")),
        name="cond_linear",
    )(c, w, b)


def _rwkv_proj_kernel(x_ref, mod_ref, prev_ref, ng_ref, mu_ref, wrkv_ref, w0_ref, w1_ref, w2_ref,
                      a0_ref, a1_ref, a2_ref, g1_ref, g2_ref, kk_ref, ka_ref,
                      r_ref, lw_ref, k_ref, v_ref, a_ref, b_ref, g_ref, last_ref, carry_ref):
    tm = x_ref.shape[0]
    tb = mod_ref.shape[0]
    rps = tm // tb

    @pl.when(pl.program_id(1) == 0)
    def _():
        carry_ref[...] = prev_ref[...]

    h = _modulate(_rms(x_ref[...]), ng_ref[...], mod_ref[:, :, 0:D_MODEL], mod_ref[:, :, D_MODEL:2 * D_MODEL], tb)
    seqs = lambda t: t.reshape(tb, rps, D_MODEL)
    first = lax.broadcasted_iota(jnp.int32, (tb, rps, 1), 1) == 0
    shifted = jnp.where(first, carry_ref[...], seqs(pltpu.roll(h, 1, axis=0))).reshape(tm, D_MODEL)
    carry_ref[...] = seqs(h)[:, rps - 1:rps, :]
    last_ref[...] = seqs(h)[:, rps - 1:rps, :]
    dx = shifted - h

    def mix(n):
        return (h + dx * mu_ref[n:n + 1, :]).astype(BF16)

    r = _dot(mix(0), wrkv_ref[0])
    k = _dot(mix(1), wrkv_ref[1])
    v = _dot(mix(2), wrkv_ref[2])
    ww = _dot(jnp.tanh(_dot(mix(3), w1_ref[...])).astype(BF16), w2_ref[...])
    z = -(w0_ref[...] + ww)
    softplus = jnp.maximum(z, 0.0) + jnp.log(1.0 + jnp.exp(-jnp.abs(z)))
    lw = -jnp.exp(-softplus - 0.5)
    asig = _sigmoid(a0_ref[...] + _dot(_dot(mix(4), a1_ref[...]).astype(BF16), a2_ref[...]))
    g = _dot(_sigmoid(_dot(mix(5), g1_ref[...])).astype(BF16), g2_ref[...])

    kk = k * kk_ref[...]
    kk2 = kk * kk
    lo_cols = lax.broadcasted_iota(jnp.int32, (tm, LANES), 1) < HEAD

    def head_sums(t):
        lo_sum = jnp.sum(jnp.where(lo_cols, t, 0.0), axis=-1, keepdims=True)
        hi_sum = jnp.sum(jnp.where(lo_cols, 0.0, t), axis=-1, keepdims=True)
        return jnp.where(lo_cols, lo_sum, hi_sum)

    ss = jnp.concatenate([head_sums(kk2[:, p * LANES:(p + 1) * LANES]) for p in range(PAIRS)], axis=1)
    kkn = kk * lax.rsqrt(jnp.maximum(ss, 1e-24))

    r_ref[...] = r.astype(r_ref.dtype)
    lw_ref[...] = lw
    k_ref[...] = (k * (1.0 + (asig - 1.0) * ka_ref[...])).astype(k_ref.dtype)
    v_ref[...] = v.astype(v_ref.dtype)
    a_ref[...] = (-kkn).astype(a_ref.dtype)
    b_ref[...] = (kkn * asig).astype(b_ref.dtype)
    g_ref[...] = g.astype(g_ref.dtype)


def _rwkv_proj(x, mod, prev_x, w, nb, s, tm):
    n = nb * s
    tb, nt = max(tm // s, 1), max(s // tm, 1)
    row_spec = pl.BlockSpec((tm, D_MODEL), lambda b, j: (b * nt + j, 0))
    per_seq = lambda width: pl.BlockSpec((tb, 1, width), lambda b, j: (b, 0, 0))
    vec = _const_spec((1, D_MODEL))
    lora_w, lora_g = w["w1"].shape[1], w["g1"].shape[1]
    in_specs = [
        row_spec,
        per_seq(6 * D_MODEL),
        per_seq(D_MODEL),
        vec, _const_spec((6, D_MODEL)), _const_spec((3, D_MODEL, D_MODEL)),
        vec, _const_spec((D_MODEL, lora_w)), _const_spec((lora_w, D_MODEL)),
        vec, _const_spec((D_MODEL, lora_w)), _const_spec((lora_w, D_MODEL)),
        _const_spec((D_MODEL, lora_g)), _const_spec((lora_g, D_MODEL)),
        vec, vec,
    ]
    act = lambda dt: jax.ShapeDtypeStruct((n, D_MODEL), dt)
    out_types = [act(BF16), act(F32)] + [act(BF16)] * 5
    outs = pl.pallas_call(
        _rwkv_proj_kernel,
        grid=(nb // tb, nt),
        in_specs=in_specs,
        out_specs=[row_spec] * 7 + [per_seq(D_MODEL)],
        out_shape=out_types + [jax.ShapeDtypeStruct((nb, 1, D_MODEL), F32)],
        scratch_shapes=[pltpu.VMEM((tb, 1, D_MODEL), F32)],
        compiler_params=_params(("parallel", "arbitrary")),
        name="rwkv_proj",
    )(x, mod, prev_x, w["norm_g"], w["mu"], w["w_rkv"], w["w0"], w["w1"], w["w2"],
      w["a0"], w["a1"], w["a2"], w["g1"], w["g2"], w["k_k"], w["k_a"])
    return outs


def _split3(x):
    hi = x.astype(BF16)
    rem = x - hi.astype(F32)
    mid = rem.astype(BF16)
    return hi, mid, (rem - mid.astype(F32)).astype(BF16)


def _wkv_kernel(r_ref, lw_ref, k_ref, v_ref, a_ref, b_ref, g_ref, s0_ref, rk_ref, lnw_ref, lnb_ref,
                z_ref, st_ref):
    @pl.when(pl.program_id(1) == 0)
    def _():
        st_ref[...] = s0_ref[...]

    nseq = r_ref.shape[0]
    wide = lambda ref: jnp.concatenate([ref[i] for i in range(nseq)], axis=1)
    per_seq = lambda ref: jnp.concatenate([ref[...]] * nseq, axis=1)
    b16 = lambda t: t.astype(BF16)
    pairs = range(nseq * PAIRS)
    blk = lambda t, p: t[:, p * LANES:(p + 1) * LANES]
    rows = lambda ts: jnp.concatenate(ts, axis=0)
    lo_cols = lax.broadcasted_iota(jnp.int32, (CHUNK, LANES), 1) < HEAD
    rr = lax.broadcasted_iota(jnp.int32, (LANES, LANES), 0)
    cc = lax.broadcasted_iota(jnp.int32, (LANES, LANES), 1)
    head0 = lambda t: jnp.where(lo_cols, t, 0.0)
    head1 = lambda t: jnp.where(lo_cols, 0.0, t)
    blockdiag = lambda q: rows([head0(q), head1(q)])

    r, lw, k, v, a, b = (wide(t).astype(F32) for t in (r_ref, lw_ref, k_ref, v_ref, a_ref, b_ref))
    ti = lax.broadcasted_iota(jnp.int32, (CHUNK, CHUNK), 0)
    tj = lax.broadcasted_iota(jnp.int32, (CHUNK, CHUNK), 1)
    tri = (tj <= ti).astype(BF16)
    cum = sum(_dot(tri, t) for t in _split3(lw))
    cl = cum[CHUNK - 1:CHUNK, :]
    e_neg = jnp.exp(-cum)
    e_end = jnp.exp(cl - cum)
    e_cl = jnp.exp(cl)
    at = b16(a * jnp.exp(cum - lw))
    rt = b16(r * jnp.exp(cum))
    bt = b * e_neg
    kt = k * e_neg
    bkh = rows([b16(b * e_end), b16(k * e_end)])
    v16 = b16(v)

    t_idx = rr & (CHUNK - 1)
    j_idx = cc & (CHUNK - 1)
    keep = (j_idx < t_idx) | ((rr >= CHUNK) & (j_idx == t_idx))
    g0, g1 = [], []
    for p in pairs:
        btp, ktp = blk(bt, p), blk(kt, p)
        w = b16(rows([head0(btp), head0(ktp), head1(ktp), head1(btp)]))
        g = _dot_nt(rows([blk(at, p), blk(rt, p)]), w)
        g0.append(jnp.where(keep, g[:, :LANES], 0.0))
        g1.append(jnp.where(keep, g[:, LANES:], 0.0))

    eye2 = jnp.where((lax.broadcasted_iota(jnp.int32, (CHUNK, LANES), 1) & (CHUNK - 1))
                     == lax.broadcasted_iota(jnp.int32, (CHUNK, LANES), 0), 1.0, 0.0)
    pc = [jnp.where(lo_cols, g0[p][:CHUNK], g1[p][:CHUNK]) for p in pairs]
    tc = [eye2 + pc[p] for p in pairs]
    q = [_dot(b16(pc[p]), b16(blockdiag(pc[p]))) for p in pairs]
    for _ in range(4):
        res = [_dot(b16(rows([tc[p], q[p]])), b16(blockdiag(q[p]))) for p in pairs]
        tc = [tc[p] + res[p][:CHUNK] for p in pairs]
        q = [res[p][CHUNK:] for p in pairs]
    tc = [b16(tc[p] + _dot(b16(tc[p]), b16(blockdiag(q[p])))) for p in pairs]

    akv = []
    for p in pairs:
        w_ak = b16(jnp.where(lo_cols, g1[p][:CHUNK], g0[p][:CHUNK]))
        vp = blk(v16, p)
        akv.append(_dot(w_ak, rows([head1(vp), head0(vp)])))

    s = [st_ref[p // PAIRS, p % PAIRS] for p in pairs]
    s16 = [b16(s[p]) for p in pairs]
    rhs = [_dot_nt(blk(at, p), s16[p]) + akv[p] for p in pairs]
    u = [_dot(tc[p], b16(rows([head0(rhs[p]), head1(rhs[p])]))) for p in pairs]
    u16 = [b16(u[p]) for p in pairs]

    ys = []
    bd_mask = (rr < HEAD) == (cc < HEAD)
    for p in pairs:
        vp = blk(v16, p)
        rbk = b16(jnp.concatenate([g0[p][CHUNK:], g1[p][CHUNK:]], axis=1))
        uv = rows([head0(u16[p]), head0(vp), head1(vp), head1(u16[p])])
        ys.append(_dot_nt(blk(rt, p), s16[p]) + _dot(rbk, uv))
        fresh = _dot_tn(rows([u16[p], vp]), blk(bkh, p))
        st_ref[p // PAIRS, p % PAIRS] = s[p] * blk(e_cl, p) + jnp.where(bd_mask, fresh, 0.0)

    def head_sums(t):
        lo_sum = jnp.sum(head0(t), axis=-1, keepdims=True)
        hi_sum = jnp.sum(head1(t), axis=-1, keepdims=True)
        return jnp.where(lo_cols, lo_sum, hi_sum)

    rkk = r * k * per_seq(rk_ref)
    for p in pairs:
        seq, cols = p // PAIRS, slice((p % PAIRS) * LANES, (p % PAIRS + 1) * LANES)
        yc = ys[p] - head_sums(ys[p]) * (1.0 / HEAD)
        yn = yc * lax.rsqrt(head_sums(yc * yc) * (1.0 / HEAD) + GN_EPS)
        zp = (yn * lnw_ref[:, cols] + lnb_ref[:, cols] + head_sums(blk(rkk, p)) * blk(v, p)) * g_ref[seq, :, cols].astype(F32)
        z_ref[seq, :, cols] = zp.astype(z_ref.dtype)


def _wkv(r, lw, k, v, a, b, g, st0, w, nb, s):
    nseq = math.gcd(nb, WKV_SEQS)
    blk = pl.BlockSpec((nseq, CHUNK, D_MODEL), lambda bi, c: (bi, c, 0))
    vec = _const_spec((1, D_MODEL))
    st_spec = pl.BlockSpec((nseq, PAIRS, LANES, LANES), lambda bi, c: (bi, 0, 0, 0))
    return pl.pallas_call(
        _wkv_kernel,
        grid=(nb // nseq, s // CHUNK),
        in_specs=[blk] * 7 + [st_spec, vec, vec, vec],
        out_specs=[blk, st_spec],
        out_shape=[jax.ShapeDtypeStruct((nb, s, D_MODEL), BF16),
                   jax.ShapeDtypeStruct((nb, PAIRS, LANES, LANES), F32)],
        compiler_params=_params(("parallel", "arbitrary")),
        name="wkv",
    )(r, lw, k, v, a, b, g, st0, w["r_k"], w["lnx_w"], w["lnx_b"])


def _out_proj_kernel(x_ref, z_ref, gate_ref, w_ref, bias_ref, o_ref, *, nb):
    rows = x_ref.shape[0]
    y = _dot(z_ref[...], w_ref[...]) + bias_ref[...]
    y = (y.reshape(nb, rows // nb, D_MODEL) * gate_ref[...]).reshape(rows, D_MODEL)
    o_ref[...] = x_ref[...] + y


def _out_proj(x, z, gate, w, bias, nb, s, tm):
    n = nb * s
    tb = max(tm // s, 1)
    per_seq = max(s // tm, 1)
    row_spec = pl.BlockSpec((tm, D_MODEL), lambda i: (i, 0))
    return pl.pallas_call(
        functools.partial(_out_proj_kernel, nb=tb),
        grid=(n // tm,),
        in_specs=[row_spec, row_spec,
                  pl.BlockSpec((tb, 1, D_MODEL), lambda i: (i // per_seq, 0, 0)),
                  _const_spec((D_MODEL, D_MODEL)), _const_spec((1, D_MODEL))],
        out_specs=row_spec,
        out_shape=jax.ShapeDtypeStruct((n, D_MODEL), F32),
        compiler_params=_params(("parallel",)),
        name="out_proj",
    )(x, z, gate, w, bias)


def _top2(logits):
    lane = lax.broadcasted_iota(jnp.int32, logits.shape, 1).astype(F32)
    first_of = lambda hit: jnp.min(jnp.where(hit, lane, float(LANES)), axis=-1, keepdims=True)
    lg = jnp.where(lane < N_GROUPS, logits, -jnp.inf)
    gmax = jnp.max(lg, axis=-1, keepdims=True)
    gi = first_of(lg == gmax)
    gp = 1.0 / jnp.sum(jnp.exp(lg - gmax), axis=-1, keepdims=True)
    first = ROUTE_LANE0 + gi * EXP_PER_GROUP
    le = jnp.where((lane >= first) & (lane < first + EXP_PER_GROUP), logits, -jnp.inf)
    top1 = jnp.max(le, axis=-1, keepdims=True)
    i1 = first_of(le == top1)
    le2 = jnp.where(lane == i1, -jnp.inf, le)
    top2 = jnp.max(le2, axis=-1, keepdims=True)
    i2 = first_of(le2 == top2)
    e2 = jnp.exp(top2 - top1)
    w1 = gp / (1.0 + e2)
    return i1, i2, w1, w1 * e2


def _route(logits):
    lane = lax.broadcasted_iota(jnp.int32, logits.shape, 1).astype(F32)
    i1, i2, w1, w2 = _top2(logits)
    return jnp.where(lane == i1, w1, 0.0) + jnp.where(lane == i2, w2, 0.0)


def _moe_kernel(x_ref, mod_ref, ng_ref, wr_ref, wgu_ref, wdn_ref, fg_ref, o_ref, gu16_ref, dn16_ref,
                h_scr, cw_scr, acc_scr, *, nb, final_norm):
    e = pl.program_id(1)
    rows = x_ref.shape[0]

    @pl.when(e == 0)
    def _():
        shift = mod_ref[:, :, 3 * D_MODEL:4 * D_MODEL]
        scale = mod_ref[:, :, 4 * D_MODEL:5 * D_MODEL]
        h = _modulate(_rms(x_ref[...]), ng_ref[...], shift, scale, nb)
        h_scr[...] = h.astype(BF16)
        cw_scr[...] = _route(_dot_x3(h, wr_ref[...]))
        acc_scr[...] = jnp.zeros_like(acc_scr)

    lane = lax.broadcasted_iota(jnp.int32, (rows, LANES), 1)
    gu16_ref[...] = wgu_ref[...].astype(BF16)
    dn16_ref[...] = wdn_ref[...].astype(BF16)
    hid = []
    for j in range(DENSE_GROUP):
        cw = jnp.sum(jnp.where(lane == ROUTE_LANE0 + e * DENSE_GROUP + j, cw_scr[...], 0.0), axis=-1, keepdims=True)
        gu = _dot(h_scr[...], gu16_ref[j])
        gate = gu[:, :D_EXPERT]
        hid.append((gate * _sigmoid(gate) * gu[:, D_EXPERT:] * cw).astype(BF16))
    acc_scr[...] += _dot(jnp.concatenate(hid, axis=1), dn16_ref[...].reshape(DENSE_GROUP * D_EXPERT, D_MODEL))

    @pl.when(e == pl.num_programs(1) - 1)
    def _():
        g2 = mod_ref[:, :, 5 * D_MODEL:6 * D_MODEL]
        y = (acc_scr[...].reshape(nb, rows // nb, D_MODEL) * g2).reshape(rows, D_MODEL)
        out = x_ref[...] + y
        if final_norm:
            out = _rms(out) * fg_ref[...]
        o_ref[...] = out


def _moe(x, mod, norm_g, w_route, w_gu, w_down, final_g, nb, s, layer, tm, final_norm):
    n = nb * s
    tb = max(tm // s, 1)
    per_seq = max(s // tm, 1)
    row_spec = pl.BlockSpec((tm, D_MODEL), lambda i, e: (i, 0))
    return pl.pallas_call(
        functools.partial(_moe_kernel, nb=tb, final_norm=final_norm),
        grid=(n // tm, N_EXPERTS // DENSE_GROUP),
        in_specs=[row_spec,
                  pl.BlockSpec((tb, 1, 6 * D_MODEL), lambda i, e: (i // per_seq, 0, 0)),
                  _const_spec((1, D_MODEL)), _const_spec((D_MODEL, LANES)),
                  pl.BlockSpec((None, DENSE_GROUP, D_MODEL, 2 * D_EXPERT), lambda i, e: (layer, e, 0, 0)),
                  pl.BlockSpec((None, DENSE_GROUP, D_EXPERT, D_MODEL), lambda i, e: (layer, e, 0, 0)),
                  _const_spec((1, D_MODEL))],
        out_specs=[row_spec,
                   pl.BlockSpec((DENSE_GROUP, D_MODEL, 2 * D_EXPERT), lambda i, e: (e, 0, 0)),
                   pl.BlockSpec((DENSE_GROUP, D_EXPERT, D_MODEL), lambda i, e: (e, 0, 0))],
        out_shape=[jax.ShapeDtypeStruct((n, D_MODEL), F32),
                   jax.ShapeDtypeStruct((N_EXPERTS, D_MODEL, 2 * D_EXPERT), BF16),
                   jax.ShapeDtypeStruct((N_EXPERTS, D_EXPERT, D_MODEL), BF16)],
        scratch_shapes=[pltpu.VMEM((tm, D_MODEL), BF16), pltpu.VMEM((tm, LANES), F32),
                        pltpu.VMEM((tm, D_MODEL), F32)],
        compiler_params=_params(("arbitrary", "arbitrary")),
        name="moe",
    )(x, mod, norm_g, w_route, w_gu, w_down, final_g)


PAIRS_PER_GROUP = EXP_PER_GROUP * (EXP_PER_GROUP - 1) // 2
N_BUCKETS = N_GROUPS * PAIRS_PER_GROUP
BUCKET_TILE = 256
ROW_W = D_MODEL + LANES
META_BUCKET, META_RANK, META_W_LO, META_W_HI = 0, 1, 2, 3
ROUTER_TILE = 1024
MOVE_TILE = 1024
DMA_UNROLL = 8
DMA_PRIORITIES = 2
SPARSE_MIN_TOKENS = 4096


def _moe_norm(x, mod_ref, ng_ref, nb):
    shift = mod_ref[:, :, 3 * D_MODEL:4 * D_MODEL]
    scale = mod_ref[:, :, 4 * D_MODEL:5 * D_MODEL]
    return _modulate(_rms(x), ng_ref[...], shift, scale, nb)


def _router_kernel(x_ref, mod_ref, ng_ref, wr_ref, earlier_ref, meta_ref, cnt_ref, *, nb):
    @pl.when(pl.program_id(0) == 0)
    def _():
        cnt_ref[...] = jnp.zeros_like(cnt_ref)

    rows = x_ref.shape[0]
    h = _moe_norm(x_ref[...], mod_ref, ng_ref, nb)
    i1, i2, w1, w2 = _top2(_dot_x3(h, wr_ref[...]))
    lo = (jnp.minimum(i1, i2) - ROUTE_LANE0).astype(jnp.int32)
    hi = (jnp.maximum(i1, i2) - ROUTE_LANE0).astype(jnp.int32)
    first_is_lo = i1 < i2
    a = lo & (EXP_PER_GROUP - 1)
    b = hi & (EXP_PER_GROUP - 1)
    group = lo >> (EXP_PER_GROUP.bit_length() - 1)
    bucket = group * PAIRS_PER_GROUP + ((a * (2 * EXP_PER_GROUP - 1 - a)) >> 1) + (b - a - 1)

    lane = lax.broadcasted_iota(jnp.int32, (rows, LANES), 1)
    mine = lane == bucket
    onehot = jnp.where(mine, 1.0, 0.0)
    seen = cnt_ref[...]
    before = _dot(earlier_ref[...], onehot.astype(BF16)) + seen
    rank = jnp.sum(jnp.where(mine, before, 0.0), axis=-1, keepdims=True)
    cnt_ref[...] = seen + jnp.sum(onehot, axis=0, keepdims=True)

    meta = jnp.where(lane == META_BUCKET, bucket.astype(F32), 0.0)
    meta = jnp.where(lane == META_RANK, rank, meta)
    meta = jnp.where(lane == META_W_LO, jnp.where(first_is_lo, w1, w2), meta)
    meta_ref[...] = jnp.where(lane == META_W_HI, jnp.where(first_is_lo, w2, w1), meta)


def _positions_kernel(meta_ref, start_ref, pos_ref):
    meta = meta_ref[...]
    lane = lax.broadcasted_iota(jnp.int32, meta.shape, 1)
    column = lambda c: jnp.sum(jnp.where(lane == c, meta, 0.0), axis=-1, keepdims=True)
    value = jnp.where(lane == column(META_BUCKET).astype(jnp.int32), start_ref[...] + column(META_RANK), 0.0)
    ones = jnp.ones((8, LANES), BF16)
    pos = sum(_dot_nt(ones, piece) for piece in _split3(value))
    pos_ref[...] = pos[0:1, :].astype(jnp.int32)


def _dispatch_kernel(pos_ref, x_ref, mod_ref, ng_ref, meta_ref, xs_ref, buf, sem, *, nb):
    i = pl.program_id(0)
    rows = x_ref.shape[0]
    slot = i % 2
    buf[slot, :, 0:D_MODEL] = _moe_norm(x_ref[...], mod_ref, ng_ref, nb)
    buf[slot, :, D_MODEL:ROW_W] = meta_ref[...]

    def send(j, carry):
        for priority in range(DMA_PRIORITIES):
            r = j * DMA_PRIORITIES + priority
            pltpu.make_async_copy(buf.at[slot, pl.ds(r, 1), :],
                                  xs_ref.at[pl.ds(pos_ref[i * rows + r], 1), :], sem.at[slot]).start(priority)
        return carry

    lax.fori_loop(0, rows // DMA_PRIORITIES, send, 0, unroll=DMA_UNROLL // DMA_PRIORITIES)

    def wait_all(sl):
        pltpu.make_async_copy(buf.at[sl], xs_ref.at[pl.ds(0, rows), :], sem.at[sl]).wait()

    @pl.when(i >= 1)
    def _():
        wait_all(1 - slot)

    @pl.when(i == pl.num_programs(0) - 1)
    def _():
        wait_all(slot)


def _experts_kernel(tile_ref, bucket_ref, lo_ref, hi_ref, first_ref, last_ref, used_ref, xs_ref,
                    gu_lo_ref, gu_hi_ref, dn_lo_ref, dn_hi_ref, ys_ref):
    k = pl.program_id(0)
    half = BUCKET_TILE // 2

    @pl.when(k < used_ref[0])
    def _():
        @pl.when((k == 0) | (tile_ref[k] != tile_ref[jnp.maximum(k - 1, 0)]))
        def _():
            ys_ref[...] = jnp.zeros_like(ys_ref)

        def run(base, rows):
            window = pl.ds(base, rows)
            x16 = xs_ref[window, 0:D_MODEL].astype(BF16)
            meta = xs_ref[window, D_MODEL:ROW_W]
            lane = lax.broadcasted_iota(jnp.int32, (rows, LANES), 1)
            column = lambda c: jnp.sum(jnp.where(lane == c, meta, 0.0), axis=-1, keepdims=True)
            member = column(META_BUCKET) == bucket_ref[k].astype(F32)

            def expert(gu_ref, dn_ref, col):
                gu = _dot(x16, gu_ref[...])
                gate = gu[:, :D_EXPERT]
                hid = gate * _sigmoid(gate) * gu[:, D_EXPERT:] * jnp.where(member, column(col), 0.0)
                return _dot(hid.astype(BF16), dn_ref[...])

            ys_ref[window, :] += expert(gu_lo_ref, dn_lo_ref, META_W_LO) + expert(gu_hi_ref, dn_hi_ref, META_W_HI)

        sublanes = 8
        base = jnp.minimum(first_ref[k] // sublanes * sublanes, BUCKET_TILE - half)
        fits = last_ref[k] - base <= half

        @pl.when(fits)
        def _():
            run(pl.multiple_of(base, sublanes), half)

        @pl.when(jnp.logical_not(fits))
        def _():
            run(0, BUCKET_TILE)


def _combine_kernel(pos_ref, x_ref, mod_ref, fg_ref, ys_ref, o_ref, buf, sem, *, nb, final_norm):
    i = pl.program_id(0)
    rows = x_ref.shape[0]
    slot = i % 2

    def fetch(step, sl):
        def one(j, carry):
            for priority in range(DMA_PRIORITIES):
                r = j * DMA_PRIORITIES + priority
                pltpu.make_async_copy(ys_ref.at[pl.ds(pos_ref[step * rows + r], 1), :],
                                      buf.at[sl, pl.ds(r, 1), :], sem.at[sl]).start(priority)
            return carry
        lax.fori_loop(0, rows // DMA_PRIORITIES, one, 0, unroll=DMA_UNROLL // DMA_PRIORITIES)

    @pl.when(i == 0)
    def _():
        fetch(0, 0)

    pltpu.make_async_copy(ys_ref.at[pl.ds(0, rows), :], buf.at[slot], sem.at[slot]).wait()

    @pl.when(i + 1 < pl.num_programs(0))
    def _():
        fetch(i + 1, 1 - slot)

    g2 = mod_ref[:, :, 5 * D_MODEL:6 * D_MODEL]
    y = (buf[slot].reshape(nb, rows // nb, D_MODEL) * g2).reshape(rows, D_MODEL)
    out = x_ref[...] + y
    if final_norm:
        out = _rms(out) * fg_ref[...]
    o_ref[...] = out


def _bucket_experts():
    lo, hi = [], []
    for g in range(N_GROUPS):
        for a in range(EXP_PER_GROUP):
            for b in range(a + 1, EXP_PER_GROUP):
                lo.append(g * EXP_PER_GROUP + a)
                hi.append(g * EXP_PER_GROUP + b)
    return jnp.asarray(lo, jnp.int32), jnp.asarray(hi, jnp.int32)


def _moe_sparse(x, mod, norm_g, w_route, w_gu, w_down, final_g, nb, s, final_norm):
    n = nb * s
    move_params = pltpu.CompilerParams(dimension_semantics=("arbitrary",), vmem_limit_bytes=VMEM_LIMIT,
                                       disable_bounds_checks=True)

    def token_specs(tm):
        tb, per_seq = max(tm // s, 1), max(s // tm, 1)
        row = pl.BlockSpec((tm, D_MODEL), lambda i, *_: (i, 0))
        modspec = pl.BlockSpec((tb, 1, 6 * D_MODEL), lambda i, *_: (i // per_seq, 0, 0))
        meta = pl.BlockSpec((tm, LANES), lambda i, *_: (i, 0))
        return tb, row, modspec, meta

    vec = pl.BlockSpec((1, D_MODEL), lambda *_: (0, 0))
    tb, row, modspec, metaspec = token_specs(ROUTER_TILE)
    meta, counts = pl.pallas_call(
        functools.partial(_router_kernel, nb=tb),
        grid=(n // ROUTER_TILE,),
        in_specs=[row, modspec, vec, pl.BlockSpec((D_MODEL, LANES), lambda i: (0, 0)),
                  pl.BlockSpec((ROUTER_TILE, ROUTER_TILE), lambda i: (0, 0))],
        out_specs=[metaspec, pl.BlockSpec((1, LANES), lambda i: (0, 0))],
        out_shape=[jax.ShapeDtypeStruct((n, LANES), F32), jax.ShapeDtypeStruct((1, LANES), F32)],
        compiler_params=_params(("arbitrary",)),
        name="moe_router",
    )(x, mod, norm_g, w_route, jnp.tril(jnp.ones((ROUTER_TILE, ROUTER_TILE), BF16), -1))

    counts = counts[0, :N_BUCKETS].astype(jnp.int32)
    end = jnp.cumsum(counts)
    start = end - counts
    first_tile = start // BUCKET_TILE
    per_bucket = jnp.where(counts > 0, (end - 1) // BUCKET_TILE - first_tile + 1, 0)
    item_end = jnp.cumsum(per_bucket)
    used = item_end[-1:]
    n_items = n // BUCKET_TILE + N_BUCKETS
    k = jnp.minimum(jnp.arange(n_items, dtype=jnp.int32), used - 1)
    item_bucket = jnp.sum(item_end[None, :] <= k[:, None], axis=1, dtype=jnp.int32)
    item_tile = first_tile[item_bucket] + k - (item_end - per_bucket)[item_bucket]
    bucket_lo, bucket_hi = _bucket_experts()
    item_lo, item_hi = bucket_lo[item_bucket], bucket_hi[item_bucket]
    tile_row0 = item_tile * BUCKET_TILE
    item_first = jnp.clip(start[item_bucket] - tile_row0, 0, BUCKET_TILE)
    item_last = jnp.clip(end[item_bucket] - tile_row0, 0, BUCKET_TILE)
    start_row = jnp.pad(start, (0, LANES - N_BUCKETS)).astype(F32)[None, :]
    pos = pl.pallas_call(
        _positions_kernel,
        grid=(n // ROUTER_TILE,),
        in_specs=[metaspec, pl.BlockSpec((1, LANES), lambda i: (0, 0))],
        out_specs=pl.BlockSpec((1, ROUTER_TILE), lambda i: (0, i)),
        out_shape=jax.ShapeDtypeStruct((1, n), jnp.int32),
        compiler_params=_params(("parallel",)),
        name="moe_positions",
    )(meta, start_row).reshape(n)

    tb, row, modspec, metaspec = token_specs(MOVE_TILE)
    any_spec = pl.BlockSpec(memory_space=pl.ANY)
    xs = pl.pallas_call(
        functools.partial(_dispatch_kernel, nb=tb),
        grid_spec=pltpu.PrefetchScalarGridSpec(
            num_scalar_prefetch=1, grid=(n // MOVE_TILE,),
            in_specs=[row, modspec, vec, metaspec], out_specs=any_spec,
            scratch_shapes=[pltpu.VMEM((2, MOVE_TILE, ROW_W), F32), pltpu.SemaphoreType.DMA((2,))]),
        out_shape=jax.ShapeDtypeStruct((n, ROW_W), F32),
        compiler_params=move_params,
        name="moe_dispatch",
    )(pos, x, mod, norm_g, meta)

    tile_of = lambda k, tile, *_: (tile[k], 0)
    ys = pl.pallas_call(
        _experts_kernel,
        grid_spec=pltpu.PrefetchScalarGridSpec(
            num_scalar_prefetch=7, grid=(n_items,),
            in_specs=[pl.BlockSpec((BUCKET_TILE, ROW_W), tile_of),
                      pl.BlockSpec((None, D_MODEL, 2 * D_EXPERT), lambda k, t, b, lo, hi, *_: (lo[k], 0, 0)),
                      pl.BlockSpec((None, D_MODEL, 2 * D_EXPERT), lambda k, t, b, lo, hi, *_: (hi[k], 0, 0)),
                      pl.BlockSpec((None, D_EXPERT, D_MODEL), lambda k, t, b, lo, hi, *_: (lo[k], 0, 0)),
                      pl.BlockSpec((None, D_EXPERT, D_MODEL), lambda k, t, b, lo, hi, *_: (hi[k], 0, 0))],
            out_specs=pl.BlockSpec((BUCKET_TILE, D_MODEL), tile_of)),
        out_shape=jax.ShapeDtypeStruct((n, D_MODEL), F32),
        compiler_params=_params(("arbitrary",)),
        name="moe_experts",
    )(item_tile, item_bucket, item_lo, item_hi, item_first, item_last, used, xs, w_gu, w_gu, w_down, w_down)

    return pl.pallas_call(
        functools.partial(_combine_kernel, nb=tb, final_norm=final_norm),
        grid_spec=pltpu.PrefetchScalarGridSpec(
            num_scalar_prefetch=1, grid=(n // MOVE_TILE,),
            in_specs=[row, modspec, vec, any_spec], out_specs=row,
            scratch_shapes=[pltpu.VMEM((2, MOVE_TILE, D_MODEL), F32), pltpu.SemaphoreType.DMA((2,))]),
        out_shape=jax.ShapeDtypeStruct((n, D_MODEL), F32),
        compiler_params=move_params,
        name="moe_combine",
    )(pos, x, mod, final_g, ys)


def _rope(x, cos, sin_lo, sin_hi):
    return x * cos + pltpu.roll(x, LANES - ROPE_DIM // 2, axis=1) * sin_lo + pltpu.roll(x, ROPE_DIM // 2, axis=1) * sin_hi


def _qkv_kernel(x_ref, mod_ref, kvmod_ref, ngq_ref, ngkv_ref, wq_ref, bq_ref, wkv_ref, bkv_ref,
                cos_ref, slo_ref, shi_ref, q_ref, k_ref, v_ref):
    n = _rms(x_ref[...])
    tb = mod_ref.shape[0]
    hq = _modulate(n, ngq_ref[...], mod_ref[:, :, 0:D_MODEL], mod_ref[:, :, D_MODEL:2 * D_MODEL], tb)
    hkv = _modulate(n, ngkv_ref[...], kvmod_ref[:, :, 0:D_MODEL], kvmod_ref[:, :, D_MODEL:2 * D_MODEL], tb)
    cos, slo, shi = (jnp.concatenate([t[...]] * tb, axis=0) for t in (cos_ref, slo_ref, shi_ref))
    q = _dot(hq.astype(BF16), wq_ref[...]) + bq_ref[...]
    for p in range(PAIRS):
        cols = slice(p * LANES, (p + 1) * LANES)
        q_ref[:, cols] = (_rope(q[:, cols], cos, slo, shi) * ATT_SCALE).astype(q_ref.dtype)
    kv = _dot(hkv.astype(BF16), wkv_ref[...]) + bkv_ref[...]
    for p in range(KV_W // LANES):
        cols = slice(p * LANES, (p + 1) * LANES)
        k_ref[:, cols] = _rope(kv[:, cols], cos, slo, shi)
    v_ref[...] = kv[:, KV_W:]


def _qkv_proj(x, mod, kvmod, w, tables, nb, s, tm):
    n = nb * s
    tb, nt = max(tm // s, 1), max(s // tm, 1)
    row_spec = pl.BlockSpec((tm, D_MODEL), lambda b, j: (b * nt + j, 0))
    kv_spec = pl.BlockSpec((tm, KV_W), lambda b, j: (b * nt + j, 0))
    tab_spec = pl.BlockSpec((tm // tb, LANES), lambda b, j: (j, 0))
    vec = _const_spec((1, D_MODEL))
    return pl.pallas_call(
        _qkv_kernel,
        grid=(nb // tb, nt),
        in_specs=[row_spec,
                  pl.BlockSpec((tb, 1, 6 * D_MODEL), lambda b, j: (b, 0, 0)),
                  pl.BlockSpec((tb, 1, 2 * D_MODEL), lambda b, j: (b, 0, 0)),
                  vec, vec, _const_spec((D_MODEL, D_MODEL)), vec,
                  _const_spec((D_MODEL, 2 * KV_W)), _const_spec((1, 2 * KV_W)),
                  tab_spec, tab_spec, tab_spec],
        out_specs=[row_spec, kv_spec, kv_spec],
        out_shape=[jax.ShapeDtypeStruct((n, D_MODEL), BF16),
                   jax.ShapeDtypeStruct((n, KV_W), F32), jax.ShapeDtypeStruct((n, KV_W), F32)],
        compiler_params=_params(("parallel", "parallel")),
        name="qkv_proj",
    )(x, mod, kvmod, w["norm_gq"], w["norm_gkv"], w["w_q"], w["b_q"], w["w_kv"], w["b_kv"], *tables)


def _attn_kernel(sink_ref, q_ref, k0_ref, k1_ref, k2_ref, v0_ref, v1_ref, v2_ref, o_ref, *, banded):
    n = pl.program_id(1)
    seqs = range(q_ref.shape[0])
    k_all = [jnp.concatenate([k0_ref[i], k1_ref[i], k2_ref[i]], axis=0) for i in seqs]
    v_all = [jnp.concatenate([v0_ref[i], v1_ref[i], v2_ref[i]], axis=0) for i in seqs]
    lane = lax.broadcasted_iota(jnp.int32, (1, LANES), 1)
    lo = lane < HEAD
    col = lax.broadcasted_iota(jnp.int32, (1, 3 * CHUNK), 1)
    if banded:
        visible = (col >= 2 * CHUNK) | ((col >= CHUNK) & (n >= 1)) | (n >= 2)

    def halves(t, kv_head):
        blk = t[:, (kv_head // 2) * LANES:(kv_head // 2 + 1) * LANES]
        if kv_head % 2 == 0:
            t_lo = jnp.where(lo, blk, 0.0)
            t_hi = pltpu.roll(t_lo, HEAD, axis=1)
        else:
            t_hi = jnp.where(lo, 0.0, blk)
            t_lo = pltpu.roll(t_hi, HEAD, axis=1)
        return t_lo.astype(BF16), t_hi.astype(BF16)

    first_pair = lax.broadcasted_iota(jnp.int32, (2 * CHUNK, 1), 0) < CHUNK

    def probs(s, head_a, head_b):
        if banded:
            s = jnp.where(visible, s, NEG_INF)
        sink = jnp.where(first_pair, sink_ref[head_a], sink_ref[head_b])
        m = jnp.maximum(jnp.max(s, axis=-1, keepdims=True), sink)
        e = jnp.exp(s - m)
        return (e / (jnp.sum(e, axis=-1, keepdims=True) + jnp.exp(sink - m))).astype(BF16)

    units = [(i, g) for i in seqs for g in range(N_KV)]
    scores, values = [], []
    for i, g in units:
        k_lo, k_hi = halves(k_all[i], g)
        values.append(halves(v_all[i], g))
        q2 = jnp.concatenate([q_ref[i, :, (2 * g + j) * LANES:(2 * g + j + 1) * LANES] for j in range(2)], axis=0)
        scores.append((_dot_nt(q2, k_lo), _dot_nt(q2, k_hi)))
    for (i, g), (s_lo, s_hi), (v_lo, v_hi) in zip(units, scores, values):
        o = _dot(probs(s_lo, 4 * g, 4 * g + 2), v_lo) + _dot(probs(s_hi, 4 * g + 1, 4 * g + 3), v_hi)
        for j in range(2):
            pair = 2 * g + j
            o_ref[i, :, pair * LANES:(pair + 1) * LANES] = o[j * CHUNK:(j + 1) * CHUNK].astype(o_ref.dtype)


def _attention(q, kv_arrays, kv_chunks, sinks, nb, s, banded):
    nseq = math.gcd(nb, ATTN_SEQS)
    q_spec = pl.BlockSpec((nseq, CHUNK, D_MODEL), lambda b, c: (b, c, 0))
    kv_specs = [pl.BlockSpec((nseq, CHUNK, KV_W), lambda b, c, f=f: (b, f(c), 0)) for f in kv_chunks]
    return pl.pallas_call(
        functools.partial(_attn_kernel, banded=banded),
        grid=(nb // nseq, s // CHUNK),
        in_specs=[pl.BlockSpec(memory_space=pltpu.SMEM), q_spec] + kv_specs,
        out_specs=q_spec,
        out_shape=jax.ShapeDtypeStruct((nb, s, D_MODEL), BF16),
        compiler_params=_params(("parallel", "parallel")),
        name="attention",
    )(sinks, q, *kv_arrays)


def _pad_cols(w, n):
    return jnp.pad(w, ((0, 0), (0, n - w.shape[1])))


def _pad_rows(w, n):
    return jnp.pad(w, ((0, n - w.shape[0]), (0, 0)))


def _rope_tables(pos):
    half = ROPE_DIM // 2
    inv = jnp.power(jnp.float32(ROPE_THETA), -jnp.arange(half, dtype=F32) * (2.0 / ROPE_DIM))
    ang = pos[:, None] * inv[None, :]
    cos, sin = jnp.cos(ang), jnp.sin(ang)
    rest = HEAD - ROPE_DIM
    ones = jnp.ones((pos.shape[0], rest), F32)
    zeros = jnp.zeros((pos.shape[0], rest), F32)
    z8 = jnp.zeros_like(sin)
    per_head = (jnp.concatenate([cos, cos, ones], axis=1),
                jnp.concatenate([-sin, z8, zeros], axis=1),
                jnp.concatenate([z8, sin, zeros], axis=1))
    return tuple(jnp.tile(t, (1, LANES // HEAD)) for t in per_head)


def _state_to_pairs(state):
    nb = state.shape[0]
    st = state.astype(F32).reshape(nb, PAIRS, 2, HEAD, HEAD)
    z = jnp.zeros_like(st[:, :, 0])
    top = jnp.concatenate([st[:, :, 0], z], axis=-1)
    bot = jnp.concatenate([z, st[:, :, 1]], axis=-1)
    return jnp.concatenate([top, bot], axis=-2)


def _pairs_to_state(st):
    nb = st.shape[0]
    s0 = st[:, :, :HEAD, :HEAD]
    s1 = st[:, :, HEAD:, HEAD:]
    return jnp.stack([s0, s1], axis=2).reshape(nb, 2 * PAIRS, HEAD, HEAD)


def _trunk(x, mods, kvmod, pos, prev_x, prev_wkv, past_k, past_v, w, nb, s, expert_w16=None):
    n = nb * s
    made_w16 = {}
    x = x.reshape(n, D_MODEL)
    mod0, mod1 = mods[0][:, None, :], mods[1][:, None, :]
    kvmod = kvmod[:, None, :]

    tm = min(n, PROJ_TILE)
    r, lw, k, v, a, b, g, last_x = _rwkv_proj(x, mod0, prev_x[:, None, :], w["rw"], nb, s, tm)
    seqs = lambda t: t.reshape(nb, s, D_MODEL)
    z, st = _wkv(*(seqs(t) for t in (r, lw, k, v, a, b, g)), _state_to_pairs(prev_wkv), w["rw"], nb, s)
    z = z.reshape(n, D_MODEL)

    def residual_moe(x, z, mixer, mod, l):
        x = _out_proj(x, z, mod[:, :, 2 * D_MODEL:3 * D_MODEL], mixer["w_o"], mixer["b_o"], nb, s, min(n, 1024))
        route = (mod, w["norm_g"][l][1:2], w["moe_route"][l])
        if n >= SPARSE_MIN_TOKENS:
            if expert_w16 is not None and l in expert_w16:
                gu16, dn16 = expert_w16[l]
            else:
                gu16, dn16 = w["moe_gu"][l].astype(BF16), w["moe_down"][l].astype(BF16)
            return _moe_sparse(x, *route, gu16, dn16, w["final_g"], nb, s, l == 1)
        y, *made_w16[l] = _moe(x, *route, w["moe_gu"], w["moe_down"], w["final_g"], nb, s, l, min(n, 1024), l == 1)
        return y

    x = residual_moe(x, z, w["rw"], mod0, 0)

    q, k_new, v_new = _qkv_proj(x, mod1, kvmod, w["at"], _rope_tables(pos), nb, s, tm)
    k_seq, v_seq = k_new.reshape(nb, s, KV_W), v_new.reshape(nb, s, KV_W)
    if past_k is None:
        arrays = [k_seq] * 3 + [v_seq] * 3
        chunks = [lambda c, d=d: jnp.maximum(c - d, 0) for d in (2, 1, 0)] * 2
    else:
        pk = past_k.astype(F32).reshape(nb, 2 * CHUNK, KV_W)
        pv = past_v.astype(F32).reshape(nb, 2 * CHUNK, KV_W)
        arrays = [pk, pk, k_seq, pv, pv, v_seq]
        chunks = [lambda c: 0, lambda c: 1, lambda c: 0] * 2
    o = _attention(q.reshape(nb, s, D_MODEL), arrays, chunks, w["at"]["sinks"], nb, s, past_k is None)
    y = residual_moe(x, o.reshape(n, D_MODEL), w["at"], mod1, 1)

    return (y.reshape(nb, s, D_MODEL), last_x.reshape(1, nb, D_MODEL), _pairs_to_state(st)[None], k_seq, v_seq,
            made_w16)


def kernel(x_prompt, x_sample, state_shift, state_wkv, cache_k, cache_v, c_prompt, c_sample, ada_w, ada_b, norm_g, rw_mu, rw_w_rkv, rw_w0, rw_w1, rw_w2, rw_a0, rw_a1, rw_a2, rw_g1, rw_g2, rw_k_k, rw_k_a, rw_r_k, rw_lnx_w, rw_lnx_b, rw_w_o, kv_ada_w, kv_ada_b, kv_norm_g, w_kv, b_kv, at_w_q, at_b_q, at_sinks, at_w_o, at_b_o, moe_w_group, moe_w_expert, moe_w_gu, moe_w_down, final_norm_g):
    bp, sp, _ = x_prompt.shape
    bs, ss, _ = x_sample.shape
    row = lambda t: t.reshape(1, -1).astype(F32)

    c_all = jnp.concatenate([c_prompt, c_sample], axis=0)
    mods = _cond_linear(c_all, ada_w, ada_b[:, None, :])
    kvmods = _cond_linear(c_all, kv_ada_w[None], kv_ada_b[None, None, :])[0]

    lora_pad, gate_pad = LANES, 2 * LANES
    w = {
        "norm_g": norm_g,
        "final_g": row(final_norm_g),
        "rw": {
            "norm_g": norm_g[0, 0:1], "mu": rw_mu[0], "w_rkv": rw_w_rkv[0].astype(BF16),
            "w0": row(rw_w0[0]), "w1": _pad_cols(rw_w1[0], lora_pad).astype(BF16),
            "w2": _pad_rows(rw_w2[0], lora_pad).astype(BF16),
            "a0": row(rw_a0[0]), "a1": _pad_cols(rw_a1[0], lora_pad).astype(BF16),
            "a2": _pad_rows(rw_a2[0], lora_pad).astype(BF16),
            "g1": _pad_cols(rw_g1[0], gate_pad).astype(BF16), "g2": _pad_rows(rw_g2[0], gate_pad).astype(BF16),
            "k_k": row(rw_k_k[0]), "k_a": row(rw_k_a[0]), "r_k": row(rw_r_k[0]),
            "lnx_w": row(rw_lnx_w[0]), "lnx_b": row(rw_lnx_b[0]),
            "w_o": rw_w_o[0].astype(BF16), "b_o": jnp.zeros((1, D_MODEL), F32),
        },
        "at": {
            "norm_gq": norm_g[1, 0:1], "norm_gkv": row(kv_norm_g),
            "w_q": at_w_q[0].astype(BF16), "b_q": row(at_b_q[0]),
            "w_kv": w_kv.astype(BF16), "b_kv": row(b_kv),
            "sinks": at_sinks[0].astype(F32),
            "w_o": at_w_o[0].astype(BF16), "b_o": row(at_b_o[0]),
        },
        "moe_route": jnp.pad(jnp.concatenate([moe_w_group, moe_w_expert], axis=-1),
                             ((0, 0), (0, 0), (0, LANES - N_GROUPS - N_EXPERTS))),
        "moe_gu": moe_w_gu,
        "moe_down": moe_w_down,
    }

    pos_p = jnp.arange(sp, dtype=F32)
    pos_s = PAST_LEN + jnp.arange(ss, dtype=F32)
    zero_x = jnp.zeros((bp, D_MODEL), x_prompt.dtype)
    zero_wkv = jnp.zeros((bp,) + state_wkv.shape[2:], state_wkv.dtype)
    y_s, s_shift, s_wkv, s_k, s_v, w16 = _trunk(x_sample, mods[:, bp:], kvmods[bp:], pos_s, state_shift[0],
                                                state_wkv[0], cache_k, cache_v, w, bs, ss)
    y_p, p_shift, p_wkv, p_k, p_v, _ = _trunk(x_prompt, mods[:, :bp], kvmods[:bp], pos_p, zero_x, zero_wkv,
                                              None, None, w, bp, sp, w16)
    keep = min(2 * CHUNK, sp)
    heads = lambda t: t.reshape(t.shape[0], t.shape[1], N_KV, HEAD)
    return (y_p, y_s, p_shift, p_wkv.astype(state_wkv.dtype), heads(p_k[:, sp - keep:]), heads(p_v[:, sp - keep:]),
            s_shift, s_wkv.astype(state_wkv.dtype), heads(s_k), heads(s_v))
```
